```python
import jax, jax.numpy as jnp
from jax import lax
import numpy as np

D_MODEL = 2048
BATCH = 8
SEQ = 4096
DEPTH = 4

HEAD_DIM = 128
BLOCK = 128
EPS = 1e-6
ROPE_THETA = 500000.0
PARTIAL_ROPE = HEAD_DIM // 4
FOX_HEADS = (D_MODEL // 4) // HEAD_DIM
FOX_W = FOX_HEADS * HEAD_DIM
FORGET_BIAS_CENTER = 2.0
MLA_HEADS = (D_MODEL // 2) // HEAD_DIM
MLA_Q_RANK = D_MODEL // 4
MLA_KV_RANK = D_MODEL // 4
MLA_NOPE = 128
MLA_ROPE = 64
MLA_V = HEAD_DIM
MLA_W = MLA_HEADS * MLA_V
DIL_HEADS = (D_MODEL // 4) // HEAD_DIM
DIL_W = DIL_HEADS * HEAD_DIM
DIL_BRANCHES = ((128, 1), (512, 4), (2048, 16))
MIX_W = FOX_W + MLA_W + DIL_W
IN_SIZES = (FOX_W, FOX_W, FOX_W, FOX_HEADS,
            MLA_Q_RANK, MLA_KV_RANK, MLA_ROPE,
            DIL_W, DIL_W, DIL_W)
IN_W = sum(IN_SIZES)
D_FF = 5632

kernel_name = "hybrid_fox_mla_dilated_macaron"


def rms_norm(x, g):
    xf = x.astype(jnp.float32)
    y = xf * lax.rsqrt(jnp.mean(xf * xf, axis=-1, keepdims=True) + EPS)
    return (y * g.astype(jnp.float32)).astype(x.dtype)


def swiglu(h, w_gate, w_up, w_down):
    return (jax.nn.silu(h @ w_gate) * (h @ w_up)) @ w_down


def rope_tables(seq, dim):
    inv = 1.0 / (ROPE_THETA ** (jnp.arange(0, dim, 2, dtype=jnp.float32) / dim))
    ang = jnp.arange(seq, dtype=jnp.float32)[:, None] * inv[None, :]
    return jnp.cos(ang), jnp.sin(ang)


def apply_rope(x, cos, sin):
    x1, x2 = jnp.split(x, 2, axis=-1)
    c, s = cos.astype(x.dtype), sin.astype(x.dtype)
    return jnp.concatenate([x1 * c - x2 * s, x1 * s + x2 * c], axis=-1)


def partial_rope(x, cos, sin):
    return jnp.concatenate([apply_rope(x[..., :PARTIAL_ROPE], cos, sin), x[..., PARTIAL_ROPE:]], axis=-1)


def to_heads(t, n_heads):
    b, s, _ = t.shape
    return t.reshape(b, s, n_heads, -1).transpose(0, 2, 1, 3)


def merge_heads(t):
    b, h, s, d = t.shape
    return t.transpose(0, 2, 1, 3).reshape(b, s, h * d)


def causal_block_attention(q, k, v, scale, cum_log_f=None):
    b, h, s_len, _ = q.shape
    nb = s_len // BLOCK
    kpos = jnp.arange(s_len)
    xs = [jnp.arange(nb), q.reshape(b, h, nb, BLOCK, -1).transpose(2, 0, 1, 3, 4)]
    if cum_log_f is not None:
        xs.append(cum_log_f.reshape(b, h, nb, BLOCK).transpose(2, 0, 1, 3))

    def attend(blk):
        i, qi = blk[0], blk[1]
        sc = jnp.einsum("bhqd,bhkd->bhqk", qi, k, preferred_element_type=jnp.float32) * scale
        if cum_log_f is not None:
            sc = sc + (blk[2][..., :, None] - cum_log_f[..., None, :])
        qpos = i * BLOCK + jnp.arange(BLOCK)
        sc = jnp.where(kpos[None, :] <= qpos[:, None], sc, -jnp.inf)
        p = jax.nn.softmax(sc, axis=-1)
        return jnp.einsum("bhqk,bhkd->bhqd", p.astype(v.dtype), v)

    out = lax.map(attend, tuple(xs))
    return out.transpose(1, 2, 0, 3, 4).reshape(b, h, s_len, -1)


def dilated_branch(q, k, v, window, dilation):
    b, h, s_len, hd = q.shape
    L = s_len // dilation
    n_back = window // dilation
    Lp = -(-L // BLOCK) * BLOCK
    nb = Lp // BLOCK

    def to_blocks(t):
        t = t.reshape(b, h, L, dilation, hd).transpose(0, 1, 3, 2, 4)
        t = jnp.pad(t, ((0, 0), (0, 0), (0, 0), (0, Lp - L), (0, 0)))
        return t.reshape(b, h, dilation, nb, BLOCK, hd)

    def with_prev(t):
        prev = jnp.pad(t, ((0, 0), (0, 0), (0, 0), (1, 0), (0, 0), (0, 0)))[:, :, :, :-1]
        return jnp.concatenate([prev, t], axis=4)

    qb = to_blocks(q)
    kc = with_prev(to_blocks(k))
    vc = with_prev(to_blocks(v))
    sc = jnp.einsum("bhrnqd,bhrnkd->bhrnqk", qb, kc, preferred_element_type=jnp.float32) * (hd ** -0.5)
    kidx = jnp.arange(2 * BLOCK)
    dist = (BLOCK + jnp.arange(BLOCK))[:, None] - kidx[None, :]
    band = (dist >= 0) & (dist <= n_back)
    has_prev = (jnp.arange(nb)[:, None, None] > 0) | (kidx[None, None, :] >= BLOCK)
    sc = jnp.where(band[None] & has_prev, sc, -jnp.inf)
    m = jnp.max(sc, axis=-1, keepdims=True)
    e = jnp.exp(sc - m)
    l = jnp.sum(e, axis=-1, keepdims=True)
    o = jnp.einsum("bhrnqk,bhrnkd->bhrnqd", (e / l).astype(v.dtype), vc)
    lse = (m + jnp.log(l))[..., 0]
    o = o.reshape(b, h, dilation, Lp, hd)[:, :, :, :L].transpose(0, 1, 3, 2, 4).reshape(b, h, s_len, hd)
    lse = lse.reshape(b, h, dilation, Lp)[..., :L].transpose(0, 1, 3, 2).reshape(b, h, s_len)
    return o, lse


def dilated_mixture(q, k, v):
    outs, lses = [], []
    for window, dilation in DIL_BRANCHES:
        o, lse = dilated_branch(q, k, v, window, dilation)
        outs.append(o)
        lses.append(lse)
    wts = jax.nn.softmax(jnp.stack(lses, axis=0), axis=0)
    return jnp.sum(wts[..., None].astype(q.dtype) * jnp.stack(outs, axis=0), axis=0)


def _fwd_setup_inputs(seed: int = 0) -> dict:
    key = jax.random.key(seed)
    ks = iter(jax.random.split(key, 32))

    def dense(shape, fan_in):
        return jax.random.normal(next(ks), shape, jnp.float32) * (fan_in ** -0.5)

    def gain(shape):
        return 1.0 + 0.02 * jax.random.normal(next(ks), shape, jnp.float32)

    x = jax.random.normal(next(ks), (BATCH, SEQ, D_MODEL), jnp.float32)
    return {
        "x": x,
        "ffn1_norm": gain((DEPTH, D_MODEL)),
        "ffn1_w_gate": dense((DEPTH, D_MODEL, D_FF), D_MODEL),
        "ffn1_w_up": dense((DEPTH, D_MODEL, D_FF), D_MODEL),
        "ffn1_w_down": dense((DEPTH, D_FF, D_MODEL), D_FF),
        "mix_norm": gain((DEPTH, D_MODEL)),
        "w_in": dense((DEPTH, D_MODEL, IN_W), D_MODEL),
        "fox_forget_bias": FORGET_BIAS_CENTER + 0.5 * jax.random.normal(next(ks), (DEPTH, FOX_HEADS), jnp.float32),
        "mla_q_norm": gain((DEPTH, MLA_Q_RANK)),
        "mla_kv_norm": gain((DEPTH, MLA_KV_RANK)),
        "mla_w_uq": dense((DEPTH, MLA_Q_RANK, MLA_HEADS * (MLA_NOPE + MLA_ROPE)), MLA_Q_RANK),
        "mla_w_ukv": dense((DEPTH, MLA_KV_RANK, MLA_HEADS * (MLA_NOPE + MLA_V)), MLA_KV_RANK),
        "w_out": dense((DEPTH, MIX_W, D_MODEL), MIX_W),
        "ffn2_norm": gain((DEPTH, D_MODEL)),
        "ffn2_w_gate": dense((DEPTH, D_MODEL, D_FF), D_MODEL),
        "ffn2_w_up": dense((DEPTH, D_MODEL, D_FF), D_MODEL),
        "ffn2_w_down": dense((DEPTH, D_FF, D_MODEL), D_FF),
        "final_norm": gain((D_MODEL,)),
    }


def _fwd_reference(x, ffn1_norm, ffn1_w_gate, ffn1_w_up, ffn1_w_down, mix_norm, w_in, fox_forget_bias,
              mla_q_norm, mla_kv_norm, mla_w_uq, mla_w_ukv, w_out, ffn2_norm, ffn2_w_gate, ffn2_w_up,
              ffn2_w_down, final_norm):
    b, s_len, _ = x.shape
    cos_p, sin_p = rope_tables(s_len, PARTIAL_ROPE)
    cos_m, sin_m = rope_tables(s_len, MLA_ROPE)
    offsets = []
    acc = 0
    for size in IN_SIZES[:-1]:
        acc += size
        offsets.append(acc)

    for l in range(DEPTH):
        x = x + 0.5 * swiglu(rms_norm(x, ffn1_norm[l]), ffn1_w_gate[l], ffn1_w_up[l], ffn1_w_down[l])

        h = rms_norm(x, mix_norm[l])
        proj = h @ w_in[l]
        fq, fk, fv, f_logit, c_q, c_kv, k_r, dq, dk, dv = jnp.split(proj, offsets, axis=-1)

        log_f = jax.nn.log_sigmoid((f_logit + fox_forget_bias[l]).astype(jnp.float32))
        cum = jnp.cumsum(log_f, axis=1).transpose(0, 2, 1)
        out_a = causal_block_attention(to_heads(fq, FOX_HEADS), to_heads(fk, FOX_HEADS),
                                       to_heads(fv, FOX_HEADS), HEAD_DIM ** -0.5, cum)

        q_b = to_heads(rms_norm(c_q, mla_q_norm[l]) @ mla_w_uq[l], MLA_HEADS)
        q_b = jnp.concatenate([q_b[..., :MLA_NOPE], apply_rope(q_b[..., MLA_NOPE:], cos_m, sin_m)], axis=-1)
        kv_b = to_heads(rms_norm(c_kv, mla_kv_norm[l]) @ mla_w_ukv[l], MLA_HEADS)
        k_rope = apply_rope(k_r[:, None], cos_m, sin_m)
        k_b = jnp.concatenate([kv_b[..., :MLA_NOPE],
                               jnp.broadcast_to(k_rope, (b, MLA_HEADS, s_len, MLA_ROPE))], axis=-1)
        out_b = causal_block_attention(q_b, k_b, kv_b[..., MLA_NOPE:], (MLA_NOPE + MLA_ROPE) ** -0.5)

        out_c = dilated_mixture(partial_rope(to_heads(dq, DIL_HEADS), cos_p, sin_p),
                                partial_rope(to_heads(dk, DIL_HEADS), cos_p, sin_p),
                                to_heads(dv, DIL_HEADS))

        mixed = jnp.concatenate([merge_heads(out_a), merge_heads(out_b), merge_heads(out_c)], axis=-1)
        x = x + mixed @ w_out[l]

        x = x + 0.5 * swiglu(rms_norm(x, ffn2_norm[l]), ffn2_w_gate[l], ffn2_w_up[l], ffn2_w_down[l])

    return rms_norm(x, final_norm)


import jax as _jax
import jax.numpy as _jnp

TWIN_FORMAT = 'train_step'
FWD_PARAMS = ['x', 'ffn1_norm', 'ffn1_w_gate', 'ffn1_w_up', 'ffn1_w_down', 'mix_norm', 'w_in', 'fox_forget_bias', 'mla_q_norm', 'mla_kv_norm', 'mla_w_uq', 'mla_w_ukv', 'w_out', 'ffn2_norm', 'ffn2_w_gate', 'ffn2_w_up', 'ffn2_w_down', 'final_norm']
TWIN_WEIGHTS = ['ffn1_norm', 'ffn1_w_gate', 'ffn1_w_up', 'ffn1_w_down', 'mix_norm', 'w_in', 'fox_forget_bias', 'mla_q_norm', 'mla_kv_norm', 'mla_w_uq', 'mla_w_ukv', 'w_out', 'ffn2_norm', 'ffn2_w_gate', 'ffn2_w_up', 'ffn2_w_down', 'final_norm']
TWIN_DIFF_INPUT = 'x'
TWIN_INPUTS = ['x', 'ffn1_norm', 'ffn1_w_gate', 'ffn1_w_up', 'ffn1_w_down', 'mix_norm', 'w_in', 'fox_forget_bias', 'mla_q_norm', 'mla_kv_norm', 'mla_w_uq', 'mla_w_ukv', 'w_out', 'ffn2_norm', 'ffn2_w_gate', 'ffn2_w_up', 'ffn2_w_down', 'final_norm', 'loss_target', 'm_ffn1_norm', 'm_ffn1_w_gate', 'm_ffn1_w_up', 'm_ffn1_w_down', 'm_mix_norm', 'm_w_in', 'm_fox_forget_bias', 'm_mla_q_norm', 'm_mla_kv_norm', 'm_mla_w_uq', 'm_mla_w_ukv', 'm_w_out', 'm_ffn2_norm', 'm_ffn2_w_gate', 'm_ffn2_w_up', 'm_ffn2_w_down', 'm_final_norm', 'v_ffn1_norm', 'v_ffn1_w_gate', 'v_ffn1_w_up', 'v_ffn1_w_down', 'v_mix_norm', 'v_w_in', 'v_fox_forget_bias', 'v_mla_q_norm', 'v_mla_kv_norm', 'v_mla_w_uq', 'v_mla_w_ukv', 'v_w_out', 'v_ffn2_norm', 'v_ffn2_w_gate', 'v_ffn2_w_up', 'v_ffn2_w_down', 'v_final_norm']
TWIN_OUTPUTS = ['loss', 'grad_x', 'grad_ffn1_norm', 'grad_ffn1_w_gate', 'grad_ffn1_w_up', 'grad_ffn1_w_down', 'grad_mix_norm', 'grad_w_in', 'grad_fox_forget_bias', 'grad_mla_q_norm', 'grad_mla_kv_norm', 'grad_mla_w_uq', 'grad_mla_w_ukv', 'grad_w_out', 'grad_ffn2_norm', 'grad_ffn2_w_gate', 'grad_ffn2_w_up', 'grad_ffn2_w_down', 'grad_final_norm', 'delta_ffn1_norm', 'delta_ffn1_w_gate', 'delta_ffn1_w_up', 'delta_ffn1_w_down', 'delta_mix_norm', 'delta_w_in', 'delta_fox_forget_bias', 'delta_mla_q_norm', 'delta_mla_kv_norm', 'delta_mla_w_uq', 'delta_mla_w_ukv', 'delta_w_out', 'delta_ffn2_norm', 'delta_ffn2_w_gate', 'delta_ffn2_w_up', 'delta_ffn2_w_down', 'delta_final_norm', 'new_m_ffn1_norm', 'new_m_ffn1_w_gate', 'new_m_ffn1_w_up', 'new_m_ffn1_w_down', 'new_m_mix_norm', 'new_m_w_in', 'new_m_fox_forget_bias', 'new_m_mla_q_norm', 'new_m_mla_kv_norm', 'new_m_mla_w_uq', 'new_m_mla_w_ukv', 'new_m_w_out', 'new_m_ffn2_norm', 'new_m_ffn2_w_gate', 'new_m_ffn2_w_up', 'new_m_ffn2_w_down', 'new_m_final_norm', 'new_v_ffn1_norm', 'new_v_ffn1_w_gate', 'new_v_ffn1_w_up', 'new_v_ffn1_w_down', 'new_v_mix_norm', 'new_v_w_in', 'new_v_fox_forget_bias', 'new_v_mla_q_norm', 'new_v_mla_kv_norm', 'new_v_mla_w_uq', 'new_v_mla_w_ukv', 'new_v_w_out', 'new_v_ffn2_norm', 'new_v_ffn2_w_gate', 'new_v_ffn2_w_up', 'new_v_ffn2_w_down', 'new_v_final_norm']
TWIN_LEAF_KINDS = {'loss': 'loss', 'grad_x': 'grad_x', 'grad_ffn1_norm': 'grad_w', 'grad_ffn1_w_gate': 'grad_w', 'grad_ffn1_w_up': 'grad_w', 'grad_ffn1_w_down': 'grad_w', 'grad_mix_norm': 'grad_w', 'grad_w_in': 'grad_w', 'grad_fox_forget_bias': 'grad_w', 'grad_mla_q_norm': 'grad_w', 'grad_mla_kv_norm': 'grad_w', 'grad_mla_w_uq': 'grad_w', 'grad_mla_w_ukv': 'grad_w', 'grad_w_out': 'grad_w', 'grad_ffn2_norm': 'grad_w', 'grad_ffn2_w_gate': 'grad_w', 'grad_ffn2_w_up': 'grad_w', 'grad_ffn2_w_down': 'grad_w', 'grad_final_norm': 'grad_w', 'delta_ffn1_norm': 'delta_w', 'delta_ffn1_w_gate': 'delta_w', 'delta_ffn1_w_up': 'delta_w', 'delta_ffn1_w_down': 'delta_w', 'delta_mix_norm': 'delta_w', 'delta_w_in': 'delta_w', 'delta_fox_forget_bias': 'delta_w', 'delta_mla_q_norm': 'delta_w', 'delta_mla_kv_norm': 'delta_w', 'delta_mla_w_uq': 'delta_w', 'delta_mla_w_ukv': 'delta_w', 'delta_w_out': 'delta_w', 'delta_ffn2_norm': 'delta_w', 'delta_ffn2_w_gate': 'delta_w', 'delta_ffn2_w_up': 'delta_w', 'delta_ffn2_w_down': 'delta_w', 'delta_final_norm': 'delta_w', 'new_m_ffn1_norm': 'new_m', 'new_m_ffn1_w_gate': 'new_m', 'new_m_ffn1_w_up': 'new_m', 'new_m_ffn1_w_down': 'new_m', 'new_m_mix_norm': 'new_m', 'new_m_w_in': 'new_m', 'new_m_fox_forget_bias': 'new_m', 'new_m_mla_q_norm': 'new_m', 'new_m_mla_kv_norm': 'new_m', 'new_m_mla_w_uq': 'new_m', 'new_m_mla_w_ukv': 'new_m', 'new_m_w_out': 'new_m', 'new_m_ffn2_norm': 'new_m', 'new_m_ffn2_w_gate': 'new_m', 'new_m_ffn2_w_up': 'new_m', 'new_m_ffn2_w_down': 'new_m', 'new_m_final_norm': 'new_m', 'new_v_ffn1_norm': 'new_v', 'new_v_ffn1_w_gate': 'new_v', 'new_v_ffn1_w_up': 'new_v', 'new_v_ffn1_w_down': 'new_v', 'new_v_mix_norm': 'new_v', 'new_v_w_in': 'new_v', 'new_v_fox_forget_bias': 'new_v', 'new_v_mla_q_norm': 'new_v', 'new_v_mla_kv_norm': 'new_v', 'new_v_mla_w_uq': 'new_v', 'new_v_mla_w_ukv': 'new_v', 'new_v_w_out': 'new_v', 'new_v_ffn2_norm': 'new_v', 'new_v_ffn2_w_gate': 'new_v', 'new_v_ffn2_w_up': 'new_v', 'new_v_ffn2_w_down': 'new_v', 'new_v_final_norm': 'new_v'}


def _forward(args):
    return _fwd_reference(*[args[k] for k in FWD_PARAMS])


def _output_shape():
    out = _jax.eval_shape(lambda: _forward(_fwd_setup_inputs(0)))
    return out.shape, out.dtype

N_MICROBATCH = 1
ADAM_LR = 0.001
ADAM_B1 = 0.9
ADAM_B2 = 0.999
ADAM_EPS = 1e-08
ADAM_WD = 0.01
ADAM_STEP = 10
PER_EXAMPLE_BATCH_AXIS = {'x': 0, 'loss_target': 0}
SHARED_INPUTS = []
_WEIGHT_DTYPES = {'ffn1_norm': _jnp.float32, 'ffn1_w_gate': _jnp.float32, 'ffn1_w_up': _jnp.float32, 'ffn1_w_down': _jnp.float32, 'mix_norm': _jnp.float32, 'w_in': _jnp.float32, 'fox_forget_bias': _jnp.float32, 'mla_q_norm': _jnp.float32, 'mla_kv_norm': _jnp.float32, 'mla_w_uq': _jnp.float32, 'mla_w_ukv': _jnp.float32, 'w_out': _jnp.float32, 'ffn2_norm': _jnp.float32, 'ffn2_w_gate': _jnp.float32, 'ffn2_w_up': _jnp.float32, 'ffn2_w_down': _jnp.float32, 'final_norm': _jnp.float32}
MOMENT_SCALE = {'ffn1_norm': 4.047317e-02, 'ffn1_w_gate': 1.682735e-02, 'ffn1_w_up': 1.627749e-02, 'ffn1_w_down': 2.698951e-02, 'mix_norm': 3.977902e-02, 'w_in': 2.776149e-02, 'fox_forget_bias': 2.167151e-01, 'mla_q_norm': 2.035620e-02, 'mla_kv_norm': 3.897100e-02, 'mla_w_uq': 1.150043e-02, 'mla_w_ukv': 1.972984e-02, 'w_out': 2.997479e-02, 'ffn2_norm': 3.583780e-02, 'ffn2_w_gate': 1.537863e-02, 'ffn2_w_up': 1.487479e-02, 'ffn2_w_down': 2.468440e-02, 'final_norm': 1.597750e+01}


def _to_microbatches(a, axis):
    t = _jnp.moveaxis(a, axis, 0)
    t = t.reshape((N_MICROBATCH, t.shape[0] // N_MICROBATCH) + t.shape[1:])
    return _jnp.moveaxis(t, 1, axis + 1)


def setup_inputs(seed: int = 0) -> dict:
    inp = _fwd_setup_inputs(seed)
    key = _jax.random.fold_in(_jax.random.key(seed), 7919)
    shape, _ = _output_shape()
    out = dict(inp)
    out["loss_target"] = _jax.random.normal(_jax.random.fold_in(key, 0), shape, _jnp.float32)
    for i, name in enumerate(TWIN_WEIGHTS):
        w = inp[name].astype(_jnp.float32)
        if MOMENT_SCALE is None:
            s = _jnp.sqrt(_jnp.mean(_jnp.square(w)) + 1e-30)
        else:
            s = MOMENT_SCALE[name]
        km, kv = _jax.random.split(_jax.random.fold_in(key, i + 1))
        out[name] = w
        out["m_" + name] = s * _jax.random.normal(km, w.shape, _jnp.float32)
        out["v_" + name] = (s * s) * _jax.random.uniform(kv, w.shape, _jnp.float32, 0.5, 1.5)
    if N_MICROBATCH > 1:
        for name, axis in PER_EXAMPLE_BATCH_AXIS.items():
            out[name] = _to_microbatches(out[name], axis)
    return {'x': out['x'], 'ffn1_norm': out['ffn1_norm'], 'ffn1_w_gate': out['ffn1_w_gate'], 'ffn1_w_up': out['ffn1_w_up'], 'ffn1_w_down': out['ffn1_w_down'], 'mix_norm': out['mix_norm'], 'w_in': out['w_in'], 'fox_forget_bias': out['fox_forget_bias'], 'mla_q_norm': out['mla_q_norm'], 'mla_kv_norm': out['mla_kv_norm'], 'mla_w_uq': out['mla_w_uq'], 'mla_w_ukv': out['mla_w_ukv'], 'w_out': out['w_out'], 'ffn2_norm': out['ffn2_norm'], 'ffn2_w_gate': out['ffn2_w_gate'], 'ffn2_w_up': out['ffn2_w_up'], 'ffn2_w_down': out['ffn2_w_down'], 'final_norm': out['final_norm'], 'loss_target': out['loss_target'], 'm_ffn1_norm': out['m_ffn1_norm'], 'm_ffn1_w_gate': out['m_ffn1_w_gate'], 'm_ffn1_w_up': out['m_ffn1_w_up'], 'm_ffn1_w_down': out['m_ffn1_w_down'], 'm_mix_norm': out['m_mix_norm'], 'm_w_in': out['m_w_in'], 'm_fox_forget_bias': out['m_fox_forget_bias'], 'm_mla_q_norm': out['m_mla_q_norm'], 'm_mla_kv_norm': out['m_mla_kv_norm'], 'm_mla_w_uq': out['m_mla_w_uq'], 'm_mla_w_ukv': out['m_mla_w_ukv'], 'm_w_out': out['m_w_out'], 'm_ffn2_norm': out['m_ffn2_norm'], 'm_ffn2_w_gate': out['m_ffn2_w_gate'], 'm_ffn2_w_up': out['m_ffn2_w_up'], 'm_ffn2_w_down': out['m_ffn2_w_down'], 'm_final_norm': out['m_final_norm'], 'v_ffn1_norm': out['v_ffn1_norm'], 'v_ffn1_w_gate': out['v_ffn1_w_gate'], 'v_ffn1_w_up': out['v_ffn1_w_up'], 'v_ffn1_w_down': out['v_ffn1_w_down'], 'v_mix_norm': out['v_mix_norm'], 'v_w_in': out['v_w_in'], 'v_fox_forget_bias': out['v_fox_forget_bias'], 'v_mla_q_norm': out['v_mla_q_norm'], 'v_mla_kv_norm': out['v_mla_kv_norm'], 'v_mla_w_uq': out['v_mla_w_uq'], 'v_mla_w_ukv': out['v_mla_w_ukv'], 'v_w_out': out['v_w_out'], 'v_ffn2_norm': out['v_ffn2_norm'], 'v_ffn2_w_gate': out['v_ffn2_w_gate'], 'v_ffn2_w_up': out['v_ffn2_w_up'], 'v_ffn2_w_down': out['v_ffn2_w_down'], 'v_final_norm': out['v_final_norm']}


def _loss(weights, diff, rest, loss_target):
    with _jax.named_scope("forward"):
        args = {**rest, TWIN_DIFF_INPUT: diff, **{k: w.astype(_WEIGHT_DTYPES[k]) for k, w in weights.items()}}
        y = _forward(args)
    with _jax.named_scope("loss_head"):
        err = _jnp.square(y.astype(_jnp.float32) - loss_target)
        return 0.5 * _jnp.sum(_jnp.mean(err, axis=-1)) if err.ndim else 0.5 * err


def _adamw(w, g, m, v):
    m = ADAM_B1 * m + (1.0 - ADAM_B1) * g
    v = ADAM_B2 * v + (1.0 - ADAM_B2) * _jnp.square(g)
    m_hat = m / (1.0 - ADAM_B1 ** ADAM_STEP)
    v_hat = v / (1.0 - ADAM_B2 ** ADAM_STEP)
    delta = -ADAM_LR * (m_hat / (_jnp.sqrt(v_hat) + ADAM_EPS) + ADAM_WD * w)
    return delta, m, v


def reference(x, ffn1_norm, ffn1_w_gate, ffn1_w_up, ffn1_w_down, mix_norm, w_in, fox_forget_bias, mla_q_norm, mla_kv_norm, mla_w_uq, mla_w_ukv, w_out, ffn2_norm, ffn2_w_gate, ffn2_w_up, ffn2_w_down, final_norm, loss_target, m_ffn1_norm, m_ffn1_w_gate, m_ffn1_w_up, m_ffn1_w_down, m_mix_norm, m_w_in, m_fox_forget_bias, m_mla_q_norm, m_mla_kv_norm, m_mla_w_uq, m_mla_w_ukv, m_w_out, m_ffn2_norm, m_ffn2_w_gate, m_ffn2_w_up, m_ffn2_w_down, m_final_norm, v_ffn1_norm, v_ffn1_w_gate, v_ffn1_w_up, v_ffn1_w_down, v_mix_norm, v_w_in, v_fox_forget_bias, v_mla_q_norm, v_mla_kv_norm, v_mla_w_uq, v_mla_w_ukv, v_w_out, v_ffn2_norm, v_ffn2_w_gate, v_ffn2_w_up, v_ffn2_w_down, v_final_norm):
    given = dict(x=x, ffn1_norm=ffn1_norm, ffn1_w_gate=ffn1_w_gate, ffn1_w_up=ffn1_w_up, ffn1_w_down=ffn1_w_down, mix_norm=mix_norm, w_in=w_in, fox_forget_bias=fox_forget_bias, mla_q_norm=mla_q_norm, mla_kv_norm=mla_kv_norm, mla_w_uq=mla_w_uq, mla_w_ukv=mla_w_ukv, w_out=w_out, ffn2_norm=ffn2_norm, ffn2_w_gate=ffn2_w_gate, ffn2_w_up=ffn2_w_up, ffn2_w_down=ffn2_w_down, final_norm=final_norm, loss_target=loss_target, m_ffn1_norm=m_ffn1_norm, m_ffn1_w_gate=m_ffn1_w_gate, m_ffn1_w_up=m_ffn1_w_up, m_ffn1_w_down=m_ffn1_w_down, m_mix_norm=m_mix_norm, m_w_in=m_w_in, m_fox_forget_bias=m_fox_forget_bias, m_mla_q_norm=m_mla_q_norm, m_mla_kv_norm=m_mla_kv_norm, m_mla_w_uq=m_mla_w_uq, m_mla_w_ukv=m_mla_w_ukv, m_w_out=m_w_out, m_ffn2_norm=m_ffn2_norm, m_ffn2_w_gate=m_ffn2_w_gate, m_ffn2_w_up=m_ffn2_w_up, m_ffn2_w_down=m_ffn2_w_down, m_final_norm=m_final_norm, v_ffn1_norm=v_ffn1_norm, v_ffn1_w_gate=v_ffn1_w_gate, v_ffn1_w_up=v_ffn1_w_up, v_ffn1_w_down=v_ffn1_w_down, v_mix_norm=v_mix_norm, v_w_in=v_w_in, v_fox_forget_bias=v_fox_forget_bias, v_mla_q_norm=v_mla_q_norm, v_mla_kv_norm=v_mla_kv_norm, v_mla_w_uq=v_mla_w_uq, v_mla_w_ukv=v_mla_w_ukv, v_w_out=v_w_out, v_ffn2_norm=v_ffn2_norm, v_ffn2_w_gate=v_ffn2_w_gate, v_ffn2_w_up=v_ffn2_w_up, v_ffn2_w_down=v_ffn2_w_down, v_final_norm=v_final_norm)
    weights = {n: given[n] for n in TWIN_WEIGHTS}
    shared = {n: given[n] for n in SHARED_INPUTS}
    per_example = {n: given[n] for n in ['x']}
    grad_fn = _jax.value_and_grad(_loss, argnums=(0, 1))

    def one_microbatch(ex, loss_target):
        ex = dict(ex)
        diff = ex.pop(TWIN_DIFF_INPUT)
        return grad_fn(weights, diff, {**shared, **ex}, loss_target)

    if N_MICROBATCH == 1:
        loss, (grad_w, grad_x) = one_microbatch(per_example, given["loss_target"])
    else:
        def body(carry, xs):
            loss_sum, grad_sum = carry
            l_k, (gw_k, gx_k) = one_microbatch(xs[0], xs[1])
            with _jax.named_scope("update"):
                return (loss_sum + l_k, _jax.tree.map(_jnp.add, grad_sum, gw_k)), gx_k

        init = (_jnp.zeros((), _jnp.float32), _jax.tree.map(_jnp.zeros_like, weights))
        (loss, grad_w), grad_x = _jax.lax.scan(body, init, (per_example, given["loss_target"]))
    with _jax.named_scope("update"):
        delta_w, new_m, new_v = {}, {}, {}
        for n in TWIN_WEIGHTS:
            delta_w[n], new_m[n], new_v[n] = _adamw(weights[n], grad_w[n], given["m_" + n], given["v_" + n])
    return (loss, grad_x, *[grad_w[n] for n in TWIN_WEIGHTS], *[delta_w[n] for n in TWIN_WEIGHTS],
            *[new_m[n] for n in TWIN_WEIGHTS], *[new_v[n] for n in TWIN_WEIGHTS])
```

```python
import functools

import jax
import jax.numpy as jnp
from jax import lax
from jax.experimental import pallas as pl
from jax.experimental.pallas import tpu as pltpu

F32 = jnp.float32
MXU = jnp.bfloat16
ACT = jnp.bfloat16
COMM = jnp.bfloat16
HI = lax.Precision.HIGHEST
NN = (((1,), (0,)), ((), ()))
NT = (((1,), (1,)), ((), ()))
TN = (((0,), (0,)), ((), ()))

NDEV = 8
HEAD = 128
EPS = 1e-6
ROPE_THETA = 500000.0
PARTIAL_ROPE = HEAD // 4
MLA_ROPE = 64
MLA_QK = HEAD + MLA_ROPE
DIL_BRANCHES = (1, 4, 16)
NEG = -1e30
VMEM_LIMIT = 48 * 1024 * 1024

ADAMW_STEP_BYTES = 12 * 1024 * 1024

ADAM_LR, ADAM_B1, ADAM_B2, ADAM_EPS, ADAM_WD, ADAM_STEP = 0.001, 0.9, 0.999, 1e-08, 0.01, 10


def _cparams(n_axes):
    return pltpu.CompilerParams(dimension_semantics=("arbitrary",) * n_axes, vmem_limit_bytes=VMEM_LIMIT)


def _tile(n, t):
    t = min(n, t)
    assert n % t == 0, (n, t)
    return t


def mm(a, b, *, name, dims, grid, a_spec, b_spec, o_spec, out_shape, nk, extras=(), extra_specs=(), epilogue=None):
    n_ex = len(extras)
    kaxis = len(grid) - 1

    def body(a_ref, b_ref, *rest):
        ex, o_ref = rest[:n_ex], rest[n_ex]
        part = lax.dot_general(a_ref[...].astype(MXU), b_ref[...].astype(MXU), dims, preferred_element_type=F32)

        def finish(acc):
            res = acc if epilogue is None else epilogue(acc, *[e[...] for e in ex])
            o_ref[...] = res.astype(o_ref.dtype)

        if nk == 1:
            finish(part)
        else:
            acc_ref = rest[n_ex + 1]
            k = pl.program_id(kaxis)

            @pl.when(k == 0)
            def _():
                acc_ref[...] = part

            @pl.when(k > 0)
            def _():
                acc_ref[...] += part

            @pl.when(k == nk - 1)
            def _():
                finish(acc_ref[...])

    acc_shape = tuple(d for d in o_spec.block_shape if d is not None)
    return pl.pallas_call(
        body, name=name, grid=grid, in_specs=[a_spec, b_spec, *extra_specs], out_specs=o_spec, out_shape=out_shape,
        scratch_shapes=[] if nk == 1 else [pltpu.VMEM(acc_shape, F32)], compiler_params=_cparams(len(grid)),
    )(a, b, *extras)


def rowwise(fn, ins, in_specs, outs, out_specs, grid, name, acc=None):
    acc = acc or {}
    n_in = len(ins)

    def body(*refs):
        vals = fn(*[r[...] for r in refs[:n_in]])
        if not isinstance(vals, (tuple, list)):
            vals = (vals,)
        for i, (r, v) in enumerate(zip(refs[n_in:], vals)):
            if i in acc:
                first = functools.reduce(jnp.logical_and, [pl.program_id(ax) == 0 for ax in acc[i]])

                @pl.when(first)
                def _(r=r, v=v):
                    r[...] = v.astype(r.dtype)

                @pl.when(jnp.logical_not(first))
                def _(r=r, v=v):
                    r[...] += v.astype(r.dtype)
            else:
                r[...] = v.astype(r.dtype)

    return pl.pallas_call(
        body, name=name, grid=grid, in_specs=in_specs, out_specs=out_specs, out_shape=outs,
        compiler_params=_cparams(len(grid)),
    )(*ins)


def _rows(t, c, col=0):
    return pl.BlockSpec((t, c), lambda m, col=col: (m, col))


def _whole(shape):
    nd = len(shape)
    return pl.BlockSpec(shape, lambda *_: (0,) * nd)


def _rms(x, g):
    x = x.astype(F32)
    return x * lax.rsqrt(jnp.mean(x * x, axis=-1, keepdims=True) + EPS) * g


def _rms_bwd(x, g, dy):
    x = x.astype(F32)
    dy = dy.astype(F32)
    r = lax.rsqrt(jnp.mean(x * x, axis=-1, keepdims=True) + EPS)
    xh = x * r
    dg = jnp.sum(dy * xh, axis=0, keepdims=True)
    dxh = dy * g
    dx = r * (dxh - xh * jnp.mean(dxh * xh, axis=-1, keepdims=True))
    return dx, dg


def _swap_matrix(n):
    i = lax.broadcasted_iota(jnp.int32, (n, n), 0)
    j = lax.broadcasted_iota(jnp.int32, (n, n), 1)
    return jnp.where(((i + n // 2) % n) == j, 1.0, 0.0).astype(F32)


def _rope(x, cos, sin_signed, perm):
    return x * cos + jnp.dot(x, perm, precision=HI, preferred_element_type=F32) * sin_signed


def _rope_t(dy, cos, sin_signed, perm):
    return dy * cos + jnp.dot(dy * sin_signed, perm, precision=HI, preferred_element_type=F32)


def rope_tables(seq, dim):
    inv = 1.0 / (ROPE_THETA ** (jnp.arange(0, dim, 2, dtype=F32) / dim))
    ang = jnp.arange(seq, dtype=F32)[:, None] * inv[None, :]
    cos, sin = jnp.cos(ang), jnp.sin(ang)
    return jnp.concatenate([cos, cos], axis=1), jnp.concatenate([-sin, sin], axis=1)


def _hspec(arr_kind, t, w, off, seq_of):
    if arr_kind == "cols":
        return pl.BlockSpec((t, w), lambda h, i, j: (seq_of(i, j), off + h))
    return pl.BlockSpec((None, t, w), lambda h, i, j: (h, seq_of(i, j), off))


def _colspec(t, seq_of):
    return pl.BlockSpec((None, t, 1), lambda h, i, j: (h, seq_of(i, j), 0))


def _rowspec(t, seq_of):
    return pl.BlockSpec((None, 1, t), lambda h, i, j: (h, 0, seq_of(i, j)))


def _causal(s, qi, kj, t, transposed=False):
    a = lax.broadcasted_iota(jnp.int32, (t, t), 0)
    b = lax.broadcasted_iota(jnp.int32, (t, t), 1)
    keep = (kj * t + a <= qi * t + b) if transposed else (kj * t + b <= qi * t + a)
    return jnp.where(keep, s, NEG)


def flash_fwd(q, k, v, cum, *, n_heads, seq, t, scale, name):
    nb = seq // t
    qs, ks = (lambda i, j: i), (lambda i, j: jnp.minimum(j, i))
    ins = [q[0], k[0], v[0]]
    specs = [_hspec(q[1], t, q[2], q[3], qs), _hspec(k[1], t, k[2], k[3], ks), _hspec(v[1], t, v[2], v[3], ks)]
    if cum is not None:
        ins += [cum[0], cum[1]]
        specs += [_colspec(t, qs), _rowspec(t, ks)]

    def body(*refs):
        q_ref, k_ref, v_ref = refs[:3]
        o_ref, lse_ref, m_s, l_s, acc_s = refs[-5:]
        i, j = pl.program_id(1), pl.program_id(2)

        @pl.when(j == 0)
        def _():
            m_s[...] = jnp.full(m_s.shape, NEG, F32)
            l_s[...] = jnp.zeros(l_s.shape, F32)
            acc_s[...] = jnp.zeros(acc_s.shape, F32)

        @pl.when(j <= i)
        def _():
            s = lax.dot_general(q_ref[...].astype(MXU), k_ref[...].astype(MXU), NT, preferred_element_type=F32) * scale
            if cum is not None:
                s = s + (refs[3][...] - refs[4][...])
            s = _causal(s, i, j, t)
            m_new = jnp.maximum(m_s[...], jnp.max(s, axis=1, keepdims=True))
            alpha = jnp.exp(m_s[...] - m_new)
            p = jnp.exp(s - m_new)
            l_s[...] = alpha * l_s[...] + jnp.sum(p, axis=1, keepdims=True)
            acc_s[...] = alpha * acc_s[...] + jnp.dot(p.astype(MXU), v_ref[...].astype(MXU), preferred_element_type=F32)
            m_s[...] = m_new

        @pl.when(j == nb - 1)
        def _():
            o_ref[...] = (acc_s[...] / l_s[...]).astype(o_ref.dtype)
            lse_ref[...] = m_s[...] + jnp.log(l_s[...])

    return pl.pallas_call(
        body, name=name, grid=(n_heads, nb, nb), in_specs=specs,
        out_specs=[pl.BlockSpec((t, HEAD), lambda h, i, j: (i, h)), _colspec(t, qs)],
        out_shape=[jax.ShapeDtypeStruct((seq, n_heads * HEAD), F32), jax.ShapeDtypeStruct((n_heads, seq, 1), F32)],
        scratch_shapes=[pltpu.VMEM((t, 1), F32), pltpu.VMEM((t, 1), F32), pltpu.VMEM((t, HEAD), F32)],
        compiler_params=_cparams(3),
    )(*ins)


def flash_bwd_dq(q, k, v, do, lse, delta, cum, *, n_heads, seq, t, scale, name):
    nb = seq // t
    qs, ks = (lambda i, j: i), (lambda i, j: jnp.minimum(j, i))
    ins = [q[0], k[0], v[0], do, lse, delta]
    specs = [_hspec(q[1], t, q[2], q[3], qs), _hspec(k[1], t, k[2], k[3], ks), _hspec(v[1], t, v[2], v[3], ks),
             _hspec("cols", t, HEAD, 0, qs), _colspec(t, qs), _colspec(t, qs)]
    if cum is not None:
        ins += [cum[0], cum[1]]
        specs += [_colspec(t, qs), _rowspec(t, ks)]
    wq = q[2]
    n_out = 1 if cum is None else 2

    def body(*refs):
        q_ref, k_ref, v_ref, do_ref, lse_ref, dl_ref = refs[:6]
        outs = refs[-2 * n_out:-n_out]
        accs = refs[-n_out:]
        i, j = pl.program_id(1), pl.program_id(2)

        @pl.when(j == 0)
        def _():
            for a in accs:
                a[...] = jnp.zeros(a.shape, F32)

        @pl.when(j <= i)
        def _():
            kb = k_ref[...].astype(MXU)
            s = lax.dot_general(q_ref[...].astype(MXU), kb, NT, preferred_element_type=F32) * scale
            if cum is not None:
                s = s + (refs[6][...] - refs[7][...])
            s = _causal(s, i, j, t)
            p = jnp.exp(s - lse_ref[...])
            dp = lax.dot_general(do_ref[...].astype(MXU), v_ref[...].astype(MXU), NT, preferred_element_type=F32)
            ds = p * (dp - dl_ref[...])
            accs[0][...] += jnp.dot(ds.astype(MXU), kb, preferred_element_type=F32)
            if cum is not None:
                accs[1][...] += jnp.sum(ds, axis=1, keepdims=True)

        @pl.when(j == nb - 1)
        def _():
            outs[0][...] = accs[0][...] * scale
            if cum is not None:
                outs[1][...] = accs[1][...]

    out_specs = [pl.BlockSpec((None, t, wq), lambda h, i, j: (h, i, 0))]
    out_shape = [jax.ShapeDtypeStruct((n_heads, seq, wq), F32)]
    scratch = [pltpu.VMEM((t, wq), F32)]
    if cum is not None:
        out_specs.append(_colspec(t, qs))
        out_shape.append(jax.ShapeDtypeStruct((n_heads, seq, 1), F32))
        scratch.append(pltpu.VMEM((t, 1), F32))
    res = pl.pallas_call(
        body, name=name, grid=(n_heads, nb, nb), in_specs=specs, out_specs=out_specs, out_shape=out_shape,
        scratch_shapes=scratch, compiler_params=_cparams(3),
    )(*ins)
    return res[0] if cum is None else res


def flash_bwd_dkv(q, k, v, do, lse_row, delta_row, cum, *, n_heads, seq, t, scale, name):
    nb = seq // t
    ks, qs = (lambda j, i: j), (lambda j, i: jnp.maximum(i, j))
    ins = [q[0], k[0], v[0], do, lse_row, delta_row]
    specs = [_hspec(q[1], t, q[2], q[3], qs), _hspec(k[1], t, k[2], k[3], ks), _hspec(v[1], t, v[2], v[3], ks),
             _hspec("cols", t, HEAD, 0, qs), _rowspec(t, qs), _rowspec(t, qs)]
    if cum is not None:
        ins += [cum[0], cum[1]]
        specs += [_colspec(t, ks), _rowspec(t, qs)]
    wk = k[2]
    n_out = 2 if cum is None else 3

    def body(*refs):
        q_ref, k_ref, v_ref, do_ref, lse_ref, dl_ref = refs[:6]
        outs = refs[-2 * n_out:-n_out]
        accs = refs[-n_out:]
        j, i = pl.program_id(1), pl.program_id(2)

        @pl.when(i == 0)
        def _():
            for a in accs:
                a[...] = jnp.zeros(a.shape, F32)

        @pl.when(i >= j)
        def _():
            qb = q_ref[...].astype(MXU)
            dob = do_ref[...].astype(MXU)
            st = lax.dot_general(k_ref[...].astype(MXU), qb, NT, preferred_element_type=F32) * scale
            if cum is not None:
                st = st + (refs[7][...] - refs[6][...])
            st = _causal(st, i, j, t, transposed=True)
            pt = jnp.exp(st - lse_ref[...])
            dpt = lax.dot_general(v_ref[...].astype(MXU), dob, NT, preferred_element_type=F32)
            dst = pt * (dpt - dl_ref[...])
            accs[0][...] += jnp.dot(dst.astype(MXU), qb, preferred_element_type=F32)
            accs[1][...] += jnp.dot(pt.astype(MXU), dob, preferred_element_type=F32)
            if cum is not None:
                accs[2][...] -= jnp.sum(dst, axis=1, keepdims=True)

        @pl.when(i == nb - 1)
        def _():
            outs[0][...] = accs[0][...] * scale
            for o, a in zip(outs[1:], accs[1:]):
                o[...] = a[...]

    out_specs = [pl.BlockSpec((None, t, wk), lambda h, j, i: (h, j, 0)), pl.BlockSpec((None, t, HEAD), lambda h, j, i: (h, j, 0))]
    out_shape = [jax.ShapeDtypeStruct((n_heads, seq, wk), F32), jax.ShapeDtypeStruct((n_heads, seq, HEAD), F32)]
    scratch = [pltpu.VMEM((t, wk), F32), pltpu.VMEM((t, HEAD), F32)]
    if cum is not None:
        out_specs.append(_colspec(t, ks))
        out_shape.append(jax.ShapeDtypeStruct((n_heads, seq, 1), F32))
        scratch.append(pltpu.VMEM((t, 1), F32))
    return pl.pallas_call(
        body, name=name, grid=(n_heads, nb, nb), in_specs=specs, out_specs=out_specs, out_shape=out_shape,
        scratch_shapes=scratch, compiler_params=_cparams(3),
    )(*ins)


def attn_delta(do, o, *, n_heads, seq, name):
    t = _tile(seq, 512)
    spec = pl.BlockSpec((t, HEAD), lambda h, m: (m, h))
    return rowwise(
        lambda a, b: jnp.sum(a.astype(F32) * b.astype(F32), axis=1, keepdims=True), [do, o], [spec, spec],
        jax.ShapeDtypeStruct((n_heads, seq, 1), F32), pl.BlockSpec((None, t, 1), lambda h, m: (h, m, 0)),
        (n_heads, seq // t), name)


def _dil_scores(q, kc, kp, n, scale):
    i = lax.broadcasted_iota(jnp.int32, (HEAD, HEAD), 0)
    j = lax.broadcasted_iota(jnp.int32, (HEAD, HEAD), 1)
    sc = lax.dot_general(q, kc, NT, preferred_element_type=F32) * scale
    sp = lax.dot_general(q, kp, NT, preferred_element_type=F32) * scale
    sc = jnp.where(j <= i, sc, NEG)
    sp = jnp.where(jnp.logical_and(j >= i, n > 0), sp, NEG)
    return sc, sp


def _strip_spec(length, n_heads, col_blocks, off):
    return pl.BlockSpec((length, HEAD), lambda r, h: (0, r * col_blocks + off + h))


def dil_fwd(q, k, v, *, seq, dil, n_heads, name):
    length = seq // dil
    nb = length // HEAD
    scale = HEAD ** -0.5
    view = lambda a: a.reshape(length, dil * a.shape[1])
    spec = _strip_spec(length, n_heads, n_heads, 0)

    def body(q_ref, k_ref, v_ref, o_ref, lse_ref):
        def step(n, carry):
            cur = pl.ds(pl.multiple_of(n * HEAD, HEAD), HEAD)
            prev = pl.ds(pl.multiple_of(jnp.maximum(n - 1, 0) * HEAD, HEAD), HEAD)
            qb = q_ref[cur, :].astype(MXU)
            sc, sp = _dil_scores(qb, k_ref[cur, :].astype(MXU), k_ref[prev, :].astype(MXU), n, scale)
            m = jnp.maximum(jnp.max(sc, axis=1, keepdims=True), jnp.max(sp, axis=1, keepdims=True))
            ec, ep = jnp.exp(sc - m), jnp.exp(sp - m)
            l = jnp.sum(ec, axis=1, keepdims=True) + jnp.sum(ep, axis=1, keepdims=True)
            o = jnp.dot((ec / l).astype(MXU), v_ref[cur, :].astype(MXU), preferred_element_type=F32)
            o = o + jnp.dot((ep / l).astype(MXU), v_ref[prev, :].astype(MXU), preferred_element_type=F32)
            o_ref[cur, :] = o
            lse_ref[cur, :] = jnp.broadcast_to(m + jnp.log(l), (HEAD, HEAD))
            return carry

        lax.fori_loop(0, nb, step, 0)

    out = jax.ShapeDtypeStruct((length, dil * n_heads * HEAD), F32)
    o, lse = pl.pallas_call(
        body, name=name, grid=(dil, n_heads), in_specs=[spec, spec, spec], out_specs=[spec, spec], out_shape=[out, out],
        compiler_params=_cparams(2),
    )(view(q), view(k), view(v))
    return o.reshape(seq, -1), lse.reshape(seq, -1)


def dil_bwd(q, k, v, o, lse, do, dlse, *, seq, dil, n_heads, name):
    length = seq // dil
    nb = length // HEAD
    scale = HEAD ** -0.5
    view = lambda a: a.reshape(length, dil * a.shape[1])
    spec = _strip_spec(length, n_heads, n_heads, 0)

    def body(q_ref, k_ref, v_ref, o_ref, lse_ref, do_ref, dlse_ref, dq_ref, dk_ref, dv_ref):
        dk_ref[...] = jnp.zeros(dk_ref.shape, F32)
        dv_ref[...] = jnp.zeros(dv_ref.shape, F32)

        def step(n, carry):
            cur = pl.ds(pl.multiple_of(n * HEAD, HEAD), HEAD)
            prev = pl.ds(pl.multiple_of(jnp.maximum(n - 1, 0) * HEAD, HEAD), HEAD)
            qb = q_ref[cur, :].astype(MXU)
            kc, kp = k_ref[cur, :].astype(MXU), k_ref[prev, :].astype(MXU)
            vc, vp = v_ref[cur, :].astype(MXU), v_ref[prev, :].astype(MXU)
            sc, sp = _dil_scores(qb, kc, kp, n, scale)
            lse_b = jnp.max(lse_ref[cur, :], axis=1, keepdims=True)
            pc, pp = jnp.exp(sc - lse_b), jnp.exp(sp - lse_b)
            dob = do_ref[cur, :]
            shift = jnp.sum(dlse_ref[cur, :], axis=1, keepdims=True) - jnp.sum(dob * o_ref[cur, :], axis=1, keepdims=True)
            dob = dob.astype(MXU)
            dsc = pc * (lax.dot_general(dob, vc, NT, preferred_element_type=F32) + shift)
            dsp = pp * (lax.dot_general(dob, vp, NT, preferred_element_type=F32) + shift)
            dscb, dspb = dsc.astype(MXU), dsp.astype(MXU)
            dq = jnp.dot(dscb, kc, preferred_element_type=F32) + jnp.dot(dspb, kp, preferred_element_type=F32)
            dq_ref[cur, :] = dq * scale
            dk_ref[cur, :] += lax.dot_general(dscb, qb, TN, preferred_element_type=F32) * scale
            dv_ref[cur, :] += lax.dot_general(pc.astype(MXU), dob, TN, preferred_element_type=F32)
            dk_ref[prev, :] += lax.dot_general(dspb, qb, TN, preferred_element_type=F32) * scale
            dv_ref[prev, :] += lax.dot_general(pp.astype(MXU), dob, TN, preferred_element_type=F32)
            return carry

        lax.fori_loop(0, nb, step, 0)

    out = jax.ShapeDtypeStruct((length, dil * n_heads * HEAD), F32)
    res = pl.pallas_call(
        body, name=name, grid=(dil, n_heads), in_specs=[spec] * 7, out_specs=[spec] * 3, out_shape=[out] * 3,
        compiler_params=_cparams(2),
    )(*[view(a) for a in (q, k, v, o, lse, do, dlse)])
    return [r.reshape(seq, -1) for r in res]


def _tri(n, kind):
    i = lax.broadcasted_iota(jnp.int32, (n, n), 0)
    j = lax.broadcasted_iota(jnp.int32, (n, n), 1)
    return jnp.where({"le": i <= j, "ge": i >= j}[kind], 1.0, 0.0).astype(F32)


def _block_matrix(n_rows, per_head, kind):
    r = lax.broadcasted_iota(jnp.int32, (n_rows, n_rows), 0)
    c = lax.broadcasted_iota(jnp.int32, (n_rows, n_rows), 1)
    same = (r // per_head) == (c // per_head)
    rel = {"lt": c < r, "gt": c > r, "all": c == c}[kind]
    return jnp.where(jnp.logical_and(same, rel), 1.0, 0.0).astype(F32)


def _lane_pick(x, lane):
    j = lax.broadcasted_iota(jnp.int32, x.shape, 1)
    return jnp.sum(jnp.where(j == lane, x, 0.0), axis=1, keepdims=True)


def _log_sigmoid(z):
    return jnp.minimum(z, 0.0) - jnp.log1p(jnp.exp(-jnp.abs(z)))


def fox_gate_fwd(z, bias_rows, per_head, name):
    n_rows = z.shape[0]

    def body(z_ref, b_ref, c_ref):
        logf = _log_sigmoid(z_ref[...] + b_ref[...])
        within = jnp.dot(logf, _tri(HEAD, "le"), precision=HI, preferred_element_type=F32)
        tot = jnp.broadcast_to(_lane_pick(within, HEAD - 1), (n_rows, HEAD))
        c_ref[...] = within + jnp.dot(_block_matrix(n_rows, per_head, "lt"), tot, precision=HI, preferred_element_type=F32)

    return pl.pallas_call(body, name=name, out_shape=jax.ShapeDtypeStruct(z.shape, F32),
                          compiler_params=pltpu.CompilerParams(vmem_limit_bytes=VMEM_LIMIT))(z, bias_rows)


def fox_gate_bwd(z, bias_rows, dcum_q, dcum_k, per_head, name):
    n_rows = z.shape[0]

    def body(z_ref, b_ref, dcq_ref, dck_ref, dz_ref, db_ref):
        within = jnp.dot(dcq_ref[...] + dck_ref[...], _tri(HEAD, "ge"), precision=HI, preferred_element_type=F32)
        tot = jnp.broadcast_to(_lane_pick(within, 0), (n_rows, HEAD))
        dlogf = within + jnp.dot(_block_matrix(n_rows, per_head, "gt"), tot, precision=HI, preferred_element_type=F32)
        dz = dlogf * jax.nn.sigmoid(-(z_ref[...] + b_ref[...]))
        dz_ref[...] = dz
        rs = jnp.broadcast_to(jnp.sum(dz, axis=1, keepdims=True), (n_rows, HEAD))
        db_ref[...] = jnp.dot(_block_matrix(n_rows, per_head, "all"), rs, precision=HI, preferred_element_type=F32)

    shp = jax.ShapeDtypeStruct(z.shape, F32)
    return pl.pallas_call(body, name=name, out_shape=[shp, shp],
                          compiler_params=pltpu.CompilerParams(vmem_limit_bytes=VMEM_LIMIT))(z, bias_rows, dcum_q, dcum_k)


def exchange(x, scatter, name):
    blk = x.shape[1:] if scatter else x.shape

    def body(x_ref, o_ref, send_sems, recv_sems, local_sem):
        mx, my, mc = lax.axis_index("x"), lax.axis_index("y"), lax.axis_index("c")
        me = 4 * mx + 2 * my + mc
        flip = lambda v, f: 1 - v if f else v
        local = pltpu.make_async_copy(x_ref.at[me] if scatter else x_ref, o_ref.at[me], local_sem)
        local.start()
        sends, recvs = [], []
        for n in range(1, NDEV):
            px, py, pc = flip(mx, n & 4), flip(my, n & 2), flip(mc, n & 1)
            p = 4 * px + 2 * py + pc
            sends.append(pltpu.make_async_remote_copy(
                src_ref=x_ref.at[p] if scatter else x_ref, dst_ref=o_ref.at[me], send_sem=send_sems.at[n - 1],
                recv_sem=recv_sems.at[n - 1], device_id=(px, py, pc), device_id_type=pl.DeviceIdType.MESH))
            recvs.append(pltpu.make_async_remote_copy(
                src_ref=x_ref.at[me] if scatter else x_ref, dst_ref=o_ref.at[p], send_sem=send_sems.at[n - 1],
                recv_sem=recv_sems.at[n - 1], device_id=(px, py, pc), device_id_type=pl.DeviceIdType.MESH))
        for cp in sends:
            cp.start()
        for cp in recvs:
            cp.wait_recv()
        for cp in sends:
            cp.wait_send()
        local.wait()

    hbm = pl.BlockSpec(memory_space=pltpu.HBM)
    return pl.pallas_call(
        body, name=name, in_specs=[hbm], out_specs=hbm, out_shape=jax.ShapeDtypeStruct((NDEV, *blk), x.dtype),
        scratch_shapes=[pltpu.SemaphoreType.DMA((NDEV - 1,)), pltpu.SemaphoreType.DMA((NDEV - 1,)), pltpu.SemaphoreType.DMA],
    )(x)


def adamw(parts, w, m, v, name):
    n_parts, rows, cols = parts.shape
    t = rows
    for cand in (256, 128, 64, 32, 16, 8):
        if rows % cand == 0 and (n_parts * parts.dtype.itemsize + 7 * 4) * cand * cols <= ADAMW_STEP_BYTES:
            t = cand
            break

    def fn(p, w_, m_, v_):
        g = p[0].astype(F32)
        for i in range(1, n_parts):
            g = g + p[i].astype(F32)
        m_new = ADAM_B1 * m_ + (1.0 - ADAM_B1) * g
        v_new = ADAM_B2 * v_ + (1.0 - ADAM_B2) * jnp.square(g)
        m_hat = m_new / (1.0 - ADAM_B1 ** ADAM_STEP)
        v_hat = v_new / (1.0 - ADAM_B2 ** ADAM_STEP)
        delta = -ADAM_LR * (m_hat / (jnp.sqrt(v_hat) + ADAM_EPS) + ADAM_WD * w_)
        return g, delta, m_new, v_new

    spec = pl.BlockSpec((t, cols), lambda r: (r, 0))
    out = jax.ShapeDtypeStruct((rows, cols), F32)
    return rowwise(fn, [parts, w, m, v], [pl.BlockSpec((n_parts, t, cols), lambda r: (0, r, 0)), spec, spec, spec],
                   [out] * 4, [spec] * 4, (rows // t,), name)


def sum_parts(parts, name):
    n_parts, rows, cols = parts.shape
    t = _tile(rows, 128)

    def fn(p):
        g = p[0].astype(F32)
        for i in range(1, n_parts):
            g = g + p[i].astype(F32)
        return g

    return rowwise(fn, [parts], [pl.BlockSpec((n_parts, t, cols), lambda r: (0, r, 0))],
                   jax.ShapeDtypeStruct((rows, cols), F32), pl.BlockSpec((t, cols), lambda r: (r, 0)), (rows // t,), name)


def _sd(shape, dtype=F32):
    return jax.ShapeDtypeStruct(shape, dtype)


def rms_fwd(x, g, name, col=0, width=None):
    seq = x.shape[0]
    width = width or x.shape[1]
    t = _tile(seq, 512)
    return rowwise(_rms, [x, g.reshape(1, width)], [_rows(t, width, col), _whole((1, width))], _sd((seq, width), ACT),
                   _rows(t, width), (seq // t,), name)


def rms_bwd(x, g, dy, name, col=0, width=None, add=None, dx_dtype=F32):
    seq = x.shape[0]
    width = width or x.shape[1]
    t = _tile(seq, 512)
    ins, specs = [x, g.reshape(1, width), dy], [_rows(t, width, col), _whole((1, width)), _rows(t, width)]
    if add is None:
        fn = _rms_bwd
    else:
        ins.append(add)
        specs.append(_rows(t, width))

        def fn(x_, g_, dy_, add_):
            dx, dg = _rms_bwd(x_, g_, dy_)
            return dx + add_, dg
    return rowwise(fn, ins, specs, [_sd((seq, width), dx_dtype), _sd((1, width))], [_rows(t, width), _whole((1, width))],
                   (seq // t,), name, acc={1: (0,)})


def ffn_fwd(x, g, wg, wu, wd, l, tag):
    seq, d = x.shape
    f = wg.shape[-1]
    tm = _tile(seq, 1024)
    h = rms_fwd(x, g, f"{tag}_rms")
    up = lambda w, nm: mm(
        h, w, name=nm, dims=NN, grid=(NDEV, seq // tm, 1), nk=1,
        a_spec=pl.BlockSpec((tm, d), lambda j, m, k: (m, 0)),
        b_spec=pl.BlockSpec((None, None, d, f), lambda j, m, k: (j, l, 0, 0)),
        o_spec=pl.BlockSpec((None, tm, f), lambda j, m, k: (j, m, 0)), out_shape=_sd((NDEV, seq, f), ACT))
    a, b = up(wg, f"{tag}_gate"), up(wu, f"{tag}_up")
    t = _tile(seq, 512)
    spec3 = pl.BlockSpec((None, t, f), lambda j, m: (j, m, 0))
    hid = rowwise(lambda a_, b_: jax.nn.silu(a_.astype(F32)) * b_.astype(F32), [a, b], [spec3, spec3],
                  _sd((NDEV, seq, f), ACT), spec3, (NDEV, seq // t), f"{tag}_act")
    tn = _tile(d, 1024)
    out = mm(hid, wd, name=f"{tag}_down", dims=NN, grid=(seq // tm, d // tn, NDEV), nk=NDEV,
             a_spec=pl.BlockSpec((None, tm, f), lambda m, n, k: (k, m, 0)),
             b_spec=pl.BlockSpec((None, None, f, tn), lambda m, n, k: (k, l, 0, n)),
             o_spec=pl.BlockSpec((tm, tn), lambda m, n, k: (m, n)), out_shape=_sd((seq, d)),
             extras=[x], extra_specs=[pl.BlockSpec((tm, tn), lambda m, n, k: (m, n))],
             epilogue=lambda acc, x_: x_ + 0.5 * acc)
    return out, (x, h, a, b, hid)


def ffn_bwd(dout, saved, g, wg, wu, wd, l, tag):
    x, h, a, b, hid = saved
    seq, d = x.shape
    f = wg.shape[-1]
    tm = _tile(seq, 1024)
    tk = _tile(seq, 1024)
    dhid = mm(dout, wd, name=f"{tag}_dhid", dims=NT, grid=(NDEV, seq // tm, 1), nk=1,
              a_spec=pl.BlockSpec((tm, d), lambda j, m, k: (m, 0)),
              b_spec=pl.BlockSpec((None, None, f, d), lambda j, m, k: (j, l, 0, 0)),
              o_spec=pl.BlockSpec((None, tm, f), lambda j, m, k: (j, m, 0)), out_shape=_sd((NDEV, seq, f), ACT),
              epilogue=lambda acc: 0.5 * acc)
    tn = _tile(d, 1024)
    dwd = mm(hid, dout, name=f"{tag}_dwd", dims=TN, grid=(NDEV, d // tn, seq // tk), nk=seq // tk,
             a_spec=pl.BlockSpec((None, tk, f), lambda j, n, k: (j, k, 0)),
             b_spec=pl.BlockSpec((tk, tn), lambda j, n, k: (k, n)),
             o_spec=pl.BlockSpec((None, f, tn), lambda j, n, k: (j, 0, n)), out_shape=_sd((NDEV, f, d), COMM),
             epilogue=lambda acc: 0.5 * acc)
    t = _tile(seq, 512)
    spec3 = pl.BlockSpec((None, t, f), lambda j, m: (j, m, 0))

    def act_bwd(dh_, a_, b_):
        dh_, a_, b_ = dh_.astype(F32), a_.astype(F32), b_.astype(F32)
        sig = jax.nn.sigmoid(a_)
        return dh_ * b_ * sig * (1.0 + a_ * (1.0 - sig)), dh_ * a_ * sig

    da, db = rowwise(act_bwd, [dhid, a, b], [spec3] * 3, [_sd((NDEV, seq, f), ACT)] * 2, [spec3] * 2, (NDEV, seq // t),
                     f"{tag}_dact")
    dw = lambda dz, nm: mm(
        h, dz, name=nm, dims=TN, grid=(NDEV, 1, seq // tk), nk=seq // tk,
        a_spec=pl.BlockSpec((tk, d), lambda j, n, k: (k, 0)),
        b_spec=pl.BlockSpec((None, tk, f), lambda j, n, k: (j, k, 0)),
        o_spec=pl.BlockSpec((None, d, f), lambda j, n, k: (j, 0, 0)), out_shape=_sd((NDEV, d, f), COMM))
    dwg, dwu = dw(da, f"{tag}_dwg"), dw(db, f"{tag}_dwu")
    dh_of = lambda dz, w, nm, extras, epi: mm(
        dz, w, name=nm, dims=NT, grid=(seq // tm, d // tn, NDEV), nk=NDEV,
        a_spec=pl.BlockSpec((None, tm, f), lambda m, n, k: (k, m, 0)),
        b_spec=pl.BlockSpec((None, None, tn, f), lambda m, n, k: (k, l, n, 0)),
        o_spec=pl.BlockSpec((tm, tn), lambda m, n, k: (m, n)), out_shape=_sd((seq, d)),
        extras=extras, extra_specs=[pl.BlockSpec((tm, tn), lambda m, n, k: (m, n))] * len(extras), epilogue=epi)
    dh = dh_of(da, wg, f"{tag}_dh_gate", [], None)
    dh = dh_of(db, wu, f"{tag}_dh_up", [dh], lambda acc, prev: acc + prev)
    dx, dg = rms_bwd(x, g, dh, f"{tag}_drms", add=dout)
    return dx, (dg, dwg, dwu, dwd)


def _dense(a, w, l, name, out_dtype=F32, extras=(), epilogue=None, row0=0):
    seq, kdim = a.shape
    kb, n = w.shape[2], w.shape[3]
    nk = kdim // kb
    tm, tn = _tile(seq, 1024), _tile(n, 1024)
    return mm(a, w, name=name, dims=NN, grid=(seq // tm, n // tn, nk), nk=nk,
              a_spec=pl.BlockSpec((tm, kb), lambda m, c, k: (m, k)),
              b_spec=pl.BlockSpec((None, None, kb, tn), lambda m, c, k: (k + row0, l, 0, c)),
              o_spec=pl.BlockSpec((tm, tn), lambda m, c, k: (m, c)), out_shape=_sd((seq, n), out_dtype),
              extras=list(extras), extra_specs=[pl.BlockSpec((tm, tn), lambda m, c, k: (m, c))] * len(extras),
              epilogue=epilogue)


def _dense_dx(dy, w, l, name, extras=(), epilogue=None):
    seq, n = dy.shape
    kb = w.shape[2]
    tm = _tile(seq, 1024)
    return mm(dy, w, name=name, dims=NT, grid=(seq // tm, NDEV, 1), nk=1,
              a_spec=pl.BlockSpec((tm, n), lambda m, j, k: (m, 0)),
              b_spec=pl.BlockSpec((None, None, kb, n), lambda m, j, k: (j, l, 0, 0)),
              o_spec=pl.BlockSpec((tm, kb), lambda m, j, k: (m, j)), out_shape=_sd((seq, NDEV * kb)),
              extras=list(extras), extra_specs=[pl.BlockSpec((tm, kb), lambda m, j, k: (m, j))] * len(extras),
              epilogue=epilogue)


def _dense_dw(a, dy, kb, name):
    seq, n = dy.shape
    tk, tn = _tile(seq, 1024), _tile(n, 1024)
    return mm(a, dy, name=name, dims=TN, grid=(NDEV, n // tn, seq // tk), nk=seq // tk,
              a_spec=pl.BlockSpec((tk, kb), lambda j, c, k: (k, j)),
              b_spec=pl.BlockSpec((tk, tn), lambda j, c, k: (k, c)),
              o_spec=pl.BlockSpec((None, kb, tn), lambda j, c, k: (j, 0, c)), out_shape=_sd((NDEV, kb, n), COMM))


def _heads_up(a, w, l, name, out_dtype):
    seq, r = a.shape
    c = w.shape[3]
    tm = _tile(seq, 1024)
    return mm(a, w, name=name, dims=NN, grid=(NDEV, seq // tm, 1), nk=1,
              a_spec=pl.BlockSpec((tm, r), lambda j, m, k: (m, 0)),
              b_spec=pl.BlockSpec((None, None, r, c), lambda j, m, k: (j, l, 0, 0)),
              o_spec=pl.BlockSpec((None, tm, c), lambda j, m, k: (j, m, 0)), out_shape=_sd((NDEV, seq, c), out_dtype))


def _heads_dx(dy, w, l, name):
    _, seq, c = dy.shape
    r = w.shape[2]
    tm = _tile(seq, 1024)
    return mm(dy, w, name=name, dims=NT, grid=(seq // tm, 1, NDEV), nk=NDEV,
              a_spec=pl.BlockSpec((None, tm, c), lambda m, n, k: (k, m, 0)),
              b_spec=pl.BlockSpec((None, None, r, c), lambda m, n, k: (k, l, 0, 0)),
              o_spec=pl.BlockSpec((tm, r), lambda m, n, k: (m, 0)), out_shape=_sd((seq, r)))


def _heads_dw(a, dy, name):
    seq, r = a.shape
    c = dy.shape[2]
    tk = _tile(seq, 1024)
    return mm(a, dy, name=name, dims=TN, grid=(NDEV, 1, seq // tk), nk=seq // tk,
              a_spec=pl.BlockSpec((tk, r), lambda j, n, k: (k, 0)),
              b_spec=pl.BlockSpec((None, tk, c), lambda j, n, k: (j, k, 0)),
              o_spec=pl.BlockSpec((None, r, c), lambda j, n, k: (j, 0, 0)), out_shape=_sd((NDEV, r, c), COMM))


C_FQ, C_FK, C_FV, C_CQ, C_CKV, C_DQ, C_DK, C_DV = range(8)
MAIN_W = 8 * 512
TAIL_W = 128


def split_w_in(w):
    fq, fk, fv, fl, cq, ckv, kr, dq, dk, dv = jnp.split(w, [512, 1024, 1536, 1540, 2052, 2564, 2628, 3140, 3652], axis=-1)
    main = jnp.concatenate([fq, fk, fv, cq, ckv, dq, dk, dv], axis=-1)
    pad = jnp.zeros((*w.shape[:-1], TAIL_W - 68), w.dtype)
    return main, jnp.concatenate([kr, fl, pad], axis=-1)


def merge_w_in(main, tail):
    fq, fk, fv, cq, ckv, dq, dk, dv = jnp.split(main, 8, axis=-1)
    return jnp.concatenate([fq, fk, fv, tail[..., 64:68], cq, ckv, tail[..., 0:64], dq, dk, dv], axis=-1)


def mixer_fwd(x, p, l, consts):
    seq, d = x.shape
    nfox, nmla, ndil = 4, 8, 4
    cos_m, sin_m, cos_p, sin_p = consts
    t = _tile(seq, 512)
    tf = _tile(seq, 512)
    h = rms_fwd(x, p["mix_norm"][l], f"mix{l}_rms")
    proj = _dense(h, p["w_in_main"], l, f"mix{l}_proj")
    tail = _dense(h, p["w_in_tail"], l, f"mix{l}_tail")

    nb = seq // HEAD
    z = tail[:, 64:68].T.reshape(nfox * nb, HEAD)
    bias_rows = jnp.repeat(p["fox_forget_bias"][l], nb).reshape(nfox * nb, 1)
    cum = fox_gate_fwd(z, bias_rows, nb, f"mix{l}_gate").reshape(nfox, seq)
    cum2 = (cum.reshape(nfox, seq, 1), cum.reshape(nfox, 1, seq))
    fox_qkv = ((proj, "cols", HEAD, C_FQ * 4), (proj, "cols", HEAD, C_FK * 4), (proj, "cols", HEAD, C_FV * 4))
    out_a, lse_a = flash_fwd(*fox_qkv, cum2, n_heads=nfox, seq=seq, t=tf, scale=HEAD ** -0.5, name=f"mix{l}_fox")

    cq = rms_fwd(proj, p["mla_q_norm"][l], f"mix{l}_cq", col=C_CQ, width=512)
    ckv = rms_fwd(proj, p["mla_kv_norm"][l], f"mix{l}_ckv", col=C_CKV, width=512)
    q_raw = _heads_up(cq, p["mla_w_uq"], l, f"mix{l}_uq", F32)
    kv = _heads_up(ckv, p["mla_w_ukv"], l, f"mix{l}_ukv", ACT)

    def mla_prep(q_, kv_, tail_, cos_, sin_, q_out, k_out):
        perm = _swap_matrix(MLA_ROPE)
        c, s = cos_[...], sin_[...]
        q_out[:, 0:HEAD] = q_[:, 0:HEAD].astype(q_out.dtype)
        q_out[:, HEAD:MLA_QK] = _rope(q_[:, HEAD:MLA_QK], c, s, perm).astype(q_out.dtype)
        k_out[:, 0:HEAD] = kv_[:, 0:HEAD].astype(k_out.dtype)
        k_out[:, HEAD:MLA_QK] = _rope(tail_[:, 0:MLA_ROPE], c, s, perm).astype(k_out.dtype)

    hs = lambda w: pl.BlockSpec((None, t, w), lambda hh, m: (hh, m, 0))
    rs = lambda w: pl.BlockSpec((t, w), lambda hh, m: (m, 0))
    q_b, k_b = pl.pallas_call(
        lambda q_, kv_, tl_, c_, s_, qo, ko: mla_prep(q_[...], kv_[...], tl_[...], c_, s_, qo, ko),
        name=f"mix{l}_mla_prep", grid=(nmla, seq // t),
        in_specs=[hs(MLA_QK), hs(2 * HEAD), rs(TAIL_W), rs(MLA_ROPE), rs(MLA_ROPE)], out_specs=[hs(MLA_QK), hs(MLA_QK)],
        out_shape=[_sd((nmla, seq, MLA_QK), ACT)] * 2, compiler_params=_cparams(2),
    )(q_raw, kv, tail, cos_m, sin_m)
    mla_qkv = ((q_b, "heads", MLA_QK, 0), (k_b, "heads", MLA_QK, 0), (kv, "heads", HEAD, 1))
    out_b, lse_b = flash_fwd(*mla_qkv, None, n_heads=nmla, seq=seq, t=tf, scale=MLA_QK ** -0.5, name=f"mix{l}_mla")

    wd_ = ndil * HEAD

    def dil_prep(q_, k_, c_, s_):
        perm = _pad_perm(PARTIAL_ROPE)
        rot = lambda a: jnp.concatenate(
            [_rope(a[:, i * HEAD:(i + 1) * HEAD], c_, s_, perm) for i in range(ndil)], axis=1)
        return rot(q_), rot(k_)

    dq_r, dk_r = rowwise(dil_prep, [proj, proj, cos_p, sin_p],
                         [_rows(t, wd_, C_DQ), _rows(t, wd_, C_DK), _rows(t, HEAD), _rows(t, HEAD)],
                         [_sd((seq, wd_), ACT)] * 2, [_rows(t, wd_)] * 2, (seq // t,), f"mix{l}_dil_prep")
    dv = proj[:, C_DV * 512:(C_DV + 1) * 512]
    branches = [dil_fwd(dq_r, dk_r, dv, seq=seq, dil=dl, n_heads=ndil, name=f"mix{l}_dil{dl}") for dl in DIL_BRANCHES]

    def mix(o1, o2, o3, l1, l2, l3):
        m = jnp.maximum(jnp.maximum(l1, l2), l3)
        e1, e2, e3 = jnp.exp(l1 - m), jnp.exp(l2 - m), jnp.exp(l3 - m)
        return (e1 * o1 + e2 * o2 + e3 * o3) / (e1 + e2 + e3)

    out_c = rowwise(mix, [b[0] for b in branches] + [b[1] for b in branches], [_rows(t, wd_)] * 6, _sd((seq, wd_)),
                    _rows(t, wd_), (seq // t,), f"mix{l}_dil_mix")

    mixed = jnp.concatenate([out_a, out_b, out_c], axis=1)
    out = _dense(mixed, p["w_out"], l, f"mix{l}_out", extras=[x], epilogue=lambda acc, x_: x_ + acc)
    saved = dict(x=x, h=h, proj=proj, tail=tail, z=z, bias_rows=bias_rows, cum2=cum2, out_a=out_a, lse_a=lse_a, cq=cq,
                 ckv=ckv, q_raw=q_raw, kv=kv, q_b=q_b, k_b=k_b, out_b=out_b, lse_b=lse_b, dq_r=dq_r, dk_r=dk_r, dv=dv,
                 branches=branches, out_c=out_c, mixed=mixed)
    return out, saved


def _pad_perm(n):
    i = lax.broadcasted_iota(jnp.int32, (HEAD, HEAD), 0)
    j = lax.broadcasted_iota(jnp.int32, (HEAD, HEAD), 1)
    inside = jnp.logical_and(i < n, j < n)
    return jnp.where(jnp.logical_and(inside, ((i + n // 2) % n) == j), 1.0, 0.0).astype(F32)


def mixer_bwd(dout, sv, p, l, consts):
    seq, d = dout.shape
    nfox, nmla, ndil = 4, 8, 4
    cos_m, sin_m, cos_p, sin_p = consts
    t = _tile(seq, 512)
    tf = _tile(seq, 512)
    nb = seq // HEAD
    proj, tail = sv["proj"], sv["tail"]
    dmixed = _dense_dx(dout, p["w_out"], l, f"mix{l}_dmixed")
    dw_out = _dense_dw(sv["mixed"], dout, d // NDEV, f"mix{l}_dw_out")
    do_a, do_b, do_c = dmixed[:, 0:512], dmixed[:, 512:1536], dmixed[:, 1536:2048]

    fox_qkv = ((proj, "cols", HEAD, C_FQ * 4), (proj, "cols", HEAD, C_FK * 4), (proj, "cols", HEAD, C_FV * 4))
    delta_a = attn_delta(do_a, sv["out_a"], n_heads=nfox, seq=seq, name=f"mix{l}_fox_delta")
    row = lambda a: a.reshape(a.shape[0], 1, seq)
    dfq, dcum_q = flash_bwd_dq(*fox_qkv, do_a, sv["lse_a"], delta_a, sv["cum2"], n_heads=nfox, seq=seq, t=tf,
                               scale=HEAD ** -0.5, name=f"mix{l}_fox_dq")
    dfk, dfv, dcum_k = flash_bwd_dkv(*fox_qkv, do_a, row(sv["lse_a"]), row(delta_a), sv["cum2"], n_heads=nfox, seq=seq,
                                     t=tf, scale=HEAD ** -0.5, name=f"mix{l}_fox_dkv")
    dz, dbias = fox_gate_bwd(sv["z"], sv["bias_rows"], dcum_q.reshape(nfox * nb, HEAD), dcum_k.reshape(nfox * nb, HEAD),
                             nb, f"mix{l}_dgate")
    d_fox_bias = dbias.reshape(nfox, nb, HEAD)[:, 0, 0]
    dfl = dz.reshape(nfox, seq).T
    unheads = lambda a: a.transpose(1, 0, 2).reshape(seq, -1)

    mla_qkv = ((sv["q_b"], "heads", MLA_QK, 0), (sv["k_b"], "heads", MLA_QK, 0), (sv["kv"], "heads", HEAD, 1))
    delta_b = attn_delta(do_b, sv["out_b"], n_heads=nmla, seq=seq, name=f"mix{l}_mla_delta")
    dq_b = flash_bwd_dq(*mla_qkv, do_b, sv["lse_b"], delta_b, None, n_heads=nmla, seq=seq, t=tf, scale=MLA_QK ** -0.5,
                        name=f"mix{l}_mla_dq")
    dk_b, dv_b = flash_bwd_dkv(*mla_qkv, do_b, row(sv["lse_b"]), row(delta_b), None, n_heads=nmla, seq=seq, t=tf,
                               scale=MLA_QK ** -0.5, name=f"mix{l}_mla_dkv")

    def mla_unprep(dq_, dk_, dv_, cos_, sin_, dq_out, dkv_out, dkr_out):
        perm = _swap_matrix(MLA_ROPE)
        c, s = cos_[...], sin_[...]
        dq_out[:, 0:HEAD] = dq_[:, 0:HEAD].astype(dq_out.dtype)
        dq_out[:, HEAD:MLA_QK] = _rope_t(dq_[:, HEAD:MLA_QK], c, s, perm).astype(dq_out.dtype)
        dkv_out[:, 0:HEAD] = dk_[:, 0:HEAD].astype(dkv_out.dtype)
        dkv_out[:, HEAD:2 * HEAD] = dv_.astype(dkv_out.dtype)
        dkr = _rope_t(dk_[:, HEAD:MLA_QK], c, s, perm)
        first = pl.program_id(1) == 0

        @pl.when(first)
        def _():
            dkr_out[...] = dkr

        @pl.when(jnp.logical_not(first))
        def _():
            dkr_out[...] += dkr

    hs = lambda w: pl.BlockSpec((None, t, w), lambda m, hh: (hh, m, 0))
    rs = lambda w: pl.BlockSpec((t, w), lambda m, hh: (m, 0))
    dq_raw, dkv, dk_r = pl.pallas_call(
        lambda a, b, c, cs, sn, o1, o2, o3: mla_unprep(a[...], b[...], c[...], cs, sn, o1, o2, o3),
        name=f"mix{l}_mla_unprep", grid=(seq // t, nmla),
        in_specs=[hs(MLA_QK), hs(MLA_QK), hs(HEAD), rs(MLA_ROPE), rs(MLA_ROPE)],
        out_specs=[hs(MLA_QK), hs(2 * HEAD), rs(MLA_ROPE)],
        out_shape=[_sd((nmla, seq, MLA_QK), ACT), _sd((nmla, seq, 2 * HEAD), ACT), _sd((seq, MLA_ROPE))],
        compiler_params=_cparams(2),
    )(dq_b, dk_b, dv_b, cos_m, sin_m)
    dcq_n = _heads_dx(dq_raw, p["mla_w_uq"], l, f"mix{l}_dcq")
    dckv_n = _heads_dx(dkv, p["mla_w_ukv"], l, f"mix{l}_dckv")
    dw_uq = _heads_dw(sv["cq"], dq_raw, f"mix{l}_dw_uq")
    dw_ukv = _heads_dw(sv["ckv"], dkv, f"mix{l}_dw_ukv")
    dcq, dg_q = rms_bwd(proj, p["mla_q_norm"][l], dcq_n, f"mix{l}_dcq_rms", col=C_CQ, width=512)
    dckv, dg_kv = rms_bwd(proj, p["mla_kv_norm"][l], dckv_n, f"mix{l}_dckv_rms", col=C_CKV, width=512)

    wd_ = ndil * HEAD
    outs = [b[0] for b in sv["branches"]]
    lses = [b[1] for b in sv["branches"]]

    def mix_bwd(do_, o1, o2, o3, l1, l2, l3):
        m = jnp.maximum(jnp.maximum(l1, l2), l3)
        e1, e2, e3 = jnp.exp(l1 - m), jnp.exp(l2 - m), jnp.exp(l3 - m)
        z_ = e1 + e2 + e3
        w1, w2, w3 = e1 / z_, e2 / z_, e3 / z_
        out = w1 * o1 + w2 * o2 + w3 * o3
        return (w1 * do_, w2 * do_, w3 * do_, do_ * w1 * (o1 - out), do_ * w2 * (o2 - out), do_ * w3 * (o3 - out))

    mb = rowwise(mix_bwd, [do_c] + outs + lses, [_rows(t, wd_)] * 7, [_sd((seq, wd_))] * 6, [_rows(t, wd_)] * 6,
                 (seq // t,), f"mix{l}_dil_dmix")
    grads = [dil_bwd(sv["dq_r"], sv["dk_r"], sv["dv"], outs[i], lses[i], mb[i], mb[3 + i], seq=seq, dil=dl,
                     n_heads=ndil, name=f"mix{l}_dil{dl}_bwd") for i, dl in enumerate(DIL_BRANCHES)]

    def dil_unprep(q1, q2, q3, k1, k2, k3, v1, v2, v3, c_, s_):
        perm = _pad_perm(PARTIAL_ROPE)
        rot_t = lambda a: jnp.concatenate(
            [_rope_t(a[:, i * HEAD:(i + 1) * HEAD], c_, s_, perm) for i in range(ndil)], axis=1)
        return rot_t(q1 + q2 + q3), rot_t(k1 + k2 + k3), v1 + v2 + v3

    ddq, ddk, ddv = rowwise(dil_unprep, [g[0] for g in grads] + [g[1] for g in grads] + [g[2] for g in grads] + [cos_p, sin_p],
                            [_rows(t, wd_)] * 9 + [_rows(t, HEAD)] * 2, [_sd((seq, wd_))] * 3, [_rows(t, wd_)] * 3,
                            (seq // t,), f"mix{l}_dil_unprep")

    dproj = jnp.concatenate([unheads(dfq), unheads(dfk), unheads(dfv), dcq, dckv, ddq, ddk, ddv], axis=1).astype(ACT)
    dtail = jnp.concatenate([dk_r, dfl, jnp.zeros((seq, TAIL_W - 68), F32)], axis=1)
    dh = _dense_dx(dproj, p["w_in_main"], l, f"mix{l}_dh_main")
    dh = _dense_dx(dtail, p["w_in_tail"], l, f"mix{l}_dh_tail", extras=[dh], epilogue=lambda acc, prev: acc + prev)
    dw_main = _dense_dw(sv["h"], dproj, d // NDEV, f"mix{l}_dw_in_main")
    dw_tail = _dense_dw(sv["h"], dtail, d // NDEV, f"mix{l}_dw_in_tail")
    dx, dg_mix = rms_bwd(sv["x"], p["mix_norm"][l], dh, f"mix{l}_drms", add=dout)
    return dx, dict(mix_norm=dg_mix, w_in_main=dw_main, w_in_tail=dw_tail, fox_forget_bias=d_fox_bias, mla_q_norm=dg_q,
                    mla_kv_norm=dg_kv, mla_w_uq=dw_uq, mla_w_ukv=dw_ukv, w_out=dw_out)


def loss_head(x, g, target, name):
    seq, d = x.shape
    t = _tile(seq, 512)

    def fn(x_, g_, tgt):
        err = _rms(x_, g_) - tgt
        part = 0.5 * jnp.sum(jnp.mean(err * err, axis=-1, keepdims=True), axis=0, keepdims=True)
        dx, dg = _rms_bwd(x_, g_, err / d)
        return jnp.broadcast_to(part, (1, HEAD)), dx, dg

    return rowwise(fn, [x, g.reshape(1, d), target], [_rows(t, d), _whole((1, d)), _rows(t, d)],
                   [_sd((1, HEAD)), _sd((seq, d)), _sd((1, d))], [_whole((1, HEAD)), _rows(t, d), _whole((1, d))],
                   (seq // t,), name, acc={0: (0,), 2: (0,)})


BIG = ("ffn1_w_gate", "ffn1_w_up", "ffn1_w_down", "w_in", "mla_w_uq", "mla_w_ukv", "w_out", "ffn2_w_gate", "ffn2_w_up",
       "ffn2_w_down")
SMALL_D = ("ffn1_norm", "mix_norm", "ffn2_norm")
WEIGHTS = ("ffn1_norm", "ffn1_w_gate", "ffn1_w_up", "ffn1_w_down", "mix_norm", "w_in", "fox_forget_bias", "mla_q_norm",
           "mla_kv_norm", "mla_w_uq", "mla_w_ukv", "w_out", "ffn2_norm", "ffn2_w_gate", "ffn2_w_up", "ffn2_w_down",
           "final_norm")


def pack_small(vals, depth, d):
    rows = [vals[n].reshape(depth, d) for n in SMALL_D]
    rows.append(vals["final_norm"].reshape(1, d))
    qk = jnp.concatenate([vals["mla_q_norm"].reshape(-1), vals["mla_kv_norm"].reshape(-1)])
    rows.append(jnp.pad(qk, (0, -qk.shape[0] % d)).reshape(-1, d))
    last = jnp.concatenate([vals["fox_forget_bias"].reshape(-1), vals["loss"].reshape(-1)])
    rows.append(jnp.pad(last, (0, d - last.shape[0])).reshape(1, d))
    out = jnp.concatenate(rows, axis=0)
    return jnp.pad(out, ((0, -out.shape[0] % 8), (0, 0)))


def unpack_small(a, depth, d, rank):
    out, r = {}, 0
    for n in SMALL_D:
        out[n] = a[r:r + depth]
        r += depth
    out["final_norm"] = a[r]
    r += 1
    n_qk = -(-2 * depth * rank // d)
    qk = a[r:r + n_qk].reshape(-1)[:2 * depth * rank].reshape(2, depth, rank)
    out["mla_q_norm"], out["mla_kv_norm"] = qk[0], qk[1]
    r += n_qk
    out["fox_forget_bias"] = a[r, :depth * 4].reshape(depth, 4)
    out["loss"] = a[r, depth * 4]
    return out


def step(x, target, w, m, v):
    depth = w["ffn1_norm"].shape[0]
    seq, d = x.shape[1], x.shape[2]
    rank = w["mla_q_norm"].shape[1]
    x = x.reshape(seq, d)
    target = target.reshape(seq, d)

    local = {n: w[n] for n in BIG if n != "w_in"}
    local["w_in_main"], local["w_in_tail"] = split_w_in(w["w_in"])
    p = {n: exchange(a.astype(COMM), False, f"gather_{n}") for n, a in local.items()}
    for n in ("mix_norm", "fox_forget_bias", "mla_q_norm", "mla_kv_norm"):
        p[n] = w[n]
    consts = (*rope_tables(seq, MLA_ROPE), *[jnp.pad(a, ((0, 0), (0, HEAD - PARTIAL_ROPE)), constant_values=c)
                                             for a, c in zip(rope_tables(seq, PARTIAL_ROPE), (1.0, 0.0))])

    saved = []
    for l in range(depth):
        x, s1 = ffn_fwd(x, w["ffn1_norm"][l], p["ffn1_w_gate"], p["ffn1_w_up"], p["ffn1_w_down"], l, f"ffn1_{l}")
        x, s2 = mixer_fwd(x, p, l, consts)
        x, s3 = ffn_fwd(x, w["ffn2_norm"][l], p["ffn2_w_gate"], p["ffn2_w_up"], p["ffn2_w_down"], l, f"ffn2_{l}")
        saved.append((s1, s2, s3))
    loss, dx, d_final = loss_head(x, w["final_norm"], target, "loss_head")

    big = {n: [None] * depth for n in local}
    small = {n: [None] * depth for n in SMALL_D + ("mla_q_norm", "mla_kv_norm", "fox_forget_bias")}
    for l in reversed(range(depth)):
        s1, s2, s3 = saved[l]
        dx, (dg, dwg, dwu, dwd) = ffn_bwd(dx, s3, w["ffn2_norm"][l], p["ffn2_w_gate"], p["ffn2_w_up"], p["ffn2_w_down"], l,
                                          f"ffn2_{l}")
        small["ffn2_norm"][l], big["ffn2_w_gate"][l], big["ffn2_w_up"][l], big["ffn2_w_down"][l] = dg, dwg, dwu, dwd
        dx, gm = mixer_bwd(dx, s2, p, l, consts)
        for n, g in gm.items():
            (big if n in big else small)[n][l] = g
        dx, (dg, dwg, dwu, dwd) = ffn_bwd(dx, s1, w["ffn1_norm"][l], p["ffn1_w_gate"], p["ffn1_w_up"], p["ffn1_w_down"], l,
                                          f"ffn1_{l}")
        small["ffn1_norm"][l], big["ffn1_w_gate"][l], big["ffn1_w_up"][l], big["ffn1_w_down"][l] = dg, dwg, dwu, dwd

    out = {"grad_x": dx.reshape(1, seq, d)}

    def update(name, parts, shape):
        flat = lambda a: a.reshape(-1, shape[-1])
        res = adamw(parts.reshape(parts.shape[0], -1, shape[-1]), flat(w[name]), flat(m[name]), flat(v[name]), f"adamw_{name}")
        for kind, r in zip(("grad", "delta", "new_m", "new_v"), res):
            out[f"{kind}_{name}"] = r.reshape(shape)

    recv = {n: exchange(jnp.stack(big[n], axis=1), True, f"scatter_{n}") for n in local}
    for n in BIG:
        if n != "w_in":
            update(n, recv[n], w[n].shape)
    g_main = sum_parts(recv["w_in_main"].reshape(NDEV, -1, MAIN_W), "sum_w_in_main").reshape(depth, -1, MAIN_W)
    g_tail = sum_parts(recv["w_in_tail"].reshape(NDEV, -1, TAIL_W), "sum_w_in_tail").reshape(depth, -1, TAIL_W)
    update("w_in", merge_w_in(g_main, g_tail)[None], w["w_in"].shape)

    part = {n: jnp.stack(g).reshape(depth, -1) for n, g in small.items()}
    part["final_norm"], part["loss"] = d_final, loss[0, 0:1]
    parts = exchange(pack_small(part, depth, d), False, "gather_small")
    zero = jnp.zeros((1,), F32)
    packed = [pack_small({**{n: a[n] for n in part if n != "loss"}, "loss": zero}, depth, d) for a in (w, m, v)]
    res = [unpack_small(r, depth, d, rank) for r in adamw(parts, *packed, "adamw_small")]
    out["loss"] = res[0]["loss"]
    for n in small.keys() | {"final_norm"}:
        for kind, r in zip(("grad", "delta", "new_m", "new_v"), res):
            out[f"{kind}_{n}"] = r[n].reshape(w[n].shape)
    return out


def kernel(x, ffn1_norm, ffn1_w_gate, ffn1_w_up, ffn1_w_down, mix_norm, w_in, fox_forget_bias, mla_q_norm, mla_kv_norm, mla_w_uq, mla_w_ukv, w_out, ffn2_norm, ffn2_w_gate, ffn2_w_up, ffn2_w_down, final_norm, loss_target, m_ffn1_norm, m_ffn1_w_gate, m_ffn1_w_up, m_ffn1_w_down, m_mix_norm, m_w_in, m_fox_forget_bias, m_mla_q_norm, m_mla_kv_norm, m_mla_w_uq, m_mla_w_ukv, m_w_out, m_ffn2_norm, m_ffn2_w_gate, m_ffn2_w_up, m_ffn2_w_down, m_final_norm, v_ffn1_norm, v_ffn1_w_gate, v_ffn1_w_up, v_ffn1_w_down, v_mix_norm, v_w_in, v_fox_forget_bias, v_mla_q_norm, v_mla_kv_norm, v_mla_w_uq, v_mla_w_ukv, v_w_out, v_ffn2_norm, v_ffn2_w_gate, v_ffn2_w_up, v_ffn2_w_down, v_final_norm):
    args = locals()
    w = {n: args[n] for n in WEIGHTS}
    m = {n: args["m_" + n] for n in WEIGHTS}
    v = {n: args["v_" + n] for n in WEIGHTS}
    out = step(x, loss_target, w, m, v)
    return (out["loss"], out["grad_x"], *[out["grad_" + n] for n in WEIGHTS], *[out["delta_" + n] for n in WEIGHTS],
            *[out["new_m_" + n] for n in WEIGHTS], *[out["new_v_" + n] for n in WEIGHTS])
```

```python
import functools

import jax
import jax.numpy as jnp
from jax import lax
from jax.experimental import pallas as pl
from jax.experimental.pallas import tpu as pltpu

F32 = jnp.float32
MXU = jnp.bfloat16
ACT = jnp.bfloat16
COMM = jnp.bfloat16
HI = lax.Precision.HIGHEST
NN = (((1,), (0,)), ((), ()))
NT = (((1,), (1,)), ((), ()))
TN = (((0,), (0,)), ((), ()))

NDEV = 8
HEAD = 128
EPS = 1e-6
ROPE_THETA = 500000.0
PARTIAL_ROPE = HEAD // 4
MLA_ROPE = 64
MLA_QK = HEAD + MLA_ROPE
DIL_BRANCHES = (1, 4, 16)
NEG = -1e30
VMEM_LIMIT = 48 * 1024 * 1024

ADAMW_STEP_BYTES = 12 * 1024 * 1024

ADAM_LR, ADAM_B1, ADAM_B2, ADAM_EPS, ADAM_WD, ADAM_STEP = 0.001, 0.9, 0.999, 1e-08, 0.01, 10


def _cparams(n_axes):
    return pltpu.CompilerParams(dimension_semantics=("arbitrary",) * n_axes, vmem_limit_bytes=VMEM_LIMIT)


def _tile(n, t):
    t = min(n, t)
    assert n % t == 0, (n, t)
    return t


def _dep_spec(dep):
    nd = dep.ndim
    return pl.BlockSpec(dep.shape, lambda *_: (0,) * nd)


def mm(a, b, *, name, dims, grid, a_spec, b_spec, o_spec, out_shape, nk, extras=(), extra_specs=(), epilogue=None, dep=None):
    n_ex = len(extras)
    kaxis = len(grid) - 1
    if dep is not None:
        extras, extra_specs = [*extras, dep], [*extra_specs, _dep_spec(dep)]
    n_more = len(extras)

    def body(a_ref, b_ref, *rest):
        ex, o_ref = rest[:n_ex], rest[n_more]
        part = lax.dot_general(a_ref[...].astype(MXU), b_ref[...].astype(MXU), dims, preferred_element_type=F32)

        def finish(acc):
            res = acc if epilogue is None else epilogue(acc, *[e[...] for e in ex])
            o_ref[...] = res.astype(o_ref.dtype)

        if nk == 1:
            finish(part)
        else:
            acc_ref = rest[n_more + 1]
            k = pl.program_id(kaxis)

            @pl.when(k == 0)
            def _():
                acc_ref[...] = part

            @pl.when(k > 0)
            def _():
                acc_ref[...] += part

            @pl.when(k == nk - 1)
            def _():
                finish(acc_ref[...])

    acc_shape = tuple(d for d in o_spec.block_shape if d is not None)
    return pl.pallas_call(
        body, name=name, grid=grid, in_specs=[a_spec, b_spec, *extra_specs], out_specs=o_spec, out_shape=out_shape,
        scratch_shapes=[] if nk == 1 else [pltpu.VMEM(acc_shape, F32)], compiler_params=_cparams(len(grid)),
    )(a, b, *extras)


def rowwise(fn, ins, in_specs, outs, out_specs, grid, name, acc=None, dep=None):
    acc = acc or {}
    n_in = len(ins)
    if dep is not None:
        ins, in_specs = [*ins, dep], [*in_specs, _dep_spec(dep)]
    n_all = len(ins)

    def body(*refs):
        vals = fn(*[r[...] for r in refs[:n_in]])
        if not isinstance(vals, (tuple, list)):
            vals = (vals,)
        for i, (r, v) in enumerate(zip(refs[n_all:], vals)):
            if i in acc:
                first = functools.reduce(jnp.logical_and, [pl.program_id(ax) == 0 for ax in acc[i]])

                @pl.when(first)
                def _(r=r, v=v):
                    r[...] = v.astype(r.dtype)

                @pl.when(jnp.logical_not(first))
                def _(r=r, v=v):
                    r[...] += v.astype(r.dtype)
            else:
                r[...] = v.astype(r.dtype)

    return pl.pallas_call(
        body, name=name, grid=grid, in_specs=in_specs, out_specs=out_specs, out_shape=outs,
        compiler_params=_cparams(len(grid)),
    )(*ins)


def _rows(t, c, col=0):
    return pl.BlockSpec((t, c), lambda m, col=col: (m, col))


def _whole(shape):
    nd = len(shape)
    return pl.BlockSpec(shape, lambda *_: (0,) * nd)


def _rms(x, g):
    x = x.astype(F32)
    return x * lax.rsqrt(jnp.mean(x * x, axis=-1, keepdims=True) + EPS) * g


def _rms_bwd(x, g, dy):
    x = x.astype(F32)
    dy = dy.astype(F32)
    r = lax.rsqrt(jnp.mean(x * x, axis=-1, keepdims=True) + EPS)
    xh = x * r
    dg = jnp.sum(dy * xh, axis=0, keepdims=True)
    dxh = dy * g
    dx = r * (dxh - xh * jnp.mean(dxh * xh, axis=-1, keepdims=True))
    return dx, dg


def _swap_matrix(n):
    i = lax.broadcasted_iota(jnp.int32, (n, n), 0)
    j = lax.broadcasted_iota(jnp.int32, (n, n), 1)
    return jnp.where(((i + n // 2) % n) == j, 1.0, 0.0).astype(F32)


def _rope(x, cos, sin_signed, perm):
    return x * cos + jnp.dot(x, perm, precision=HI, preferred_element_type=F32) * sin_signed


def _rope_t(dy, cos, sin_signed, perm):
    return dy * cos + jnp.dot(dy * sin_signed, perm, precision=HI, preferred_element_type=F32)


def rope_tables(seq, dim):
    inv = 1.0 / (ROPE_THETA ** (jnp.arange(0, dim, 2, dtype=F32) / dim))
    ang = jnp.arange(seq, dtype=F32)[:, None] * inv[None, :]
    cos, sin = jnp.cos(ang), jnp.sin(ang)
    return jnp.concatenate([cos, cos], axis=1), jnp.concatenate([-sin, sin], axis=1)


def _hspec(arr_kind, t, w, off, seq_of):
    if arr_kind == "cols":
        return pl.BlockSpec((t, w), lambda h, i, j: (seq_of(i, j), off + h))
    return pl.BlockSpec((None, t, w), lambda h, i, j: (h, seq_of(i, j), off))


def _colspec(t, seq_of):
    return pl.BlockSpec((None, t, 1), lambda h, i, j: (h, seq_of(i, j), 0))


def _rowspec(t, seq_of):
    return pl.BlockSpec((None, 1, t), lambda h, i, j: (h, 0, seq_of(i, j)))


def _causal(s, qi, kj, t, transposed=False):
    a = lax.broadcasted_iota(jnp.int32, (t, t), 0)
    b = lax.broadcasted_iota(jnp.int32, (t, t), 1)
    keep = (kj * t + a <= qi * t + b) if transposed else (kj * t + b <= qi * t + a)
    return jnp.where(keep, s, NEG)


def flash_fwd(q, k, v, cum, *, n_heads, seq, t, scale, name):
    nb = seq // t
    qs, ks = (lambda i, j: i), (lambda i, j: jnp.minimum(j, i))
    ins = [q[0], k[0], v[0]]
    specs = [_hspec(q[1], t, q[2], q[3], qs), _hspec(k[1], t, k[2], k[3], ks), _hspec(v[1], t, v[2], v[3], ks)]
    if cum is not None:
        ins += [cum[0], cum[1]]
        specs += [_colspec(t, qs), _rowspec(t, ks)]

    def body(*refs):
        q_ref, k_ref, v_ref = refs[:3]
        o_ref, lse_ref, m_s, l_s, acc_s = refs[-5:]
        i, j = pl.program_id(1), pl.program_id(2)

        @pl.when(j == 0)
        def _():
            m_s[...] = jnp.full(m_s.shape, NEG, F32)
            l_s[...] = jnp.zeros(l_s.shape, F32)
            acc_s[...] = jnp.zeros(acc_s.shape, F32)

        @pl.when(j <= i)
        def _():
            s = lax.dot_general(q_ref[...].astype(MXU), k_ref[...].astype(MXU), NT, preferred_element_type=F32) * scale
            if cum is not None:
                s = s + (refs[3][...] - refs[4][...])
            s = _causal(s, i, j, t)
            m_new = jnp.maximum(m_s[...], jnp.max(s, axis=1, keepdims=True))
            alpha = jnp.exp(m_s[...] - m_new)
            p = jnp.exp(s - m_new)
            l_s[...] = alpha * l_s[...] + jnp.sum(p, axis=1, keepdims=True)
            acc_s[...] = alpha * acc_s[...] + jnp.dot(p.astype(MXU), v_ref[...].astype(MXU), preferred_element_type=F32)
            m_s[...] = m_new

        @pl.when(j == nb - 1)
        def _():
            o_ref[...] = (acc_s[...] / l_s[...]).astype(o_ref.dtype)
            lse_ref[...] = m_s[...] + jnp.log(l_s[...])

    return pl.pallas_call(
        body, name=name, grid=(n_heads, nb, nb), in_specs=specs,
        out_specs=[pl.BlockSpec((t, HEAD), lambda h, i, j: (i, h)), _colspec(t, qs)],
        out_shape=[jax.ShapeDtypeStruct((seq, n_heads * HEAD), F32), jax.ShapeDtypeStruct((n_heads, seq, 1), F32)],
        scratch_shapes=[pltpu.VMEM((t, 1), F32), pltpu.VMEM((t, 1), F32), pltpu.VMEM((t, HEAD), F32)],
        compiler_params=_cparams(3),
    )(*ins)


def flash_bwd_dq(q, k, v, do, lse, delta, cum, *, n_heads, seq, t, scale, name):
    nb = seq // t
    qs, ks = (lambda i, j: i), (lambda i, j: jnp.minimum(j, i))
    ins = [q[0], k[0], v[0], do, lse, delta]
    specs = [_hspec(q[1], t, q[2], q[3], qs), _hspec(k[1], t, k[2], k[3], ks), _hspec(v[1], t, v[2], v[3], ks),
             _hspec("cols", t, HEAD, 0, qs), _colspec(t, qs), _colspec(t, qs)]
    if cum is not None:
        ins += [cum[0], cum[1]]
        specs += [_colspec(t, qs), _rowspec(t, ks)]
    wq = q[2]
    n_out = 1 if cum is None else 2

    def body(*refs):
        q_ref, k_ref, v_ref, do_ref, lse_ref, dl_ref = refs[:6]
        outs = refs[-2 * n_out:-n_out]
        accs = refs[-n_out:]
        i, j = pl.program_id(1), pl.program_id(2)

        @pl.when(j == 0)
        def _():
            for a in accs:
                a[...] = jnp.zeros(a.shape, F32)

        @pl.when(j <= i)
        def _():
            kb = k_ref[...].astype(MXU)
            s = lax.dot_general(q_ref[...].astype(MXU), kb, NT, preferred_element_type=F32) * scale
            if cum is not None:
                s = s + (refs[6][...] - refs[7][...])
            s = _causal(s, i, j, t)
            p = jnp.exp(s - lse_ref[...])
            dp = lax.dot_general(do_ref[...].astype(MXU), v_ref[...].astype(MXU), NT, preferred_element_type=F32)
            ds = p * (dp - dl_ref[...])
            accs[0][...] += jnp.dot(ds.astype(MXU), kb, preferred_element_type=F32)
            if cum is not None:
                accs[1][...] += jnp.sum(ds, axis=1, keepdims=True)

        @pl.when(j == nb - 1)
        def _():
            outs[0][...] = accs[0][...] * scale
            if cum is not None:
                outs[1][...] = accs[1][...]

    out_specs = [pl.BlockSpec((None, t, wq), lambda h, i, j: (h, i, 0))]
    out_shape = [jax.ShapeDtypeStruct((n_heads, seq, wq), F32)]
    scratch = [pltpu.VMEM((t, wq), F32)]
    if cum is not None:
        out_specs.append(_colspec(t, qs))
        out_shape.append(jax.ShapeDtypeStruct((n_heads, seq, 1), F32))
        scratch.append(pltpu.VMEM((t, 1), F32))
    res = pl.pallas_call(
        body, name=name, grid=(n_heads, nb, nb), in_specs=specs, out_specs=out_specs, out_shape=out_shape,
        scratch_shapes=scratch, compiler_params=_cparams(3),
    )(*ins)
    return res[0] if cum is None else res


def flash_bwd_dkv(q, k, v, do, lse_row, delta_row, cum, *, n_heads, seq, t, scale, name):
    nb = seq // t
    ks, qs = (lambda j, i: j), (lambda j, i: jnp.maximum(i, j))
    ins = [q[0], k[0], v[0], do, lse_row, delta_row]
    specs = [_hspec(q[1], t, q[2], q[3], qs), _hspec(k[1], t, k[2], k[3], ks), _hspec(v[1], t, v[2], v[3], ks),
             _hspec("cols", t, HEAD, 0, qs), _rowspec(t, qs), _rowspec(t, qs)]
    if cum is not None:
        ins += [cum[0], cum[1]]
        specs += [_colspec(t, ks), _rowspec(t, qs)]
    wk = k[2]
    n_out = 2 if cum is None else 3

    def body(*refs):
        q_ref, k_ref, v_ref, do_ref, lse_ref, dl_ref = refs[:6]
        outs = refs[-2 * n_out:-n_out]
        accs = refs[-n_out:]
        j, i = pl.program_id(1), pl.program_id(2)

        @pl.when(i == 0)
        def _():
            for a in accs:
                a[...] = jnp.zeros(a.shape, F32)

        @pl.when(i >= j)
        def _():
            qb = q_ref[...].astype(MXU)
            dob = do_ref[...].astype(MXU)
            st = lax.dot_general(k_ref[...].astype(MXU), qb, NT, preferred_element_type=F32) * scale
            if cum is not None:
                st = st + (refs[7][...] - refs[6][...])
            st = _causal(st, i, j, t, transposed=True)
            pt = jnp.exp(st - lse_ref[...])
            dpt = lax.dot_general(v_ref[...].astype(MXU), dob, NT, preferred_element_type=F32)
            dst = pt * (dpt - dl_ref[...])
            accs[0][...] += jnp.dot(dst.astype(MXU), qb, preferred_element_type=F32)
            accs[1][...] += jnp.dot(pt.astype(MXU), dob, preferred_element_type=F32)
            if cum is not None:
                accs[2][...] -= jnp.sum(dst, axis=1, keepdims=True)

        @pl.when(i == nb - 1)
        def _():
            outs[0][...] = accs[0][...] * scale
            for o, a in zip(outs[1:], accs[1:]):
                o[...] = a[...]

    out_specs = [pl.BlockSpec((None, t, wk), lambda h, j, i: (h, j, 0)), pl.BlockSpec((None, t, HEAD), lambda h, j, i: (h, j, 0))]
    out_shape = [jax.ShapeDtypeStruct((n_heads, seq, wk), F32), jax.ShapeDtypeStruct((n_heads, seq, HEAD), F32)]
    scratch = [pltpu.VMEM((t, wk), F32), pltpu.VMEM((t, HEAD), F32)]
    if cum is not None:
        out_specs.append(_colspec(t, ks))
        out_shape.append(jax.ShapeDtypeStruct((n_heads, seq, 1), F32))
        scratch.append(pltpu.VMEM((t, 1), F32))
    return pl.pallas_call(
        body, name=name, grid=(n_heads, nb, nb), in_specs=specs, out_specs=out_specs, out_shape=out_shape,
        scratch_shapes=scratch, compiler_params=_cparams(3),
    )(*ins)


def attn_delta(do, o, *, n_heads, seq, name):
    t = _tile(seq, 512)
    spec = pl.BlockSpec((t, HEAD), lambda h, m: (m, h))
    return rowwise(
        lambda a, b: jnp.sum(a.astype(F32) * b.astype(F32), axis=1, keepdims=True), [do, o], [spec, spec],
        jax.ShapeDtypeStruct((n_heads, seq, 1), F32), pl.BlockSpec((None, t, 1), lambda h, m: (h, m, 0)),
        (n_heads, seq // t), name)


def _dil_scores(q, kc, kp, n, scale):
    i = lax.broadcasted_iota(jnp.int32, (HEAD, HEAD), 0)
    j = lax.broadcasted_iota(jnp.int32, (HEAD, HEAD), 1)
    sc = lax.dot_general(q, kc, NT, preferred_element_type=F32) * scale
    sp = lax.dot_general(q, kp, NT, preferred_element_type=F32) * scale
    sc = jnp.where(j <= i, sc, NEG)
    sp = jnp.where(jnp.logical_and(j >= i, n > 0), sp, NEG)
    return sc, sp


def _strip_spec(length, n_heads, col_blocks, off):
    return pl.BlockSpec((length, HEAD), lambda r, h: (0, r * col_blocks + off + h))


def dil_fwd(q, k, v, *, seq, dil, n_heads, name):
    length = seq // dil
    nb = length // HEAD
    scale = HEAD ** -0.5
    view = lambda a: a.reshape(length, dil * a.shape[1])
    spec = _strip_spec(length, n_heads, n_heads, 0)

    def body(q_ref, k_ref, v_ref, o_ref, lse_ref):
        def step(n, carry):
            cur = pl.ds(pl.multiple_of(n * HEAD, HEAD), HEAD)
            prev = pl.ds(pl.multiple_of(jnp.maximum(n - 1, 0) * HEAD, HEAD), HEAD)
            qb = q_ref[cur, :].astype(MXU)
            sc, sp = _dil_scores(qb, k_ref[cur, :].astype(MXU), k_ref[prev, :].astype(MXU), n, scale)
            m = jnp.maximum(jnp.max(sc, axis=1, keepdims=True), jnp.max(sp, axis=1, keepdims=True))
            ec, ep = jnp.exp(sc - m), jnp.exp(sp - m)
            l = jnp.sum(ec, axis=1, keepdims=True) + jnp.sum(ep, axis=1, keepdims=True)
            o = jnp.dot((ec / l).astype(MXU), v_ref[cur, :].astype(MXU), preferred_element_type=F32)
            o = o + jnp.dot((ep / l).astype(MXU), v_ref[prev, :].astype(MXU), preferred_element_type=F32)
            o_ref[cur, :] = o
            lse_ref[cur, :] = jnp.broadcast_to(m + jnp.log(l), (HEAD, HEAD))
            return carry

        lax.fori_loop(0, nb, step, 0)

    out = jax.ShapeDtypeStruct((length, dil * n_heads * HEAD), F32)
    o, lse = pl.pallas_call(
        body, name=name, grid=(dil, n_heads), in_specs=[spec, spec, spec], out_specs=[spec, spec], out_shape=[out, out],
        compiler_params=_cparams(2),
    )(view(q), view(k), view(v))
    return o.reshape(seq, -1), lse.reshape(seq, -1)


def dil_bwd(q, k, v, o, lse, do, dlse, *, seq, dil, n_heads, name):
    length = seq // dil
    nb = length // HEAD
    scale = HEAD ** -0.5
    view = lambda a: a.reshape(length, dil * a.shape[1])
    spec = _strip_spec(length, n_heads, n_heads, 0)

    def body(q_ref, k_ref, v_ref, o_ref, lse_ref, do_ref, dlse_ref, dq_ref, dk_ref, dv_ref):
        dk_ref[...] = jnp.zeros(dk_ref.shape, F32)
        dv_ref[...] = jnp.zeros(dv_ref.shape, F32)

        def step(n, carry):
            cur = pl.ds(pl.multiple_of(n * HEAD, HEAD), HEAD)
            prev = pl.ds(pl.multiple_of(jnp.maximum(n - 1, 0) * HEAD, HEAD), HEAD)
            qb = q_ref[cur, :].astype(MXU)
            kc, kp = k_ref[cur, :].astype(MXU), k_ref[prev, :].astype(MXU)
            vc, vp = v_ref[cur, :].astype(MXU), v_ref[prev, :].astype(MXU)
            sc, sp = _dil_scores(qb, kc, kp, n, scale)
            lse_b = jnp.max(lse_ref[cur, :], axis=1, keepdims=True)
            pc, pp = jnp.exp(sc - lse_b), jnp.exp(sp - lse_b)
            dob = do_ref[cur, :]
            shift = jnp.sum(dlse_ref[cur, :], axis=1, keepdims=True) - jnp.sum(dob * o_ref[cur, :], axis=1, keepdims=True)
            dob = dob.astype(MXU)
            dsc = pc * (lax.dot_general(dob, vc, NT, preferred_element_type=F32) + shift)
            dsp = pp * (lax.dot_general(dob, vp, NT, preferred_element_type=F32) + shift)
            dscb, dspb = dsc.astype(MXU), dsp.astype(MXU)
            dq = jnp.dot(dscb, kc, preferred_element_type=F32) + jnp.dot(dspb, kp, preferred_element_type=F32)
            dq_ref[cur, :] = dq * scale
            dk_ref[cur, :] += lax.dot_general(dscb, qb, TN, preferred_element_type=F32) * scale
            dv_ref[cur, :] += lax.dot_general(pc.astype(MXU), dob, TN, preferred_element_type=F32)
            dk_ref[prev, :] += lax.dot_general(dspb, qb, TN, preferred_element_type=F32) * scale
            dv_ref[prev, :] += lax.dot_general(pp.astype(MXU), dob, TN, preferred_element_type=F32)
            return carry

        lax.fori_loop(0, nb, step, 0)

    out = jax.ShapeDtypeStruct((length, dil * n_heads * HEAD), F32)
    res = pl.pallas_call(
        body, name=name, grid=(dil, n_heads), in_specs=[spec] * 7, out_specs=[spec] * 3, out_shape=[out] * 3,
        compiler_params=_cparams(2),
    )(*[view(a) for a in (q, k, v, o, lse, do, dlse)])
    return [r.reshape(seq, -1) for r in res]


def _tri(n, kind):
    i = lax.broadcasted_iota(jnp.int32, (n, n), 0)
    j = lax.broadcasted_iota(jnp.int32, (n, n), 1)
    return jnp.where({"le": i <= j, "ge": i >= j}[kind], 1.0, 0.0).astype(F32)


def _block_matrix(n_rows, per_head, kind):
    r = lax.broadcasted_iota(jnp.int32, (n_rows, n_rows), 0)
    c = lax.broadcasted_iota(jnp.int32, (n_rows, n_rows), 1)
    same = (r // per_head) == (c // per_head)
    rel = {"lt": c < r, "gt": c > r, "all": c == c}[kind]
    return jnp.where(jnp.logical_and(same, rel), 1.0, 0.0).astype(F32)


def _lane_pick(x, lane):
    j = lax.broadcasted_iota(jnp.int32, x.shape, 1)
    return jnp.sum(jnp.where(j == lane, x, 0.0), axis=1, keepdims=True)


def _log_sigmoid(z):
    return jnp.minimum(z, 0.0) - jnp.log1p(jnp.exp(-jnp.abs(z)))


def fox_gate_fwd(z, bias_rows, per_head, name):
    n_rows = z.shape[0]

    def body(z_ref, b_ref, c_ref):
        logf = _log_sigmoid(z_ref[...] + b_ref[...])
        within = jnp.dot(logf, _tri(HEAD, "le"), precision=HI, preferred_element_type=F32)
        tot = jnp.broadcast_to(_lane_pick(within, HEAD - 1), (n_rows, HEAD))
        c_ref[...] = within + jnp.dot(_block_matrix(n_rows, per_head, "lt"), tot, precision=HI, preferred_element_type=F32)

    return pl.pallas_call(body, name=name, out_shape=jax.ShapeDtypeStruct(z.shape, F32),
                          compiler_params=pltpu.CompilerParams(vmem_limit_bytes=VMEM_LIMIT))(z, bias_rows)


def fox_gate_bwd(z, bias_rows, dcum_q, dcum_k, per_head, name):
    n_rows = z.shape[0]

    def body(z_ref, b_ref, dcq_ref, dck_ref, dz_ref, db_ref):
        within = jnp.dot(dcq_ref[...] + dck_ref[...], _tri(HEAD, "ge"), precision=HI, preferred_element_type=F32)
        tot = jnp.broadcast_to(_lane_pick(within, 0), (n_rows, HEAD))
        dlogf = within + jnp.dot(_block_matrix(n_rows, per_head, "gt"), tot, precision=HI, preferred_element_type=F32)
        dz = dlogf * jax.nn.sigmoid(-(z_ref[...] + b_ref[...]))
        dz_ref[...] = dz
        rs = jnp.broadcast_to(jnp.sum(dz, axis=1, keepdims=True), (n_rows, HEAD))
        db_ref[...] = jnp.dot(_block_matrix(n_rows, per_head, "all"), rs, precision=HI, preferred_element_type=F32)

    shp = jax.ShapeDtypeStruct(z.shape, F32)
    return pl.pallas_call(body, name=name, out_shape=[shp, shp],
                          compiler_params=pltpu.CompilerParams(vmem_limit_bytes=VMEM_LIMIT))(z, bias_rows, dcum_q, dcum_k)


def exchange(x, scatter, name):
    blk = x.shape[1:] if scatter else x.shape

    def body(x_ref, o_ref, send_sems, recv_sems, local_sem):
        mx, my, mc = lax.axis_index("x"), lax.axis_index("y"), lax.axis_index("c")
        me = 4 * mx + 2 * my + mc
        flip = lambda v, f: 1 - v if f else v
        local = pltpu.make_async_copy(x_ref.at[me] if scatter else x_ref, o_ref.at[me], local_sem)
        local.start()
        sends, recvs = [], []
        for n in range(1, NDEV):
            px, py, pc = flip(mx, n & 4), flip(my, n & 2), flip(mc, n & 1)
            p = 4 * px + 2 * py + pc
            sends.append(pltpu.make_async_remote_copy(
                src_ref=x_ref.at[p] if scatter else x_ref, dst_ref=o_ref.at[me], send_sem=send_sems.at[n - 1],
                recv_sem=recv_sems.at[n - 1], device_id=(px, py, pc), device_id_type=pl.DeviceIdType.MESH))
            recvs.append(pltpu.make_async_remote_copy(
                src_ref=x_ref.at[me] if scatter else x_ref, dst_ref=o_ref.at[p], send_sem=send_sems.at[n - 1],
                recv_sem=recv_sems.at[n - 1], device_id=(px, py, pc), device_id_type=pl.DeviceIdType.MESH))
        for cp in sends:
            cp.start()
        for cp in recvs:
            cp.wait_recv()
        for cp in sends:
            cp.wait_send()
        local.wait()

    hbm = pl.BlockSpec(memory_space=pltpu.HBM)
    return pl.pallas_call(
        body, name=name, in_specs=[hbm], out_specs=hbm, out_shape=jax.ShapeDtypeStruct((NDEV, *blk), x.dtype),
        scratch_shapes=[pltpu.SemaphoreType.DMA((NDEV - 1,)), pltpu.SemaphoreType.DMA((NDEV - 1,)), pltpu.SemaphoreType.DMA],
    )(x)


def _exchange_copies(x_refs, land_refs, send_sems, recv_sems, local_sems, scatter):
    mx, my, mc = lax.axis_index("x"), lax.axis_index("y"), lax.axis_index("c")
    me = 4 * mx + 2 * my + mc
    flip = lambda v, f: 1 - v if f else v
    local, sends, recvs = [], [], []
    for a, (x_ref, o_ref) in enumerate(zip(x_refs, land_refs)):
        local.append(pltpu.make_async_copy(x_ref.at[me] if scatter else x_ref, o_ref.at[me], local_sems.at[a]))
        for n in range(1, NDEV):
            px, py, pc = flip(mx, n & 4), flip(my, n & 2), flip(mc, n & 1)
            p = 4 * px + 2 * py + pc
            sem = (NDEV - 1) * a + n - 1
            mk = lambda src, dst: pltpu.make_async_remote_copy(
                src_ref=src, dst_ref=dst, send_sem=send_sems.at[sem], recv_sem=recv_sems.at[sem],
                device_id=(px, py, pc), device_id_type=pl.DeviceIdType.MESH)
            sends.append(mk(x_ref.at[p] if scatter else x_ref, o_ref.at[me]))
            recvs.append(mk(x_ref.at[me] if scatter else x_ref, o_ref.at[p]))
    return local, sends, recvs


_HBM = pl.BlockSpec(memory_space=pltpu.HBM)
_SEM = pl.BlockSpec(memory_space=pltpu.SEMAPHORE)
_EFFECT = pltpu.SideEffectType.DATAFLOW_SIDE_EFFECTING


def exchange_start(xs, scatter, name):
    n = len(xs)
    lands = [jax.ShapeDtypeStruct((NDEV, *(x.shape[1:] if scatter else x.shape)), x.dtype) for x in xs]

    def body(*refs):
        x_refs, land_refs = refs[:n], refs[n:2 * n]
        send_sems, recv_sems, local_sems = refs[2 * n:2 * n + 3]
        token = refs[-1]
        local, sends, _ = _exchange_copies(x_refs, land_refs, send_sems, recv_sems, local_sems, scatter)
        for cp in local + sends:
            cp.start()
        token[...] = jnp.zeros(token.shape, token.dtype)

    n_sem = (NDEV - 1) * n
    out = pl.pallas_call(
        body, name=name,
        out_shape=(pltpu.SemaphoreType.DMA((n_sem,)), pltpu.SemaphoreType.DMA((n_sem,)), pltpu.SemaphoreType.DMA((n,)),
                   *[pltpu.HBM(x.shape, x.dtype) for x in xs], *[pltpu.HBM(s.shape, s.dtype) for s in lands],
                   jax.ShapeDtypeStruct((8, HEAD), F32)),
        in_specs=[_HBM] * (2 * n), out_specs=(_SEM, _SEM, _SEM, *[_HBM] * (2 * n), pl.BlockSpec(memory_space=pltpu.VMEM)),
        input_output_aliases={i: 3 + i for i in range(2 * n)},
        compiler_params=pltpu.CompilerParams(has_side_effects=_EFFECT),
    )(*[pltpu.with_memory_space_constraint(x, pltpu.HBM) for x in xs],
      *[pltpu.with_memory_space_constraint(lax.empty(s.shape, s.dtype), pltpu.HBM) for s in lands])
    return (out[:3], out[3:3 + n], out[3 + n:3 + 2 * n], scatter), out[-1]


def exchange_wait(handle, after, name):
    sems, xs, lands, scatter = handle
    n = len(xs)

    def body(*refs):
        x_refs, land_refs = refs[:n], refs[n:2 * n]
        send_sems, recv_sems, local_sems = refs[2 * n:2 * n + 3]
        local, sends, recvs = _exchange_copies(x_refs, land_refs, send_sems, recv_sems, local_sems, scatter)
        for cp in sends:
            cp.wait_send()
        for cp in recvs:
            cp.wait_recv()
        for cp in local:
            cp.wait()

    out = pl.pallas_call(
        body, name=name, out_shape=tuple(pltpu.HBM(a.shape, a.dtype) for a in (*xs, *lands)),
        in_specs=[_HBM] * (2 * n) + [_SEM] * 3 + [pl.BlockSpec(memory_space=pl.ANY)], out_specs=tuple([_HBM] * (2 * n)),
        input_output_aliases={i: i for i in range(2 * n)}, compiler_params=pltpu.CompilerParams(has_side_effects=_EFFECT),
    )(*xs, *lands, *sems, after)
    return list(out[n:])


def adamw(parts, w, m, v, name):
    depth = len(parts)
    n_parts, rows, cols = parts[0].shape
    t = rows
    for cand in (256, 128, 64, 32, 16, 8):
        if rows % cand == 0 and (n_parts * parts[0].dtype.itemsize + 7 * 4) * cand * cols <= ADAMW_STEP_BYTES:
            t = cand
            break
    nr = rows // t

    def body(*refs):
        p_refs = refs[:depth]
        w_ref, m_ref, v_ref, g_out, d_out, m_out, v_out = refs[depth:]
        layer = pl.program_id(0)
        for i in range(depth):
            @pl.when(layer == i)
            def _(p=p_refs[i]):
                g = p[0].astype(F32)
                for j in range(1, n_parts):
                    g = g + p[j].astype(F32)
                m_new = ADAM_B1 * m_ref[...] + (1.0 - ADAM_B1) * g
                v_new = ADAM_B2 * v_ref[...] + (1.0 - ADAM_B2) * jnp.square(g)
                m_hat = m_new / (1.0 - ADAM_B1 ** ADAM_STEP)
                v_hat = v_new / (1.0 - ADAM_B2 ** ADAM_STEP)
                g_out[...] = g
                d_out[...] = -ADAM_LR * (m_hat / (jnp.sqrt(v_hat) + ADAM_EPS) + ADAM_WD * w_ref[...])
                m_out[...] = m_new
                v_out[...] = v_new

    def part_spec(i):
        return pl.BlockSpec((n_parts, t, cols), lambda l, r: (0, jnp.where(l < i, 0, jnp.where(l == i, r, nr - 1)), 0))

    spec = pl.BlockSpec((t, cols), lambda l, r: (l * nr + r, 0))
    out = jax.ShapeDtypeStruct((depth * rows, cols), F32)
    return pl.pallas_call(
        body, name=name, grid=(depth, nr), in_specs=[*[part_spec(i) for i in range(depth)], spec, spec, spec],
        out_specs=[spec] * 4, out_shape=[out] * 4, compiler_params=_cparams(2),
    )(*parts, w, m, v)


def sum_parts(parts, name):
    n_parts, rows, cols = parts.shape
    t = _tile(rows, 128)

    def fn(p):
        g = p[0].astype(F32)
        for i in range(1, n_parts):
            g = g + p[i].astype(F32)
        return g

    return rowwise(fn, [parts], [pl.BlockSpec((n_parts, t, cols), lambda r: (0, r, 0))],
                   jax.ShapeDtypeStruct((rows, cols), F32), pl.BlockSpec((t, cols), lambda r: (r, 0)), (rows // t,), name)


def _sd(shape, dtype=F32):
    return jax.ShapeDtypeStruct(shape, dtype)


def rms_fwd(x, g, name, col=0, width=None, dep=None):
    seq = x.shape[0]
    width = width or x.shape[1]
    t = _tile(seq, 512)
    return rowwise(_rms, [x, g.reshape(1, width)], [_rows(t, width, col), _whole((1, width))], _sd((seq, width), ACT),
                   _rows(t, width), (seq // t,), name, dep=dep)


def rms_bwd(x, g, dy, name, col=0, width=None, add=None, dx_dtype=F32):
    seq = x.shape[0]
    width = width or x.shape[1]
    t = _tile(seq, 512)
    ins, specs = [x, g.reshape(1, width), dy], [_rows(t, width, col), _whole((1, width)), _rows(t, width)]
    if add is None:
        fn = _rms_bwd
    else:
        ins.append(add)
        specs.append(_rows(t, width))

        def fn(x_, g_, dy_, add_):
            dx, dg = _rms_bwd(x_, g_, dy_)
            return dx + add_, dg
    return rowwise(fn, ins, specs, [_sd((seq, width), dx_dtype), _sd((1, width))], [_rows(t, width), _whole((1, width))],
                   (seq // t,), name, acc={1: (0,)})


def ffn_fwd(x, g, wg, wu, wd, l, tag, dep=None):
    seq, d = x.shape
    f = wg.shape[-1]
    tm = _tile(seq, 1024)
    h = rms_fwd(x, g, f"{tag}_rms", dep=dep)
    up = lambda w, nm: mm(
        h, w, name=nm, dims=NN, grid=(NDEV, seq // tm, 1), nk=1,
        a_spec=pl.BlockSpec((tm, d), lambda j, m, k: (m, 0)),
        b_spec=pl.BlockSpec((None, None, d, f), lambda j, m, k: (j, l, 0, 0)),
        o_spec=pl.BlockSpec((None, tm, f), lambda j, m, k: (j, m, 0)), out_shape=_sd((NDEV, seq, f), ACT))
    a, b = up(wg, f"{tag}_gate"), up(wu, f"{tag}_up")
    t = _tile(seq, 512)
    spec3 = pl.BlockSpec((None, t, f), lambda j, m: (j, m, 0))
    hid = rowwise(lambda a_, b_: jax.nn.silu(a_.astype(F32)) * b_.astype(F32), [a, b], [spec3, spec3],
                  _sd((NDEV, seq, f), ACT), spec3, (NDEV, seq // t), f"{tag}_act")
    tn = _tile(d, 1024)
    out = mm(hid, wd, name=f"{tag}_down", dims=NN, grid=(seq // tm, d // tn, NDEV), nk=NDEV,
             a_spec=pl.BlockSpec((None, tm, f), lambda m, n, k: (k, m, 0)),
             b_spec=pl.BlockSpec((None, None, f, tn), lambda m, n, k: (k, l, 0, n)),
             o_spec=pl.BlockSpec((tm, tn), lambda m, n, k: (m, n)), out_shape=_sd((seq, d)),
             extras=[x], extra_specs=[pl.BlockSpec((tm, tn), lambda m, n, k: (m, n))],
             epilogue=lambda acc, x_: x_ + 0.5 * acc)
    return out, (x, h, a, b, hid)


def ffn_bwd(dout, saved, g, wg, wu, wd, l, tag, dep=None):
    x, h, a, b, hid = saved
    seq, d = x.shape
    f = wg.shape[-1]
    tm = _tile(seq, 1024)
    tk = _tile(seq, 1024)
    dhid = mm(dout, wd, name=f"{tag}_dhid", dims=NT, grid=(NDEV, seq // tm, 1), nk=1,
              a_spec=pl.BlockSpec((tm, d), lambda j, m, k: (m, 0)),
              b_spec=pl.BlockSpec((None, None, f, d), lambda j, m, k: (j, l, 0, 0)),
              o_spec=pl.BlockSpec((None, tm, f), lambda j, m, k: (j, m, 0)), out_shape=_sd((NDEV, seq, f), ACT),
              epilogue=lambda acc: 0.5 * acc, dep=dep)
    tn = _tile(d, 1024)
    dwd = mm(hid, dout, name=f"{tag}_dwd", dims=TN, grid=(NDEV, d // tn, seq // tk), nk=seq // tk,
             a_spec=pl.BlockSpec((None, tk, f), lambda j, n, k: (j, k, 0)),
             b_spec=pl.BlockSpec((tk, tn), lambda j, n, k: (k, n)),
             o_spec=pl.BlockSpec((None, f, tn), lambda j, n, k: (j, 0, n)), out_shape=_sd((NDEV, f, d), COMM),
             epilogue=lambda acc: 0.5 * acc)
    t = _tile(seq, 512)
    spec3 = pl.BlockSpec((None, t, f), lambda j, m: (j, m, 0))

    def act_bwd(dh_, a_, b_):
        dh_, a_, b_ = dh_.astype(F32), a_.astype(F32), b_.astype(F32)
        sig = jax.nn.sigmoid(a_)
        return dh_ * b_ * sig * (1.0 + a_ * (1.0 - sig)), dh_ * a_ * sig

    da, db = rowwise(act_bwd, [dhid, a, b], [spec3] * 3, [_sd((NDEV, seq, f), ACT)] * 2, [spec3] * 2, (NDEV, seq // t),
                     f"{tag}_dact")
    dw = lambda dz, nm: mm(
        h, dz, name=nm, dims=TN, grid=(NDEV, 1, seq // tk), nk=seq // tk,
        a_spec=pl.BlockSpec((tk, d), lambda j, n, k: (k, 0)),
        b_spec=pl.BlockSpec((None, tk, f), lambda j, n, k: (j, k, 0)),
        o_spec=pl.BlockSpec((None, d, f), lambda j, n, k: (j, 0, 0)), out_shape=_sd((NDEV, d, f), COMM))
    dwg, dwu = dw(da, f"{tag}_dwg"), dw(db, f"{tag}_dwu")
    dh_of = lambda dz, w, nm, extras, epi: mm(
        dz, w, name=nm, dims=NT, grid=(seq // tm, d // tn, NDEV), nk=NDEV,
        a_spec=pl.BlockSpec((None, tm, f), lambda m, n, k: (k, m, 0)),
        b_spec=pl.BlockSpec((None, None, tn, f), lambda m, n, k: (k, l, n, 0)),
        o_spec=pl.BlockSpec((tm, tn), lambda m, n, k: (m, n)), out_shape=_sd((seq, d)),
        extras=extras, extra_specs=[pl.BlockSpec((tm, tn), lambda m, n, k: (m, n))] * len(extras), epilogue=epi)
    dh = dh_of(da, wg, f"{tag}_dh_gate", [], None)
    dh = dh_of(db, wu, f"{tag}_dh_up", [dh], lambda acc, prev: acc + prev)
    dx, dg = rms_bwd(x, g, dh, f"{tag}_drms", add=dout)
    return dx, (dg, dwg, dwu, dwd)


def _dense(a, w, l, name, out_dtype=F32, extras=(), epilogue=None, row0=0):
    seq, kdim = a.shape
    kb, n = w.shape[2], w.shape[3]
    nk = kdim // kb
    tm, tn = _tile(seq, 1024), _tile(n, 1024)
    return mm(a, w, name=name, dims=NN, grid=(seq // tm, n // tn, nk), nk=nk,
              a_spec=pl.BlockSpec((tm, kb), lambda m, c, k: (m, k)),
              b_spec=pl.BlockSpec((None, None, kb, tn), lambda m, c, k: (k + row0, l, 0, c)),
              o_spec=pl.BlockSpec((tm, tn), lambda m, c, k: (m, c)), out_shape=_sd((seq, n), out_dtype),
              extras=list(extras), extra_specs=[pl.BlockSpec((tm, tn), lambda m, c, k: (m, c))] * len(extras),
              epilogue=epilogue)


def _dense_dx(dy, w, l, name, extras=(), epilogue=None, dep=None):
    seq, n = dy.shape
    kb = w.shape[2]
    tm = _tile(seq, 1024)
    return mm(dy, w, name=name, dims=NT, grid=(seq // tm, NDEV, 1), nk=1,
              a_spec=pl.BlockSpec((tm, n), lambda m, j, k: (m, 0)),
              b_spec=pl.BlockSpec((None, None, kb, n), lambda m, j, k: (j, l, 0, 0)),
              o_spec=pl.BlockSpec((tm, kb), lambda m, j, k: (m, j)), out_shape=_sd((seq, NDEV * kb)),
              extras=list(extras), extra_specs=[pl.BlockSpec((tm, kb), lambda m, j, k: (m, j))] * len(extras),
              epilogue=epilogue, dep=dep)


def _dense_dw(a, dy, kb, name):
    seq, n = dy.shape
    tk, tn = _tile(seq, 1024), _tile(n, 1024)
    return mm(a, dy, name=name, dims=TN, grid=(NDEV, n // tn, seq // tk), nk=seq // tk,
              a_spec=pl.BlockSpec((tk, kb), lambda j, c, k: (k, j)),
              b_spec=pl.BlockSpec((tk, tn), lambda j, c, k: (k, c)),
              o_spec=pl.BlockSpec((None, kb, tn), lambda j, c, k: (j, 0, c)), out_shape=_sd((NDEV, kb, n), COMM))


def _heads_up(a, w, l, name, out_dtype):
    seq, r = a.shape
    c = w.shape[3]
    tm = _tile(seq, 1024)
    return mm(a, w, name=name, dims=NN, grid=(NDEV, seq // tm, 1), nk=1,
              a_spec=pl.BlockSpec((tm, r), lambda j, m, k: (m, 0)),
              b_spec=pl.BlockSpec((None, None, r, c), lambda j, m, k: (j, l, 0, 0)),
              o_spec=pl.BlockSpec((None, tm, c), lambda j, m, k: (j, m, 0)), out_shape=_sd((NDEV, seq, c), out_dtype))


def _heads_dx(dy, w, l, name):
    _, seq, c = dy.shape
    r = w.shape[2]
    tm = _tile(seq, 1024)
    return mm(dy, w, name=name, dims=NT, grid=(seq // tm, 1, NDEV), nk=NDEV,
              a_spec=pl.BlockSpec((None, tm, c), lambda m, n, k: (k, m, 0)),
              b_spec=pl.BlockSpec((None, None, r, c), lambda m, n, k: (k, l, 0, 0)),
              o_spec=pl.BlockSpec((tm, r), lambda m, n, k: (m, 0)), out_shape=_sd((seq, r)))


def _heads_dw(a, dy, name):
    seq, r = a.shape
    c = dy.shape[2]
    tk = _tile(seq, 1024)
    return mm(a, dy, name=name, dims=TN, grid=(NDEV, 1, seq // tk), nk=seq // tk,
              a_spec=pl.BlockSpec((tk, r), lambda j, n, k: (k, 0)),
              b_spec=pl.BlockSpec((None, tk, c), lambda j, n, k: (j, k, 0)),
              o_spec=pl.BlockSpec((None, r, c), lambda j, n, k: (j, 0, 0)), out_shape=_sd((NDEV, r, c), COMM))


C_FQ, C_FK, C_FV, C_CQ, C_CKV, C_DQ, C_DK, C_DV = range(8)
MAIN_W = 8 * 512
TAIL_W = 128


def split_w_in(w):
    fq, fk, fv, fl, cq, ckv, kr, dq, dk, dv = jnp.split(w, [512, 1024, 1536, 1540, 2052, 2564, 2628, 3140, 3652], axis=-1)
    main = jnp.concatenate([fq, fk, fv, cq, ckv, dq, dk, dv], axis=-1)
    pad = jnp.zeros((*w.shape[:-1], TAIL_W - 68), w.dtype)
    return main, jnp.concatenate([kr, fl, pad], axis=-1)


def merge_w_in(main, tail):
    fq, fk, fv, cq, ckv, dq, dk, dv = jnp.split(main, 8, axis=-1)
    return jnp.concatenate([fq, fk, fv, tail[..., 64:68], cq, ckv, tail[..., 0:64], dq, dk, dv], axis=-1)


def mixer_fwd(x, p, l, consts, dep=None):
    seq, d = x.shape
    nfox, nmla, ndil = 4, 8, 4
    cos_m, sin_m, cos_p, sin_p = consts
    t = _tile(seq, 512)
    tf = _tile(seq, 512)
    h = rms_fwd(x, p["mix_norm"], f"mix{l}_rms", dep=dep)
    proj = _dense(h, p["w_in_main"], 0, f"mix{l}_proj")
    tail = _dense(h, p["w_in_tail"], 0, f"mix{l}_tail")

    nb = seq // HEAD
    z = tail[:, 64:68].T.reshape(nfox * nb, HEAD)
    bias_rows = jnp.repeat(p["fox_forget_bias"], nb).reshape(nfox * nb, 1)
    cum = fox_gate_fwd(z, bias_rows, nb, f"mix{l}_gate").reshape(nfox, seq)
    cum2 = (cum.reshape(nfox, seq, 1), cum.reshape(nfox, 1, seq))
    fox_qkv = ((proj, "cols", HEAD, C_FQ * 4), (proj, "cols", HEAD, C_FK * 4), (proj, "cols", HEAD, C_FV * 4))
    out_a, lse_a = flash_fwd(*fox_qkv, cum2, n_heads=nfox, seq=seq, t=tf, scale=HEAD ** -0.5, name=f"mix{l}_fox")

    cq = rms_fwd(proj, p["mla_q_norm"], f"mix{l}_cq", col=C_CQ, width=512)
    ckv = rms_fwd(proj, p["mla_kv_norm"], f"mix{l}_ckv", col=C_CKV, width=512)
    q_raw = _heads_up(cq, p["mla_w_uq"], 0, f"mix{l}_uq", F32)
    kv = _heads_up(ckv, p["mla_w_ukv"], 0, f"mix{l}_ukv", ACT)

    def mla_prep(q_, kv_, tail_, cos_, sin_, q_out, k_out):
        perm = _swap_matrix(MLA_ROPE)
        c, s = cos_[...], sin_[...]
        q_out[:, 0:HEAD] = q_[:, 0:HEAD].astype(q_out.dtype)
        q_out[:, HEAD:MLA_QK] = _rope(q_[:, HEAD:MLA_QK], c, s, perm).astype(q_out.dtype)
        k_out[:, 0:HEAD] = kv_[:, 0:HEAD].astype(k_out.dtype)
        k_out[:, HEAD:MLA_QK] = _rope(tail_[:, 0:MLA_ROPE], c, s, perm).astype(k_out.dtype)

    hs = lambda w: pl.BlockSpec((None, t, w), lambda hh, m: (hh, m, 0))
    rs = lambda w: pl.BlockSpec((t, w), lambda hh, m: (m, 0))
    q_b, k_b = pl.pallas_call(
        lambda q_, kv_, tl_, c_, s_, qo, ko: mla_prep(q_[...], kv_[...], tl_[...], c_, s_, qo, ko),
        name=f"mix{l}_mla_prep", grid=(nmla, seq // t),
        in_specs=[hs(MLA_QK), hs(2 * HEAD), rs(TAIL_W), rs(MLA_ROPE), rs(MLA_ROPE)], out_specs=[hs(MLA_QK), hs(MLA_QK)],
        out_shape=[_sd((nmla, seq, MLA_QK), ACT)] * 2, compiler_params=_cparams(2),
    )(q_raw, kv, tail, cos_m, sin_m)
    mla_qkv = ((q_b, "heads", MLA_QK, 0), (k_b, "heads", MLA_QK, 0), (kv, "heads", HEAD, 1))
    out_b, lse_b = flash_fwd(*mla_qkv, None, n_heads=nmla, seq=seq, t=tf, scale=MLA_QK ** -0.5, name=f"mix{l}_mla")

    wd_ = ndil * HEAD

    def dil_prep(q_, k_, c_, s_):
        perm = _pad_perm(PARTIAL_ROPE)
        rot = lambda a: jnp.concatenate(
            [_rope(a[:, i * HEAD:(i + 1) * HEAD], c_, s_, perm) for i in range(ndil)], axis=1)
        return rot(q_), rot(k_)

    dq_r, dk_r = rowwise(dil_prep, [proj, proj, cos_p, sin_p],
                         [_rows(t, wd_, C_DQ), _rows(t, wd_, C_DK), _rows(t, HEAD), _rows(t, HEAD)],
                         [_sd((seq, wd_), ACT)] * 2, [_rows(t, wd_)] * 2, (seq // t,), f"mix{l}_dil_prep")
    dv = proj[:, C_DV * 512:(C_DV + 1) * 512]
    branches = [dil_fwd(dq_r, dk_r, dv, seq=seq, dil=dl, n_heads=ndil, name=f"mix{l}_dil{dl}") for dl in DIL_BRANCHES]

    def mix(o1, o2, o3, l1, l2, l3):
        m = jnp.maximum(jnp.maximum(l1, l2), l3)
        e1, e2, e3 = jnp.exp(l1 - m), jnp.exp(l2 - m), jnp.exp(l3 - m)
        return (e1 * o1 + e2 * o2 + e3 * o3) / (e1 + e2 + e3)

    out_c = rowwise(mix, [b[0] for b in branches] + [b[1] for b in branches], [_rows(t, wd_)] * 6, _sd((seq, wd_)),
                    _rows(t, wd_), (seq // t,), f"mix{l}_dil_mix")

    mixed = jnp.concatenate([out_a, out_b, out_c], axis=1)
    out = _dense(mixed, p["w_out"], 0, f"mix{l}_out", extras=[x], epilogue=lambda acc, x_: x_ + acc)
    saved = dict(x=x, h=h, proj=proj, tail=tail, z=z, bias_rows=bias_rows, cum2=cum2, out_a=out_a, lse_a=lse_a, cq=cq,
                 ckv=ckv, q_raw=q_raw, kv=kv, q_b=q_b, k_b=k_b, out_b=out_b, lse_b=lse_b, dq_r=dq_r, dk_r=dk_r, dv=dv,
                 branches=branches, out_c=out_c, mixed=mixed)
    return out, saved


def _pad_perm(n):
    i = lax.broadcasted_iota(jnp.int32, (HEAD, HEAD), 0)
    j = lax.broadcasted_iota(jnp.int32, (HEAD, HEAD), 1)
    inside = jnp.logical_and(i < n, j < n)
    return jnp.where(jnp.logical_and(inside, ((i + n // 2) % n) == j), 1.0, 0.0).astype(F32)


def mixer_bwd(dout, sv, p, l, consts, dep=None):
    seq, d = dout.shape
    nfox, nmla, ndil = 4, 8, 4
    cos_m, sin_m, cos_p, sin_p = consts
    t = _tile(seq, 512)
    tf = _tile(seq, 512)
    nb = seq // HEAD
    proj, tail = sv["proj"], sv["tail"]
    dmixed = _dense_dx(dout, p["w_out"], 0, f"mix{l}_dmixed", dep=dep)
    dw_out = _dense_dw(sv["mixed"], dout, d // NDEV, f"mix{l}_dw_out")
    do_a, do_b, do_c = dmixed[:, 0:512], dmixed[:, 512:1536], dmixed[:, 1536:2048]

    fox_qkv = ((proj, "cols", HEAD, C_FQ * 4), (proj, "cols", HEAD, C_FK * 4), (proj, "cols", HEAD, C_FV * 4))
    delta_a = attn_delta(do_a, sv["out_a"], n_heads=nfox, seq=seq, name=f"mix{l}_fox_delta")
    row = lambda a: a.reshape(a.shape[0], 1, seq)
    dfq, dcum_q = flash_bwd_dq(*fox_qkv, do_a, sv["lse_a"], delta_a, sv["cum2"], n_heads=nfox, seq=seq, t=tf,
                               scale=HEAD ** -0.5, name=f"mix{l}_fox_dq")
    dfk, dfv, dcum_k = flash_bwd_dkv(*fox_qkv, do_a, row(sv["lse_a"]), row(delta_a), sv["cum2"], n_heads=nfox, seq=seq,
                                     t=tf, scale=HEAD ** -0.5, name=f"mix{l}_fox_dkv")
    dz, dbias = fox_gate_bwd(sv["z"], sv["bias_rows"], dcum_q.reshape(nfox * nb, HEAD), dcum_k.reshape(nfox * nb, HEAD),
                             nb, f"mix{l}_dgate")
    d_fox_bias = dbias.reshape(nfox, nb, HEAD)[:, 0, 0]
    dfl = dz.reshape(nfox, seq).T
    unheads = lambda a: a.transpose(1, 0, 2).reshape(seq, -1)

    mla_qkv = ((sv["q_b"], "heads", MLA_QK, 0), (sv["k_b"], "heads", MLA_QK, 0), (sv["kv"], "heads", HEAD, 1))
    delta_b = attn_delta(do_b, sv["out_b"], n_heads=nmla, seq=seq, name=f"mix{l}_mla_delta")
    dq_b = flash_bwd_dq(*mla_qkv, do_b, sv["lse_b"], delta_b, None, n_heads=nmla, seq=seq, t=tf, scale=MLA_QK ** -0.5,
                        name=f"mix{l}_mla_dq")
    dk_b, dv_b = flash_bwd_dkv(*mla_qkv, do_b, row(sv["lse_b"]), row(delta_b), None, n_heads=nmla, seq=seq, t=tf,
                               scale=MLA_QK ** -0.5, name=f"mix{l}_mla_dkv")

    def mla_unprep(dq_, dk_, dv_, cos_, sin_, dq_out, dkv_out, dkr_out):
        perm = _swap_matrix(MLA_ROPE)
        c, s = cos_[...], sin_[...]
        dq_out[:, 0:HEAD] = dq_[:, 0:HEAD].astype(dq_out.dtype)
        dq_out[:, HEAD:MLA_QK] = _rope_t(dq_[:, HEAD:MLA_QK], c, s, perm).astype(dq_out.dtype)
        dkv_out[:, 0:HEAD] = dk_[:, 0:HEAD].astype(dkv_out.dtype)
        dkv_out[:, HEAD:2 * HEAD] = dv_.astype(dkv_out.dtype)
        dkr = _rope_t(dk_[:, HEAD:MLA_QK], c, s, perm)
        first = pl.program_id(1) == 0

        @pl.when(first)
        def _():
            dkr_out[...] = dkr

        @pl.when(jnp.logical_not(first))
        def _():
            dkr_out[...] += dkr

    hs = lambda w: pl.BlockSpec((None, t, w), lambda m, hh: (hh, m, 0))
    rs = lambda w: pl.BlockSpec((t, w), lambda m, hh: (m, 0))
    dq_raw, dkv, dk_r = pl.pallas_call(
        lambda a, b, c, cs, sn, o1, o2, o3: mla_unprep(a[...], b[...], c[...], cs, sn, o1, o2, o3),
        name=f"mix{l}_mla_unprep", grid=(seq // t, nmla),
        in_specs=[hs(MLA_QK), hs(MLA_QK), hs(HEAD), rs(MLA_ROPE), rs(MLA_ROPE)],
        out_specs=[hs(MLA_QK), hs(2 * HEAD), rs(MLA_ROPE)],
        out_shape=[_sd((nmla, seq, MLA_QK), ACT), _sd((nmla, seq, 2 * HEAD), ACT), _sd((seq, MLA_ROPE))],
        compiler_params=_cparams(2),
    )(dq_b, dk_b, dv_b, cos_m, sin_m)
    dcq_n = _heads_dx(dq_raw, p["mla_w_uq"], 0, f"mix{l}_dcq")
    dckv_n = _heads_dx(dkv, p["mla_w_ukv"], 0, f"mix{l}_dckv")
    dw_uq = _heads_dw(sv["cq"], dq_raw, f"mix{l}_dw_uq")
    dw_ukv = _heads_dw(sv["ckv"], dkv, f"mix{l}_dw_ukv")
    dcq, dg_q = rms_bwd(proj, p["mla_q_norm"], dcq_n, f"mix{l}_dcq_rms", col=C_CQ, width=512)
    dckv, dg_kv = rms_bwd(proj, p["mla_kv_norm"], dckv_n, f"mix{l}_dckv_rms", col=C_CKV, width=512)

    wd_ = ndil * HEAD
    outs = [b[0] for b in sv["branches"]]
    lses = [b[1] for b in sv["branches"]]

    def mix_bwd(do_, o1, o2, o3, l1, l2, l3):
        m = jnp.maximum(jnp.maximum(l1, l2), l3)
        e1, e2, e3 = jnp.exp(l1 - m), jnp.exp(l2 - m), jnp.exp(l3 - m)
        z_ = e1 + e2 + e3
        w1, w2, w3 = e1 / z_, e2 / z_, e3 / z_
        out = w1 * o1 + w2 * o2 + w3 * o3
        return (w1 * do_, w2 * do_, w3 * do_, do_ * w1 * (o1 - out), do_ * w2 * (o2 - out), do_ * w3 * (o3 - out))

    mb = rowwise(mix_bwd, [do_c] + outs + lses, [_rows(t, wd_)] * 7, [_sd((seq, wd_))] * 6, [_rows(t, wd_)] * 6,
                 (seq // t,), f"mix{l}_dil_dmix")
    grads = [dil_bwd(sv["dq_r"], sv["dk_r"], sv["dv"], outs[i], lses[i], mb[i], mb[3 + i], seq=seq, dil=dl,
                     n_heads=ndil, name=f"mix{l}_dil{dl}_bwd") for i, dl in enumerate(DIL_BRANCHES)]

    def dil_unprep(q1, q2, q3, k1, k2, k3, v1, v2, v3, c_, s_):
        perm = _pad_perm(PARTIAL_ROPE)
        rot_t = lambda a: jnp.concatenate(
            [_rope_t(a[:, i * HEAD:(i + 1) * HEAD], c_, s_, perm) for i in range(ndil)], axis=1)
        return rot_t(q1 + q2 + q3), rot_t(k1 + k2 + k3), v1 + v2 + v3

    ddq, ddk, ddv = rowwise(dil_unprep, [g[0] for g in grads] + [g[1] for g in grads] + [g[2] for g in grads] + [cos_p, sin_p],
                            [_rows(t, wd_)] * 9 + [_rows(t, HEAD)] * 2, [_sd((seq, wd_))] * 3, [_rows(t, wd_)] * 3,
                            (seq // t,), f"mix{l}_dil_unprep")

    dproj = jnp.concatenate([unheads(dfq), unheads(dfk), unheads(dfv), dcq, dckv, ddq, ddk, ddv], axis=1).astype(ACT)
    dtail = jnp.concatenate([dk_r, dfl, jnp.zeros((seq, TAIL_W - 68), F32)], axis=1)
    dh = _dense_dx(dproj, p["w_in_main"], 0, f"mix{l}_dh_main")
    dh = _dense_dx(dtail, p["w_in_tail"], 0, f"mix{l}_dh_tail", extras=[dh], epilogue=lambda acc, prev: acc + prev)
    dw_main = _dense_dw(sv["h"], dproj, d // NDEV, f"mix{l}_dw_in_main")
    dw_tail = _dense_dw(sv["h"], dtail, d // NDEV, f"mix{l}_dw_in_tail")
    dx, dg_mix = rms_bwd(sv["x"], p["mix_norm"], dh, f"mix{l}_drms", add=dout)
    return dx, dict(mix_norm=dg_mix, w_in_main=dw_main, w_in_tail=dw_tail, fox_forget_bias=d_fox_bias, mla_q_norm=dg_q,
                    mla_kv_norm=dg_kv, mla_w_uq=dw_uq, mla_w_ukv=dw_ukv, w_out=dw_out)


def loss_head(x, g, target, name):
    seq, d = x.shape
    t = _tile(seq, 512)

    def fn(x_, g_, tgt):
        err = _rms(x_, g_) - tgt
        part = 0.5 * jnp.sum(jnp.mean(err * err, axis=-1, keepdims=True), axis=0, keepdims=True)
        dx, dg = _rms_bwd(x_, g_, err / d)
        return jnp.broadcast_to(part, (1, HEAD)), dx, dg

    return rowwise(fn, [x, g.reshape(1, d), target], [_rows(t, d), _whole((1, d)), _rows(t, d)],
                   [_sd((1, HEAD)), _sd((seq, d)), _sd((1, d))], [_whole((1, HEAD)), _rows(t, d), _whole((1, d))],
                   (seq // t,), name, acc={0: (0,), 2: (0,)})


BIG = ("ffn1_w_gate", "ffn1_w_up", "ffn1_w_down", "w_in", "mla_w_uq", "mla_w_ukv", "w_out", "ffn2_w_gate", "ffn2_w_up",
       "ffn2_w_down")
GROUPS = {
    "ffn1": ("ffn1_w_gate", "ffn1_w_up", "ffn1_w_down"),
    "mix": ("w_in_main", "w_in_tail", "mla_w_uq", "mla_w_ukv", "w_out"),
    "ffn2": ("ffn2_w_gate", "ffn2_w_up", "ffn2_w_down"),
}
PREFETCH = 2
SMALL_D = ("ffn1_norm", "mix_norm", "ffn2_norm")
WEIGHTS = ("ffn1_norm", "ffn1_w_gate", "ffn1_w_up", "ffn1_w_down", "mix_norm", "w_in", "fox_forget_bias", "mla_q_norm",
           "mla_kv_norm", "mla_w_uq", "mla_w_ukv", "w_out", "ffn2_norm", "ffn2_w_gate", "ffn2_w_up", "ffn2_w_down",
           "final_norm")


def pack_small(vals, depth, d):
    rows = [vals[n].reshape(depth, d) for n in SMALL_D]
    rows.append(vals["final_norm"].reshape(1, d))
    qk = jnp.concatenate([vals["mla_q_norm"].reshape(-1), vals["mla_kv_norm"].reshape(-1)])
    rows.append(jnp.pad(qk, (0, -qk.shape[0] % d)).reshape(-1, d))
    last = jnp.concatenate([vals["fox_forget_bias"].reshape(-1), vals["loss"].reshape(-1)])
    rows.append(jnp.pad(last, (0, d - last.shape[0])).reshape(1, d))
    out = jnp.concatenate(rows, axis=0)
    return jnp.pad(out, ((0, -out.shape[0] % 8), (0, 0)))


def unpack_small(a, depth, d, rank):
    out, r = {}, 0
    for n in SMALL_D:
        out[n] = a[r:r + depth]
        r += depth
    out["final_norm"] = a[r]
    r += 1
    n_qk = -(-2 * depth * rank // d)
    qk = a[r:r + n_qk].reshape(-1)[:2 * depth * rank].reshape(2, depth, rank)
    out["mla_q_norm"], out["mla_kv_norm"] = qk[0], qk[1]
    r += n_qk
    out["fox_forget_bias"] = a[r, :depth * 4].reshape(depth, 4)
    out["loss"] = a[r, depth * 4]
    return out


def step(x, target, w, m, v):
    depth = w["ffn1_norm"].shape[0]
    seq, d = x.shape[1], x.shape[2]
    rank = w["mla_q_norm"].shape[1]
    x = x.reshape(seq, d)
    target = target.reshape(seq, d)

    local = {n: w[n].astype(COMM) for n in BIG if n != "w_in"}
    local["w_in_main"], local["w_in_tail"] = [a.astype(COMM) for a in split_w_in(w["w_in"])]
    consts = (*rope_tables(seq, MLA_ROPE), *[jnp.pad(a, ((0, 0), (0, HEAD - PARTIAL_ROPE)), constant_values=c)
                                             for a, c in zip(rope_tables(seq, PARTIAL_ROPE), (1.0, 0.0))])
    order = [(l, k) for l in range(depth) for k in GROUPS]
    small_of = lambda l: {n: w[n][l] for n in ("mix_norm", "fox_forget_bias", "mla_q_norm", "mla_kv_norm")}

    handles, tokens = {}, {}

    def launch(i, dep):
        l, k = order[i]
        xs = [local[n][l:l + 1] for n in GROUPS[k]]
        if dep is not None:
            xs = lax.optimization_barrier((xs, dep))[0]
        handles[i], tokens[i] = exchange_start(xs, False, f"gather_start_{k}{l}")

    launch(0, None)
    launched = 1
    gathered, saved = {}, {}
    for i, (l, k) in enumerate(order):
        wts = dict(zip(GROUPS[k], exchange_wait(handles[i], x, f"gather_wait_{k}{l}")))
        gathered[l, k] = wts
        tok = None
        while launched < min(len(order), i + 1 + PREFETCH):
            launch(launched, (x, wts[GROUPS[k][0]]))
            tok = tokens[launched] if tok is None else tok + tokens[launched]
            launched += 1
        if k == "mix":
            x, saved[l, k] = mixer_fwd(x, {**wts, **small_of(l)}, l, consts, dep=tok)
        else:
            x, saved[l, k] = ffn_fwd(x, w[f"{k}_norm"][l], wts[f"{k}_w_gate"], wts[f"{k}_w_up"], wts[f"{k}_w_down"], 0,
                                     f"{k}_{l}", dep=tok)
    loss, dx, d_final = loss_head(x, w["final_norm"], target, "loss_head")

    small = {n: [None] * depth for n in SMALL_D + ("mla_q_norm", "mla_kv_norm", "fox_forget_bias")}
    pending, tok = [], None
    for l, k in reversed(order):
        wts = gathered[l, k]
        if k == "mix":
            dx, gm = mixer_bwd(dx, saved[l, k], {**wts, **small_of(l)}, l, consts, dep=tok)
        else:
            dx, (dg, dwg, dwu, dwd) = ffn_bwd(dx, saved[l, k], w[f"{k}_norm"][l], wts[f"{k}_w_gate"], wts[f"{k}_w_up"],
                                              wts[f"{k}_w_down"], 0, f"{k}_{l}", dep=tok)
            gm = {f"{k}_norm": dg, f"{k}_w_gate": dwg, f"{k}_w_up": dwu, f"{k}_w_down": dwd}
        for n in small:
            if n in gm:
                small[n][l] = gm[n]
        handle, tok = exchange_start([gm[n] for n in GROUPS[k]], True, f"scatter_start_{k}{l}")
        pending.append((l, k, handle))
    big = {n: [None] * depth for n in local}
    for l, k, handle in pending:
        for n, a in zip(GROUPS[k], exchange_wait(handle, dx, f"scatter_wait_{k}{l}")):
            big[n][l] = a

    out = {"grad_x": dx.reshape(1, seq, d)}

    def update(name, parts, shape):
        flat = lambda a: a.reshape(-1, shape[-1])
        res = adamw(parts, flat(w[name]), flat(m[name]), flat(v[name]), f"adamw_{name}")
        for kind, r in zip(("grad", "delta", "new_m", "new_v"), res):
            out[f"{kind}_{name}"] = r.reshape(shape)

    for n in BIG:
        if n != "w_in":
            update(n, [a.reshape(NDEV, -1, a.shape[-1]) for a in big[n]], w[n].shape)
    g_in = [merge_w_in(sum_parts(big["w_in_main"][l].reshape(NDEV, -1, MAIN_W), f"sum_w_in_main{l}"),
                       sum_parts(big["w_in_tail"][l].reshape(NDEV, -1, TAIL_W), f"sum_w_in_tail{l}"))[None]
            for l in range(depth)]
    update("w_in", g_in, w["w_in"].shape)

    part = {n: jnp.stack(g).reshape(depth, -1) for n, g in small.items()}
    part["final_norm"], part["loss"] = d_final, loss[0, 0:1]
    parts = exchange(pack_small(part, depth, d), False, "gather_small")
    zero = jnp.zeros((1,), F32)
    packed = [pack_small({**{n: a[n] for n in part if n != "loss"}, "loss": zero}, depth, d) for a in (w, m, v)]
    res = [unpack_small(r, depth, d, rank) for r in adamw([parts], *packed, "adamw_small")]
    out["loss"] = res[0]["loss"]
    for n in small.keys() | {"final_norm"}:
        for kind, r in zip(("grad", "delta", "new_m", "new_v"), res):
            out[f"{kind}_{n}"] = r[n].reshape(w[n].shape)
    return out


def kernel(x, ffn1_norm, ffn1_w_gate, ffn1_w_up, ffn1_w_down, mix_norm, w_in, fox_forget_bias, mla_q_norm, mla_kv_norm, mla_w_uq, mla_w_ukv, w_out, ffn2_norm, ffn2_w_gate, ffn2_w_up, ffn2_w_down, final_norm, loss_target, m_ffn1_norm, m_ffn1_w_gate, m_ffn1_w_up, m_ffn1_w_down, m_mix_norm, m_w_in, m_fox_forget_bias, m_mla_q_norm, m_mla_kv_norm, m_mla_w_uq, m_mla_w_ukv, m_w_out, m_ffn2_norm, m_ffn2_w_gate, m_ffn2_w_up, m_ffn2_w_down, m_final_norm, v_ffn1_norm, v_ffn1_w_gate, v_ffn1_w_up, v_ffn1_w_down, v_mix_norm, v_w_in, v_fox_forget_bias, v_mla_q_norm, v_mla_kv_norm, v_mla_w_uq, v_mla_w_ukv, v_w_out, v_ffn2_norm, v_ffn2_w_gate, v_ffn2_w_up, v_ffn2_w_down, v_final_norm):
    args = locals()
    w = {n: args[n] for n in WEIGHTS}
    m = {n: args["m_" + n] for n in WEIGHTS}
    v = {n: args["v_" + n] for n in WEIGHTS}
    out = step(x, loss_target, w, m, v)
    return (out["loss"], out["grad_x"], *[out["grad_" + n] for n in WEIGHTS], *[out["delta_" + n] for n in WEIGHTS],
            *[out["new_m_" + n] for n in WEIGHTS], *[out["new_v_" + n] for n in WEIGHTS])
```

```python
import functools

import jax
import jax.numpy as jnp
from jax import lax
from jax.experimental import pallas as pl
from jax.experimental.pallas import tpu as pltpu

F32 = jnp.float32
MXU = jnp.bfloat16
ACT = jnp.bfloat16
COMM = jnp.bfloat16
HI = lax.Precision.HIGHEST
NN = (((1,), (0,)), ((), ()))
NT = (((1,), (1,)), ((), ()))
TN = (((0,), (0,)), ((), ()))

NDEV = 8
HEAD = 128
EPS = 1e-6
ROPE_THETA = 500000.0
PARTIAL_ROPE = HEAD // 4
MLA_ROPE = 64
MLA_QK = HEAD + MLA_ROPE
DIL_BRANCHES = (1, 4, 16)
NEG = -1e30
VMEM_LIMIT = 48 * 1024 * 1024

ADAMW_STEP_BYTES = 12 * 1024 * 1024

ADAM_LR, ADAM_B1, ADAM_B2, ADAM_EPS, ADAM_WD, ADAM_STEP = 0.001, 0.9, 0.999, 1e-08, 0.01, 10


def _cparams(n_axes):
    return pltpu.CompilerParams(dimension_semantics=("arbitrary",) * n_axes, vmem_limit_bytes=VMEM_LIMIT)


def _tile(n, t):
    t = min(n, t)
    assert n % t == 0, (n, t)
    return t


def _dep_spec(dep):
    nd = dep.ndim
    return pl.BlockSpec(dep.shape, lambda *_: (0,) * nd)


def mm(a, b, *, name, dims, grid, a_spec, b_spec, o_spec, out_shape, nk, extras=(), extra_specs=(), epilogue=None, dep=None):
    n_ex = len(extras)
    kaxis = len(grid) - 1
    if dep is not None:
        extras, extra_specs = [*extras, dep], [*extra_specs, _dep_spec(dep)]
    n_more = len(extras)

    def body(a_ref, b_ref, *rest):
        ex, o_ref = rest[:n_ex], rest[n_more]
        part = lax.dot_general(a_ref[...].astype(MXU), b_ref[...].astype(MXU), dims, preferred_element_type=F32)

        def finish(acc):
            res = acc if epilogue is None else epilogue(acc, *[e[...] for e in ex])
            o_ref[...] = res.astype(o_ref.dtype)

        if nk == 1:
            finish(part)
        else:
            acc_ref = rest[n_more + 1]
            k = pl.program_id(kaxis)

            @pl.when(k == 0)
            def _():
                acc_ref[...] = part

            @pl.when(k > 0)
            def _():
                acc_ref[...] += part

            @pl.when(k == nk - 1)
            def _():
                finish(acc_ref[...])

    acc_shape = tuple(d for d in o_spec.block_shape if d is not None)
    return pl.pallas_call(
        body, name=name, grid=grid, in_specs=[a_spec, b_spec, *extra_specs], out_specs=o_spec, out_shape=out_shape,
        scratch_shapes=[] if nk == 1 else [pltpu.VMEM(acc_shape, F32)], compiler_params=_cparams(len(grid)),
    )(a, b, *extras)


def rowwise(fn, ins, in_specs, outs, out_specs, grid, name, acc=None, dep=None):
    acc = acc or {}
    n_in = len(ins)
    if dep is not None:
        ins, in_specs = [*ins, dep], [*in_specs, _dep_spec(dep)]
    n_all = len(ins)

    def body(*refs):
        vals = fn(*[r[...] for r in refs[:n_in]])
        if not isinstance(vals, (tuple, list)):
            vals = (vals,)
        for i, (r, v) in enumerate(zip(refs[n_all:], vals)):
            if i in acc:
                first = functools.reduce(jnp.logical_and, [pl.program_id(ax) == 0 for ax in acc[i]])

                @pl.when(first)
                def _(r=r, v=v):
                    r[...] = v.astype(r.dtype)

                @pl.when(jnp.logical_not(first))
                def _(r=r, v=v):
                    r[...] += v.astype(r.dtype)
            else:
                r[...] = v.astype(r.dtype)

    return pl.pallas_call(
        body, name=name, grid=grid, in_specs=in_specs, out_specs=out_specs, out_shape=outs,
        compiler_params=_cparams(len(grid)),
    )(*ins)


def _rows(t, c, col=0):
    return pl.BlockSpec((t, c), lambda m, col=col: (m, col))


def _whole(shape):
    nd = len(shape)
    return pl.BlockSpec(shape, lambda *_: (0,) * nd)


def _rms(x, g):
    x = x.astype(F32)
    return x * lax.rsqrt(jnp.mean(x * x, axis=-1, keepdims=True) + EPS) * g


def _rms_bwd(x, g, dy):
    x = x.astype(F32)
    dy = dy.astype(F32)
    r = lax.rsqrt(jnp.mean(x * x, axis=-1, keepdims=True) + EPS)
    xh = x * r
    dg = jnp.sum(dy * xh, axis=0, keepdims=True)
    dxh = dy * g
    dx = r * (dxh - xh * jnp.mean(dxh * xh, axis=-1, keepdims=True))
    return dx, dg


def _swap_matrix(n):
    i = lax.broadcasted_iota(jnp.int32, (n, n), 0)
    j = lax.broadcasted_iota(jnp.int32, (n, n), 1)
    return jnp.where(((i + n // 2) % n) == j, 1.0, 0.0).astype(F32)


def _rope(x, cos, sin_signed, perm):
    return x * cos + jnp.dot(x, perm, precision=HI, preferred_element_type=F32) * sin_signed


def _rope_t(dy, cos, sin_signed, perm):
    return dy * cos + jnp.dot(dy * sin_signed, perm, precision=HI, preferred_element_type=F32)


def rope_tables(seq, dim):
    inv = 1.0 / (ROPE_THETA ** (jnp.arange(0, dim, 2, dtype=F32) / dim))
    ang = jnp.arange(seq, dtype=F32)[:, None] * inv[None, :]
    cos, sin = jnp.cos(ang), jnp.sin(ang)
    return jnp.concatenate([cos, cos], axis=1), jnp.concatenate([-sin, sin], axis=1)


HP = 2


def _hspec(arr_kind, t, w, off, seq_of):
    if arr_kind == "cols":
        assert off % HP == 0
        return pl.BlockSpec((t, HP * w), lambda h, i, j: (seq_of(i, j), off // HP + h))
    return pl.BlockSpec((HP, t, w), lambda h, i, j: (h, seq_of(i, j), off))


def _head(ref, arr_kind, hh, w):
    return ref[:, hh * w:(hh + 1) * w] if arr_kind == "cols" else ref[hh]


def _colspec(t, seq_of):
    return pl.BlockSpec((HP, t, 1), lambda h, i, j: (h, seq_of(i, j), 0))


def _rowspec(t, seq_of):
    return pl.BlockSpec((HP, 1, t), lambda h, i, j: (h, 0, seq_of(i, j)))


def _causal(s, qi, kj, t, transposed=False):
    a = lax.broadcasted_iota(jnp.int32, (t, t), 0)
    b = lax.broadcasted_iota(jnp.int32, (t, t), 1)
    keep = (kj * t + a <= qi * t + b) if transposed else (kj * t + b <= qi * t + a)
    return jnp.where(keep, s, NEG)


def flash_fwd(q, k, v, cum, *, n_heads, seq, t, scale, name):
    nb = seq // t
    qs, ks = (lambda i, j: i), (lambda i, j: jnp.minimum(j, i))
    ins = [q[0], k[0], v[0]]
    specs = [_hspec(q[1], t, q[2], q[3], qs), _hspec(k[1], t, k[2], k[3], ks), _hspec(v[1], t, v[2], v[3], ks)]
    if cum is not None:
        ins += [cum[0], cum[1]]
        specs += [_colspec(t, qs), _rowspec(t, ks)]

    def body(*refs):
        q_ref, k_ref, v_ref = refs[:3]
        o_ref, lse_ref, m_s, l_s, acc_s = refs[-5:]
        i, j = pl.program_id(1), pl.program_id(2)

        @pl.when(j == 0)
        def _():
            m_s[...] = jnp.full(m_s.shape, NEG, F32)
            l_s[...] = jnp.zeros(l_s.shape, F32)
            acc_s[...] = jnp.zeros(acc_s.shape, F32)

        def block(masked):
            new = []
            for hh in range(HP):
                qb = _head(q_ref, q[1], hh, q[2]).astype(MXU)
                kb = _head(k_ref, k[1], hh, k[2]).astype(MXU)
                s = lax.dot_general(qb, kb, NT, preferred_element_type=F32) * scale
                if cum is not None:
                    s = s + (refs[3][hh] - refs[4][hh])
                if masked:
                    s = _causal(s, i, j, t)
                m_old = m_s[hh]
                m_new = jnp.maximum(m_old, jnp.max(s, axis=1, keepdims=True))
                alpha = jnp.exp(m_old - m_new)
                p = jnp.exp(s - m_new)
                vb = _head(v_ref, v[1], hh, v[2]).astype(MXU)
                pv = jnp.dot(p.astype(MXU), vb, preferred_element_type=F32)
                new.append((m_new, alpha, alpha * l_s[hh] + jnp.sum(p, axis=1, keepdims=True), pv))
            for hh, (m_new, alpha, l_new, pv) in enumerate(new):
                acc_s[hh] = alpha * acc_s[hh] + pv
                l_s[hh] = l_new
                m_s[hh] = m_new

        @pl.when(j < i)
        def _():
            block(False)

        @pl.when(j == i)
        def _():
            block(True)

        @pl.when(j == nb - 1)
        def _():
            for hh in range(HP):
                o_ref[:, hh * HEAD:(hh + 1) * HEAD] = (acc_s[hh] / l_s[hh]).astype(o_ref.dtype)
                lse_ref[hh] = m_s[hh] + jnp.log(l_s[hh])

    return pl.pallas_call(
        body, name=name, grid=(n_heads // HP, nb, nb), in_specs=specs,
        out_specs=[pl.BlockSpec((t, HP * HEAD), lambda h, i, j: (i, h)), _colspec(t, qs)],
        out_shape=[jax.ShapeDtypeStruct((seq, n_heads * HEAD), F32), jax.ShapeDtypeStruct((n_heads, seq, 1), F32)],
        scratch_shapes=[pltpu.VMEM((HP, t, 1), F32), pltpu.VMEM((HP, t, 1), F32), pltpu.VMEM((HP, t, HEAD), F32)],
        compiler_params=_cparams(3),
    )(*ins)


def flash_bwd_dq(q, k, v, do, lse, delta, cum, *, n_heads, seq, t, scale, name):
    nb = seq // t
    qs, ks = (lambda i, j: i), (lambda i, j: jnp.minimum(j, i))
    ins = [q[0], k[0], v[0], do, lse, delta]
    specs = [_hspec(q[1], t, q[2], q[3], qs), _hspec(k[1], t, k[2], k[3], ks), _hspec(v[1], t, v[2], v[3], ks),
             _hspec("cols", t, HEAD, 0, qs), _colspec(t, qs), _colspec(t, qs)]
    if cum is not None:
        ins += [cum[0], cum[1]]
        specs += [_colspec(t, qs), _rowspec(t, ks)]
    wq = q[2]
    n_out = 1 if cum is None else 2

    def body(*refs):
        q_ref, k_ref, v_ref, do_ref, lse_ref, dl_ref = refs[:6]
        outs = refs[-2 * n_out:-n_out]
        accs = refs[-n_out:]
        i, j = pl.program_id(1), pl.program_id(2)

        @pl.when(j == 0)
        def _():
            for a in accs:
                a[...] = jnp.zeros(a.shape, F32)

        def block(masked):
            for hh in range(HP):
                kb = _head(k_ref, k[1], hh, k[2]).astype(MXU)
                s = lax.dot_general(_head(q_ref, q[1], hh, q[2]).astype(MXU), kb, NT, preferred_element_type=F32) * scale
                if cum is not None:
                    s = s + (refs[6][hh] - refs[7][hh])
                if masked:
                    s = _causal(s, i, j, t)
                p = jnp.exp(s - lse_ref[hh])
                dp = lax.dot_general(_head(do_ref, "cols", hh, HEAD).astype(MXU), _head(v_ref, v[1], hh, v[2]).astype(MXU),
                                     NT, preferred_element_type=F32)
                ds = p * (dp - dl_ref[hh])
                accs[0][hh] += jnp.dot(ds.astype(MXU), kb, preferred_element_type=F32)
                if cum is not None:
                    accs[1][hh] += jnp.sum(ds, axis=1, keepdims=True)

        @pl.when(j < i)
        def _():
            block(False)

        @pl.when(j == i)
        def _():
            block(True)

        @pl.when(j == nb - 1)
        def _():
            outs[0][...] = accs[0][...] * scale
            if cum is not None:
                outs[1][...] = accs[1][...]

    out_specs = [pl.BlockSpec((HP, t, wq), lambda h, i, j: (h, i, 0))]
    out_shape = [jax.ShapeDtypeStruct((n_heads, seq, wq), F32)]
    scratch = [pltpu.VMEM((HP, t, wq), F32)]
    if cum is not None:
        out_specs.append(_colspec(t, qs))
        out_shape.append(jax.ShapeDtypeStruct((n_heads, seq, 1), F32))
        scratch.append(pltpu.VMEM((HP, t, 1), F32))
    res = pl.pallas_call(
        body, name=name, grid=(n_heads // HP, nb, nb), in_specs=specs, out_specs=out_specs, out_shape=out_shape,
        scratch_shapes=scratch, compiler_params=_cparams(3),
    )(*ins)
    return res[0] if cum is None else res


def flash_bwd_dkv(q, k, v, do, lse_row, delta_row, cum, *, n_heads, seq, t, scale, name):
    nb = seq // t
    ks, qs = (lambda j, i: j), (lambda j, i: jnp.maximum(i, j))
    ins = [q[0], k[0], v[0], do, lse_row, delta_row]
    specs = [_hspec(q[1], t, q[2], q[3], qs), _hspec(k[1], t, k[2], k[3], ks), _hspec(v[1], t, v[2], v[3], ks),
             _hspec("cols", t, HEAD, 0, qs), _rowspec(t, qs), _rowspec(t, qs)]
    if cum is not None:
        ins += [cum[0], cum[1]]
        specs += [_colspec(t, ks), _rowspec(t, qs)]
    wk = k[2]
    n_out = 2 if cum is None else 3

    def body(*refs):
        q_ref, k_ref, v_ref, do_ref, lse_ref, dl_ref = refs[:6]
        outs = refs[-2 * n_out:-n_out]
        accs = refs[-n_out:]
        j, i = pl.program_id(1), pl.program_id(2)

        @pl.when(i == 0)
        def _():
            for a in accs:
                a[...] = jnp.zeros(a.shape, F32)

        def block(masked):
            for hh in range(HP):
                qb = _head(q_ref, q[1], hh, q[2]).astype(MXU)
                dob = _head(do_ref, "cols", hh, HEAD).astype(MXU)
                st = lax.dot_general(_head(k_ref, k[1], hh, k[2]).astype(MXU), qb, NT, preferred_element_type=F32) * scale
                if cum is not None:
                    st = st + (refs[7][hh] - refs[6][hh])
                if masked:
                    st = _causal(st, i, j, t, transposed=True)
                pt = jnp.exp(st - lse_ref[hh])
                dpt = lax.dot_general(_head(v_ref, v[1], hh, v[2]).astype(MXU), dob, NT, preferred_element_type=F32)
                dst = pt * (dpt - dl_ref[hh])
                accs[0][hh] += jnp.dot(dst.astype(MXU), qb, preferred_element_type=F32)
                accs[1][hh] += jnp.dot(pt.astype(MXU), dob, preferred_element_type=F32)
                if cum is not None:
                    accs[2][hh] -= jnp.sum(dst, axis=1, keepdims=True)

        @pl.when(i > j)
        def _():
            block(False)

        @pl.when(i == j)
        def _():
            block(True)

        @pl.when(i == nb - 1)
        def _():
            outs[0][...] = accs[0][...] * scale
            for o, a in zip(outs[1:], accs[1:]):
                o[...] = a[...]

    out_specs = [pl.BlockSpec((HP, t, wk), lambda h, j, i: (h, j, 0)), pl.BlockSpec((HP, t, HEAD), lambda h, j, i: (h, j, 0))]
    out_shape = [jax.ShapeDtypeStruct((n_heads, seq, wk), F32), jax.ShapeDtypeStruct((n_heads, seq, HEAD), F32)]
    scratch = [pltpu.VMEM((HP, t, wk), F32), pltpu.VMEM((HP, t, HEAD), F32)]
    if cum is not None:
        out_specs.append(_colspec(t, ks))
        out_shape.append(jax.ShapeDtypeStruct((n_heads, seq, 1), F32))
        scratch.append(pltpu.VMEM((HP, t, 1), F32))
    return pl.pallas_call(
        body, name=name, grid=(n_heads // HP, nb, nb), in_specs=specs, out_specs=out_specs, out_shape=out_shape,
        scratch_shapes=scratch, compiler_params=_cparams(3),
    )(*ins)


def attn_delta(do, o, *, n_heads, seq, name):
    t = _tile(seq, 512)
    spec = pl.BlockSpec((t, HEAD), lambda h, m: (m, h))
    return rowwise(
        lambda a, b: jnp.sum(a.astype(F32) * b.astype(F32), axis=1, keepdims=True), [do, o], [spec, spec],
        jax.ShapeDtypeStruct((n_heads, seq, 1), F32), pl.BlockSpec((None, t, 1), lambda h, m: (h, m, 0)),
        (n_heads, seq // t), name)


def _dil_scores(q, kc, kp, n, scale):
    i = lax.broadcasted_iota(jnp.int32, (HEAD, HEAD), 0)
    j = lax.broadcasted_iota(jnp.int32, (HEAD, HEAD), 1)
    sc = lax.dot_general(q, kc, NT, preferred_element_type=F32) * scale
    sp = lax.dot_general(q, kp, NT, preferred_element_type=F32) * scale
    sc = jnp.where(j <= i, sc, NEG)
    sp = jnp.where(jnp.logical_and(j >= i, n > 0), sp, NEG)
    return sc, sp


def _strip_spec(length, n_heads, col_blocks, off):
    return pl.BlockSpec((length, HEAD), lambda r, h: (0, r * col_blocks + off + h))


def dil_fwd(q, k, v, *, seq, dil, n_heads, name):
    length = seq // dil
    nb = length // HEAD
    scale = HEAD ** -0.5
    view = lambda a: a.reshape(length, dil * a.shape[1])
    spec = _strip_spec(length, n_heads, n_heads, 0)

    def body(q_ref, k_ref, v_ref, o_ref, lse_ref):
        def step(n, carry):
            cur = pl.ds(pl.multiple_of(n * HEAD, HEAD), HEAD)
            prev = pl.ds(pl.multiple_of(jnp.maximum(n - 1, 0) * HEAD, HEAD), HEAD)
            qb = q_ref[cur, :].astype(MXU)
            sc, sp = _dil_scores(qb, k_ref[cur, :].astype(MXU), k_ref[prev, :].astype(MXU), n, scale)
            m = jnp.maximum(jnp.max(sc, axis=1, keepdims=True), jnp.max(sp, axis=1, keepdims=True))
            ec, ep = jnp.exp(sc - m), jnp.exp(sp - m)
            l = jnp.sum(ec, axis=1, keepdims=True) + jnp.sum(ep, axis=1, keepdims=True)
            o = jnp.dot((ec / l).astype(MXU), v_ref[cur, :].astype(MXU), preferred_element_type=F32)
            o = o + jnp.dot((ep / l).astype(MXU), v_ref[prev, :].astype(MXU), preferred_element_type=F32)
            o_ref[cur, :] = o
            lse_ref[cur, :] = jnp.broadcast_to(m + jnp.log(l), (HEAD, HEAD))
            return carry

        lax.fori_loop(0, nb, step, 0)

    out = jax.ShapeDtypeStruct((length, dil * n_heads * HEAD), F32)
    o, lse = pl.pallas_call(
        body, name=name, grid=(dil, n_heads), in_specs=[spec, spec, spec], out_specs=[spec, spec], out_shape=[out, out],
        compiler_params=_cparams(2),
    )(view(q), view(k), view(v))
    return o.reshape(seq, -1), lse.reshape(seq, -1)


def dil_bwd(q, k, v, o, lse, do, dlse, *, seq, dil, n_heads, name):
    length = seq // dil
    nb = length // HEAD
    scale = HEAD ** -0.5
    view = lambda a: a.reshape(length, dil * a.shape[1])
    spec = _strip_spec(length, n_heads, n_heads, 0)

    def body(q_ref, k_ref, v_ref, o_ref, lse_ref, do_ref, dlse_ref, dq_ref, dk_ref, dv_ref):
        dk_ref[...] = jnp.zeros(dk_ref.shape, F32)
        dv_ref[...] = jnp.zeros(dv_ref.shape, F32)

        def step(n, carry):
            cur = pl.ds(pl.multiple_of(n * HEAD, HEAD), HEAD)
            prev = pl.ds(pl.multiple_of(jnp.maximum(n - 1, 0) * HEAD, HEAD), HEAD)
            qb = q_ref[cur, :].astype(MXU)
            kc, kp = k_ref[cur, :].astype(MXU), k_ref[prev, :].astype(MXU)
            vc, vp = v_ref[cur, :].astype(MXU), v_ref[prev, :].astype(MXU)
            sc, sp = _dil_scores(qb, kc, kp, n, scale)
            lse_b = jnp.max(lse_ref[cur, :], axis=1, keepdims=True)
            pc, pp = jnp.exp(sc - lse_b), jnp.exp(sp - lse_b)
            dob = do_ref[cur, :]
            shift = jnp.sum(dlse_ref[cur, :], axis=1, keepdims=True) - jnp.sum(dob * o_ref[cur, :], axis=1, keepdims=True)
            dob = dob.astype(MXU)
            dsc = pc * (lax.dot_general(dob, vc, NT, preferred_element_type=F32) + shift)
            dsp = pp * (lax.dot_general(dob, vp, NT, preferred_element_type=F32) + shift)
            dscb, dspb = dsc.astype(MXU), dsp.astype(MXU)
            dq = jnp.dot(dscb, kc, preferred_element_type=F32) + jnp.dot(dspb, kp, preferred_element_type=F32)
            dq_ref[cur, :] = dq * scale
            dk_ref[cur, :] += lax.dot_general(dscb, qb, TN, preferred_element_type=F32) * scale
            dv_ref[cur, :] += lax.dot_general(pc.astype(MXU), dob, TN, preferred_element_type=F32)
            dk_ref[prev, :] += lax.dot_general(dspb, qb, TN, preferred_element_type=F32) * scale
            dv_ref[prev, :] += lax.dot_general(pp.astype(MXU), dob, TN, preferred_element_type=F32)
            return carry

        lax.fori_loop(0, nb, step, 0)

    out = jax.ShapeDtypeStruct((length, dil * n_heads * HEAD), F32)
    res = pl.pallas_call(
        body, name=name, grid=(dil, n_heads), in_specs=[spec] * 7, out_specs=[spec] * 3, out_shape=[out] * 3,
        compiler_params=_cparams(2),
    )(*[view(a) for a in (q, k, v, o, lse, do, dlse)])
    return [r.reshape(seq, -1) for r in res]


def _tri(n, kind):
    i = lax.broadcasted_iota(jnp.int32, (n, n), 0)
    j = lax.broadcasted_iota(jnp.int32, (n, n), 1)
    return jnp.where({"le": i <= j, "ge": i >= j}[kind], 1.0, 0.0).astype(F32)


def _block_matrix(n_rows, per_head, kind):
    r = lax.broadcasted_iota(jnp.int32, (n_rows, n_rows), 0)
    c = lax.broadcasted_iota(jnp.int32, (n_rows, n_rows), 1)
    same = (r // per_head) == (c // per_head)
    rel = {"lt": c < r, "gt": c > r, "all": c == c}[kind]
    return jnp.where(jnp.logical_and(same, rel), 1.0, 0.0).astype(F32)


def _lane_pick(x, lane):
    j = lax.broadcasted_iota(jnp.int32, x.shape, 1)
    return jnp.sum(jnp.where(j == lane, x, 0.0), axis=1, keepdims=True)


def _log_sigmoid(z):
    return jnp.minimum(z, 0.0) - jnp.log1p(jnp.exp(-jnp.abs(z)))


def fox_gate_fwd(z, bias_rows, per_head, name):
    n_rows = z.shape[0]

    def body(z_ref, b_ref, c_ref):
        logf = _log_sigmoid(z_ref[...] + b_ref[...])
        within = jnp.dot(logf, _tri(HEAD, "le"), precision=HI, preferred_element_type=F32)
        tot = jnp.broadcast_to(_lane_pick(within, HEAD - 1), (n_rows, HEAD))
        c_ref[...] = within + jnp.dot(_block_matrix(n_rows, per_head, "lt"), tot, precision=HI, preferred_element_type=F32)

    return pl.pallas_call(body, name=name, out_shape=jax.ShapeDtypeStruct(z.shape, F32),
                          compiler_params=pltpu.CompilerParams(vmem_limit_bytes=VMEM_LIMIT))(z, bias_rows)


def fox_gate_bwd(z, bias_rows, dcum_q, dcum_k, per_head, name):
    n_rows = z.shape[0]

    def body(z_ref, b_ref, dcq_ref, dck_ref, dz_ref, db_ref):
        within = jnp.dot(dcq_ref[...] + dck_ref[...], _tri(HEAD, "ge"), precision=HI, preferred_element_type=F32)
        tot = jnp.broadcast_to(_lane_pick(within, 0), (n_rows, HEAD))
        dlogf = within + jnp.dot(_block_matrix(n_rows, per_head, "gt"), tot, precision=HI, preferred_element_type=F32)
        dz = dlogf * jax.nn.sigmoid(-(z_ref[...] + b_ref[...]))
        dz_ref[...] = dz
        rs = jnp.broadcast_to(jnp.sum(dz, axis=1, keepdims=True), (n_rows, HEAD))
        db_ref[...] = jnp.dot(_block_matrix(n_rows, per_head, "all"), rs, precision=HI, preferred_element_type=F32)

    shp = jax.ShapeDtypeStruct(z.shape, F32)
    return pl.pallas_call(body, name=name, out_shape=[shp, shp],
                          compiler_params=pltpu.CompilerParams(vmem_limit_bytes=VMEM_LIMIT))(z, bias_rows, dcum_q, dcum_k)


def exchange(x, scatter, name):
    blk = x.shape[1:] if scatter else x.shape

    def body(x_ref, o_ref, send_sems, recv_sems, local_sem):
        mx, my, mc = lax.axis_index("x"), lax.axis_index("y"), lax.axis_index("c")
        me = 4 * mx + 2 * my + mc
        flip = lambda v, f: 1 - v if f else v
        local = pltpu.make_async_copy(x_ref.at[me] if scatter else x_ref, o_ref.at[me], local_sem)
        local.start()
        sends, recvs = [], []
        for n in range(1, NDEV):
            px, py, pc = flip(mx, n & 4), flip(my, n & 2), flip(mc, n & 1)
            p = 4 * px + 2 * py + pc
            sends.append(pltpu.make_async_remote_copy(
                src_ref=x_ref.at[p] if scatter else x_ref, dst_ref=o_ref.at[me], send_sem=send_sems.at[n - 1],
                recv_sem=recv_sems.at[n - 1], device_id=(px, py, pc), device_id_type=pl.DeviceIdType.MESH))
            recvs.append(pltpu.make_async_remote_copy(
                src_ref=x_ref.at[me] if scatter else x_ref, dst_ref=o_ref.at[p], send_sem=send_sems.at[n - 1],
                recv_sem=recv_sems.at[n - 1], device_id=(px, py, pc), device_id_type=pl.DeviceIdType.MESH))
        for cp in sends:
            cp.start()
        for cp in recvs:
            cp.wait_recv()
        for cp in sends:
            cp.wait_send()
        local.wait()

    hbm = pl.BlockSpec(memory_space=pltpu.HBM)
    return pl.pallas_call(
        body, name=name, in_specs=[hbm], out_specs=hbm, out_shape=jax.ShapeDtypeStruct((NDEV, *blk), x.dtype),
        scratch_shapes=[pltpu.SemaphoreType.DMA((NDEV - 1,)), pltpu.SemaphoreType.DMA((NDEV - 1,)), pltpu.SemaphoreType.DMA],
    )(x)


def _exchange_copies(x_refs, land_refs, send_sems, recv_sems, local_sems, scatter, with_recvs):
    mx, my, mc = lax.axis_index("x"), lax.axis_index("y"), lax.axis_index("c")
    me = 4 * mx + 2 * my + mc
    flip = lambda v, f: 1 - v if f else v
    local, sends, recvs = [], [], []
    for a, (x_ref, o_ref) in enumerate(zip(x_refs, land_refs)):
        local.append(pltpu.make_async_copy(x_ref.at[me] if scatter else x_ref, o_ref.at[me], local_sems.at[a]))
        for n in range(1, NDEV):
            px, py, pc = flip(mx, n & 4), flip(my, n & 2), flip(mc, n & 1)
            p = 4 * px + 2 * py + pc
            sem = (NDEV - 1) * a + n - 1
            mk = lambda src, dst: pltpu.make_async_remote_copy(
                src_ref=src, dst_ref=dst, send_sem=send_sems.at[sem], recv_sem=recv_sems.at[sem],
                device_id=(px, py, pc), device_id_type=pl.DeviceIdType.MESH)
            sends.append(mk(x_ref.at[p] if scatter else x_ref, o_ref.at[me]))
            if with_recvs:
                recvs.append(mk(x_ref.at[me] if scatter else x_ref, o_ref.at[p]))
    return local, sends, recvs


_HBM = pl.BlockSpec(memory_space=pltpu.HBM)
_SEM = pl.BlockSpec(memory_space=pltpu.SEMAPHORE)
_EFFECT = pltpu.SideEffectType.DATAFLOW_SIDE_EFFECTING


def exchange_start(xs, scatter, name):
    n = len(xs)
    lands = [jax.ShapeDtypeStruct((NDEV, *(x.shape[1:] if scatter else x.shape)), x.dtype) for x in xs]

    def body(*refs):
        x_refs, land_refs = refs[:n], refs[n:2 * n]
        send_sems, recv_sems, local_sems = refs[2 * n:2 * n + 3]
        token = refs[-1]
        local, sends, _ = _exchange_copies(x_refs, land_refs, send_sems, recv_sems, local_sems, scatter, False)
        for cp in local + sends:
            cp.start()
        token[...] = jnp.zeros(token.shape, token.dtype)

    n_sem = (NDEV - 1) * n
    out = pl.pallas_call(
        body, name=name,
        out_shape=(pltpu.SemaphoreType.DMA((n_sem,)), pltpu.SemaphoreType.DMA((n_sem,)), pltpu.SemaphoreType.DMA((n,)),
                   *[pltpu.HBM(x.shape, x.dtype) for x in xs], *[pltpu.HBM(s.shape, s.dtype) for s in lands],
                   jax.ShapeDtypeStruct((8, HEAD), F32)),
        in_specs=[_HBM] * (2 * n), out_specs=(_SEM, _SEM, _SEM, *[_HBM] * (2 * n), pl.BlockSpec(memory_space=pltpu.VMEM)),
        input_output_aliases={i: 3 + i for i in range(2 * n)},
        compiler_params=pltpu.CompilerParams(has_side_effects=_EFFECT),
    )(*[pltpu.with_memory_space_constraint(x, pltpu.HBM) for x in xs],
      *[pltpu.with_memory_space_constraint(lax.empty(s.shape, s.dtype), pltpu.HBM) for s in lands])
    return (out[:3], out[3:3 + n], out[3 + n:3 + 2 * n], scatter), out[-1]


def exchange_wait(handle, after, name):
    sems, xs, lands, scatter = handle
    n = len(xs)

    def body(*refs):
        x_refs, land_refs = refs[:n], refs[n:2 * n]
        send_sems, recv_sems, local_sems = refs[2 * n:2 * n + 3]
        local, sends, recvs = _exchange_copies(x_refs, land_refs, send_sems, recv_sems, local_sems, scatter, True)
        for cp in sends:
            cp.wait_send()
        for cp in recvs:
            cp.wait_recv()
        for cp in local:
            cp.wait()

    out = pl.pallas_call(
        body, name=name, out_shape=tuple(pltpu.HBM(a.shape, a.dtype) for a in (*xs, *lands)),
        in_specs=[_HBM] * (2 * n) + [_SEM] * 3 + [pl.BlockSpec(memory_space=pl.ANY)], out_specs=tuple([_HBM] * (2 * n)),
        input_output_aliases={i: i for i in range(2 * n)}, compiler_params=pltpu.CompilerParams(has_side_effects=_EFFECT),
    )(*xs, *lands, *sems, after)
    return list(out[n:])


def adamw(parts, w, m, v, name):
    depth = len(parts)
    n_parts, rows, cols = parts[0].shape
    t = rows
    for cand in (256, 128, 64, 32, 16, 8):
        if rows % cand == 0 and (n_parts * parts[0].dtype.itemsize + 7 * 4) * cand * cols <= ADAMW_STEP_BYTES:
            t = cand
            break
    nr = rows // t

    def body(*refs):
        p_refs = refs[:depth]
        w_ref, m_ref, v_ref, g_out, d_out, m_out, v_out = refs[depth:]
        layer = pl.program_id(0)
        for i in range(depth):
            @pl.when(layer == i)
            def _(p=p_refs[i]):
                g = p[0].astype(F32)
                for j in range(1, n_parts):
                    g = g + p[j].astype(F32)
                m_new = ADAM_B1 * m_ref[...] + (1.0 - ADAM_B1) * g
                v_new = ADAM_B2 * v_ref[...] + (1.0 - ADAM_B2) * jnp.square(g)
                m_hat = m_new / (1.0 - ADAM_B1 ** ADAM_STEP)
                v_hat = v_new / (1.0 - ADAM_B2 ** ADAM_STEP)
                g_out[...] = g
                d_out[...] = -ADAM_LR * (m_hat / (jnp.sqrt(v_hat) + ADAM_EPS) + ADAM_WD * w_ref[...])
                m_out[...] = m_new
                v_out[...] = v_new

    def part_spec(i):
        return pl.BlockSpec((n_parts, t, cols), lambda l, r: (0, jnp.where(l < i, 0, jnp.where(l == i, r, nr - 1)), 0))

    spec = pl.BlockSpec((t, cols), lambda l, r: (l * nr + r, 0))
    out = jax.ShapeDtypeStruct((depth * rows, cols), F32)
    return pl.pallas_call(
        body, name=name, grid=(depth, nr), in_specs=[*[part_spec(i) for i in range(depth)], spec, spec, spec],
        out_specs=[spec] * 4, out_shape=[out] * 4, compiler_params=_cparams(2),
    )(*parts, w, m, v)


def sum_parts(parts, name):
    n_parts, rows, cols = parts.shape
    t = _tile(rows, 128)

    def fn(p):
        g = p[0].astype(F32)
        for i in range(1, n_parts):
            g = g + p[i].astype(F32)
        return g

    return rowwise(fn, [parts], [pl.BlockSpec((n_parts, t, cols), lambda r: (0, r, 0))],
                   jax.ShapeDtypeStruct((rows, cols), F32), pl.BlockSpec((t, cols), lambda r: (r, 0)), (rows // t,), name)


def _sd(shape, dtype=F32):
    return jax.ShapeDtypeStruct(shape, dtype)


def rms_fwd(x, g, name, col=0, width=None, dep=None):
    seq = x.shape[0]
    width = width or x.shape[1]
    t = _tile(seq, 512)
    return rowwise(_rms, [x, g.reshape(1, width)], [_rows(t, width, col), _whole((1, width))], _sd((seq, width), ACT),
                   _rows(t, width), (seq // t,), name, dep=dep)


def rms_bwd(x, g, dy, name, col=0, width=None, add=None, dx_dtype=F32):
    seq = x.shape[0]
    width = width or x.shape[1]
    t = _tile(seq, 512)
    ins, specs = [x, g.reshape(1, width), dy], [_rows(t, width, col), _whole((1, width)), _rows(t, width)]
    if add is None:
        fn = _rms_bwd
    else:
        ins.append(add)
        specs.append(_rows(t, width))

        def fn(x_, g_, dy_, add_):
            dx, dg = _rms_bwd(x_, g_, dy_)
            return dx + add_, dg
    return rowwise(fn, ins, specs, [_sd((seq, width), dx_dtype), _sd((1, width))], [_rows(t, width), _whole((1, width))],
                   (seq // t,), name, acc={1: (0,)})


def ffn_fwd(x, g, wg, wu, wd, l, tag, dep=None):
    seq, d = x.shape
    f = wg.shape[-1]
    tm = _tile(seq, 1024)
    h = rms_fwd(x, g, f"{tag}_rms", dep=dep)
    up = lambda w, nm: mm(
        h, w, name=nm, dims=NN, grid=(NDEV, seq // tm, 1), nk=1,
        a_spec=pl.BlockSpec((tm, d), lambda j, m, k: (m, 0)),
        b_spec=pl.BlockSpec((None, None, d, f), lambda j, m, k: (j, l, 0, 0)),
        o_spec=pl.BlockSpec((None, tm, f), lambda j, m, k: (j, m, 0)), out_shape=_sd((NDEV, seq, f), ACT))
    a, b = up(wg, f"{tag}_gate"), up(wu, f"{tag}_up")
    t = _tile(seq, 512)
    spec3 = pl.BlockSpec((None, t, f), lambda j, m: (j, m, 0))
    hid = rowwise(lambda a_, b_: jax.nn.silu(a_.astype(F32)) * b_.astype(F32), [a, b], [spec3, spec3],
                  _sd((NDEV, seq, f), ACT), spec3, (NDEV, seq // t), f"{tag}_act")
    tn = _tile(d, 1024)
    out = mm(hid, wd, name=f"{tag}_down", dims=NN, grid=(seq // tm, d // tn, NDEV), nk=NDEV,
             a_spec=pl.BlockSpec((None, tm, f), lambda m, n, k: (k, m, 0)),
             b_spec=pl.BlockSpec((None, None, f, tn), lambda m, n, k: (k, l, 0, n)),
             o_spec=pl.BlockSpec((tm, tn), lambda m, n, k: (m, n)), out_shape=_sd((seq, d)),
             extras=[x], extra_specs=[pl.BlockSpec((tm, tn), lambda m, n, k: (m, n))],
             epilogue=lambda acc, x_: x_ + 0.5 * acc)
    return out, (x, h, a, b, hid)


def ffn_bwd(dout, saved, g, wg, wu, wd, l, tag, dep=None):
    x, h, a, b, hid = saved
    seq, d = x.shape
    f = wg.shape[-1]
    tm = _tile(seq, 1024)
    tk = _tile(seq, 1024)
    dhid = mm(dout, wd, name=f"{tag}_dhid", dims=NT, grid=(NDEV, seq // tm, 1), nk=1,
              a_spec=pl.BlockSpec((tm, d), lambda j, m, k: (m, 0)),
              b_spec=pl.BlockSpec((None, None, f, d), lambda j, m, k: (j, l, 0, 0)),
              o_spec=pl.BlockSpec((None, tm, f), lambda j, m, k: (j, m, 0)), out_shape=_sd((NDEV, seq, f), ACT),
              epilogue=lambda acc: 0.5 * acc, dep=dep)
    tn = _tile(d, 1024)
    dwd = mm(hid, dout, name=f"{tag}_dwd", dims=TN, grid=(NDEV, d // tn, seq // tk), nk=seq // tk,
             a_spec=pl.BlockSpec((None, tk, f), lambda j, n, k: (j, k, 0)),
             b_spec=pl.BlockSpec((tk, tn), lambda j, n, k: (k, n)),
             o_spec=pl.BlockSpec((None, f, tn), lambda j, n, k: (j, 0, n)), out_shape=_sd((NDEV, f, d), COMM),
             epilogue=lambda acc: 0.5 * acc)
    t = _tile(seq, 512)
    spec3 = pl.BlockSpec((None, t, f), lambda j, m: (j, m, 0))

    def act_bwd(dh_, a_, b_):
        dh_, a_, b_ = dh_.astype(F32), a_.astype(F32), b_.astype(F32)
        sig = jax.nn.sigmoid(a_)
        return dh_ * b_ * sig * (1.0 + a_ * (1.0 - sig)), dh_ * a_ * sig

    da, db = rowwise(act_bwd, [dhid, a, b], [spec3] * 3, [_sd((NDEV, seq, f), ACT)] * 2, [spec3] * 2, (NDEV, seq // t),
                     f"{tag}_dact")
    dw = lambda dz, nm: mm(
        h, dz, name=nm, dims=TN, grid=(NDEV, 1, seq // tk), nk=seq // tk,
        a_spec=pl.BlockSpec((tk, d), lambda j, n, k: (k, 0)),
        b_spec=pl.BlockSpec((None, tk, f), lambda j, n, k: (j, k, 0)),
        o_spec=pl.BlockSpec((None, d, f), lambda j, n, k: (j, 0, 0)), out_shape=_sd((NDEV, d, f), COMM))
    dwg, dwu = dw(da, f"{tag}_dwg"), dw(db, f"{tag}_dwu")
    dh_of = lambda dz, w, nm, extras, epi: mm(
        dz, w, name=nm, dims=NT, grid=(seq // tm, d // tn, NDEV), nk=NDEV,
        a_spec=pl.BlockSpec((None, tm, f), lambda m, n, k: (k, m, 0)),
        b_spec=pl.BlockSpec((None, None, tn, f), lambda m, n, k: (k, l, n, 0)),
        o_spec=pl.BlockSpec((tm, tn), lambda m, n, k: (m, n)), out_shape=_sd((seq, d)),
        extras=extras, extra_specs=[pl.BlockSpec((tm, tn), lambda m, n, k: (m, n))] * len(extras), epilogue=epi)
    dh = dh_of(da, wg, f"{tag}_dh_gate", [], None)
    dh = dh_of(db, wu, f"{tag}_dh_up", [dh], lambda acc, prev: acc + prev)
    dx, dg = rms_bwd(x, g, dh, f"{tag}_drms", add=dout)
    return dx, (dg, dwg, dwu, dwd)


def _dense(a, w, l, name, out_dtype=F32, extras=(), epilogue=None):
    seq, kdim = a.shape
    n = w.shape[3]
    w2 = w.reshape(kdim, n)
    tm, tn, tk = _tile(seq, 1024), _tile(n, 1024), _tile(kdim, 1024)
    nk = kdim // tk
    return mm(a, w2, name=name, dims=NN, grid=(seq // tm, n // tn, nk), nk=nk,
              a_spec=pl.BlockSpec((tm, tk), lambda m, c, k: (m, k)),
              b_spec=pl.BlockSpec((tk, tn), lambda m, c, k: (k, c)),
              o_spec=pl.BlockSpec((tm, tn), lambda m, c, k: (m, c)), out_shape=_sd((seq, n), out_dtype),
              extras=list(extras), extra_specs=[pl.BlockSpec((tm, tn), lambda m, c, k: (m, c))] * len(extras),
              epilogue=epilogue)


def _dense_dx(dy, w, l, name, extras=(), epilogue=None, dep=None):
    seq, n = dy.shape
    kb = w.shape[2]
    tm = _tile(seq, 1024)
    return mm(dy, w, name=name, dims=NT, grid=(seq // tm, NDEV, 1), nk=1,
              a_spec=pl.BlockSpec((tm, n), lambda m, j, k: (m, 0)),
              b_spec=pl.BlockSpec((None, None, kb, n), lambda m, j, k: (j, l, 0, 0)),
              o_spec=pl.BlockSpec((tm, kb), lambda m, j, k: (m, j)), out_shape=_sd((seq, NDEV * kb)),
              extras=list(extras), extra_specs=[pl.BlockSpec((tm, kb), lambda m, j, k: (m, j))] * len(extras),
              epilogue=epilogue, dep=dep)


def _dense_dw(a, dy, kb, name):
    seq, n = dy.shape
    rows = NDEV * kb
    tk, tn, tr = _tile(seq, 1024), _tile(n, 1024), _tile(rows, 1024)
    out = mm(a, dy, name=name, dims=TN, grid=(rows // tr, n // tn, seq // tk), nk=seq // tk,
             a_spec=pl.BlockSpec((tk, tr), lambda j, c, k: (k, j)),
             b_spec=pl.BlockSpec((tk, tn), lambda j, c, k: (k, c)),
             o_spec=pl.BlockSpec((tr, tn), lambda j, c, k: (j, c)), out_shape=_sd((rows, n), COMM))
    return out.reshape(NDEV, kb, n)


def _heads_up(a, w, l, name, out_dtype):
    seq, r = a.shape
    c = w.shape[3]
    tm = _tile(seq, 1024)
    return mm(a, w, name=name, dims=NN, grid=(NDEV, seq // tm, 1), nk=1,
              a_spec=pl.BlockSpec((tm, r), lambda j, m, k: (m, 0)),
              b_spec=pl.BlockSpec((None, None, r, c), lambda j, m, k: (j, l, 0, 0)),
              o_spec=pl.BlockSpec((None, tm, c), lambda j, m, k: (j, m, 0)), out_shape=_sd((NDEV, seq, c), out_dtype))


def _heads_dx(dy, w, l, name):
    _, seq, c = dy.shape
    r = w.shape[2]
    tm = _tile(seq, 1024)
    return mm(dy, w, name=name, dims=NT, grid=(seq // tm, 1, NDEV), nk=NDEV,
              a_spec=pl.BlockSpec((None, tm, c), lambda m, n, k: (k, m, 0)),
              b_spec=pl.BlockSpec((None, None, r, c), lambda m, n, k: (k, l, 0, 0)),
              o_spec=pl.BlockSpec((tm, r), lambda m, n, k: (m, 0)), out_shape=_sd((seq, r)))


def _heads_dw(a, dy, name):
    seq, r = a.shape
    c = dy.shape[2]
    tk = _tile(seq, 1024)
    return mm(a, dy, name=name, dims=TN, grid=(NDEV, 1, seq // tk), nk=seq // tk,
              a_spec=pl.BlockSpec((tk, r), lambda j, n, k: (k, 0)),
              b_spec=pl.BlockSpec((None, tk, c), lambda j, n, k: (j, k, 0)),
              o_spec=pl.BlockSpec((None, r, c), lambda j, n, k: (j, 0, 0)), out_shape=_sd((NDEV, r, c), COMM))


C_FQ, C_FK, C_FV, C_CQ, C_CKV, C_DQ, C_DK, C_DV = range(8)
MAIN_W = 8 * 512
TAIL_W = 128


def split_w_in(w):
    fq, fk, fv, fl, cq, ckv, kr, dq, dk, dv = jnp.split(w, [512, 1024, 1536, 1540, 2052, 2564, 2628, 3140, 3652], axis=-1)
    main = jnp.concatenate([fq, fk, fv, cq, ckv, dq, dk, dv], axis=-1)
    pad = jnp.zeros((*w.shape[:-1], TAIL_W - 68), w.dtype)
    return main, jnp.concatenate([kr, fl, pad], axis=-1)


def merge_w_in(main, tail):
    fq, fk, fv, cq, ckv, dq, dk, dv = jnp.split(main, 8, axis=-1)
    return jnp.concatenate([fq, fk, fv, tail[..., 64:68], cq, ckv, tail[..., 0:64], dq, dk, dv], axis=-1)


def mixer_fwd(x, p, l, consts, dep=None):
    seq, d = x.shape
    nfox, nmla, ndil = 4, 8, 4
    cos_m, sin_m, cos_p, sin_p = consts
    t = _tile(seq, 512)
    tf = _tile(seq, 512)
    h = rms_fwd(x, p["mix_norm"], f"mix{l}_rms", dep=dep)
    proj = _dense(h, p["w_in_main"], 0, f"mix{l}_proj")
    tail = _dense(h, p["w_in_tail"], 0, f"mix{l}_tail")

    nb = seq // HEAD
    z = tail[:, 64:68].T.reshape(nfox * nb, HEAD)
    bias_rows = jnp.repeat(p["fox_forget_bias"], nb).reshape(nfox * nb, 1)
    cum = fox_gate_fwd(z, bias_rows, nb, f"mix{l}_gate").reshape(nfox, seq)
    cum2 = (cum.reshape(nfox, seq, 1), cum.reshape(nfox, 1, seq))
    fox_qkv = ((proj, "cols", HEAD, C_FQ * 4), (proj, "cols", HEAD, C_FK * 4), (proj, "cols", HEAD, C_FV * 4))
    out_a, lse_a = flash_fwd(*fox_qkv, cum2, n_heads=nfox, seq=seq, t=tf, scale=HEAD ** -0.5, name=f"mix{l}_fox")

    cq = rms_fwd(proj, p["mla_q_norm"], f"mix{l}_cq", col=C_CQ, width=512)
    ckv = rms_fwd(proj, p["mla_kv_norm"], f"mix{l}_ckv", col=C_CKV, width=512)
    q_raw = _heads_up(cq, p["mla_w_uq"], 0, f"mix{l}_uq", F32)
    kv = _heads_up(ckv, p["mla_w_ukv"], 0, f"mix{l}_ukv", ACT)

    def mla_prep(q_, kv_, tail_, cos_, sin_, q_out, k_out):
        perm = _swap_matrix(MLA_ROPE)
        c, s = cos_[...], sin_[...]
        q_out[:, 0:HEAD] = q_[:, 0:HEAD].astype(q_out.dtype)
        q_out[:, HEAD:MLA_QK] = _rope(q_[:, HEAD:MLA_QK], c, s, perm).astype(q_out.dtype)
        k_out[:, 0:HEAD] = kv_[:, 0:HEAD].astype(k_out.dtype)
        k_out[:, HEAD:MLA_QK] = _rope(tail_[:, 0:MLA_ROPE], c, s, perm).astype(k_out.dtype)

    hs = lambda w: pl.BlockSpec((None, t, w), lambda hh, m: (hh, m, 0))
    rs = lambda w: pl.BlockSpec((t, w), lambda hh, m: (m, 0))
    q_b, k_b = pl.pallas_call(
        lambda q_, kv_, tl_, c_, s_, qo, ko: mla_prep(q_[...], kv_[...], tl_[...], c_, s_, qo, ko),
        name=f"mix{l}_mla_prep", grid=(nmla, seq // t),
        in_specs=[hs(MLA_QK), hs(2 * HEAD), rs(TAIL_W), rs(MLA_ROPE), rs(MLA_ROPE)], out_specs=[hs(MLA_QK), hs(MLA_QK)],
        out_shape=[_sd((nmla, seq, MLA_QK), ACT)] * 2, compiler_params=_cparams(2),
    )(q_raw, kv, tail, cos_m, sin_m)
    mla_qkv = ((q_b, "heads", MLA_QK, 0), (k_b, "heads", MLA_QK, 0), (kv, "heads", HEAD, 1))
    out_b, lse_b = flash_fwd(*mla_qkv, None, n_heads=nmla, seq=seq, t=tf, scale=MLA_QK ** -0.5, name=f"mix{l}_mla")

    wd_ = ndil * HEAD

    def dil_prep(q_, k_, c_, s_):
        perm = _pad_perm(PARTIAL_ROPE)
        rot = lambda a: jnp.concatenate(
            [_rope(a[:, i * HEAD:(i + 1) * HEAD], c_, s_, perm) for i in range(ndil)], axis=1)
        return rot(q_), rot(k_)

    dq_r, dk_r = rowwise(dil_prep, [proj, proj, cos_p, sin_p],
                         [_rows(t, wd_, C_DQ), _rows(t, wd_, C_DK), _rows(t, HEAD), _rows(t, HEAD)],
                         [_sd((seq, wd_), ACT)] * 2, [_rows(t, wd_)] * 2, (seq // t,), f"mix{l}_dil_prep")
    dv = proj[:, C_DV * 512:(C_DV + 1) * 512]
    branches = [dil_fwd(dq_r, dk_r, dv, seq=seq, dil=dl, n_heads=ndil, name=f"mix{l}_dil{dl}") for dl in DIL_BRANCHES]

    def mix(o1, o2, o3, l1, l2, l3):
        m = jnp.maximum(jnp.maximum(l1, l2), l3)
        e1, e2, e3 = jnp.exp(l1 - m), jnp.exp(l2 - m), jnp.exp(l3 - m)
        return (e1 * o1 + e2 * o2 + e3 * o3) / (e1 + e2 + e3)

    out_c = rowwise(mix, [b[0] for b in branches] + [b[1] for b in branches], [_rows(t, wd_)] * 6, _sd((seq, wd_)),
                    _rows(t, wd_), (seq // t,), f"mix{l}_dil_mix")

    mixed = jnp.concatenate([out_a, out_b, out_c], axis=1)
    out = _dense(mixed, p["w_out"], 0, f"mix{l}_out", extras=[x], epilogue=lambda acc, x_: x_ + acc)
    saved = dict(x=x, h=h, proj=proj, tail=tail, z=z, bias_rows=bias_rows, cum2=cum2, out_a=out_a, lse_a=lse_a, cq=cq,
                 ckv=ckv, q_raw=q_raw, kv=kv, q_b=q_b, k_b=k_b, out_b=out_b, lse_b=lse_b, dq_r=dq_r, dk_r=dk_r, dv=dv,
                 branches=branches, out_c=out_c, mixed=mixed)
    return out, saved


def _pad_perm(n):
    i = lax.broadcasted_iota(jnp.int32, (HEAD, HEAD), 0)
    j = lax.broadcasted_iota(jnp.int32, (HEAD, HEAD), 1)
    inside = jnp.logical_and(i < n, j < n)
    return jnp.where(jnp.logical_and(inside, ((i + n // 2) % n) == j), 1.0, 0.0).astype(F32)


def mixer_bwd(dout, sv, p, l, consts, dep=None):
    seq, d = dout.shape
    nfox, nmla, ndil = 4, 8, 4
    cos_m, sin_m, cos_p, sin_p = consts
    t = _tile(seq, 512)
    tf = _tile(seq, 512)
    nb = seq // HEAD
    proj, tail = sv["proj"], sv["tail"]
    dmixed = _dense_dx(dout, p["w_out"], 0, f"mix{l}_dmixed", dep=dep)
    dw_out = _dense_dw(sv["mixed"], dout, d // NDEV, f"mix{l}_dw_out")
    do_a, do_b, do_c = dmixed[:, 0:512], dmixed[:, 512:1536], dmixed[:, 1536:2048]

    fox_qkv = ((proj, "cols", HEAD, C_FQ * 4), (proj, "cols", HEAD, C_FK * 4), (proj, "cols", HEAD, C_FV * 4))
    delta_a = attn_delta(do_a, sv["out_a"], n_heads=nfox, seq=seq, name=f"mix{l}_fox_delta")
    row = lambda a: a.reshape(a.shape[0], 1, seq)
    dfq, dcum_q = flash_bwd_dq(*fox_qkv, do_a, sv["lse_a"], delta_a, sv["cum2"], n_heads=nfox, seq=seq, t=tf,
                               scale=HEAD ** -0.5, name=f"mix{l}_fox_dq")
    dfk, dfv, dcum_k = flash_bwd_dkv(*fox_qkv, do_a, row(sv["lse_a"]), row(delta_a), sv["cum2"], n_heads=nfox, seq=seq,
                                     t=tf, scale=HEAD ** -0.5, name=f"mix{l}_fox_dkv")
    dz, dbias = fox_gate_bwd(sv["z"], sv["bias_rows"], dcum_q.reshape(nfox * nb, HEAD), dcum_k.reshape(nfox * nb, HEAD),
                             nb, f"mix{l}_dgate")
    d_fox_bias = dbias.reshape(nfox, nb, HEAD)[:, 0, 0]
    dfl = dz.reshape(nfox, seq).T
    unheads = lambda a: a.transpose(1, 0, 2).reshape(seq, -1)

    mla_qkv = ((sv["q_b"], "heads", MLA_QK, 0), (sv["k_b"], "heads", MLA_QK, 0), (sv["kv"], "heads", HEAD, 1))
    delta_b = attn_delta(do_b, sv["out_b"], n_heads=nmla, seq=seq, name=f"mix{l}_mla_delta")
    dq_b = flash_bwd_dq(*mla_qkv, do_b, sv["lse_b"], delta_b, None, n_heads=nmla, seq=seq, t=tf, scale=MLA_QK ** -0.5,
                        name=f"mix{l}_mla_dq")
    dk_b, dv_b = flash_bwd_dkv(*mla_qkv, do_b, row(sv["lse_b"]), row(delta_b), None, n_heads=nmla, seq=seq, t=tf,
                               scale=MLA_QK ** -0.5, name=f"mix{l}_mla_dkv")

    def mla_unprep(dq_, dk_, dv_, cos_, sin_, dq_out, dkv_out, dkr_out):
        perm = _swap_matrix(MLA_ROPE)
        c, s = cos_[...], sin_[...]
        dq_out[:, 0:HEAD] = dq_[:, 0:HEAD].astype(dq_out.dtype)
        dq_out[:, HEAD:MLA_QK] = _rope_t(dq_[:, HEAD:MLA_QK], c, s, perm).astype(dq_out.dtype)
        dkv_out[:, 0:HEAD] = dk_[:, 0:HEAD].astype(dkv_out.dtype)
        dkv_out[:, HEAD:2 * HEAD] = dv_.astype(dkv_out.dtype)
        dkr = _rope_t(dk_[:, HEAD:MLA_QK], c, s, perm)
        first = pl.program_id(1) == 0

        @pl.when(first)
        def _():
            dkr_out[...] = dkr

        @pl.when(jnp.logical_not(first))
        def _():
            dkr_out[...] += dkr

    hs = lambda w: pl.BlockSpec((None, t, w), lambda m, hh: (hh, m, 0))
    rs = lambda w: pl.BlockSpec((t, w), lambda m, hh: (m, 0))
    dq_raw, dkv, dk_r = pl.pallas_call(
        lambda a, b, c, cs, sn, o1, o2, o3: mla_unprep(a[...], b[...], c[...], cs, sn, o1, o2, o3),
        name=f"mix{l}_mla_unprep", grid=(seq // t, nmla),
        in_specs=[hs(MLA_QK), hs(MLA_QK), hs(HEAD), rs(MLA_ROPE), rs(MLA_ROPE)],
        out_specs=[hs(MLA_QK), hs(2 * HEAD), rs(MLA_ROPE)],
        out_shape=[_sd((nmla, seq, MLA_QK), ACT), _sd((nmla, seq, 2 * HEAD), ACT), _sd((seq, MLA_ROPE))],
        compiler_params=_cparams(2),
    )(dq_b, dk_b, dv_b, cos_m, sin_m)
    dcq_n = _heads_dx(dq_raw, p["mla_w_uq"], 0, f"mix{l}_dcq")
    dckv_n = _heads_dx(dkv, p["mla_w_ukv"], 0, f"mix{l}_dckv")
    dw_uq = _heads_dw(sv["cq"], dq_raw, f"mix{l}_dw_uq")
    dw_ukv = _heads_dw(sv["ckv"], dkv, f"mix{l}_dw_ukv")
    dcq, dg_q = rms_bwd(proj, p["mla_q_norm"], dcq_n, f"mix{l}_dcq_rms", col=C_CQ, width=512)
    dckv, dg_kv = rms_bwd(proj, p["mla_kv_norm"], dckv_n, f"mix{l}_dckv_rms", col=C_CKV, width=512)

    wd_ = ndil * HEAD
    outs = [b[0] for b in sv["branches"]]
    lses = [b[1] for b in sv["branches"]]

    def mix_bwd(do_, o1, o2, o3, l1, l2, l3):
        m = jnp.maximum(jnp.maximum(l1, l2), l3)
        e1, e2, e3 = jnp.exp(l1 - m), jnp.exp(l2 - m), jnp.exp(l3 - m)
        z_ = e1 + e2 + e3
        w1, w2, w3 = e1 / z_, e2 / z_, e3 / z_
        out = w1 * o1 + w2 * o2 + w3 * o3
        return (w1 * do_, w2 * do_, w3 * do_, do_ * w1 * (o1 - out), do_ * w2 * (o2 - out), do_ * w3 * (o3 - out))

    mb = rowwise(mix_bwd, [do_c] + outs + lses, [_rows(t, wd_)] * 7, [_sd((seq, wd_))] * 6, [_rows(t, wd_)] * 6,
                 (seq // t,), f"mix{l}_dil_dmix")
    grads = [dil_bwd(sv["dq_r"], sv["dk_r"], sv["dv"], outs[i], lses[i], mb[i], mb[3 + i], seq=seq, dil=dl,
                     n_heads=ndil, name=f"mix{l}_dil{dl}_bwd") for i, dl in enumerate(DIL_BRANCHES)]

    def dil_unprep(q1, q2, q3, k1, k2, k3, v1, v2, v3, c_, s_):
        perm = _pad_perm(PARTIAL_ROPE)
        rot_t = lambda a: jnp.concatenate(
            [_rope_t(a[:, i * HEAD:(i + 1) * HEAD], c_, s_, perm) for i in range(ndil)], axis=1)
        return rot_t(q1 + q2 + q3), rot_t(k1 + k2 + k3), v1 + v2 + v3

    ddq, ddk, ddv = rowwise(dil_unprep, [g[0] for g in grads] + [g[1] for g in grads] + [g[2] for g in grads] + [cos_p, sin_p],
                            [_rows(t, wd_)] * 9 + [_rows(t, HEAD)] * 2, [_sd((seq, wd_))] * 3, [_rows(t, wd_)] * 3,
                            (seq // t,), f"mix{l}_dil_unprep")

    dproj = jnp.concatenate([unheads(dfq), unheads(dfk), unheads(dfv), dcq, dckv, ddq, ddk, ddv], axis=1).astype(ACT)
    dtail = jnp.concatenate([dk_r, dfl, jnp.zeros((seq, TAIL_W - 68), F32)], axis=1)
    dh = _dense_dx(dproj, p["w_in_main"], 0, f"mix{l}_dh_main")
    dh = _dense_dx(dtail, p["w_in_tail"], 0, f"mix{l}_dh_tail", extras=[dh], epilogue=lambda acc, prev: acc + prev)
    dw_main = _dense_dw(sv["h"], dproj, d // NDEV, f"mix{l}_dw_in_main")
    dw_tail = _dense_dw(sv["h"], dtail, d // NDEV, f"mix{l}_dw_in_tail")
    dx, dg_mix = rms_bwd(sv["x"], p["mix_norm"], dh, f"mix{l}_drms", add=dout)
    return dx, dict(mix_norm=dg_mix, w_in_main=dw_main, w_in_tail=dw_tail, fox_forget_bias=d_fox_bias, mla_q_norm=dg_q,
                    mla_kv_norm=dg_kv, mla_w_uq=dw_uq, mla_w_ukv=dw_ukv, w_out=dw_out)


def loss_head(x, g, target, name):
    seq, d = x.shape
    t = _tile(seq, 512)

    def fn(x_, g_, tgt):
        err = _rms(x_, g_) - tgt
        part = 0.5 * jnp.sum(jnp.mean(err * err, axis=-1, keepdims=True), axis=0, keepdims=True)
        dx, dg = _rms_bwd(x_, g_, err / d)
        return jnp.broadcast_to(part, (1, HEAD)), dx, dg

    return rowwise(fn, [x, g.reshape(1, d), target], [_rows(t, d), _whole((1, d)), _rows(t, d)],
                   [_sd((1, HEAD)), _sd((seq, d)), _sd((1, d))], [_whole((1, HEAD)), _rows(t, d), _whole((1, d))],
                   (seq // t,), name, acc={0: (0,), 2: (0,)})


BIG = ("ffn1_w_gate", "ffn1_w_up", "ffn1_w_down", "w_in", "mla_w_uq", "mla_w_ukv", "w_out", "ffn2_w_gate", "ffn2_w_up",
       "ffn2_w_down")
GROUPS = {
    "ffn1": ("ffn1_w_gate", "ffn1_w_up", "ffn1_w_down"),
    "mix": ("w_in_main", "w_in_tail", "mla_w_uq", "mla_w_ukv", "w_out"),
    "ffn2": ("ffn2_w_gate", "ffn2_w_up", "ffn2_w_down"),
}
PREFETCH = 2
SMALL_D = ("ffn1_norm", "mix_norm", "ffn2_norm")
WEIGHTS = ("ffn1_norm", "ffn1_w_gate", "ffn1_w_up", "ffn1_w_down", "mix_norm", "w_in", "fox_forget_bias", "mla_q_norm",
           "mla_kv_norm", "mla_w_uq", "mla_w_ukv", "w_out", "ffn2_norm", "ffn2_w_gate", "ffn2_w_up", "ffn2_w_down",
           "final_norm")


def pack_small(vals, depth, d):
    rows = [vals[n].reshape(depth, d) for n in SMALL_D]
    rows.append(vals["final_norm"].reshape(1, d))
    qk = jnp.concatenate([vals["mla_q_norm"].reshape(-1), vals["mla_kv_norm"].reshape(-1)])
    rows.append(jnp.pad(qk, (0, -qk.shape[0] % d)).reshape(-1, d))
    last = jnp.concatenate([vals["fox_forget_bias"].reshape(-1), vals["loss"].reshape(-1)])
    rows.append(jnp.pad(last, (0, d - last.shape[0])).reshape(1, d))
    out = jnp.concatenate(rows, axis=0)
    return jnp.pad(out, ((0, -out.shape[0] % 8), (0, 0)))


def unpack_small(a, depth, d, rank):
    out, r = {}, 0
    for n in SMALL_D:
        out[n] = a[r:r + depth]
        r += depth
    out["final_norm"] = a[r]
    r += 1
    n_qk = -(-2 * depth * rank // d)
    qk = a[r:r + n_qk].reshape(-1)[:2 * depth * rank].reshape(2, depth, rank)
    out["mla_q_norm"], out["mla_kv_norm"] = qk[0], qk[1]
    r += n_qk
    out["fox_forget_bias"] = a[r, :depth * 4].reshape(depth, 4)
    out["loss"] = a[r, depth * 4]
    return out


def step(x, target, w, m, v):
    depth = w["ffn1_norm"].shape[0]
    seq, d = x.shape[1], x.shape[2]
    rank = w["mla_q_norm"].shape[1]
    x = x.reshape(seq, d)
    target = target.reshape(seq, d)

    local = {n: w[n].astype(COMM) for n in BIG if n != "w_in"}
    local["w_in_main"], local["w_in_tail"] = [a.astype(COMM) for a in split_w_in(w["w_in"])]
    consts = (*rope_tables(seq, MLA_ROPE), *[jnp.pad(a, ((0, 0), (0, HEAD - PARTIAL_ROPE)), constant_values=c)
                                             for a, c in zip(rope_tables(seq, PARTIAL_ROPE), (1.0, 0.0))])
    order = [(l, k) for l in range(depth) for k in GROUPS]
    small_of = lambda l: {n: w[n][l] for n in ("mix_norm", "fox_forget_bias", "mla_q_norm", "mla_kv_norm")}

    handles, tokens = {}, {}

    def launch(i, dep):
        l, k = order[i]
        xs = [local[n][l:l + 1] for n in GROUPS[k]]
        if dep is not None:
            xs = lax.optimization_barrier((xs, dep))[0]
        handles[i], tokens[i] = exchange_start(xs, False, f"gather_start_{k}{l}")

    launched = min(2, len(order))
    for i in range(launched):
        launch(i, None)
    gathered, saved = {}, {}
    for i, (l, k) in enumerate(order):
        wts = dict(zip(GROUPS[k], exchange_wait(handles[i], x, f"gather_wait_{k}{l}")))
        gathered[l, k] = wts
        tok = None
        while launched < min(len(order), i + 1 + PREFETCH):
            launch(launched, (x, wts[GROUPS[k][0]]))
            tok = tokens[launched] if tok is None else tok + tokens[launched]
            launched += 1
        if k == "mix":
            x, saved[l, k] = mixer_fwd(x, {**wts, **small_of(l)}, l, consts, dep=tok)
        else:
            x, saved[l, k] = ffn_fwd(x, w[f"{k}_norm"][l], wts[f"{k}_w_gate"], wts[f"{k}_w_up"], wts[f"{k}_w_down"], 0,
                                     f"{k}_{l}", dep=tok)
    loss, dx, d_final = loss_head(x, w["final_norm"], target, "loss_head")

    small = {n: [None] * depth for n in SMALL_D + ("mla_q_norm", "mla_kv_norm", "fox_forget_bias")}
    pending, tok = [], None
    for l, k in reversed(order):
        wts = gathered[l, k]
        if k == "mix":
            dx, gm = mixer_bwd(dx, saved[l, k], {**wts, **small_of(l)}, l, consts, dep=tok)
        else:
            dx, (dg, dwg, dwu, dwd) = ffn_bwd(dx, saved[l, k], w[f"{k}_norm"][l], wts[f"{k}_w_gate"], wts[f"{k}_w_up"],
                                              wts[f"{k}_w_down"], 0, f"{k}_{l}", dep=tok)
            gm = {f"{k}_norm": dg, f"{k}_w_gate": dwg, f"{k}_w_up": dwu, f"{k}_w_down": dwd}
        for n in small:
            if n in gm:
                small[n][l] = gm[n]
        handle, tok = exchange_start([gm[n] for n in GROUPS[k]], True, f"scatter_start_{k}{l}")
        pending.append((l, k, handle))
    big = {n: [None] * depth for n in local}

    def land(l, k, handle, after):
        for n, a in zip(GROUPS[k], exchange_wait(handle, after, f"scatter_wait_{k}{l}")):
            big[n][l] = a

    for l, k, handle in pending[:-1]:
        land(l, k, handle, dx)

    out = {"grad_x": dx.reshape(1, seq, d)}

    def update(name, parts, shape):
        flat = lambda a: a.reshape(-1, shape[-1])
        res = adamw(parts, flat(w[name]), flat(m[name]), flat(v[name]), f"adamw_{name}")
        for kind, r in zip(("grad", "delta", "new_m", "new_v"), res):
            out[f"{kind}_{name}"] = r.reshape(shape)

    last = GROUPS[pending[-1][1]]
    for n in BIG:
        if n != "w_in" and n not in last:
            update(n, [a.reshape(NDEV, -1, a.shape[-1]) for a in big[n]], w[n].shape)
    g_in = [merge_w_in(sum_parts(big["w_in_main"][l].reshape(NDEV, -1, MAIN_W), f"sum_w_in_main{l}"),
                       sum_parts(big["w_in_tail"][l].reshape(NDEV, -1, TAIL_W), f"sum_w_in_tail{l}"))[None]
            for l in range(depth)]
    update("w_in", g_in, w["w_in"].shape)
    land(*pending[-1], out["new_v_w_in"])
    for n in last:
        update(n, [a.reshape(NDEV, -1, a.shape[-1]) for a in big[n]], w[n].shape)

    part = {n: jnp.stack(g).reshape(depth, -1) for n, g in small.items()}
    part["final_norm"], part["loss"] = d_final, loss[0, 0:1]
    parts = exchange(pack_small(part, depth, d), False, "gather_small")
    zero = jnp.zeros((1,), F32)
    packed = [pack_small({**{n: a[n] for n in part if n != "loss"}, "loss": zero}, depth, d) for a in (w, m, v)]
    res = [unpack_small(r, depth, d, rank) for r in adamw([parts], *packed, "adamw_small")]
    out["loss"] = res[0]["loss"]
    for n in small.keys() | {"final_norm"}:
        for kind, r in zip(("grad", "delta", "new_m", "new_v"), res):
            out[f"{kind}_{n}"] = r[n].reshape(w[n].shape)
    return out


def kernel(x, ffn1_norm, ffn1_w_gate, ffn1_w_up, ffn1_w_down, mix_norm, w_in, fox_forget_bias, mla_q_norm, mla_kv_norm, mla_w_uq, mla_w_ukv, w_out, ffn2_norm, ffn2_w_gate, ffn2_w_up, ffn2_w_down, final_norm, loss_target, m_ffn1_norm, m_ffn1_w_gate, m_ffn1_w_up, m_ffn1_w_down, m_mix_norm, m_w_in, m_fox_forget_bias, m_mla_q_norm, m_mla_kv_norm, m_mla_w_uq, m_mla_w_ukv, m_w_out, m_ffn2_norm, m_ffn2_w_gate, m_ffn2_w_up, m_ffn2_w_down, m_final_norm, v_ffn1_norm, v_ffn1_w_gate, v_ffn1_w_up, v_ffn1_w_down, v_mix_norm, v_w_in, v_fox_forget_bias, v_mla_q_norm, v_mla_kv_norm, v_mla_w_uq, v_mla_w_ukv, v_w_out, v_ffn2_norm, v_ffn2_w_gate, v_ffn2_w_up, v_ffn2_w_down, v_final_norm):
    args = locals()
    w = {n: args[n] for n in WEIGHTS}
    m = {n: args["m_" + n] for n in WEIGHTS}
    v = {n: args["v_" + n] for n in WEIGHTS}
    out = step(x, loss_target, w, m, v)
    return (out["loss"], out["grad_x"], *[out["grad_" + n] for n in WEIGHTS], *[out["delta_" + n] for n in WEIGHTS],
            *[out["new_m_" + n] for n in WEIGHTS], *[out["new_v_" + n] for n in WEIGHTS])
```

```python
import functools

import jax
import jax.numpy as jnp
from jax import lax
from jax.experimental import pallas as pl
from jax.experimental.pallas import tpu as pltpu

F32 = jnp.float32
MXU = jnp.bfloat16
ACT = jnp.bfloat16
COMM = jnp.bfloat16
HI = lax.Precision.HIGHEST
NN = (((1,), (0,)), ((), ()))
NT = (((1,), (1,)), ((), ()))
TN = (((0,), (0,)), ((), ()))

NDEV = 8
HEAD = 128
EPS = 1e-6
ROPE_THETA = 500000.0
PARTIAL_ROPE = HEAD // 4
MLA_ROPE = 64
MLA_QK = HEAD + MLA_ROPE
DIL_BRANCHES = (1, 4, 16)
NEG = -1e30
VMEM_LIMIT = 48 * 1024 * 1024

ADAMW_STEP_BYTES = 12 * 1024 * 1024

ADAM_LR, ADAM_B1, ADAM_B2, ADAM_EPS, ADAM_WD, ADAM_STEP = 0.001, 0.9, 0.999, 1e-08, 0.01, 10


def _cparams(n_axes):
    return pltpu.CompilerParams(dimension_semantics=("arbitrary",) * n_axes, vmem_limit_bytes=VMEM_LIMIT)


def _tile(n, t):
    t = min(n, t)
    assert n % t == 0, (n, t)
    return t


def _dep_spec(dep):
    nd = dep.ndim
    return pl.BlockSpec(dep.shape, lambda *_: (0,) * nd)


def mm(a, b, *, name, dims, grid, a_spec, b_spec, o_spec, out_shape, nk, extras=(), extra_specs=(), epilogue=None, dep=None):
    n_ex = len(extras)
    kaxis = len(grid) - 1
    if dep is not None:
        extras, extra_specs = [*extras, dep], [*extra_specs, _dep_spec(dep)]
    n_more = len(extras)
    n_out = len(out_shape) if isinstance(out_shape, (list, tuple)) else 1

    def body(a_ref, b_ref, *rest):
        ex, o_refs = rest[:n_ex], rest[n_more:n_more + n_out]
        part = lax.dot_general(a_ref[...].astype(MXU), b_ref[...].astype(MXU), dims, preferred_element_type=F32)

        def finish(acc):
            res = acc if epilogue is None else epilogue(acc, *[e[...] for e in ex])
            for o_ref, r in zip(o_refs, res if n_out > 1 else (res,)):
                o_ref[...] = r.astype(o_ref.dtype)

        if nk == 1:
            finish(part)
        else:
            acc_ref = rest[n_more + n_out]
            k = pl.program_id(kaxis)

            @pl.when(k == 0)
            def _():
                acc_ref[...] = part

            @pl.when(k > 0)
            def _():
                acc_ref[...] += part

            @pl.when(k == nk - 1)
            def _():
                finish(acc_ref[...])

    acc_shape = tuple(d for d in o_spec.block_shape if d is not None)
    return pl.pallas_call(
        body, name=name, grid=grid, in_specs=[a_spec, b_spec, *extra_specs],
        out_specs=[o_spec] * n_out if n_out > 1 else o_spec, out_shape=out_shape,
        scratch_shapes=[] if nk == 1 else [pltpu.VMEM(acc_shape, F32)], compiler_params=_cparams(len(grid)),
    )(a, b, *extras)


def rowwise(fn, ins, in_specs, outs, out_specs, grid, name, acc=None, dep=None):
    acc = acc or {}
    n_in = len(ins)
    if dep is not None:
        ins, in_specs = [*ins, dep], [*in_specs, _dep_spec(dep)]
    n_all = len(ins)

    def body(*refs):
        vals = fn(*[r[...] for r in refs[:n_in]])
        if not isinstance(vals, (tuple, list)):
            vals = (vals,)
        for i, (r, v) in enumerate(zip(refs[n_all:], vals)):
            if i in acc:
                first = functools.reduce(jnp.logical_and, [pl.program_id(ax) == 0 for ax in acc[i]])

                @pl.when(first)
                def _(r=r, v=v):
                    r[...] = v.astype(r.dtype)

                @pl.when(jnp.logical_not(first))
                def _(r=r, v=v):
                    r[...] += v.astype(r.dtype)
            else:
                r[...] = v.astype(r.dtype)

    return pl.pallas_call(
        body, name=name, grid=grid, in_specs=in_specs, out_specs=out_specs, out_shape=outs,
        compiler_params=_cparams(len(grid)),
    )(*ins)


def _rows(t, c, col=0):
    return pl.BlockSpec((t, c), lambda m, col=col: (m, col))


def _whole(shape):
    nd = len(shape)
    return pl.BlockSpec(shape, lambda *_: (0,) * nd)


def _rms(x, g):
    x = x.astype(F32)
    return x * lax.rsqrt(jnp.mean(x * x, axis=-1, keepdims=True) + EPS) * g


def _rms_bwd(x, g, dy):
    x = x.astype(F32)
    dy = dy.astype(F32)
    r = lax.rsqrt(jnp.mean(x * x, axis=-1, keepdims=True) + EPS)
    xh = x * r
    dg = jnp.sum(dy * xh, axis=0, keepdims=True)
    dxh = dy * g
    dx = r * (dxh - xh * jnp.mean(dxh * xh, axis=-1, keepdims=True))
    return dx, dg


def _swap_matrix(n):
    i = lax.broadcasted_iota(jnp.int32, (n, n), 0)
    j = lax.broadcasted_iota(jnp.int32, (n, n), 1)
    return jnp.where(((i + n // 2) % n) == j, 1.0, 0.0).astype(F32)


def _rope(x, cos, sin_signed, perm):
    return x * cos + jnp.dot(x, perm, precision=HI, preferred_element_type=F32) * sin_signed


def _rope_t(dy, cos, sin_signed, perm):
    return dy * cos + jnp.dot(dy * sin_signed, perm, precision=HI, preferred_element_type=F32)


def rope_tables(seq, dim):
    inv = 1.0 / (ROPE_THETA ** (jnp.arange(0, dim, 2, dtype=F32) / dim))
    ang = jnp.arange(seq, dtype=F32)[:, None] * inv[None, :]
    cos, sin = jnp.cos(ang), jnp.sin(ang)
    return jnp.concatenate([cos, cos], axis=1), jnp.concatenate([-sin, sin], axis=1)


HP = 2


def _hspec(arr_kind, t, w, off, seq_of):
    if arr_kind == "cols":
        assert off % HP == 0
        return pl.BlockSpec((t, HP * w), lambda h, i, j: (seq_of(i, j), off // HP + h))
    return pl.BlockSpec((HP, t, w), lambda h, i, j: (h, seq_of(i, j), off))


def _head(ref, arr_kind, hh, w):
    return ref[:, hh * w:(hh + 1) * w] if arr_kind == "cols" else ref[hh]


def _colspec(t, seq_of):
    return pl.BlockSpec((HP, t, 1), lambda h, i, j: (h, seq_of(i, j), 0))


def _rowspec(t, seq_of):
    return pl.BlockSpec((HP, 1, t), lambda h, i, j: (h, 0, seq_of(i, j)))


def _causal(s, qi, kj, t, transposed=False):
    a = lax.broadcasted_iota(jnp.int32, (t, t), 0)
    b = lax.broadcasted_iota(jnp.int32, (t, t), 1)
    keep = (kj * t + a <= qi * t + b) if transposed else (kj * t + b <= qi * t + a)
    return jnp.where(keep, s, NEG)


def flash_fwd(q, k, v, cum, *, n_heads, seq, t, scale, name):
    nb = seq // t
    qs, ks = (lambda i, j: i), (lambda i, j: jnp.minimum(j, i))
    ins = [q[0], k[0], v[0]]
    specs = [_hspec(q[1], t, q[2], q[3], qs), _hspec(k[1], t, k[2], k[3], ks), _hspec(v[1], t, v[2], v[3], ks)]
    if cum is not None:
        ins += [cum[0], cum[1]]
        specs += [_colspec(t, qs), _rowspec(t, ks)]

    def body(*refs):
        q_ref, k_ref, v_ref = refs[:3]
        o_ref, lse_ref, m_s, l_s, acc_s = refs[-5:]
        i, j = pl.program_id(1), pl.program_id(2)

        @pl.when(j == 0)
        def _():
            m_s[...] = jnp.full(m_s.shape, NEG, F32)
            l_s[...] = jnp.zeros(l_s.shape, F32)
            acc_s[...] = jnp.zeros(acc_s.shape, F32)

        def block(masked):
            new = []
            for hh in range(HP):
                qb = _head(q_ref, q[1], hh, q[2]).astype(MXU)
                kb = _head(k_ref, k[1], hh, k[2]).astype(MXU)
                s = lax.dot_general(qb, kb, NT, preferred_element_type=F32) * scale
                if cum is not None:
                    s = s + (refs[3][hh] - refs[4][hh])
                if masked:
                    s = _causal(s, i, j, t)
                m_old = m_s[hh]
                m_new = jnp.maximum(m_old, jnp.max(s, axis=1, keepdims=True))
                alpha = jnp.exp(m_old - m_new)
                p = jnp.exp(s - m_new)
                vb = _head(v_ref, v[1], hh, v[2]).astype(MXU)
                pv = jnp.dot(p.astype(MXU), vb, preferred_element_type=F32)
                new.append((m_new, alpha, alpha * l_s[hh] + jnp.sum(p, axis=1, keepdims=True), pv))
            for hh, (m_new, alpha, l_new, pv) in enumerate(new):
                acc_s[hh] = alpha * acc_s[hh] + pv
                l_s[hh] = l_new
                m_s[hh] = m_new

        @pl.when(j < i)
        def _():
            block(False)

        @pl.when(j == i)
        def _():
            block(True)

        @pl.when(j == nb - 1)
        def _():
            for hh in range(HP):
                o_ref[:, hh * HEAD:(hh + 1) * HEAD] = (acc_s[hh] / l_s[hh]).astype(o_ref.dtype)
                lse_ref[hh] = m_s[hh] + jnp.log(l_s[hh])

    return pl.pallas_call(
        body, name=name, grid=(n_heads // HP, nb, nb), in_specs=specs,
        out_specs=[pl.BlockSpec((t, HP * HEAD), lambda h, i, j: (i, h)), _colspec(t, qs)],
        out_shape=[jax.ShapeDtypeStruct((seq, n_heads * HEAD), F32), jax.ShapeDtypeStruct((n_heads, seq, 1), F32)],
        scratch_shapes=[pltpu.VMEM((HP, t, 1), F32), pltpu.VMEM((HP, t, 1), F32), pltpu.VMEM((HP, t, HEAD), F32)],
        compiler_params=_cparams(3),
    )(*ins)


def flash_bwd_dq(q, k, v, do, lse, delta, cum, *, n_heads, seq, t, scale, name):
    nb = seq // t
    qs, ks = (lambda i, j: i), (lambda i, j: jnp.minimum(j, i))
    ins = [q[0], k[0], v[0], do, lse, delta]
    specs = [_hspec(q[1], t, q[2], q[3], qs), _hspec(k[1], t, k[2], k[3], ks), _hspec(v[1], t, v[2], v[3], ks),
             _hspec("cols", t, HEAD, 0, qs), _colspec(t, qs), _colspec(t, qs)]
    if cum is not None:
        ins += [cum[0], cum[1]]
        specs += [_colspec(t, qs), _rowspec(t, ks)]
    wq = q[2]
    n_out = 1 if cum is None else 2

    def body(*refs):
        q_ref, k_ref, v_ref, do_ref, lse_ref, dl_ref = refs[:6]
        outs = refs[-2 * n_out:-n_out]
        accs = refs[-n_out:]
        i, j = pl.program_id(1), pl.program_id(2)

        @pl.when(j == 0)
        def _():
            for a in accs:
                a[...] = jnp.zeros(a.shape, F32)

        def block(masked):
            for hh in range(HP):
                kb = _head(k_ref, k[1], hh, k[2]).astype(MXU)
                s = lax.dot_general(_head(q_ref, q[1], hh, q[2]).astype(MXU), kb, NT, preferred_element_type=F32) * scale
                if cum is not None:
                    s = s + (refs[6][hh] - refs[7][hh])
                if masked:
                    s = _causal(s, i, j, t)
                p = jnp.exp(s - lse_ref[hh])
                dp = lax.dot_general(_head(do_ref, "cols", hh, HEAD).astype(MXU), _head(v_ref, v[1], hh, v[2]).astype(MXU),
                                     NT, preferred_element_type=F32)
                ds = p * (dp - dl_ref[hh])
                accs[0][hh] += jnp.dot(ds.astype(MXU), kb, preferred_element_type=F32)
                if cum is not None:
                    accs[1][hh] += jnp.sum(ds, axis=1, keepdims=True)

        @pl.when(j < i)
        def _():
            block(False)

        @pl.when(j == i)
        def _():
            block(True)

        @pl.when(j == nb - 1)
        def _():
            outs[0][...] = accs[0][...] * scale
            if cum is not None:
                outs[1][...] = accs[1][...]

    out_specs = [pl.BlockSpec((HP, t, wq), lambda h, i, j: (h, i, 0))]
    out_shape = [jax.ShapeDtypeStruct((n_heads, seq, wq), F32)]
    scratch = [pltpu.VMEM((HP, t, wq), F32)]
    if cum is not None:
        out_specs.append(_colspec(t, qs))
        out_shape.append(jax.ShapeDtypeStruct((n_heads, seq, 1), F32))
        scratch.append(pltpu.VMEM((HP, t, 1), F32))
    res = pl.pallas_call(
        body, name=name, grid=(n_heads // HP, nb, nb), in_specs=specs, out_specs=out_specs, out_shape=out_shape,
        scratch_shapes=scratch, compiler_params=_cparams(3),
    )(*ins)
    return res[0] if cum is None else res


def flash_bwd_dkv(q, k, v, do, lse_row, delta_row, cum, *, n_heads, seq, t, scale, name):
    nb = seq // t
    ks, qs = (lambda j, i: j), (lambda j, i: jnp.maximum(i, j))
    ins = [q[0], k[0], v[0], do, lse_row, delta_row]
    specs = [_hspec(q[1], t, q[2], q[3], qs), _hspec(k[1], t, k[2], k[3], ks), _hspec(v[1], t, v[2], v[3], ks),
             _hspec("cols", t, HEAD, 0, qs), _rowspec(t, qs), _rowspec(t, qs)]
    if cum is not None:
        ins += [cum[0], cum[1]]
        specs += [_colspec(t, ks), _rowspec(t, qs)]
    wk = k[2]
    n_out = 2 if cum is None else 3

    def body(*refs):
        q_ref, k_ref, v_ref, do_ref, lse_ref, dl_ref = refs[:6]
        outs = refs[-2 * n_out:-n_out]
        accs = refs[-n_out:]
        j, i = pl.program_id(1), pl.program_id(2)

        @pl.when(i == 0)
        def _():
            for a in accs:
                a[...] = jnp.zeros(a.shape, F32)

        def block(masked):
            for hh in range(HP):
                qb = _head(q_ref, q[1], hh, q[2]).astype(MXU)
                dob = _head(do_ref, "cols", hh, HEAD).astype(MXU)
                st = lax.dot_general(_head(k_ref, k[1], hh, k[2]).astype(MXU), qb, NT, preferred_element_type=F32) * scale
                if cum is not None:
                    st = st + (refs[7][hh] - refs[6][hh])
                if masked:
                    st = _causal(st, i, j, t, transposed=True)
                pt = jnp.exp(st - lse_ref[hh])
                dpt = lax.dot_general(_head(v_ref, v[1], hh, v[2]).astype(MXU), dob, NT, preferred_element_type=F32)
                dst = pt * (dpt - dl_ref[hh])
                accs[0][hh] += jnp.dot(dst.astype(MXU), qb, preferred_element_type=F32)
                accs[1][hh] += jnp.dot(pt.astype(MXU), dob, preferred_element_type=F32)
                if cum is not None:
                    accs[2][hh] -= jnp.sum(dst, axis=1, keepdims=True)

        @pl.when(i > j)
        def _():
            block(False)

        @pl.when(i == j)
        def _():
            block(True)

        @pl.when(i == nb - 1)
        def _():
            outs[0][...] = accs[0][...] * scale
            for o, a in zip(outs[1:], accs[1:]):
                o[...] = a[...]

    out_specs = [pl.BlockSpec((HP, t, wk), lambda h, j, i: (h, j, 0)), pl.BlockSpec((HP, t, HEAD), lambda h, j, i: (h, j, 0))]
    out_shape = [jax.ShapeDtypeStruct((n_heads, seq, wk), F32), jax.ShapeDtypeStruct((n_heads, seq, HEAD), F32)]
    scratch = [pltpu.VMEM((HP, t, wk), F32), pltpu.VMEM((HP, t, HEAD), F32)]
    if cum is not None:
        out_specs.append(_colspec(t, ks))
        out_shape.append(jax.ShapeDtypeStruct((n_heads, seq, 1), F32))
        scratch.append(pltpu.VMEM((HP, t, 1), F32))
    return pl.pallas_call(
        body, name=name, grid=(n_heads // HP, nb, nb), in_specs=specs, out_specs=out_specs, out_shape=out_shape,
        scratch_shapes=scratch, compiler_params=_cparams(3),
    )(*ins)


def attn_delta(do, o, *, n_heads, seq, name):
    t = _tile(seq, 512)
    spec = pl.BlockSpec((t, HEAD), lambda h, m: (m, h))
    return rowwise(
        lambda a, b: jnp.sum(a.astype(F32) * b.astype(F32), axis=1, keepdims=True), [do, o], [spec, spec],
        jax.ShapeDtypeStruct((n_heads, seq, 1), F32), pl.BlockSpec((None, t, 1), lambda h, m: (h, m, 0)),
        (n_heads, seq // t), name)


def _dil_scores(q, kc, kp, n, scale):
    i = lax.broadcasted_iota(jnp.int32, (HEAD, HEAD), 0)
    j = lax.broadcasted_iota(jnp.int32, (HEAD, HEAD), 1)
    sc = lax.dot_general(q, kc, NT, preferred_element_type=F32) * scale
    sp = lax.dot_general(q, kp, NT, preferred_element_type=F32) * scale
    sc = jnp.where(j <= i, sc, NEG)
    sp = jnp.where(jnp.logical_and(j >= i, n > 0), sp, NEG)
    return sc, sp


def _strip_spec(length, n_heads, col_blocks, off):
    return pl.BlockSpec((length, HEAD), lambda r, h: (0, r * col_blocks + off + h))


def dil_fwd(q, k, v, *, seq, dil, n_heads, name):
    length = seq // dil
    nb = length // HEAD
    scale = HEAD ** -0.5
    view = lambda a: a.reshape(length, dil * a.shape[1])
    spec = _strip_spec(length, n_heads, n_heads, 0)

    def body(q_ref, k_ref, v_ref, o_ref, lse_ref):
        def step(n, carry):
            cur = pl.ds(pl.multiple_of(n * HEAD, HEAD), HEAD)
            prev = pl.ds(pl.multiple_of(jnp.maximum(n - 1, 0) * HEAD, HEAD), HEAD)
            qb = q_ref[cur, :].astype(MXU)
            sc, sp = _dil_scores(qb, k_ref[cur, :].astype(MXU), k_ref[prev, :].astype(MXU), n, scale)
            m = jnp.maximum(jnp.max(sc, axis=1, keepdims=True), jnp.max(sp, axis=1, keepdims=True))
            ec, ep = jnp.exp(sc - m), jnp.exp(sp - m)
            l = jnp.sum(ec, axis=1, keepdims=True) + jnp.sum(ep, axis=1, keepdims=True)
            o = jnp.dot((ec / l).astype(MXU), v_ref[cur, :].astype(MXU), preferred_element_type=F32)
            o = o + jnp.dot((ep / l).astype(MXU), v_ref[prev, :].astype(MXU), preferred_element_type=F32)
            o_ref[cur, :] = o
            lse_ref[cur, :] = jnp.broadcast_to(m + jnp.log(l), (HEAD, HEAD))
            return carry

        lax.fori_loop(0, nb, step, 0)

    out = jax.ShapeDtypeStruct((length, dil * n_heads * HEAD), F32)
    o, lse = pl.pallas_call(
        body, name=name, grid=(dil, n_heads), in_specs=[spec, spec, spec], out_specs=[spec, spec], out_shape=[out, out],
        compiler_params=_cparams(2),
    )(view(q), view(k), view(v))
    return o.reshape(seq, -1), lse.reshape(seq, -1)


def dil_bwd(q, k, v, o, lse, do, dlse, *, seq, dil, n_heads, name):
    length = seq // dil
    nb = length // HEAD
    scale = HEAD ** -0.5
    view = lambda a: a.reshape(length, dil * a.shape[1])
    spec = _strip_spec(length, n_heads, n_heads, 0)

    def body(q_ref, k_ref, v_ref, o_ref, lse_ref, do_ref, dlse_ref, dq_ref, dk_ref, dv_ref):
        dk_ref[...] = jnp.zeros(dk_ref.shape, F32)
        dv_ref[...] = jnp.zeros(dv_ref.shape, F32)

        def step(n, carry):
            cur = pl.ds(pl.multiple_of(n * HEAD, HEAD), HEAD)
            prev = pl.ds(pl.multiple_of(jnp.maximum(n - 1, 0) * HEAD, HEAD), HEAD)
            qb = q_ref[cur, :].astype(MXU)
            kc, kp = k_ref[cur, :].astype(MXU), k_ref[prev, :].astype(MXU)
            vc, vp = v_ref[cur, :].astype(MXU), v_ref[prev, :].astype(MXU)
            sc, sp = _dil_scores(qb, kc, kp, n, scale)
            lse_b = jnp.max(lse_ref[cur, :], axis=1, keepdims=True)
            pc, pp = jnp.exp(sc - lse_b), jnp.exp(sp - lse_b)
            dob = do_ref[cur, :]
            shift = jnp.sum(dlse_ref[cur, :], axis=1, keepdims=True) - jnp.sum(dob * o_ref[cur, :], axis=1, keepdims=True)
            dob = dob.astype(MXU)
            dsc = pc * (lax.dot_general(dob, vc, NT, preferred_element_type=F32) + shift)
            dsp = pp * (lax.dot_general(dob, vp, NT, preferred_element_type=F32) + shift)
            dscb, dspb = dsc.astype(MXU), dsp.astype(MXU)
            dq = jnp.dot(dscb, kc, preferred_element_type=F32) + jnp.dot(dspb, kp, preferred_element_type=F32)
            dq_ref[cur, :] = dq * scale
            dk_ref[cur, :] += lax.dot_general(dscb, qb, TN, preferred_element_type=F32) * scale
            dv_ref[cur, :] += lax.dot_general(pc.astype(MXU), dob, TN, preferred_element_type=F32)
            dk_ref[prev, :] += lax.dot_general(dspb, qb, TN, preferred_element_type=F32) * scale
            dv_ref[prev, :] += lax.dot_general(pp.astype(MXU), dob, TN, preferred_element_type=F32)
            return carry

        lax.fori_loop(0, nb, step, 0)

    out = jax.ShapeDtypeStruct((length, dil * n_heads * HEAD), F32)
    res = pl.pallas_call(
        body, name=name, grid=(dil, n_heads), in_specs=[spec] * 7, out_specs=[spec] * 3, out_shape=[out] * 3,
        compiler_params=_cparams(2),
    )(*[view(a) for a in (q, k, v, o, lse, do, dlse)])
    return [r.reshape(seq, -1) for r in res]


def _tri(n, kind):
    i = lax.broadcasted_iota(jnp.int32, (n, n), 0)
    j = lax.broadcasted_iota(jnp.int32, (n, n), 1)
    return jnp.where({"le": i <= j, "ge": i >= j}[kind], 1.0, 0.0).astype(F32)


def _block_matrix(n_rows, per_head, kind):
    r = lax.broadcasted_iota(jnp.int32, (n_rows, n_rows), 0)
    c = lax.broadcasted_iota(jnp.int32, (n_rows, n_rows), 1)
    same = (r // per_head) == (c // per_head)
    rel = {"lt": c < r, "gt": c > r, "all": c == c}[kind]
    return jnp.where(jnp.logical_and(same, rel), 1.0, 0.0).astype(F32)


def _lane_pick(x, lane):
    j = lax.broadcasted_iota(jnp.int32, x.shape, 1)
    return jnp.sum(jnp.where(j == lane, x, 0.0), axis=1, keepdims=True)


def _log_sigmoid(z):
    return jnp.minimum(z, 0.0) - jnp.log1p(jnp.exp(-jnp.abs(z)))


def fox_gate_fwd(z, bias_rows, per_head, name):
    n_rows = z.shape[0]

    def body(z_ref, b_ref, c_ref):
        logf = _log_sigmoid(z_ref[...] + b_ref[...])
        within = jnp.dot(logf, _tri(HEAD, "le"), precision=HI, preferred_element_type=F32)
        tot = jnp.broadcast_to(_lane_pick(within, HEAD - 1), (n_rows, HEAD))
        c_ref[...] = within + jnp.dot(_block_matrix(n_rows, per_head, "lt"), tot, precision=HI, preferred_element_type=F32)

    return pl.pallas_call(body, name=name, out_shape=jax.ShapeDtypeStruct(z.shape, F32),
                          compiler_params=pltpu.CompilerParams(vmem_limit_bytes=VMEM_LIMIT))(z, bias_rows)


def fox_gate_bwd(z, bias_rows, dcum_q, dcum_k, per_head, name):
    n_rows = z.shape[0]

    def body(z_ref, b_ref, dcq_ref, dck_ref, dz_ref, db_ref):
        within = jnp.dot(dcq_ref[...] + dck_ref[...], _tri(HEAD, "ge"), precision=HI, preferred_element_type=F32)
        tot = jnp.broadcast_to(_lane_pick(within, 0), (n_rows, HEAD))
        dlogf = within + jnp.dot(_block_matrix(n_rows, per_head, "gt"), tot, precision=HI, preferred_element_type=F32)
        dz = dlogf * jax.nn.sigmoid(-(z_ref[...] + b_ref[...]))
        dz_ref[...] = dz
        rs = jnp.broadcast_to(jnp.sum(dz, axis=1, keepdims=True), (n_rows, HEAD))
        db_ref[...] = jnp.dot(_block_matrix(n_rows, per_head, "all"), rs, precision=HI, preferred_element_type=F32)

    shp = jax.ShapeDtypeStruct(z.shape, F32)
    return pl.pallas_call(body, name=name, out_shape=[shp, shp],
                          compiler_params=pltpu.CompilerParams(vmem_limit_bytes=VMEM_LIMIT))(z, bias_rows, dcum_q, dcum_k)


def exchange(x, scatter, name):
    blk = x.shape[1:] if scatter else x.shape

    def body(x_ref, o_ref, send_sems, recv_sems, local_sem):
        mx, my, mc = lax.axis_index("x"), lax.axis_index("y"), lax.axis_index("c")
        me = 4 * mx + 2 * my + mc
        flip = lambda v, f: 1 - v if f else v
        local = pltpu.make_async_copy(x_ref.at[me] if scatter else x_ref, o_ref.at[me], local_sem)
        local.start()
        sends, recvs = [], []
        for n in range(1, NDEV):
            px, py, pc = flip(mx, n & 4), flip(my, n & 2), flip(mc, n & 1)
            p = 4 * px + 2 * py + pc
            sends.append(pltpu.make_async_remote_copy(
                src_ref=x_ref.at[p] if scatter else x_ref, dst_ref=o_ref.at[me], send_sem=send_sems.at[n - 1],
                recv_sem=recv_sems.at[n - 1], device_id=(px, py, pc), device_id_type=pl.DeviceIdType.MESH))
            recvs.append(pltpu.make_async_remote_copy(
                src_ref=x_ref.at[me] if scatter else x_ref, dst_ref=o_ref.at[p], send_sem=send_sems.at[n - 1],
                recv_sem=recv_sems.at[n - 1], device_id=(px, py, pc), device_id_type=pl.DeviceIdType.MESH))
        for cp in sends:
            cp.start()
        for cp in recvs:
            cp.wait_recv()
        for cp in sends:
            cp.wait_send()
        local.wait()

    hbm = pl.BlockSpec(memory_space=pltpu.HBM)
    return pl.pallas_call(
        body, name=name, in_specs=[hbm], out_specs=hbm, out_shape=jax.ShapeDtypeStruct((NDEV, *blk), x.dtype),
        scratch_shapes=[pltpu.SemaphoreType.DMA((NDEV - 1,)), pltpu.SemaphoreType.DMA((NDEV - 1,)), pltpu.SemaphoreType.DMA],
    )(x)


def _exchange_copies(x_refs, land_refs, send_sems, recv_sems, local_sems, scatter, with_recvs):
    mx, my, mc = lax.axis_index("x"), lax.axis_index("y"), lax.axis_index("c")
    me = 4 * mx + 2 * my + mc
    flip = lambda v, f: 1 - v if f else v
    local, sends, recvs = [], [], []
    for a, (x_ref, o_ref) in enumerate(zip(x_refs, land_refs)):
        local.append(pltpu.make_async_copy(x_ref.at[me] if scatter else x_ref, o_ref.at[me], local_sems.at[a]))
        for n in range(1, NDEV):
            px, py, pc = flip(mx, n & 4), flip(my, n & 2), flip(mc, n & 1)
            p = 4 * px + 2 * py + pc
            sem = (NDEV - 1) * a + n - 1
            mk = lambda src, dst: pltpu.make_async_remote_copy(
                src_ref=src, dst_ref=dst, send_sem=send_sems.at[sem], recv_sem=recv_sems.at[sem],
                device_id=(px, py, pc), device_id_type=pl.DeviceIdType.MESH)
            sends.append(mk(x_ref.at[p] if scatter else x_ref, o_ref.at[me]))
            if with_recvs:
                recvs.append(mk(x_ref.at[me] if scatter else x_ref, o_ref.at[p]))
    return local, sends, recvs


_HBM = pl.BlockSpec(memory_space=pltpu.HBM)
_SEM = pl.BlockSpec(memory_space=pltpu.SEMAPHORE)
_EFFECT = pltpu.SideEffectType.DATAFLOW_SIDE_EFFECTING


def exchange_start(xs, scatter, name):
    n = len(xs)
    lands = [jax.ShapeDtypeStruct((NDEV, *(x.shape[1:] if scatter else x.shape)), x.dtype) for x in xs]

    def body(*refs):
        x_refs, land_refs = refs[:n], refs[n:2 * n]
        send_sems, recv_sems, local_sems = refs[2 * n:2 * n + 3]
        token = refs[-1]
        local, sends, _ = _exchange_copies(x_refs, land_refs, send_sems, recv_sems, local_sems, scatter, False)
        for cp in local + sends:
            cp.start()
        token[...] = jnp.zeros(token.shape, token.dtype)

    n_sem = (NDEV - 1) * n
    out = pl.pallas_call(
        body, name=name,
        out_shape=(pltpu.SemaphoreType.DMA((n_sem,)), pltpu.SemaphoreType.DMA((n_sem,)), pltpu.SemaphoreType.DMA((n,)),
                   *[pltpu.HBM(x.shape, x.dtype) for x in xs], *[pltpu.HBM(s.shape, s.dtype) for s in lands],
                   jax.ShapeDtypeStruct((8, HEAD), F32)),
        in_specs=[_HBM] * (2 * n), out_specs=(_SEM, _SEM, _SEM, *[_HBM] * (2 * n), pl.BlockSpec(memory_space=pltpu.VMEM)),
        input_output_aliases={i: 3 + i for i in range(2 * n)},
        compiler_params=pltpu.CompilerParams(has_side_effects=_EFFECT),
    )(*[pltpu.with_memory_space_constraint(x, pltpu.HBM) for x in xs],
      *[pltpu.with_memory_space_constraint(lax.empty(s.shape, s.dtype), pltpu.HBM) for s in lands])
    return (out[:3], out[3:3 + n], out[3 + n:3 + 2 * n], scatter), out[-1]


def exchange_wait(handle, after, name):
    sems, xs, lands, scatter = handle
    n = len(xs)

    def body(*refs):
        x_refs, land_refs = refs[:n], refs[n:2 * n]
        send_sems, recv_sems, local_sems = refs[2 * n:2 * n + 3]
        local, sends, recvs = _exchange_copies(x_refs, land_refs, send_sems, recv_sems, local_sems, scatter, True)
        for cp in sends:
            cp.wait_send()
        for cp in recvs:
            cp.wait_recv()
        for cp in local:
            cp.wait()

    out = pl.pallas_call(
        body, name=name, out_shape=tuple(pltpu.HBM(a.shape, a.dtype) for a in (*xs, *lands)),
        in_specs=[_HBM] * (2 * n) + [_SEM] * 3 + [pl.BlockSpec(memory_space=pl.ANY)], out_specs=tuple([_HBM] * (2 * n)),
        input_output_aliases={i: i for i in range(2 * n)}, compiler_params=pltpu.CompilerParams(has_side_effects=_EFFECT),
    )(*xs, *lands, *sems, after)
    return list(out[n:])


def adamw(parts, w, m, v, name, dep=None):
    depth = len(parts)
    deps = [] if dep is None else [dep]
    n_parts, rows, cols = parts[0].shape
    t = rows
    for cand in (256, 128, 64, 32, 16, 8):
        if rows % cand == 0 and (n_parts * parts[0].dtype.itemsize + 7 * 4) * cand * cols <= ADAMW_STEP_BYTES:
            t = cand
            break
    nr = rows // t

    def body(*refs):
        p_refs = refs[:depth]
        w_ref, m_ref, v_ref = refs[depth:depth + 3]
        g_out, d_out, m_out, v_out = refs[depth + 3 + len(deps):]
        layer = pl.program_id(0)
        for i in range(depth):
            @pl.when(layer == i)
            def _(p=p_refs[i]):
                g = p[0].astype(F32)
                for j in range(1, n_parts):
                    g = g + p[j].astype(F32)
                m_new = ADAM_B1 * m_ref[...] + (1.0 - ADAM_B1) * g
                v_new = ADAM_B2 * v_ref[...] + (1.0 - ADAM_B2) * jnp.square(g)
                m_hat = m_new / (1.0 - ADAM_B1 ** ADAM_STEP)
                v_hat = v_new / (1.0 - ADAM_B2 ** ADAM_STEP)
                g_out[...] = g
                d_out[...] = -ADAM_LR * (m_hat / (jnp.sqrt(v_hat) + ADAM_EPS) + ADAM_WD * w_ref[...])
                m_out[...] = m_new
                v_out[...] = v_new

    def part_spec(i):
        return pl.BlockSpec((n_parts, t, cols), lambda l, r: (0, jnp.where(l < i, 0, jnp.where(l == i, r, nr - 1)), 0))

    spec = pl.BlockSpec((t, cols), lambda l, r: (l * nr + r, 0))
    out = jax.ShapeDtypeStruct((depth * rows, cols), F32)
    return pl.pallas_call(
        body, name=name, grid=(depth, nr),
        in_specs=[*[part_spec(i) for i in range(depth)], spec, spec, spec, *[_dep_spec(a) for a in deps]],
        out_specs=[spec] * 4, out_shape=[out] * 4, compiler_params=_cparams(2),
    )(*parts, w, m, v, *deps)


def sum_parts(parts, name):
    n_parts, rows, cols = parts.shape
    t = _tile(rows, 128)

    def fn(p):
        g = p[0].astype(F32)
        for i in range(1, n_parts):
            g = g + p[i].astype(F32)
        return g

    return rowwise(fn, [parts], [pl.BlockSpec((n_parts, t, cols), lambda r: (0, r, 0))],
                   jax.ShapeDtypeStruct((rows, cols), F32), pl.BlockSpec((t, cols), lambda r: (r, 0)), (rows // t,), name)


def _sd(shape, dtype=F32):
    return jax.ShapeDtypeStruct(shape, dtype)


def rms_fwd(x, g, name, col=0, width=None, dep=None):
    seq = x.shape[0]
    width = width or x.shape[1]
    t = _tile(seq, 512)
    return rowwise(_rms, [x, g.reshape(1, width)], [_rows(t, width, col), _whole((1, width))], _sd((seq, width), ACT),
                   _rows(t, width), (seq // t,), name, dep=dep)


def rms_bwd(x, g, dy, name, col=0, width=None, add=None, dx_dtype=F32):
    seq = x.shape[0]
    width = width or x.shape[1]
    t = _tile(seq, 512)
    ins, specs = [x, g.reshape(1, width), dy], [_rows(t, width, col), _whole((1, width)), _rows(t, width)]
    if add is None:
        fn = _rms_bwd
    else:
        ins.append(add)
        specs.append(_rows(t, width))

        def fn(x_, g_, dy_, add_):
            dx, dg = _rms_bwd(x_, g_, dy_)
            return dx + add_, dg
    return rowwise(fn, ins, specs, [_sd((seq, width), dx_dtype), _sd((1, width))], [_rows(t, width), _whole((1, width))],
                   (seq // t,), name, acc={1: (0,)})


def ffn_fwd(x, g, wg, wu, wd, l, tag, dep=None):
    seq, d = x.shape
    f = wg.shape[-1]
    tm = _tile(seq, 1024)
    h = rms_fwd(x, g, f"{tag}_rms", dep=dep)
    hid_spec = pl.BlockSpec((None, tm, f), lambda j, m, k: (j, m, 0))
    up = lambda w, nm, **kw: mm(
        h, w, name=nm, dims=NN, grid=(NDEV, seq // tm, 1), nk=1,
        a_spec=pl.BlockSpec((tm, d), lambda j, m, k: (m, 0)),
        b_spec=pl.BlockSpec((None, None, d, f), lambda j, m, k: (j, l, 0, 0)), o_spec=hid_spec, **kw)
    a = up(wg, f"{tag}_gate", out_shape=_sd((NDEV, seq, f), ACT))
    b, hid = up(wu, f"{tag}_up", out_shape=[_sd((NDEV, seq, f), ACT)] * 2, extras=[a], extra_specs=[hid_spec],
                epilogue=lambda acc, a_: (acc, jax.nn.silu(a_.astype(F32)) * acc))
    tn = _tile(d, 1024)
    out = mm(hid, wd, name=f"{tag}_down", dims=NN, grid=(seq // tm, d // tn, NDEV), nk=NDEV,
             a_spec=pl.BlockSpec((None, tm, f), lambda m, n, k: (k, m, 0)),
             b_spec=pl.BlockSpec((None, None, f, tn), lambda m, n, k: (k, l, 0, n)),
             o_spec=pl.BlockSpec((tm, tn), lambda m, n, k: (m, n)), out_shape=_sd((seq, d)),
             extras=[x], extra_specs=[pl.BlockSpec((tm, tn), lambda m, n, k: (m, n))],
             epilogue=lambda acc, x_: x_ + 0.5 * acc)
    return out, (x, h, a, b, hid)


def ffn_bwd(dout, saved, g, wg, wu, wd, l, tag, dep=None):
    x, h, a, b, hid = saved
    seq, d = x.shape
    f = wg.shape[-1]
    tm = _tile(seq, 1024)
    tk = _tile(seq, 1024)
    def act_bwd(acc, a_, b_):
        dh_, a_, b_ = 0.5 * acc, a_.astype(F32), b_.astype(F32)
        sig = jax.nn.sigmoid(a_)
        return dh_ * b_ * sig * (1.0 + a_ * (1.0 - sig)), dh_ * a_ * sig

    hid_spec = pl.BlockSpec((None, tm, f), lambda j, m, k: (j, m, 0))
    da, db = mm(dout, wd, name=f"{tag}_dhid", dims=NT, grid=(NDEV, seq // tm, 1), nk=1,
                a_spec=pl.BlockSpec((tm, d), lambda j, m, k: (m, 0)),
                b_spec=pl.BlockSpec((None, None, f, d), lambda j, m, k: (j, l, 0, 0)),
                o_spec=hid_spec, out_shape=[_sd((NDEV, seq, f), ACT)] * 2, extras=[a, b], extra_specs=[hid_spec] * 2,
                epilogue=act_bwd, dep=dep)
    tn = _tile(d, 1024)
    dwd = mm(hid, dout, name=f"{tag}_dwd", dims=TN, grid=(NDEV, d // tn, seq // tk), nk=seq // tk,
             a_spec=pl.BlockSpec((None, tk, f), lambda j, n, k: (j, k, 0)),
             b_spec=pl.BlockSpec((tk, tn), lambda j, n, k: (k, n)),
             o_spec=pl.BlockSpec((None, f, tn), lambda j, n, k: (j, 0, n)), out_shape=_sd((NDEV, f, d), COMM),
             epilogue=lambda acc: 0.5 * acc)
    dw = lambda dz, nm: mm(
        h, dz, name=nm, dims=TN, grid=(NDEV, 1, seq // tk), nk=seq // tk,
        a_spec=pl.BlockSpec((tk, d), lambda j, n, k: (k, 0)),
        b_spec=pl.BlockSpec((None, tk, f), lambda j, n, k: (j, k, 0)),
        o_spec=pl.BlockSpec((None, d, f), lambda j, n, k: (j, 0, 0)), out_shape=_sd((NDEV, d, f), COMM))
    dwg, dwu = dw(da, f"{tag}_dwg"), dw(db, f"{tag}_dwu")
    dh_of = lambda dz, w, nm, extras, epi: mm(
        dz, w, name=nm, dims=NT, grid=(seq // tm, d // tn, NDEV), nk=NDEV,
        a_spec=pl.BlockSpec((None, tm, f), lambda m, n, k: (k, m, 0)),
        b_spec=pl.BlockSpec((None, None, tn, f), lambda m, n, k: (k, l, n, 0)),
        o_spec=pl.BlockSpec((tm, tn), lambda m, n, k: (m, n)), out_shape=_sd((seq, d)),
        extras=extras, extra_specs=[pl.BlockSpec((tm, tn), lambda m, n, k: (m, n))] * len(extras), epilogue=epi)
    dh = dh_of(da, wg, f"{tag}_dh_gate", [], None)
    dh = dh_of(db, wu, f"{tag}_dh_up", [dh], lambda acc, prev: acc + prev)
    dx, dg = rms_bwd(x, g, dh, f"{tag}_drms", add=dout)
    return dx, (dg, dwg, dwu, dwd)


def _dense(a, w, l, name, out_dtype=F32, extras=(), epilogue=None):
    seq, kdim = a.shape
    n = w.shape[3]
    w2 = w.reshape(kdim, n)
    tm, tn, tk = _tile(seq, 1024), _tile(n, 1024), _tile(kdim, 1024)
    nk = kdim // tk
    return mm(a, w2, name=name, dims=NN, grid=(seq // tm, n // tn, nk), nk=nk,
              a_spec=pl.BlockSpec((tm, tk), lambda m, c, k: (m, k)),
              b_spec=pl.BlockSpec((tk, tn), lambda m, c, k: (k, c)),
              o_spec=pl.BlockSpec((tm, tn), lambda m, c, k: (m, c)), out_shape=_sd((seq, n), out_dtype),
              extras=list(extras), extra_specs=[pl.BlockSpec((tm, tn), lambda m, c, k: (m, c))] * len(extras),
              epilogue=epilogue)


def _dense_dx(dy, w, l, name, extras=(), epilogue=None, dep=None):
    seq, n = dy.shape
    kb = w.shape[2]
    tm = _tile(seq, 1024)
    return mm(dy, w, name=name, dims=NT, grid=(seq // tm, NDEV, 1), nk=1,
              a_spec=pl.BlockSpec((tm, n), lambda m, j, k: (m, 0)),
              b_spec=pl.BlockSpec((None, None, kb, n), lambda m, j, k: (j, l, 0, 0)),
              o_spec=pl.BlockSpec((tm, kb), lambda m, j, k: (m, j)), out_shape=_sd((seq, NDEV * kb)),
              extras=list(extras), extra_specs=[pl.BlockSpec((tm, kb), lambda m, j, k: (m, j))] * len(extras),
              epilogue=epilogue, dep=dep)


def _dense_dw(a, dy, kb, name):
    seq, n = dy.shape
    rows = NDEV * kb
    tk, tn, tr = _tile(seq, 1024), _tile(n, 1024), _tile(rows, 1024)
    out = mm(a, dy, name=name, dims=TN, grid=(rows // tr, n // tn, seq // tk), nk=seq // tk,
             a_spec=pl.BlockSpec((tk, tr), lambda j, c, k: (k, j)),
             b_spec=pl.BlockSpec((tk, tn), lambda j, c, k: (k, c)),
             o_spec=pl.BlockSpec((tr, tn), lambda j, c, k: (j, c)), out_shape=_sd((rows, n), COMM))
    return out.reshape(NDEV, kb, n)


def _heads_up(a, w, l, name, out_dtype):
    seq, r = a.shape
    c = w.shape[3]
    tm = _tile(seq, 1024)
    return mm(a, w, name=name, dims=NN, grid=(NDEV, seq // tm, 1), nk=1,
              a_spec=pl.BlockSpec((tm, r), lambda j, m, k: (m, 0)),
              b_spec=pl.BlockSpec((None, None, r, c), lambda j, m, k: (j, l, 0, 0)),
              o_spec=pl.BlockSpec((None, tm, c), lambda j, m, k: (j, m, 0)), out_shape=_sd((NDEV, seq, c), out_dtype))


def _heads_dx(dy, w, l, name):
    _, seq, c = dy.shape
    r = w.shape[2]
    tm = _tile(seq, 1024)
    return mm(dy, w, name=name, dims=NT, grid=(seq // tm, 1, NDEV), nk=NDEV,
              a_spec=pl.BlockSpec((None, tm, c), lambda m, n, k: (k, m, 0)),
              b_spec=pl.BlockSpec((None, None, r, c), lambda m, n, k: (k, l, 0, 0)),
              o_spec=pl.BlockSpec((tm, r), lambda m, n, k: (m, 0)), out_shape=_sd((seq, r)))


def _heads_dw(a, dy, name):
    seq, r = a.shape
    c = dy.shape[2]
    tk = _tile(seq, 1024)
    return mm(a, dy, name=name, dims=TN, grid=(NDEV, 1, seq // tk), nk=seq // tk,
              a_spec=pl.BlockSpec((tk, r), lambda j, n, k: (k, 0)),
              b_spec=pl.BlockSpec((None, tk, c), lambda j, n, k: (j, k, 0)),
              o_spec=pl.BlockSpec((None, r, c), lambda j, n, k: (j, 0, 0)), out_shape=_sd((NDEV, r, c), COMM))


C_FQ, C_FK, C_FV, C_CQ, C_CKV, C_DQ, C_DK, C_DV = range(8)
MAIN_W = 8 * 512
TAIL_W = 128


def split_w_in(w):
    fq, fk, fv, fl, cq, ckv, kr, dq, dk, dv = jnp.split(w, [512, 1024, 1536, 1540, 2052, 2564, 2628, 3140, 3652], axis=-1)
    main = jnp.concatenate([fq, fk, fv, cq, ckv, dq, dk, dv], axis=-1)
    pad = jnp.zeros((*w.shape[:-1], TAIL_W - 68), w.dtype)
    return main, jnp.concatenate([kr, fl, pad], axis=-1)


def merge_w_in(main, tail):
    fq, fk, fv, cq, ckv, dq, dk, dv = jnp.split(main, 8, axis=-1)
    return jnp.concatenate([fq, fk, fv, tail[..., 64:68], cq, ckv, tail[..., 0:64], dq, dk, dv], axis=-1)


def mixer_fwd(x, p, l, consts, dep=None):
    seq, d = x.shape
    nfox, nmla, ndil = 4, 8, 4
    cos_m, sin_m, cos_p, sin_p = consts
    t = _tile(seq, 512)
    tf = _tile(seq, 512)
    h = rms_fwd(x, p["mix_norm"], f"mix{l}_rms", dep=dep)
    proj = _dense(h, p["w_in_main"], 0, f"mix{l}_proj")
    tail = _dense(h, p["w_in_tail"], 0, f"mix{l}_tail")

    nb = seq // HEAD
    z = tail[:, 64:68].T.reshape(nfox * nb, HEAD)
    bias_rows = jnp.repeat(p["fox_forget_bias"], nb).reshape(nfox * nb, 1)
    cum = fox_gate_fwd(z, bias_rows, nb, f"mix{l}_gate").reshape(nfox, seq)
    cum2 = (cum.reshape(nfox, seq, 1), cum.reshape(nfox, 1, seq))
    fox_qkv = ((proj, "cols", HEAD, C_FQ * 4), (proj, "cols", HEAD, C_FK * 4), (proj, "cols", HEAD, C_FV * 4))
    out_a, lse_a = flash_fwd(*fox_qkv, cum2, n_heads=nfox, seq=seq, t=tf, scale=HEAD ** -0.5, name=f"mix{l}_fox")

    cq = rms_fwd(proj, p["mla_q_norm"], f"mix{l}_cq", col=C_CQ, width=512)
    ckv = rms_fwd(proj, p["mla_kv_norm"], f"mix{l}_ckv", col=C_CKV, width=512)
    q_raw = _heads_up(cq, p["mla_w_uq"], 0, f"mix{l}_uq", F32)
    kv = _heads_up(ckv, p["mla_w_ukv"], 0, f"mix{l}_ukv", ACT)

    def mla_prep(q_, kv_, tail_, cos_, sin_, q_out, k_out):
        perm = _swap_matrix(MLA_ROPE)
        c, s = cos_[...], sin_[...]
        q_out[:, 0:HEAD] = q_[:, 0:HEAD].astype(q_out.dtype)
        q_out[:, HEAD:MLA_QK] = _rope(q_[:, HEAD:MLA_QK], c, s, perm).astype(q_out.dtype)
        k_out[:, 0:HEAD] = kv_[:, 0:HEAD].astype(k_out.dtype)
        k_out[:, HEAD:MLA_QK] = _rope(tail_[:, 0:MLA_ROPE], c, s, perm).astype(k_out.dtype)

    hs = lambda w: pl.BlockSpec((None, t, w), lambda hh, m: (hh, m, 0))
    rs = lambda w: pl.BlockSpec((t, w), lambda hh, m: (m, 0))
    q_b, k_b = pl.pallas_call(
        lambda q_, kv_, tl_, c_, s_, qo, ko: mla_prep(q_[...], kv_[...], tl_[...], c_, s_, qo, ko),
        name=f"mix{l}_mla_prep", grid=(nmla, seq // t),
        in_specs=[hs(MLA_QK), hs(2 * HEAD), rs(TAIL_W), rs(MLA_ROPE), rs(MLA_ROPE)], out_specs=[hs(MLA_QK), hs(MLA_QK)],
        out_shape=[_sd((nmla, seq, MLA_QK), ACT)] * 2, compiler_params=_cparams(2),
    )(q_raw, kv, tail, cos_m, sin_m)
    mla_qkv = ((q_b, "heads", MLA_QK, 0), (k_b, "heads", MLA_QK, 0), (kv, "heads", HEAD, 1))
    out_b, lse_b = flash_fwd(*mla_qkv, None, n_heads=nmla, seq=seq, t=tf, scale=MLA_QK ** -0.5, name=f"mix{l}_mla")

    wd_ = ndil * HEAD

    def dil_prep(q_, k_, c_, s_):
        perm = _pad_perm(PARTIAL_ROPE)
        rot = lambda a: jnp.concatenate(
            [_rope(a[:, i * HEAD:(i + 1) * HEAD], c_, s_, perm) for i in range(ndil)], axis=1)
        return rot(q_), rot(k_)

    dq_r, dk_r = rowwise(dil_prep, [proj, proj, cos_p, sin_p],
                         [_rows(t, wd_, C_DQ), _rows(t, wd_, C_DK), _rows(t, HEAD), _rows(t, HEAD)],
                         [_sd((seq, wd_), ACT)] * 2, [_rows(t, wd_)] * 2, (seq // t,), f"mix{l}_dil_prep")
    dv = proj[:, C_DV * 512:(C_DV + 1) * 512]
    branches = [dil_fwd(dq_r, dk_r, dv, seq=seq, dil=dl, n_heads=ndil, name=f"mix{l}_dil{dl}") for dl in DIL_BRANCHES]

    def mix(o1, o2, o3, l1, l2, l3):
        m = jnp.maximum(jnp.maximum(l1, l2), l3)
        e1, e2, e3 = jnp.exp(l1 - m), jnp.exp(l2 - m), jnp.exp(l3 - m)
        return (e1 * o1 + e2 * o2 + e3 * o3) / (e1 + e2 + e3)

    out_c = rowwise(mix, [b[0] for b in branches] + [b[1] for b in branches], [_rows(t, wd_)] * 6, _sd((seq, wd_)),
                    _rows(t, wd_), (seq // t,), f"mix{l}_dil_mix")

    mixed = jnp.concatenate([out_a, out_b, out_c], axis=1)
    out = _dense(mixed, p["w_out"], 0, f"mix{l}_out", extras=[x], epilogue=lambda acc, x_: x_ + acc)
    saved = dict(x=x, h=h, proj=proj, tail=tail, z=z, bias_rows=bias_rows, cum2=cum2, out_a=out_a, lse_a=lse_a, cq=cq,
                 ckv=ckv, q_raw=q_raw, kv=kv, q_b=q_b, k_b=k_b, out_b=out_b, lse_b=lse_b, dq_r=dq_r, dk_r=dk_r, dv=dv,
                 branches=branches, out_c=out_c, mixed=mixed)
    return out, saved


def _pad_perm(n):
    i = lax.broadcasted_iota(jnp.int32, (HEAD, HEAD), 0)
    j = lax.broadcasted_iota(jnp.int32, (HEAD, HEAD), 1)
    inside = jnp.logical_and(i < n, j < n)
    return jnp.where(jnp.logical_and(inside, ((i + n // 2) % n) == j), 1.0, 0.0).astype(F32)


def mixer_bwd(dout, sv, p, l, consts, dep=None):
    seq, d = dout.shape
    nfox, nmla, ndil = 4, 8, 4
    cos_m, sin_m, cos_p, sin_p = consts
    t = _tile(seq, 512)
    tf = _tile(seq, 512)
    nb = seq // HEAD
    proj, tail = sv["proj"], sv["tail"]
    dmixed = _dense_dx(dout, p["w_out"], 0, f"mix{l}_dmixed", dep=dep)
    dw_out = _dense_dw(sv["mixed"], dout, d // NDEV, f"mix{l}_dw_out")
    do_a, do_b, do_c = dmixed[:, 0:512], dmixed[:, 512:1536], dmixed[:, 1536:2048]

    fox_qkv = ((proj, "cols", HEAD, C_FQ * 4), (proj, "cols", HEAD, C_FK * 4), (proj, "cols", HEAD, C_FV * 4))
    delta_a = attn_delta(do_a, sv["out_a"], n_heads=nfox, seq=seq, name=f"mix{l}_fox_delta")
    row = lambda a: a.reshape(a.shape[0], 1, seq)
    dfq, dcum_q = flash_bwd_dq(*fox_qkv, do_a, sv["lse_a"], delta_a, sv["cum2"], n_heads=nfox, seq=seq, t=tf,
                               scale=HEAD ** -0.5, name=f"mix{l}_fox_dq")
    dfk, dfv, dcum_k = flash_bwd_dkv(*fox_qkv, do_a, row(sv["lse_a"]), row(delta_a), sv["cum2"], n_heads=nfox, seq=seq,
                                     t=tf, scale=HEAD ** -0.5, name=f"mix{l}_fox_dkv")
    dz, dbias = fox_gate_bwd(sv["z"], sv["bias_rows"], dcum_q.reshape(nfox * nb, HEAD), dcum_k.reshape(nfox * nb, HEAD),
                             nb, f"mix{l}_dgate")
    d_fox_bias = dbias.reshape(nfox, nb, HEAD)[:, 0, 0]
    dfl = dz.reshape(nfox, seq).T
    unheads = lambda a: a.transpose(1, 0, 2).reshape(seq, -1)

    mla_qkv = ((sv["q_b"], "heads", MLA_QK, 0), (sv["k_b"], "heads", MLA_QK, 0), (sv["kv"], "heads", HEAD, 1))
    delta_b = attn_delta(do_b, sv["out_b"], n_heads=nmla, seq=seq, name=f"mix{l}_mla_delta")
    dq_b = flash_bwd_dq(*mla_qkv, do_b, sv["lse_b"], delta_b, None, n_heads=nmla, seq=seq, t=tf, scale=MLA_QK ** -0.5,
                        name=f"mix{l}_mla_dq")
    dk_b, dv_b = flash_bwd_dkv(*mla_qkv, do_b, row(sv["lse_b"]), row(delta_b), None, n_heads=nmla, seq=seq, t=tf,
                               scale=MLA_QK ** -0.5, name=f"mix{l}_mla_dkv")

    def mla_unprep(dq_, dk_, dv_, cos_, sin_, dq_out, dkv_out, dkr_out):
        perm = _swap_matrix(MLA_ROPE)
        c, s = cos_[...], sin_[...]
        dq_out[:, 0:HEAD] = dq_[:, 0:HEAD].astype(dq_out.dtype)
        dq_out[:, HEAD:MLA_QK] = _rope_t(dq_[:, HEAD:MLA_QK], c, s, perm).astype(dq_out.dtype)
        dkv_out[:, 0:HEAD] = dk_[:, 0:HEAD].astype(dkv_out.dtype)
        dkv_out[:, HEAD:2 * HEAD] = dv_.astype(dkv_out.dtype)
        dkr = _rope_t(dk_[:, HEAD:MLA_QK], c, s, perm)
        first = pl.program_id(1) == 0

        @pl.when(first)
        def _():
            dkr_out[...] = dkr

        @pl.when(jnp.logical_not(first))
        def _():
            dkr_out[...] += dkr

    hs = lambda w: pl.BlockSpec((None, t, w), lambda m, hh: (hh, m, 0))
    rs = lambda w: pl.BlockSpec((t, w), lambda m, hh: (m, 0))
    dq_raw, dkv, dk_r = pl.pallas_call(
        lambda a, b, c, cs, sn, o1, o2, o3: mla_unprep(a[...], b[...], c[...], cs, sn, o1, o2, o3),
        name=f"mix{l}_mla_unprep", grid=(seq // t, nmla),
        in_specs=[hs(MLA_QK), hs(MLA_QK), hs(HEAD), rs(MLA_ROPE), rs(MLA_ROPE)],
        out_specs=[hs(MLA_QK), hs(2 * HEAD), rs(MLA_ROPE)],
        out_shape=[_sd((nmla, seq, MLA_QK), ACT), _sd((nmla, seq, 2 * HEAD), ACT), _sd((seq, MLA_ROPE))],
        compiler_params=_cparams(2),
    )(dq_b, dk_b, dv_b, cos_m, sin_m)
    dcq_n = _heads_dx(dq_raw, p["mla_w_uq"], 0, f"mix{l}_dcq")
    dckv_n = _heads_dx(dkv, p["mla_w_ukv"], 0, f"mix{l}_dckv")
    dw_uq = _heads_dw(sv["cq"], dq_raw, f"mix{l}_dw_uq")
    dw_ukv = _heads_dw(sv["ckv"], dkv, f"mix{l}_dw_ukv")
    dcq, dg_q = rms_bwd(proj, p["mla_q_norm"], dcq_n, f"mix{l}_dcq_rms", col=C_CQ, width=512)
    dckv, dg_kv = rms_bwd(proj, p["mla_kv_norm"], dckv_n, f"mix{l}_dckv_rms", col=C_CKV, width=512)

    wd_ = ndil * HEAD
    outs = [b[0] for b in sv["branches"]]
    lses = [b[1] for b in sv["branches"]]

    def mix_bwd(do_, o1, o2, o3, l1, l2, l3):
        m = jnp.maximum(jnp.maximum(l1, l2), l3)
        e1, e2, e3 = jnp.exp(l1 - m), jnp.exp(l2 - m), jnp.exp(l3 - m)
        z_ = e1 + e2 + e3
        w1, w2, w3 = e1 / z_, e2 / z_, e3 / z_
        out = w1 * o1 + w2 * o2 + w3 * o3
        return (w1 * do_, w2 * do_, w3 * do_, do_ * w1 * (o1 - out), do_ * w2 * (o2 - out), do_ * w3 * (o3 - out))

    mb = rowwise(mix_bwd, [do_c] + outs + lses, [_rows(t, wd_)] * 7, [_sd((seq, wd_))] * 6, [_rows(t, wd_)] * 6,
                 (seq // t,), f"mix{l}_dil_dmix")
    grads = [dil_bwd(sv["dq_r"], sv["dk_r"], sv["dv"], outs[i], lses[i], mb[i], mb[3 + i], seq=seq, dil=dl,
                     n_heads=ndil, name=f"mix{l}_dil{dl}_bwd") for i, dl in enumerate(DIL_BRANCHES)]

    def dil_unprep(q1, q2, q3, k1, k2, k3, v1, v2, v3, c_, s_):
        perm = _pad_perm(PARTIAL_ROPE)
        rot_t = lambda a: jnp.concatenate(
            [_rope_t(a[:, i * HEAD:(i + 1) * HEAD], c_, s_, perm) for i in range(ndil)], axis=1)
        return rot_t(q1 + q2 + q3), rot_t(k1 + k2 + k3), v1 + v2 + v3

    ddq, ddk, ddv = rowwise(dil_unprep, [g[0] for g in grads] + [g[1] for g in grads] + [g[2] for g in grads] + [cos_p, sin_p],
                            [_rows(t, wd_)] * 9 + [_rows(t, HEAD)] * 2, [_sd((seq, wd_))] * 3, [_rows(t, wd_)] * 3,
                            (seq // t,), f"mix{l}_dil_unprep")

    dproj = jnp.concatenate([unheads(dfq), unheads(dfk), unheads(dfv), dcq, dckv, ddq, ddk, ddv], axis=1).astype(ACT)
    dtail = jnp.concatenate([dk_r, dfl, jnp.zeros((seq, TAIL_W - 68), F32)], axis=1)
    dh = _dense_dx(dproj, p["w_in_main"], 0, f"mix{l}_dh_main")
    dh = _dense_dx(dtail, p["w_in_tail"], 0, f"mix{l}_dh_tail", extras=[dh], epilogue=lambda acc, prev: acc + prev)
    dw_main = _dense_dw(sv["h"], dproj, d // NDEV, f"mix{l}_dw_in_main")
    dw_tail = _dense_dw(sv["h"], dtail, d // NDEV, f"mix{l}_dw_in_tail")
    dx, dg_mix = rms_bwd(sv["x"], p["mix_norm"], dh, f"mix{l}_drms", add=dout)
    return dx, dict(mix_norm=dg_mix, w_in_main=dw_main, w_in_tail=dw_tail, fox_forget_bias=d_fox_bias, mla_q_norm=dg_q,
                    mla_kv_norm=dg_kv, mla_w_uq=dw_uq, mla_w_ukv=dw_ukv, w_out=dw_out)


def loss_head(x, g, target, name):
    seq, d = x.shape
    t = _tile(seq, 512)

    def fn(x_, g_, tgt):
        err = _rms(x_, g_) - tgt
        part = 0.5 * jnp.sum(jnp.mean(err * err, axis=-1, keepdims=True), axis=0, keepdims=True)
        dx, dg = _rms_bwd(x_, g_, err / d)
        return jnp.broadcast_to(part, (1, HEAD)), dx, dg

    return rowwise(fn, [x, g.reshape(1, d), target], [_rows(t, d), _whole((1, d)), _rows(t, d)],
                   [_sd((1, HEAD)), _sd((seq, d)), _sd((1, d))], [_whole((1, HEAD)), _rows(t, d), _whole((1, d))],
                   (seq // t,), name, acc={0: (0,), 2: (0,)})


BIG = ("ffn1_w_gate", "ffn1_w_up", "ffn1_w_down", "w_in", "mla_w_uq", "mla_w_ukv", "w_out", "ffn2_w_gate", "ffn2_w_up",
       "ffn2_w_down")
GROUPS = {
    "ffn1": ("ffn1_w_gate", "ffn1_w_up", "ffn1_w_down"),
    "mix": ("w_in_main", "w_in_tail", "mla_w_uq", "mla_w_ukv", "w_out"),
    "ffn2": ("ffn2_w_gate", "ffn2_w_up", "ffn2_w_down"),
}
PREFETCH = 2
SMALL_D = ("ffn1_norm", "mix_norm", "ffn2_norm")
WEIGHTS = ("ffn1_norm", "ffn1_w_gate", "ffn1_w_up", "ffn1_w_down", "mix_norm", "w_in", "fox_forget_bias", "mla_q_norm",
           "mla_kv_norm", "mla_w_uq", "mla_w_ukv", "w_out", "ffn2_norm", "ffn2_w_gate", "ffn2_w_up", "ffn2_w_down",
           "final_norm")


def pack_small(vals, depth, d):
    rows = [vals[n].reshape(depth, d) for n in SMALL_D]
    rows.append(vals["final_norm"].reshape(1, d))
    qk = jnp.concatenate([vals["mla_q_norm"].reshape(-1), vals["mla_kv_norm"].reshape(-1)])
    rows.append(jnp.pad(qk, (0, -qk.shape[0] % d)).reshape(-1, d))
    last = jnp.concatenate([vals["fox_forget_bias"].reshape(-1), vals["loss"].reshape(-1)])
    rows.append(jnp.pad(last, (0, d - last.shape[0])).reshape(1, d))
    out = jnp.concatenate(rows, axis=0)
    return jnp.pad(out, ((0, -out.shape[0] % 8), (0, 0)))


def unpack_small(a, depth, d, rank):
    out, r = {}, 0
    for n in SMALL_D:
        out[n] = a[r:r + depth]
        r += depth
    out["final_norm"] = a[r]
    r += 1
    n_qk = -(-2 * depth * rank // d)
    qk = a[r:r + n_qk].reshape(-1)[:2 * depth * rank].reshape(2, depth, rank)
    out["mla_q_norm"], out["mla_kv_norm"] = qk[0], qk[1]
    r += n_qk
    out["fox_forget_bias"] = a[r, :depth * 4].reshape(depth, 4)
    out["loss"] = a[r, depth * 4]
    return out


def step(x, target, w, m, v):
    depth = w["ffn1_norm"].shape[0]
    seq, d = x.shape[1], x.shape[2]
    rank = w["mla_q_norm"].shape[1]
    x = x.reshape(seq, d)
    target = target.reshape(seq, d)

    consts = (*rope_tables(seq, MLA_ROPE), *[jnp.pad(a, ((0, 0), (0, HEAD - PARTIAL_ROPE)), constant_values=c)
                                             for a, c in zip(rope_tables(seq, PARTIAL_ROPE), (1.0, 0.0))])
    order = [(l, k) for l in range(depth) for k in GROUPS]
    small_of = lambda l: {n: w[n][l] for n in ("mix_norm", "fox_forget_bias", "mla_q_norm", "mla_kv_norm")}

    def shards(l, k):
        if k == "mix":
            main, tail = split_w_in(w["w_in"][l:l + 1])
            xs = [main, tail, *[w[n][l:l + 1] for n in GROUPS[k][2:]]]
        else:
            xs = [w[n][l:l + 1] for n in GROUPS[k]]
        return [a.astype(COMM) for a in xs]

    handles, fresh = {}, []

    def launch(i, dep):
        l, k = order[i]
        xs = shards(l, k)
        if dep is not None:
            xs = lax.optimization_barrier((xs, dep))[0]
        handles[i], token = exchange_start(xs, False, f"gather_start_{k}{l}")
        fresh.append(token)

    def take_tokens():
        tok = functools.reduce(jnp.add, fresh) if fresh else None
        fresh.clear()
        return tok

    launched = min(2, len(order))
    for i in range(launched):
        launch(i, None)
    gathered, saved = {}, {}
    for i, (l, k) in enumerate(order):
        wts = dict(zip(GROUPS[k], exchange_wait(handles[i], fresh[-1] if i == 0 else x, f"gather_wait_{k}{l}")))
        gathered[l, k] = wts
        while launched < min(len(order), i + 1 + PREFETCH):
            launch(launched, (x, wts[GROUPS[k][0]]))
            launched += 1
        tok = take_tokens()
        if k == "mix":
            x, saved[l, k] = mixer_fwd(x, {**wts, **small_of(l)}, l, consts, dep=tok)
        else:
            x, saved[l, k] = ffn_fwd(x, w[f"{k}_norm"][l], wts[f"{k}_w_gate"], wts[f"{k}_w_up"], wts[f"{k}_w_down"], 0,
                                     f"{k}_{l}", dep=tok)
    loss, dx, d_final = loss_head(x, w["final_norm"], target, "loss_head")

    small = {n: [None] * depth for n in SMALL_D + ("mla_q_norm", "mla_kv_norm", "fox_forget_bias")}
    pending, tok = [], None
    for l, k in reversed(order):
        wts = gathered[l, k]
        if k == "mix":
            dx, gm = mixer_bwd(dx, saved[l, k], {**wts, **small_of(l)}, l, consts, dep=tok)
        else:
            dx, (dg, dwg, dwu, dwd) = ffn_bwd(dx, saved[l, k], w[f"{k}_norm"][l], wts[f"{k}_w_gate"], wts[f"{k}_w_up"],
                                              wts[f"{k}_w_down"], 0, f"{k}_{l}", dep=tok)
            gm = {f"{k}_norm": dg, f"{k}_w_gate": dwg, f"{k}_w_up": dwu, f"{k}_w_down": dwd}
        for n in small:
            if n in gm:
                small[n][l] = gm[n]
        handle, tok = exchange_start([gm[n] for n in GROUPS[k]], True, f"scatter_start_{k}{l}")
        pending.append((l, k, handle))
    big = {n: [None] * depth for k in GROUPS for n in GROUPS[k]}

    def land(l, k, handle, after):
        for n, a in zip(GROUPS[k], exchange_wait(handle, after, f"scatter_wait_{k}{l}")):
            big[n][l] = a

    for l, k, handle in pending[:-1]:
        land(l, k, handle, dx)

    out = {}

    def update(name, parts, shape, dep=None):
        flat = lambda a: a.reshape(-1, shape[-1])
        res = adamw(parts, flat(w[name]), flat(m[name]), flat(v[name]), f"adamw_{name}", dep=dep)
        for kind, r in zip(("grad", "delta", "new_m", "new_v"), res):
            out[f"{kind}_{name}"] = r.reshape(shape)

    last = GROUPS[pending[-1][1]]
    for n in BIG:
        if n != "w_in" and n not in last:
            update(n, [a.reshape(NDEV, -1, a.shape[-1]) for a in big[n]], w[n].shape, dep=tok)
    g_in = [merge_w_in(sum_parts(big["w_in_main"][l].reshape(NDEV, -1, MAIN_W), f"sum_w_in_main{l}"),
                       sum_parts(big["w_in_tail"][l].reshape(NDEV, -1, TAIL_W), f"sum_w_in_tail{l}"))[None]
            for l in range(depth)]
    update("w_in", g_in, w["w_in"].shape, dep=tok)
    names = list(out)
    out.update(zip(names, lax.optimization_barrier([out[n] for n in names])))
    land(*pending[-1], out[names[0]])
    for n in last:
        update(n, [a.reshape(NDEV, -1, a.shape[-1]) for a in big[n]], w[n].shape)
    out["grad_x"] = dx.reshape(1, seq, d)

    part = {n: jnp.stack(g).reshape(depth, -1) for n, g in small.items()}
    part["final_norm"], part["loss"] = d_final, loss[0, 0:1]
    parts = exchange(pack_small(part, depth, d), False, "gather_small")
    zero = jnp.zeros((1,), F32)
    packed = [pack_small({**{n: a[n] for n in part if n != "loss"}, "loss": zero}, depth, d) for a in (w, m, v)]
    res = [unpack_small(r, depth, d, rank) for r in adamw([parts], *packed, "adamw_small")]
    out["loss"] = res[0]["loss"]
    for n in small.keys() | {"final_norm"}:
        for kind, r in zip(("grad", "delta", "new_m", "new_v"), res):
            out[f"{kind}_{n}"] = r[n].reshape(w[n].shape)
    return out


def kernel(x, ffn1_norm, ffn1_w_gate, ffn1_w_up, ffn1_w_down, mix_norm, w_in, fox_forget_bias, mla_q_norm, mla_kv_norm, mla_w_uq, mla_w_ukv, w_out, ffn2_norm, ffn2_w_gate, ffn2_w_up, ffn2_w_down, final_norm, loss_target, m_ffn1_norm, m_ffn1_w_gate, m_ffn1_w_up, m_ffn1_w_down, m_mix_norm, m_w_in, m_fox_forget_bias, m_mla_q_norm, m_mla_kv_norm, m_mla_w_uq, m_mla_w_ukv, m_w_out, m_ffn2_norm, m_ffn2_w_gate, m_ffn2_w_up, m_ffn2_w_down, m_final_norm, v_ffn1_norm, v_ffn1_w_gate, v_ffn1_w_up, v_ffn1_w_down, v_mix_norm, v_w_in, v_fox_forget_bias, v_mla_q_norm, v_mla_kv_norm, v_mla_w_uq, v_mla_w_ukv, v_w_out, v_ffn2_norm, v_ffn2_w_gate, v_ffn2_w_up, v_ffn2_w_down, v_final_norm):
    args = locals()
    w = {n: args[n] for n in WEIGHTS}
    m = {n: args["m_" + n] for n in WEIGHTS}
    v = {n: args["v_" + n] for n in WEIGHTS}
    out = step(x, loss_target, w, m, v)
    return (out["loss"], out["grad_x"], *[out["grad_" + n] for n in WEIGHTS], *[out["delta_" + n] for n in WEIGHTS],
            *[out["new_m_" + n] for n in WEIGHTS], *[out["new_v_" + n] for n in WEIGHTS])
```

```python
import functools

import jax
import jax.numpy as jnp
from jax import lax
from jax.experimental import pallas as pl
from jax.experimental.pallas import tpu as pltpu

F32 = jnp.float32
MXU = jnp.bfloat16
ACT = jnp.bfloat16
COMM = jnp.bfloat16
HI = lax.Precision.HIGHEST
NN = (((1,), (0,)), ((), ()))
NT = (((1,), (1,)), ((), ()))
TN = (((0,), (0,)), ((), ()))

NDEV = 8
HEAD = 128
EPS = 1e-6
ROPE_THETA = 500000.0
PARTIAL_ROPE = HEAD // 4
MLA_ROPE = 64
MLA_QK = HEAD + MLA_ROPE
DIL_BRANCHES = (1, 4, 16)
NEG = -1e30
VMEM_LIMIT = 48 * 1024 * 1024

ADAMW_STEP_BYTES = 12 * 1024 * 1024

ADAM_LR, ADAM_B1, ADAM_B2, ADAM_EPS, ADAM_WD, ADAM_STEP = 0.001, 0.9, 0.999, 1e-08, 0.01, 10


def _cparams(n_axes):
    return pltpu.CompilerParams(dimension_semantics=("arbitrary",) * n_axes, vmem_limit_bytes=VMEM_LIMIT)


def _tile(n, t):
    t = min(n, t)
    assert n % t == 0, (n, t)
    return t


def _dep_spec(dep):
    nd = dep.ndim
    return pl.BlockSpec(dep.shape, lambda *_: (0,) * nd)


def mm(a, b, *, name, dims, grid, a_spec, b_spec, o_spec, out_shape, nk, extras=(), extra_specs=(), epilogue=None, dep=None):
    n_ex = len(extras)
    kaxis = len(grid) - 1
    if dep is not None:
        extras, extra_specs = [*extras, dep], [*extra_specs, _dep_spec(dep)]
    n_more = len(extras)
    n_out = len(out_shape) if isinstance(out_shape, (list, tuple)) else 1

    def body(a_ref, b_ref, *rest):
        ex, o_refs = rest[:n_ex], rest[n_more:n_more + n_out]
        part = lax.dot_general(a_ref[...].astype(MXU), b_ref[...].astype(MXU), dims, preferred_element_type=F32)

        def finish(acc):
            res = acc if epilogue is None else epilogue(acc, *[e[...] for e in ex])
            for o_ref, r in zip(o_refs, res if n_out > 1 else (res,)):
                o_ref[...] = r.astype(o_ref.dtype)

        if nk == 1:
            finish(part)
        else:
            acc_ref = rest[n_more + n_out]
            k = pl.program_id(kaxis)

            @pl.when(k == 0)
            def _():
                acc_ref[...] = part

            @pl.when(k > 0)
            def _():
                acc_ref[...] += part

            @pl.when(k == nk - 1)
            def _():
                finish(acc_ref[...])

    acc_shape = tuple(d for d in o_spec.block_shape if d is not None)
    return pl.pallas_call(
        body, name=name, grid=grid, in_specs=[a_spec, b_spec, *extra_specs],
        out_specs=[o_spec] * n_out if n_out > 1 else o_spec, out_shape=out_shape,
        scratch_shapes=[] if nk == 1 else [pltpu.VMEM(acc_shape, F32)], compiler_params=_cparams(len(grid)),
    )(a, b, *extras)


def rowwise(fn, ins, in_specs, outs, out_specs, grid, name, acc=None, dep=None):
    acc = acc or {}
    n_in = len(ins)
    if dep is not None:
        ins, in_specs = [*ins, dep], [*in_specs, _dep_spec(dep)]
    n_all = len(ins)

    def body(*refs):
        vals = fn(*[r[...] for r in refs[:n_in]])
        if not isinstance(vals, (tuple, list)):
            vals = (vals,)
        for i, (r, v) in enumerate(zip(refs[n_all:], vals)):
            if i in acc:
                first = functools.reduce(jnp.logical_and, [pl.program_id(ax) == 0 for ax in acc[i]])

                @pl.when(first)
                def _(r=r, v=v):
                    r[...] = v.astype(r.dtype)

                @pl.when(jnp.logical_not(first))
                def _(r=r, v=v):
                    r[...] += v.astype(r.dtype)
            else:
                r[...] = v.astype(r.dtype)

    return pl.pallas_call(
        body, name=name, grid=grid, in_specs=in_specs, out_specs=out_specs, out_shape=outs,
        compiler_params=_cparams(len(grid)),
    )(*ins)


def _rows(t, c, col=0):
    return pl.BlockSpec((t, c), lambda m, col=col: (m, col))


def _whole(shape):
    nd = len(shape)
    return pl.BlockSpec(shape, lambda *_: (0,) * nd)


def _rms(x, g):
    x = x.astype(F32)
    return x * lax.rsqrt(jnp.mean(x * x, axis=-1, keepdims=True) + EPS) * g


def _rms_bwd(x, g, dy):
    x = x.astype(F32)
    dy = dy.astype(F32)
    r = lax.rsqrt(jnp.mean(x * x, axis=-1, keepdims=True) + EPS)
    xh = x * r
    dg = jnp.sum(dy * xh, axis=0, keepdims=True)
    dxh = dy * g
    dx = r * (dxh - xh * jnp.mean(dxh * xh, axis=-1, keepdims=True))
    return dx, dg


def _swap_matrix(n):
    i = lax.broadcasted_iota(jnp.int32, (n, n), 0)
    j = lax.broadcasted_iota(jnp.int32, (n, n), 1)
    return jnp.where(((i + n // 2) % n) == j, 1.0, 0.0).astype(F32)


def _rope(x, cos, sin_signed, perm):
    return x * cos + jnp.dot(x, perm, precision=HI, preferred_element_type=F32) * sin_signed


def _rope_t(dy, cos, sin_signed, perm):
    return dy * cos + jnp.dot(dy * sin_signed, perm, precision=HI, preferred_element_type=F32)


def rope_tables(seq, dim):
    inv = 1.0 / (ROPE_THETA ** (jnp.arange(0, dim, 2, dtype=F32) / dim))
    ang = jnp.arange(seq, dtype=F32)[:, None] * inv[None, :]
    cos, sin = jnp.cos(ang), jnp.sin(ang)
    return jnp.concatenate([cos, cos], axis=1), jnp.concatenate([-sin, sin], axis=1)


HP = 2


def _hspec(arr_kind, t, w, off, seq_of):
    if arr_kind == "cols":
        assert off % HP == 0
        return pl.BlockSpec((t, HP * w), lambda h, i, j: (seq_of(i, j), off // HP + h))
    return pl.BlockSpec((HP, t, w), lambda h, i, j: (h, seq_of(i, j), off))


def _head(ref, arr_kind, hh, w):
    return ref[:, hh * w:(hh + 1) * w] if arr_kind == "cols" else ref[hh]


def _colspec(t, seq_of):
    return pl.BlockSpec((HP, t, 1), lambda h, i, j: (h, seq_of(i, j), 0))


def _rowspec(t, seq_of):
    return pl.BlockSpec((HP, 1, t), lambda h, i, j: (h, 0, seq_of(i, j)))


def _causal(s, qi, kj, t, transposed=False):
    a = lax.broadcasted_iota(jnp.int32, (t, t), 0)
    b = lax.broadcasted_iota(jnp.int32, (t, t), 1)
    keep = (kj * t + a <= qi * t + b) if transposed else (kj * t + b <= qi * t + a)
    return jnp.where(keep, s, NEG)


def flash_fwd(q, k, v, cum, *, n_heads, seq, t, scale, name):
    nb = seq // t
    qs, ks = (lambda i, j: i), (lambda i, j: jnp.minimum(j, i))
    ins = [q[0], k[0], v[0]]
    specs = [_hspec(q[1], t, q[2], q[3], qs), _hspec(k[1], t, k[2], k[3], ks), _hspec(v[1], t, v[2], v[3], ks)]
    if cum is not None:
        ins += [cum[0], cum[1]]
        specs += [_colspec(t, qs), _rowspec(t, ks)]

    def body(*refs):
        q_ref, k_ref, v_ref = refs[:3]
        o_ref, lse_ref, m_s, l_s, acc_s = refs[-5:]
        i, j = pl.program_id(1), pl.program_id(2)

        @pl.when(j == 0)
        def _():
            m_s[...] = jnp.full(m_s.shape, NEG, F32)
            l_s[...] = jnp.zeros(l_s.shape, F32)
            acc_s[...] = jnp.zeros(acc_s.shape, F32)

        def block(masked):
            new = []
            for hh in range(HP):
                qb = _head(q_ref, q[1], hh, q[2]).astype(MXU)
                kb = _head(k_ref, k[1], hh, k[2]).astype(MXU)
                s = lax.dot_general(qb, kb, NT, preferred_element_type=F32) * scale
                if cum is not None:
                    s = s + (refs[3][hh] - refs[4][hh])
                if masked:
                    s = _causal(s, i, j, t)
                m_old = m_s[hh]
                m_new = jnp.maximum(m_old, jnp.max(s, axis=1, keepdims=True))
                alpha = jnp.exp(m_old - m_new)
                p = jnp.exp(s - m_new)
                vb = _head(v_ref, v[1], hh, v[2]).astype(MXU)
                pv = jnp.dot(p.astype(MXU), vb, preferred_element_type=F32)
                new.append((m_new, alpha, alpha * l_s[hh] + jnp.sum(p, axis=1, keepdims=True), pv))
            for hh, (m_new, alpha, l_new, pv) in enumerate(new):
                acc_s[hh] = alpha * acc_s[hh] + pv
                l_s[hh] = l_new
                m_s[hh] = m_new

        @pl.when(j < i)
        def _():
            block(False)

        @pl.when(j == i)
        def _():
            block(True)

        @pl.when(j == nb - 1)
        def _():
            for hh in range(HP):
                o_ref[:, hh * HEAD:(hh + 1) * HEAD] = (acc_s[hh] / l_s[hh]).astype(o_ref.dtype)
                lse_ref[hh] = m_s[hh] + jnp.log(l_s[hh])

    return pl.pallas_call(
        body, name=name, grid=(n_heads // HP, nb, nb), in_specs=specs,
        out_specs=[pl.BlockSpec((t, HP * HEAD), lambda h, i, j: (i, h)), _colspec(t, qs)],
        out_shape=[jax.ShapeDtypeStruct((seq, n_heads * HEAD), F32), jax.ShapeDtypeStruct((n_heads, seq, 1), F32)],
        scratch_shapes=[pltpu.VMEM((HP, t, 1), F32), pltpu.VMEM((HP, t, 1), F32), pltpu.VMEM((HP, t, HEAD), F32)],
        compiler_params=_cparams(3),
    )(*ins)


def flash_bwd_dq(q, k, v, do, lse, delta, cum, *, n_heads, seq, t, scale, name):
    nb = seq // t
    qs, ks = (lambda i, j: i), (lambda i, j: jnp.minimum(j, i))
    ins = [q[0], k[0], v[0], do, lse, delta]
    specs = [_hspec(q[1], t, q[2], q[3], qs), _hspec(k[1], t, k[2], k[3], ks), _hspec(v[1], t, v[2], v[3], ks),
             _hspec("cols", t, HEAD, 0, qs), _colspec(t, qs), _colspec(t, qs)]
    if cum is not None:
        ins += [cum[0], cum[1]]
        specs += [_colspec(t, qs), _rowspec(t, ks)]
    wq = q[2]
    n_out = 1 if cum is None else 2

    def body(*refs):
        q_ref, k_ref, v_ref, do_ref, lse_ref, dl_ref = refs[:6]
        outs = refs[-2 * n_out:-n_out]
        accs = refs[-n_out:]
        i, j = pl.program_id(1), pl.program_id(2)

        @pl.when(j == 0)
        def _():
            for a in accs:
                a[...] = jnp.zeros(a.shape, F32)

        def block(masked):
            for hh in range(HP):
                kb = _head(k_ref, k[1], hh, k[2]).astype(MXU)
                s = lax.dot_general(_head(q_ref, q[1], hh, q[2]).astype(MXU), kb, NT, preferred_element_type=F32) * scale
                if cum is not None:
                    s = s + (refs[6][hh] - refs[7][hh])
                if masked:
                    s = _causal(s, i, j, t)
                p = jnp.exp(s - lse_ref[hh])
                dp = lax.dot_general(_head(do_ref, "cols", hh, HEAD).astype(MXU), _head(v_ref, v[1], hh, v[2]).astype(MXU),
                                     NT, preferred_element_type=F32)
                ds = p * (dp - dl_ref[hh])
                accs[0][hh] += jnp.dot(ds.astype(MXU), kb, preferred_element_type=F32)
                if cum is not None:
                    accs[1][hh] += jnp.sum(ds, axis=1, keepdims=True)

        @pl.when(j < i)
        def _():
            block(False)

        @pl.when(j == i)
        def _():
            block(True)

        @pl.when(j == nb - 1)
        def _():
            outs[0][...] = accs[0][...] * scale
            if cum is not None:
                outs[1][...] = accs[1][...]

    out_specs = [pl.BlockSpec((HP, t, wq), lambda h, i, j: (h, i, 0))]
    out_shape = [jax.ShapeDtypeStruct((n_heads, seq, wq), F32)]
    scratch = [pltpu.VMEM((HP, t, wq), F32)]
    if cum is not None:
        out_specs.append(_colspec(t, qs))
        out_shape.append(jax.ShapeDtypeStruct((n_heads, seq, 1), F32))
        scratch.append(pltpu.VMEM((HP, t, 1), F32))
    res = pl.pallas_call(
        body, name=name, grid=(n_heads // HP, nb, nb), in_specs=specs, out_specs=out_specs, out_shape=out_shape,
        scratch_shapes=scratch, compiler_params=_cparams(3),
    )(*ins)
    return res[0] if cum is None else res


def flash_bwd_dkv(q, k, v, do, lse_row, delta_row, cum, *, n_heads, seq, t, scale, name):
    nb = seq // t
    ks, qs = (lambda j, i: j), (lambda j, i: jnp.maximum(i, j))
    ins = [q[0], k[0], v[0], do, lse_row, delta_row]
    specs = [_hspec(q[1], t, q[2], q[3], qs), _hspec(k[1], t, k[2], k[3], ks), _hspec(v[1], t, v[2], v[3], ks),
             _hspec("cols", t, HEAD, 0, qs), _rowspec(t, qs), _rowspec(t, qs)]
    if cum is not None:
        ins += [cum[0], cum[1]]
        specs += [_colspec(t, ks), _rowspec(t, qs)]
    wk = k[2]
    n_out = 2 if cum is None else 3

    def body(*refs):
        q_ref, k_ref, v_ref, do_ref, lse_ref, dl_ref = refs[:6]
        outs = refs[-2 * n_out:-n_out]
        accs = refs[-n_out:]
        j, i = pl.program_id(1), pl.program_id(2)

        @pl.when(i == 0)
        def _():
            for a in accs:
                a[...] = jnp.zeros(a.shape, F32)

        def block(masked):
            for hh in range(HP):
                qb = _head(q_ref, q[1], hh, q[2]).astype(MXU)
                dob = _head(do_ref, "cols", hh, HEAD).astype(MXU)
                st = lax.dot_general(_head(k_ref, k[1], hh, k[2]).astype(MXU), qb, NT, preferred_element_type=F32) * scale
                if cum is not None:
                    st = st + (refs[7][hh] - refs[6][hh])
                if masked:
                    st = _causal(st, i, j, t, transposed=True)
                pt = jnp.exp(st - lse_ref[hh])
                dpt = lax.dot_general(_head(v_ref, v[1], hh, v[2]).astype(MXU), dob, NT, preferred_element_type=F32)
                dst = pt * (dpt - dl_ref[hh])
                accs[0][hh] += jnp.dot(dst.astype(MXU), qb, preferred_element_type=F32)
                accs[1][hh] += jnp.dot(pt.astype(MXU), dob, preferred_element_type=F32)
                if cum is not None:
                    accs[2][hh] -= jnp.sum(dst, axis=1, keepdims=True)

        @pl.when(i > j)
        def _():
            block(False)

        @pl.when(i == j)
        def _():
            block(True)

        @pl.when(i == nb - 1)
        def _():
            outs[0][...] = accs[0][...] * scale
            for o, a in zip(outs[1:], accs[1:]):
                o[...] = a[...]

    out_specs = [pl.BlockSpec((HP, t, wk), lambda h, j, i: (h, j, 0)), pl.BlockSpec((HP, t, HEAD), lambda h, j, i: (h, j, 0))]
    out_shape = [jax.ShapeDtypeStruct((n_heads, seq, wk), F32), jax.ShapeDtypeStruct((n_heads, seq, HEAD), F32)]
    scratch = [pltpu.VMEM((HP, t, wk), F32), pltpu.VMEM((HP, t, HEAD), F32)]
    if cum is not None:
        out_specs.append(_colspec(t, ks))
        out_shape.append(jax.ShapeDtypeStruct((n_heads, seq, 1), F32))
        scratch.append(pltpu.VMEM((HP, t, 1), F32))
    return pl.pallas_call(
        body, name=name, grid=(n_heads // HP, nb, nb), in_specs=specs, out_specs=out_specs, out_shape=out_shape,
        scratch_shapes=scratch, compiler_params=_cparams(3),
    )(*ins)


def attn_delta(do, o, *, n_heads, seq, name):
    t = _tile(seq, 512)
    spec = pl.BlockSpec((t, HEAD), lambda h, m: (m, h))
    return rowwise(
        lambda a, b: jnp.sum(a.astype(F32) * b.astype(F32), axis=1, keepdims=True), [do, o], [spec, spec],
        jax.ShapeDtypeStruct((n_heads, seq, 1), F32), pl.BlockSpec((None, t, 1), lambda h, m: (h, m, 0)),
        (n_heads, seq // t), name)


def _dil_scores(q, kc, kp, n, scale):
    i = lax.broadcasted_iota(jnp.int32, (HEAD, HEAD), 0)
    j = lax.broadcasted_iota(jnp.int32, (HEAD, HEAD), 1)
    sc = lax.dot_general(q, kc, NT, preferred_element_type=F32) * scale
    sp = lax.dot_general(q, kp, NT, preferred_element_type=F32) * scale
    sc = jnp.where(j <= i, sc, NEG)
    sp = jnp.where(jnp.logical_and(j >= i, n > 0), sp, NEG)
    return sc, sp


def _strip_spec(length, n_heads, col_blocks, off):
    return pl.BlockSpec((length, HEAD), lambda r, h: (0, r * col_blocks + off + h))


def dil_fwd(q, k, v, *, seq, dil, n_heads, name):
    length = seq // dil
    nb = length // HEAD
    scale = HEAD ** -0.5
    view = lambda a: a.reshape(length, dil * a.shape[1])
    spec = _strip_spec(length, n_heads, n_heads, 0)

    def body(q_ref, k_ref, v_ref, o_ref, lse_ref):
        def step(n, carry):
            cur = pl.ds(pl.multiple_of(n * HEAD, HEAD), HEAD)
            prev = pl.ds(pl.multiple_of(jnp.maximum(n - 1, 0) * HEAD, HEAD), HEAD)
            qb = q_ref[cur, :].astype(MXU)
            sc, sp = _dil_scores(qb, k_ref[cur, :].astype(MXU), k_ref[prev, :].astype(MXU), n, scale)
            m = jnp.maximum(jnp.max(sc, axis=1, keepdims=True), jnp.max(sp, axis=1, keepdims=True))
            ec, ep = jnp.exp(sc - m), jnp.exp(sp - m)
            l = jnp.sum(ec, axis=1, keepdims=True) + jnp.sum(ep, axis=1, keepdims=True)
            o = jnp.dot((ec / l).astype(MXU), v_ref[cur, :].astype(MXU), preferred_element_type=F32)
            o = o + jnp.dot((ep / l).astype(MXU), v_ref[prev, :].astype(MXU), preferred_element_type=F32)
            o_ref[cur, :] = o
            lse_ref[cur, :] = jnp.broadcast_to(m + jnp.log(l), (HEAD, HEAD))
            return carry

        lax.fori_loop(0, nb, step, 0)

    out = jax.ShapeDtypeStruct((length, dil * n_heads * HEAD), F32)
    o, lse = pl.pallas_call(
        body, name=name, grid=(dil, n_heads), in_specs=[spec, spec, spec], out_specs=[spec, spec], out_shape=[out, out],
        compiler_params=_cparams(2),
    )(view(q), view(k), view(v))
    return o.reshape(seq, -1), lse.reshape(seq, -1)


def dil_bwd(q, k, v, o, lse, do, dlse, *, seq, dil, n_heads, name):
    length = seq // dil
    nb = length // HEAD
    scale = HEAD ** -0.5
    view = lambda a: a.reshape(length, dil * a.shape[1])
    spec = _strip_spec(length, n_heads, n_heads, 0)

    def body(q_ref, k_ref, v_ref, o_ref, lse_ref, do_ref, dlse_ref, dq_ref, dk_ref, dv_ref):
        dk_ref[...] = jnp.zeros(dk_ref.shape, F32)
        dv_ref[...] = jnp.zeros(dv_ref.shape, F32)

        def step(n, carry):
            cur = pl.ds(pl.multiple_of(n * HEAD, HEAD), HEAD)
            prev = pl.ds(pl.multiple_of(jnp.maximum(n - 1, 0) * HEAD, HEAD), HEAD)
            qb = q_ref[cur, :].astype(MXU)
            kc, kp = k_ref[cur, :].astype(MXU), k_ref[prev, :].astype(MXU)
            vc, vp = v_ref[cur, :].astype(MXU), v_ref[prev, :].astype(MXU)
            sc, sp = _dil_scores(qb, kc, kp, n, scale)
            lse_b = jnp.max(lse_ref[cur, :], axis=1, keepdims=True)
            pc, pp = jnp.exp(sc - lse_b), jnp.exp(sp - lse_b)
            dob = do_ref[cur, :]
            shift = jnp.sum(dlse_ref[cur, :], axis=1, keepdims=True) - jnp.sum(dob * o_ref[cur, :], axis=1, keepdims=True)
            dob = dob.astype(MXU)
            dsc = pc * (lax.dot_general(dob, vc, NT, preferred_element_type=F32) + shift)
            dsp = pp * (lax.dot_general(dob, vp, NT, preferred_element_type=F32) + shift)
            dscb, dspb = dsc.astype(MXU), dsp.astype(MXU)
            dq = jnp.dot(dscb, kc, preferred_element_type=F32) + jnp.dot(dspb, kp, preferred_element_type=F32)
            dq_ref[cur, :] = dq * scale
            dk_ref[cur, :] += lax.dot_general(dscb, qb, TN, preferred_element_type=F32) * scale
            dv_ref[cur, :] += lax.dot_general(pc.astype(MXU), dob, TN, preferred_element_type=F32)
            dk_ref[prev, :] += lax.dot_general(dspb, qb, TN, preferred_element_type=F32) * scale
            dv_ref[prev, :] += lax.dot_general(pp.astype(MXU), dob, TN, preferred_element_type=F32)
            return carry

        lax.fori_loop(0, nb, step, 0)

    out = jax.ShapeDtypeStruct((length, dil * n_heads * HEAD), F32)
    res = pl.pallas_call(
        body, name=name, grid=(dil, n_heads), in_specs=[spec] * 7, out_specs=[spec] * 3, out_shape=[out] * 3,
        compiler_params=_cparams(2),
    )(*[view(a) for a in (q, k, v, o, lse, do, dlse)])
    return [r.reshape(seq, -1) for r in res]


def _tri(n, kind):
    i = lax.broadcasted_iota(jnp.int32, (n, n), 0)
    j = lax.broadcasted_iota(jnp.int32, (n, n), 1)
    return jnp.where({"le": i <= j, "ge": i >= j}[kind], 1.0, 0.0).astype(F32)


def _block_matrix(n_rows, per_head, kind):
    r = lax.broadcasted_iota(jnp.int32, (n_rows, n_rows), 0)
    c = lax.broadcasted_iota(jnp.int32, (n_rows, n_rows), 1)
    same = (r // per_head) == (c // per_head)
    rel = {"lt": c < r, "gt": c > r, "all": c == c}[kind]
    return jnp.where(jnp.logical_and(same, rel), 1.0, 0.0).astype(F32)


def _lane_pick(x, lane):
    j = lax.broadcasted_iota(jnp.int32, x.shape, 1)
    return jnp.sum(jnp.where(j == lane, x, 0.0), axis=1, keepdims=True)


def _log_sigmoid(z):
    return jnp.minimum(z, 0.0) - jnp.log1p(jnp.exp(-jnp.abs(z)))


def fox_gate_fwd(z, bias_rows, per_head, name):
    n_rows = z.shape[0]

    def body(z_ref, b_ref, c_ref):
        logf = _log_sigmoid(z_ref[...] + b_ref[...])
        within = jnp.dot(logf, _tri(HEAD, "le"), precision=HI, preferred_element_type=F32)
        tot = jnp.broadcast_to(_lane_pick(within, HEAD - 1), (n_rows, HEAD))
        c_ref[...] = within + jnp.dot(_block_matrix(n_rows, per_head, "lt"), tot, precision=HI, preferred_element_type=F32)

    return pl.pallas_call(body, name=name, out_shape=jax.ShapeDtypeStruct(z.shape, F32),
                          compiler_params=pltpu.CompilerParams(vmem_limit_bytes=VMEM_LIMIT))(z, bias_rows)


def fox_gate_bwd(z, bias_rows, dcum_q, dcum_k, per_head, name):
    n_rows = z.shape[0]

    def body(z_ref, b_ref, dcq_ref, dck_ref, dz_ref, db_ref):
        within = jnp.dot(dcq_ref[...] + dck_ref[...], _tri(HEAD, "ge"), precision=HI, preferred_element_type=F32)
        tot = jnp.broadcast_to(_lane_pick(within, 0), (n_rows, HEAD))
        dlogf = within + jnp.dot(_block_matrix(n_rows, per_head, "gt"), tot, precision=HI, preferred_element_type=F32)
        dz = dlogf * jax.nn.sigmoid(-(z_ref[...] + b_ref[...]))
        dz_ref[...] = dz
        rs = jnp.broadcast_to(jnp.sum(dz, axis=1, keepdims=True), (n_rows, HEAD))
        db_ref[...] = jnp.dot(_block_matrix(n_rows, per_head, "all"), rs, precision=HI, preferred_element_type=F32)

    shp = jax.ShapeDtypeStruct(z.shape, F32)
    return pl.pallas_call(body, name=name, out_shape=[shp, shp],
                          compiler_params=pltpu.CompilerParams(vmem_limit_bytes=VMEM_LIMIT))(z, bias_rows, dcum_q, dcum_k)


def exchange(x, scatter, name):
    blk = x.shape[1:] if scatter else x.shape

    def body(x_ref, o_ref, send_sems, recv_sems, local_sem):
        mx, my, mc = lax.axis_index("x"), lax.axis_index("y"), lax.axis_index("c")
        me = 4 * mx + 2 * my + mc
        flip = lambda v, f: 1 - v if f else v
        local = pltpu.make_async_copy(x_ref.at[me] if scatter else x_ref, o_ref.at[me], local_sem)
        local.start()
        sends, recvs = [], []
        for n in range(1, NDEV):
            px, py, pc = flip(mx, n & 4), flip(my, n & 2), flip(mc, n & 1)
            p = 4 * px + 2 * py + pc
            sends.append(pltpu.make_async_remote_copy(
                src_ref=x_ref.at[p] if scatter else x_ref, dst_ref=o_ref.at[me], send_sem=send_sems.at[n - 1],
                recv_sem=recv_sems.at[n - 1], device_id=(px, py, pc), device_id_type=pl.DeviceIdType.MESH))
            recvs.append(pltpu.make_async_remote_copy(
                src_ref=x_ref.at[me] if scatter else x_ref, dst_ref=o_ref.at[p], send_sem=send_sems.at[n - 1],
                recv_sem=recv_sems.at[n - 1], device_id=(px, py, pc), device_id_type=pl.DeviceIdType.MESH))
        for cp in sends:
            cp.start()
        for cp in recvs:
            cp.wait_recv()
        for cp in sends:
            cp.wait_send()
        local.wait()

    hbm = pl.BlockSpec(memory_space=pltpu.HBM)
    return pl.pallas_call(
        body, name=name, in_specs=[hbm], out_specs=hbm, out_shape=jax.ShapeDtypeStruct((NDEV, *blk), x.dtype),
        scratch_shapes=[pltpu.SemaphoreType.DMA((NDEV - 1,)), pltpu.SemaphoreType.DMA((NDEV - 1,)), pltpu.SemaphoreType.DMA],
    )(x)


def _exchange_copies(x_refs, land_refs, send_sems, recv_sems, local_sems, scatter, with_recvs):
    mx, my, mc = lax.axis_index("x"), lax.axis_index("y"), lax.axis_index("c")
    me = 4 * mx + 2 * my + mc
    flip = lambda v, f: 1 - v if f else v
    local, sends, recvs = [], [], []
    for a, (x_ref, o_ref) in enumerate(zip(x_refs, land_refs)):
        local.append(pltpu.make_async_copy(x_ref.at[me] if scatter else x_ref, o_ref.at[me], local_sems.at[a]))
        for n in range(1, NDEV):
            px, py, pc = flip(mx, n & 4), flip(my, n & 2), flip(mc, n & 1)
            p = 4 * px + 2 * py + pc
            sem = (NDEV - 1) * a + n - 1
            mk = lambda src, dst: pltpu.make_async_remote_copy(
                src_ref=src, dst_ref=dst, send_sem=send_sems.at[sem], recv_sem=recv_sems.at[sem],
                device_id=(px, py, pc), device_id_type=pl.DeviceIdType.MESH)
            sends.append(mk(x_ref.at[p] if scatter else x_ref, o_ref.at[me]))
            if with_recvs:
                recvs.append(mk(x_ref.at[me] if scatter else x_ref, o_ref.at[p]))
    return local, sends, recvs


_HBM = pl.BlockSpec(memory_space=pltpu.HBM)
_SEM = pl.BlockSpec(memory_space=pltpu.SEMAPHORE)
_EFFECT = pltpu.SideEffectType.DATAFLOW_SIDE_EFFECTING


def exchange_start(xs, scatter, name):
    n = len(xs)
    lands = [jax.ShapeDtypeStruct((NDEV, *(x.shape[1:] if scatter else x.shape)), x.dtype) for x in xs]

    def body(*refs):
        x_refs, land_refs = refs[:n], refs[n:2 * n]
        send_sems, recv_sems, local_sems = refs[2 * n:2 * n + 3]
        token = refs[-1]
        local, sends, _ = _exchange_copies(x_refs, land_refs, send_sems, recv_sems, local_sems, scatter, False)
        for cp in local + sends:
            cp.start()
        token[...] = jnp.zeros(token.shape, token.dtype)

    n_sem = (NDEV - 1) * n
    out = pl.pallas_call(
        body, name=name,
        out_shape=(pltpu.SemaphoreType.DMA((n_sem,)), pltpu.SemaphoreType.DMA((n_sem,)), pltpu.SemaphoreType.DMA((n,)),
                   *[pltpu.HBM(x.shape, x.dtype) for x in xs], *[pltpu.HBM(s.shape, s.dtype) for s in lands],
                   jax.ShapeDtypeStruct((8, HEAD), F32)),
        in_specs=[_HBM] * (2 * n), out_specs=(_SEM, _SEM, _SEM, *[_HBM] * (2 * n), pl.BlockSpec(memory_space=pltpu.VMEM)),
        input_output_aliases={i: 3 + i for i in range(2 * n)},
        compiler_params=pltpu.CompilerParams(has_side_effects=_EFFECT),
    )(*[pltpu.with_memory_space_constraint(x, pltpu.HBM) for x in xs],
      *[pltpu.with_memory_space_constraint(lax.empty(s.shape, s.dtype), pltpu.HBM) for s in lands])
    return (out[:3], out[3:3 + n], out[3 + n:3 + 2 * n], scatter), out[-1]


def exchange_wait(handle, after, name):
    sems, xs, lands, scatter = handle
    n = len(xs)

    def body(*refs):
        x_refs, land_refs = refs[:n], refs[n:2 * n]
        send_sems, recv_sems, local_sems = refs[2 * n:2 * n + 3]
        local, sends, recvs = _exchange_copies(x_refs, land_refs, send_sems, recv_sems, local_sems, scatter, True)
        for cp in sends:
            cp.wait_send()
        for cp in recvs:
            cp.wait_recv()
        for cp in local:
            cp.wait()

    out = pl.pallas_call(
        body, name=name, out_shape=tuple(pltpu.HBM(a.shape, a.dtype) for a in (*xs, *lands)),
        in_specs=[_HBM] * (2 * n) + [_SEM] * 3 + [pl.BlockSpec(memory_space=pl.ANY)], out_specs=tuple([_HBM] * (2 * n)),
        input_output_aliases={i: i for i in range(2 * n)}, compiler_params=pltpu.CompilerParams(has_side_effects=_EFFECT),
    )(*xs, *lands, *sems, after)
    return list(out[n:])


def adamw(parts, w, m, v, name, dep=None):
    depth = len(parts)
    deps = [] if dep is None else [dep]
    n_parts, rows, cols = parts[0].shape
    t = rows
    for cand in (256, 128, 64, 32, 16, 8):
        if rows % cand == 0 and (n_parts * parts[0].dtype.itemsize + 7 * 4) * cand * cols <= ADAMW_STEP_BYTES:
            t = cand
            break
    nr = rows // t

    def body(*refs):
        p_refs = refs[:depth]
        w_ref, m_ref, v_ref = refs[depth:depth + 3]
        g_out, d_out, m_out, v_out = refs[depth + 3 + len(deps):]
        layer = pl.program_id(0)
        for i in range(depth):
            @pl.when(layer == i)
            def _(p=p_refs[i]):
                g = p[0].astype(F32)
                for j in range(1, n_parts):
                    g = g + p[j].astype(F32)
                m_new = ADAM_B1 * m_ref[...] + (1.0 - ADAM_B1) * g
                v_new = ADAM_B2 * v_ref[...] + (1.0 - ADAM_B2) * jnp.square(g)
                m_hat = m_new / (1.0 - ADAM_B1 ** ADAM_STEP)
                v_hat = v_new / (1.0 - ADAM_B2 ** ADAM_STEP)
                g_out[...] = g
                d_out[...] = -ADAM_LR * (m_hat / (jnp.sqrt(v_hat) + ADAM_EPS) + ADAM_WD * w_ref[...])
                m_out[...] = m_new
                v_out[...] = v_new

    def part_spec(i):
        return pl.BlockSpec((n_parts, t, cols), lambda l, r: (0, jnp.where(l < i, 0, jnp.where(l == i, r, nr - 1)), 0))

    spec = pl.BlockSpec((t, cols), lambda l, r: (l * nr + r, 0))
    out = jax.ShapeDtypeStruct((depth * rows, cols), F32)
    return pl.pallas_call(
        body, name=name, grid=(depth, nr),
        in_specs=[*[part_spec(i) for i in range(depth)], spec, spec, spec, *[_dep_spec(a) for a in deps]],
        out_specs=[spec] * 4, out_shape=[out] * 4, compiler_params=_cparams(2),
    )(*parts, w, m, v, *deps)


def sum_parts(parts, name):
    n_parts, rows, cols = parts.shape
    t = _tile(rows, 128)

    def fn(p):
        g = p[0].astype(F32)
        for i in range(1, n_parts):
            g = g + p[i].astype(F32)
        return g

    return rowwise(fn, [parts], [pl.BlockSpec((n_parts, t, cols), lambda r: (0, r, 0))],
                   jax.ShapeDtypeStruct((rows, cols), F32), pl.BlockSpec((t, cols), lambda r: (r, 0)), (rows // t,), name)


def _sd(shape, dtype=F32):
    return jax.ShapeDtypeStruct(shape, dtype)


def rms_fwd(x, g, name, col=0, width=None, dep=None):
    seq = x.shape[0]
    width = width or x.shape[1]
    t = _tile(seq, 512)
    return rowwise(_rms, [x, g.reshape(1, width)], [_rows(t, width, col), _whole((1, width))], _sd((seq, width), ACT),
                   _rows(t, width), (seq // t,), name, dep=dep)


def rms_bwd(x, g, dy, name, col=0, width=None, add=None, dx_dtype=F32):
    seq = x.shape[0]
    width = width or x.shape[1]
    t = _tile(seq, 512)
    ins, specs = [x, g.reshape(1, width), dy], [_rows(t, width, col), _whole((1, width)), _rows(t, width)]
    if add is None:
        fn = _rms_bwd
    else:
        ins.append(add)
        specs.append(_rows(t, width))

        def fn(x_, g_, dy_, add_):
            dx, dg = _rms_bwd(x_, g_, dy_)
            return dx + add_, dg
    return rowwise(fn, ins, specs, [_sd((seq, width), dx_dtype), _sd((1, width))], [_rows(t, width), _whole((1, width))],
                   (seq // t,), name, acc={1: (0,)})


def ffn_fwd(x, g, wg, wu, wd, l, tag, dep=None):
    seq, d = x.shape
    f = wg.shape[2]
    tm = _tile(seq, 1024)
    h = rms_fwd(x, g, f"{tag}_rms", dep=dep)
    hid_spec = pl.BlockSpec((None, tm, f), lambda j, m, k: (j, m, 0))
    up = lambda w, nm, **kw: mm(
        h, w, name=nm, dims=NT, grid=(NDEV, seq // tm, 1), nk=1,
        a_spec=pl.BlockSpec((tm, d), lambda j, m, k: (m, 0)),
        b_spec=pl.BlockSpec((None, None, f, d), lambda j, m, k: (j, l, 0, 0)), o_spec=hid_spec, **kw)
    a = up(wg, f"{tag}_gate", out_shape=_sd((NDEV, seq, f), ACT))
    b, hid = up(wu, f"{tag}_up", out_shape=[_sd((NDEV, seq, f), ACT)] * 2, extras=[a], extra_specs=[hid_spec],
                epilogue=lambda acc, a_: (acc, jax.nn.silu(a_.astype(F32)) * acc))
    tn = _tile(d, 1024)
    out = mm(hid, wd, name=f"{tag}_down", dims=NN, grid=(seq // tm, d // tn, NDEV), nk=NDEV,
             a_spec=pl.BlockSpec((None, tm, f), lambda m, n, k: (k, m, 0)),
             b_spec=pl.BlockSpec((None, None, f, tn), lambda m, n, k: (k, l, 0, n)),
             o_spec=pl.BlockSpec((tm, tn), lambda m, n, k: (m, n)), out_shape=_sd((seq, d)),
             extras=[x], extra_specs=[pl.BlockSpec((tm, tn), lambda m, n, k: (m, n))],
             epilogue=lambda acc, x_: x_ + 0.5 * acc)
    return out, (x, h, a, b, hid)


def ffn_bwd(dout, saved, g, wg, wu, wd, l, tag, send, dep=None):
    x, h, a, b, hid = saved
    seq, d = x.shape
    f = wg.shape[2]
    tm = _tile(seq, 1024)
    tk = _tile(seq, 1024)
    def act_bwd(acc, a_, b_):
        dh_, a_, b_ = 0.5 * acc, a_.astype(F32), b_.astype(F32)
        sig = jax.nn.sigmoid(a_)
        return dh_ * b_ * sig * (1.0 + a_ * (1.0 - sig)), dh_ * a_ * sig

    hid_spec = pl.BlockSpec((None, tm, f), lambda j, m, k: (j, m, 0))
    da, db = mm(dout, wd, name=f"{tag}_dhid", dims=NT, grid=(NDEV, seq // tm, 1), nk=1,
                a_spec=pl.BlockSpec((tm, d), lambda j, m, k: (m, 0)),
                b_spec=pl.BlockSpec((None, None, f, d), lambda j, m, k: (j, l, 0, 0)),
                o_spec=hid_spec, out_shape=[_sd((NDEV, seq, f), ACT)] * 2, extras=[a, b], extra_specs=[hid_spec] * 2,
                epilogue=act_bwd, dep=dep)
    tn = _tile(d, 1024)
    dw = lambda act, rhs, nm, epi: mm(
        act, rhs, name=nm, dims=TN, grid=(NDEV, d // tn, seq // tk), nk=seq // tk,
        a_spec=pl.BlockSpec((None, tk, f), lambda j, n, k: (j, k, 0)),
        b_spec=pl.BlockSpec((tk, tn), lambda j, n, k: (k, n)),
        o_spec=pl.BlockSpec((None, f, tn), lambda j, n, k: (j, 0, n)), out_shape=_sd((NDEV, f, d), COMM), epilogue=epi)
    dwd = dw(hid, dout, f"{tag}_dwd", lambda acc: 0.5 * acc)
    dwg, dwu = dw(da, h, f"{tag}_dwg", None), dw(db, h, f"{tag}_dwu", None)
    token = send(dwg, dwu, dwd)

    def dh_body(da_ref, db_ref, wg_ref, wu_ref, dep_ref, o_ref, acc_ref):
        k = pl.program_id(2)
        part = jnp.dot(da_ref[...].astype(MXU), wg_ref[...].astype(MXU), preferred_element_type=F32)
        part = part + jnp.dot(db_ref[...].astype(MXU), wu_ref[...].astype(MXU), preferred_element_type=F32)

        @pl.when(k == 0)
        def _():
            acc_ref[...] = part

        @pl.when(k > 0)
        def _():
            acc_ref[...] += part

        @pl.when(k == NDEV - 1)
        def _():
            o_ref[...] = acc_ref[...]

    act_spec = pl.BlockSpec((None, tm, f), lambda m, n, k: (k, m, 0))
    w_spec = pl.BlockSpec((None, None, f, tn), lambda m, n, k: (k, l, 0, n))
    dh = pl.pallas_call(
        dh_body, name=f"{tag}_dh", grid=(seq // tm, d // tn, NDEV),
        in_specs=[act_spec, act_spec, w_spec, w_spec, _dep_spec(token)],
        out_specs=pl.BlockSpec((tm, tn), lambda m, n, k: (m, n)), out_shape=_sd((seq, d)),
        scratch_shapes=[pltpu.VMEM((tm, tn), F32)], compiler_params=_cparams(3),
    )(da, db, wg, wu, token)
    return rms_bwd(x, g, dh, f"{tag}_drms", add=dout)


def _dense(a, w, l, name, out_dtype=F32, extras=(), epilogue=None):
    seq, kdim = a.shape
    n = w.shape[3]
    w2 = w.reshape(kdim, n)
    tm, tn, tk = _tile(seq, 1024), _tile(n, 1024), _tile(kdim, 1024)
    nk = kdim // tk
    return mm(a, w2, name=name, dims=NN, grid=(seq // tm, n // tn, nk), nk=nk,
              a_spec=pl.BlockSpec((tm, tk), lambda m, c, k: (m, k)),
              b_spec=pl.BlockSpec((tk, tn), lambda m, c, k: (k, c)),
              o_spec=pl.BlockSpec((tm, tn), lambda m, c, k: (m, c)), out_shape=_sd((seq, n), out_dtype),
              extras=list(extras), extra_specs=[pl.BlockSpec((tm, tn), lambda m, c, k: (m, c))] * len(extras),
              epilogue=epilogue)


def _dense_dx(dy, w, l, name, extras=(), epilogue=None, dep=None):
    seq, n = dy.shape
    kb = w.shape[2]
    tm = _tile(seq, 1024)
    return mm(dy, w, name=name, dims=NT, grid=(seq // tm, NDEV, 1), nk=1,
              a_spec=pl.BlockSpec((tm, n), lambda m, j, k: (m, 0)),
              b_spec=pl.BlockSpec((None, None, kb, n), lambda m, j, k: (j, l, 0, 0)),
              o_spec=pl.BlockSpec((tm, kb), lambda m, j, k: (m, j)), out_shape=_sd((seq, NDEV * kb)),
              extras=list(extras), extra_specs=[pl.BlockSpec((tm, kb), lambda m, j, k: (m, j))] * len(extras),
              epilogue=epilogue, dep=dep)


def _dense_dw(a, dy, kb, name):
    seq, n = dy.shape
    rows = NDEV * kb
    tk, tn, tr = _tile(seq, 1024), _tile(n, 1024), _tile(rows, 1024)
    out = mm(a, dy, name=name, dims=TN, grid=(rows // tr, n // tn, seq // tk), nk=seq // tk,
             a_spec=pl.BlockSpec((tk, tr), lambda j, c, k: (k, j)),
             b_spec=pl.BlockSpec((tk, tn), lambda j, c, k: (k, c)),
             o_spec=pl.BlockSpec((tr, tn), lambda j, c, k: (j, c)), out_shape=_sd((rows, n), COMM))
    return out.reshape(NDEV, kb, n)


def _heads_up(a, w, l, name, out_dtype):
    seq, r = a.shape
    c = w.shape[3]
    tm = _tile(seq, 1024)
    return mm(a, w, name=name, dims=NN, grid=(NDEV, seq // tm, 1), nk=1,
              a_spec=pl.BlockSpec((tm, r), lambda j, m, k: (m, 0)),
              b_spec=pl.BlockSpec((None, None, r, c), lambda j, m, k: (j, l, 0, 0)),
              o_spec=pl.BlockSpec((None, tm, c), lambda j, m, k: (j, m, 0)), out_shape=_sd((NDEV, seq, c), out_dtype))


def _heads_dx(dy, w, l, name):
    _, seq, c = dy.shape
    r = w.shape[2]
    tm = _tile(seq, 1024)
    return mm(dy, w, name=name, dims=NT, grid=(seq // tm, 1, NDEV), nk=NDEV,
              a_spec=pl.BlockSpec((None, tm, c), lambda m, n, k: (k, m, 0)),
              b_spec=pl.BlockSpec((None, None, r, c), lambda m, n, k: (k, l, 0, 0)),
              o_spec=pl.BlockSpec((tm, r), lambda m, n, k: (m, 0)), out_shape=_sd((seq, r)))


def _heads_dw(a, dy, name):
    seq, r = a.shape
    c = dy.shape[2]
    tk = _tile(seq, 1024)
    return mm(a, dy, name=name, dims=TN, grid=(NDEV, 1, seq // tk), nk=seq // tk,
              a_spec=pl.BlockSpec((tk, r), lambda j, n, k: (k, 0)),
              b_spec=pl.BlockSpec((None, tk, c), lambda j, n, k: (j, k, 0)),
              o_spec=pl.BlockSpec((None, r, c), lambda j, n, k: (j, 0, 0)), out_shape=_sd((NDEV, r, c), COMM))


C_FQ, C_FK, C_FV, C_CQ, C_CKV, C_DQ, C_DK, C_DV = range(8)
MAIN_W = 8 * 512
TAIL_W = 128


def split_w_in(w):
    fq, fk, fv, fl, cq, ckv, kr, dq, dk, dv = jnp.split(w, [512, 1024, 1536, 1540, 2052, 2564, 2628, 3140, 3652], axis=-1)
    main = jnp.concatenate([fq, fk, fv, cq, ckv, dq, dk, dv], axis=-1)
    pad = jnp.zeros((*w.shape[:-1], TAIL_W - 68), w.dtype)
    return main, jnp.concatenate([kr, fl, pad], axis=-1)


def merge_w_in(main, tail):
    fq, fk, fv, cq, ckv, dq, dk, dv = jnp.split(main, 8, axis=-1)
    return jnp.concatenate([fq, fk, fv, tail[..., 64:68], cq, ckv, tail[..., 0:64], dq, dk, dv], axis=-1)


def mixer_fwd(x, p, l, consts, dep=None):
    seq, d = x.shape
    nfox, nmla, ndil = 4, 8, 4
    cos_m, sin_m, cos_p, sin_p = consts
    t = _tile(seq, 512)
    tf = _tile(seq, 512)
    h = rms_fwd(x, p["mix_norm"], f"mix{l}_rms", dep=dep)
    proj = _dense(h, p["w_in_main"], 0, f"mix{l}_proj")
    tail = _dense(h, p["w_in_tail"], 0, f"mix{l}_tail")

    nb = seq // HEAD
    z = tail[:, 64:68].T.reshape(nfox * nb, HEAD)
    bias_rows = jnp.repeat(p["fox_forget_bias"], nb).reshape(nfox * nb, 1)
    cum = fox_gate_fwd(z, bias_rows, nb, f"mix{l}_gate").reshape(nfox, seq)
    cum2 = (cum.reshape(nfox, seq, 1), cum.reshape(nfox, 1, seq))
    fox_qkv = ((proj, "cols", HEAD, C_FQ * 4), (proj, "cols", HEAD, C_FK * 4), (proj, "cols", HEAD, C_FV * 4))
    out_a, lse_a = flash_fwd(*fox_qkv, cum2, n_heads=nfox, seq=seq, t=tf, scale=HEAD ** -0.5, name=f"mix{l}_fox")

    cq = rms_fwd(proj, p["mla_q_norm"], f"mix{l}_cq", col=C_CQ, width=512)
    ckv = rms_fwd(proj, p["mla_kv_norm"], f"mix{l}_ckv", col=C_CKV, width=512)
    q_raw = _heads_up(cq, p["mla_w_uq"], 0, f"mix{l}_uq", F32)
    kv = _heads_up(ckv, p["mla_w_ukv"], 0, f"mix{l}_ukv", ACT)

    def mla_prep(q_, kv_, tail_, cos_, sin_, q_out, k_out):
        perm = _swap_matrix(MLA_ROPE)
        c, s = cos_[...], sin_[...]
        q_out[:, 0:HEAD] = q_[:, 0:HEAD].astype(q_out.dtype)
        q_out[:, HEAD:MLA_QK] = _rope(q_[:, HEAD:MLA_QK], c, s, perm).astype(q_out.dtype)
        k_out[:, 0:HEAD] = kv_[:, 0:HEAD].astype(k_out.dtype)
        k_out[:, HEAD:MLA_QK] = _rope(tail_[:, 0:MLA_ROPE], c, s, perm).astype(k_out.dtype)

    hs = lambda w: pl.BlockSpec((None, t, w), lambda hh, m: (hh, m, 0))
    rs = lambda w: pl.BlockSpec((t, w), lambda hh, m: (m, 0))
    q_b, k_b = pl.pallas_call(
        lambda q_, kv_, tl_, c_, s_, qo, ko: mla_prep(q_[...], kv_[...], tl_[...], c_, s_, qo, ko),
        name=f"mix{l}_mla_prep", grid=(nmla, seq // t),
        in_specs=[hs(MLA_QK), hs(2 * HEAD), rs(TAIL_W), rs(MLA_ROPE), rs(MLA_ROPE)], out_specs=[hs(MLA_QK), hs(MLA_QK)],
        out_shape=[_sd((nmla, seq, MLA_QK), ACT)] * 2, compiler_params=_cparams(2),
    )(q_raw, kv, tail, cos_m, sin_m)
    mla_qkv = ((q_b, "heads", MLA_QK, 0), (k_b, "heads", MLA_QK, 0), (kv, "heads", HEAD, 1))
    out_b, lse_b = flash_fwd(*mla_qkv, None, n_heads=nmla, seq=seq, t=tf, scale=MLA_QK ** -0.5, name=f"mix{l}_mla")

    wd_ = ndil * HEAD

    def dil_prep(q_, k_, c_, s_):
        perm = _pad_perm(PARTIAL_ROPE)
        rot = lambda a: jnp.concatenate(
            [_rope(a[:, i * HEAD:(i + 1) * HEAD], c_, s_, perm) for i in range(ndil)], axis=1)
        return rot(q_), rot(k_)

    dq_r, dk_r = rowwise(dil_prep, [proj, proj, cos_p, sin_p],
                         [_rows(t, wd_, C_DQ), _rows(t, wd_, C_DK), _rows(t, HEAD), _rows(t, HEAD)],
                         [_sd((seq, wd_), ACT)] * 2, [_rows(t, wd_)] * 2, (seq // t,), f"mix{l}_dil_prep")
    dv = proj[:, C_DV * 512:(C_DV + 1) * 512]
    branches = [dil_fwd(dq_r, dk_r, dv, seq=seq, dil=dl, n_heads=ndil, name=f"mix{l}_dil{dl}") for dl in DIL_BRANCHES]

    def mix(o1, o2, o3, l1, l2, l3):
        m = jnp.maximum(jnp.maximum(l1, l2), l3)
        e1, e2, e3 = jnp.exp(l1 - m), jnp.exp(l2 - m), jnp.exp(l3 - m)
        return (e1 * o1 + e2 * o2 + e3 * o3) / (e1 + e2 + e3)

    out_c = rowwise(mix, [b[0] for b in branches] + [b[1] for b in branches], [_rows(t, wd_)] * 6, _sd((seq, wd_)),
                    _rows(t, wd_), (seq // t,), f"mix{l}_dil_mix")

    mixed = jnp.concatenate([out_a, out_b, out_c], axis=1)
    out = _dense(mixed, p["w_out"], 0, f"mix{l}_out", extras=[x], epilogue=lambda acc, x_: x_ + acc)
    saved = dict(x=x, h=h, proj=proj, tail=tail, z=z, bias_rows=bias_rows, cum2=cum2, out_a=out_a, lse_a=lse_a, cq=cq,
                 ckv=ckv, q_raw=q_raw, kv=kv, q_b=q_b, k_b=k_b, out_b=out_b, lse_b=lse_b, dq_r=dq_r, dk_r=dk_r, dv=dv,
                 branches=branches, out_c=out_c, mixed=mixed)
    return out, saved


def _pad_perm(n):
    i = lax.broadcasted_iota(jnp.int32, (HEAD, HEAD), 0)
    j = lax.broadcasted_iota(jnp.int32, (HEAD, HEAD), 1)
    inside = jnp.logical_and(i < n, j < n)
    return jnp.where(jnp.logical_and(inside, ((i + n // 2) % n) == j), 1.0, 0.0).astype(F32)


def mixer_bwd(dout, sv, p, l, consts, dep=None):
    seq, d = dout.shape
    nfox, nmla, ndil = 4, 8, 4
    cos_m, sin_m, cos_p, sin_p = consts
    t = _tile(seq, 512)
    tf = _tile(seq, 512)
    nb = seq // HEAD
    proj, tail = sv["proj"], sv["tail"]
    dmixed = _dense_dx(dout, p["w_out"], 0, f"mix{l}_dmixed", dep=dep)
    dw_out = _dense_dw(sv["mixed"], dout, d // NDEV, f"mix{l}_dw_out")
    do_a, do_b, do_c = dmixed[:, 0:512], dmixed[:, 512:1536], dmixed[:, 1536:2048]

    fox_qkv = ((proj, "cols", HEAD, C_FQ * 4), (proj, "cols", HEAD, C_FK * 4), (proj, "cols", HEAD, C_FV * 4))
    delta_a = attn_delta(do_a, sv["out_a"], n_heads=nfox, seq=seq, name=f"mix{l}_fox_delta")
    row = lambda a: a.reshape(a.shape[0], 1, seq)
    dfq, dcum_q = flash_bwd_dq(*fox_qkv, do_a, sv["lse_a"], delta_a, sv["cum2"], n_heads=nfox, seq=seq, t=tf,
                               scale=HEAD ** -0.5, name=f"mix{l}_fox_dq")
    dfk, dfv, dcum_k = flash_bwd_dkv(*fox_qkv, do_a, row(sv["lse_a"]), row(delta_a), sv["cum2"], n_heads=nfox, seq=seq,
                                     t=tf, scale=HEAD ** -0.5, name=f"mix{l}_fox_dkv")
    dz, dbias = fox_gate_bwd(sv["z"], sv["bias_rows"], dcum_q.reshape(nfox * nb, HEAD), dcum_k.reshape(nfox * nb, HEAD),
                             nb, f"mix{l}_dgate")
    d_fox_bias = dbias.reshape(nfox, nb, HEAD)[:, 0, 0]
    dfl = dz.reshape(nfox, seq).T
    unheads = lambda a: a.transpose(1, 0, 2).reshape(seq, -1)

    mla_qkv = ((sv["q_b"], "heads", MLA_QK, 0), (sv["k_b"], "heads", MLA_QK, 0), (sv["kv"], "heads", HEAD, 1))
    delta_b = attn_delta(do_b, sv["out_b"], n_heads=nmla, seq=seq, name=f"mix{l}_mla_delta")
    dq_b = flash_bwd_dq(*mla_qkv, do_b, sv["lse_b"], delta_b, None, n_heads=nmla, seq=seq, t=tf, scale=MLA_QK ** -0.5,
                        name=f"mix{l}_mla_dq")
    dk_b, dv_b = flash_bwd_dkv(*mla_qkv, do_b, row(sv["lse_b"]), row(delta_b), None, n_heads=nmla, seq=seq, t=tf,
                               scale=MLA_QK ** -0.5, name=f"mix{l}_mla_dkv")

    def mla_unprep(dq_, dk_, dv_, cos_, sin_, dq_out, dkv_out, dkr_out):
        perm = _swap_matrix(MLA_ROPE)
        c, s = cos_[...], sin_[...]
        dq_out[:, 0:HEAD] = dq_[:, 0:HEAD].astype(dq_out.dtype)
        dq_out[:, HEAD:MLA_QK] = _rope_t(dq_[:, HEAD:MLA_QK], c, s, perm).astype(dq_out.dtype)
        dkv_out[:, 0:HEAD] = dk_[:, 0:HEAD].astype(dkv_out.dtype)
        dkv_out[:, HEAD:2 * HEAD] = dv_.astype(dkv_out.dtype)
        dkr = _rope_t(dk_[:, HEAD:MLA_QK], c, s, perm)
        first = pl.program_id(1) == 0

        @pl.when(first)
        def _():
            dkr_out[...] = dkr

        @pl.when(jnp.logical_not(first))
        def _():
            dkr_out[...] += dkr

    hs = lambda w: pl.BlockSpec((None, t, w), lambda m, hh: (hh, m, 0))
    rs = lambda w: pl.BlockSpec((t, w), lambda m, hh: (m, 0))
    dq_raw, dkv, dk_r = pl.pallas_call(
        lambda a, b, c, cs, sn, o1, o2, o3: mla_unprep(a[...], b[...], c[...], cs, sn, o1, o2, o3),
        name=f"mix{l}_mla_unprep", grid=(seq // t, nmla),
        in_specs=[hs(MLA_QK), hs(MLA_QK), hs(HEAD), rs(MLA_ROPE), rs(MLA_ROPE)],
        out_specs=[hs(MLA_QK), hs(2 * HEAD), rs(MLA_ROPE)],
        out_shape=[_sd((nmla, seq, MLA_QK), ACT), _sd((nmla, seq, 2 * HEAD), ACT), _sd((seq, MLA_ROPE))],
        compiler_params=_cparams(2),
    )(dq_b, dk_b, dv_b, cos_m, sin_m)
    dcq_n = _heads_dx(dq_raw, p["mla_w_uq"], 0, f"mix{l}_dcq")
    dckv_n = _heads_dx(dkv, p["mla_w_ukv"], 0, f"mix{l}_dckv")
    dw_uq = _heads_dw(sv["cq"], dq_raw, f"mix{l}_dw_uq")
    dw_ukv = _heads_dw(sv["ckv"], dkv, f"mix{l}_dw_ukv")
    dcq, dg_q = rms_bwd(proj, p["mla_q_norm"], dcq_n, f"mix{l}_dcq_rms", col=C_CQ, width=512)
    dckv, dg_kv = rms_bwd(proj, p["mla_kv_norm"], dckv_n, f"mix{l}_dckv_rms", col=C_CKV, width=512)

    wd_ = ndil * HEAD
    outs = [b[0] for b in sv["branches"]]
    lses = [b[1] for b in sv["branches"]]

    def mix_bwd(do_, o1, o2, o3, l1, l2, l3):
        m = jnp.maximum(jnp.maximum(l1, l2), l3)
        e1, e2, e3 = jnp.exp(l1 - m), jnp.exp(l2 - m), jnp.exp(l3 - m)
        z_ = e1 + e2 + e3
        w1, w2, w3 = e1 / z_, e2 / z_, e3 / z_
        out = w1 * o1 + w2 * o2 + w3 * o3
        return (w1 * do_, w2 * do_, w3 * do_, do_ * w1 * (o1 - out), do_ * w2 * (o2 - out), do_ * w3 * (o3 - out))

    mb = rowwise(mix_bwd, [do_c] + outs + lses, [_rows(t, wd_)] * 7, [_sd((seq, wd_))] * 6, [_rows(t, wd_)] * 6,
                 (seq // t,), f"mix{l}_dil_dmix")
    grads = [dil_bwd(sv["dq_r"], sv["dk_r"], sv["dv"], outs[i], lses[i], mb[i], mb[3 + i], seq=seq, dil=dl,
                     n_heads=ndil, name=f"mix{l}_dil{dl}_bwd") for i, dl in enumerate(DIL_BRANCHES)]

    def dil_unprep(q1, q2, q3, k1, k2, k3, v1, v2, v3, c_, s_):
        perm = _pad_perm(PARTIAL_ROPE)
        rot_t = lambda a: jnp.concatenate(
            [_rope_t(a[:, i * HEAD:(i + 1) * HEAD], c_, s_, perm) for i in range(ndil)], axis=1)
        return rot_t(q1 + q2 + q3), rot_t(k1 + k2 + k3), v1 + v2 + v3

    ddq, ddk, ddv = rowwise(dil_unprep, [g[0] for g in grads] + [g[1] for g in grads] + [g[2] for g in grads] + [cos_p, sin_p],
                            [_rows(t, wd_)] * 9 + [_rows(t, HEAD)] * 2, [_sd((seq, wd_))] * 3, [_rows(t, wd_)] * 3,
                            (seq // t,), f"mix{l}_dil_unprep")

    dproj = jnp.concatenate([unheads(dfq), unheads(dfk), unheads(dfv), dcq, dckv, ddq, ddk, ddv], axis=1).astype(ACT)
    dtail = jnp.concatenate([dk_r, dfl, jnp.zeros((seq, TAIL_W - 68), F32)], axis=1)
    dh = _dense_dx(dproj, p["w_in_main"], 0, f"mix{l}_dh_main")
    dh = _dense_dx(dtail, p["w_in_tail"], 0, f"mix{l}_dh_tail", extras=[dh], epilogue=lambda acc, prev: acc + prev)
    dw_main = _dense_dw(sv["h"], dproj, d // NDEV, f"mix{l}_dw_in_main")
    dw_tail = _dense_dw(sv["h"], dtail, d // NDEV, f"mix{l}_dw_in_tail")
    dx, dg_mix = rms_bwd(sv["x"], p["mix_norm"], dh, f"mix{l}_drms", add=dout)
    return dx, dict(mix_norm=dg_mix, w_in_main=dw_main, w_in_tail=dw_tail, fox_forget_bias=d_fox_bias, mla_q_norm=dg_q,
                    mla_kv_norm=dg_kv, mla_w_uq=dw_uq, mla_w_ukv=dw_ukv, w_out=dw_out)


def loss_head(x, g, target, name):
    seq, d = x.shape
    t = _tile(seq, 512)

    def fn(x_, g_, tgt):
        err = _rms(x_, g_) - tgt
        part = 0.5 * jnp.sum(jnp.mean(err * err, axis=-1, keepdims=True), axis=0, keepdims=True)
        dx, dg = _rms_bwd(x_, g_, err / d)
        return jnp.broadcast_to(part, (1, HEAD)), dx, dg

    return rowwise(fn, [x, g.reshape(1, d), target], [_rows(t, d), _whole((1, d)), _rows(t, d)],
                   [_sd((1, HEAD)), _sd((seq, d)), _sd((1, d))], [_whole((1, HEAD)), _rows(t, d), _whole((1, d))],
                   (seq // t,), name, acc={0: (0,), 2: (0,)})


BIG = ("ffn1_w_gate", "ffn1_w_up", "ffn1_w_down", "w_in", "mla_w_uq", "mla_w_ukv", "w_out", "ffn2_w_gate", "ffn2_w_up",
       "ffn2_w_down")
GROUPS = {
    "ffn1": ("ffn1_w_gate", "ffn1_w_up", "ffn1_w_down"),
    "mix": ("w_in_main", "w_in_tail", "mla_w_uq", "mla_w_ukv", "w_out"),
    "ffn2": ("ffn2_w_gate", "ffn2_w_up", "ffn2_w_down"),
}
TRANSPOSED = ("ffn1_w_gate", "ffn1_w_up", "ffn2_w_gate", "ffn2_w_up")
PREFETCH = 2
SMALL_D = ("ffn1_norm", "mix_norm", "ffn2_norm")
WEIGHTS = ("ffn1_norm", "ffn1_w_gate", "ffn1_w_up", "ffn1_w_down", "mix_norm", "w_in", "fox_forget_bias", "mla_q_norm",
           "mla_kv_norm", "mla_w_uq", "mla_w_ukv", "w_out", "ffn2_norm", "ffn2_w_gate", "ffn2_w_up", "ffn2_w_down",
           "final_norm")


def pack_small(vals, depth, d):
    rows = [vals[n].reshape(depth, d) for n in SMALL_D]
    rows.append(vals["final_norm"].reshape(1, d))
    qk = jnp.concatenate([vals["mla_q_norm"].reshape(-1), vals["mla_kv_norm"].reshape(-1)])
    rows.append(jnp.pad(qk, (0, -qk.shape[0] % d)).reshape(-1, d))
    last = jnp.concatenate([vals["fox_forget_bias"].reshape(-1), vals["loss"].reshape(-1)])
    rows.append(jnp.pad(last, (0, d - last.shape[0])).reshape(1, d))
    out = jnp.concatenate(rows, axis=0)
    return jnp.pad(out, ((0, -out.shape[0] % 8), (0, 0)))


def unpack_small(a, depth, d, rank):
    out, r = {}, 0
    for n in SMALL_D:
        out[n] = a[r:r + depth]
        r += depth
    out["final_norm"] = a[r]
    r += 1
    n_qk = -(-2 * depth * rank // d)
    qk = a[r:r + n_qk].reshape(-1)[:2 * depth * rank].reshape(2, depth, rank)
    out["mla_q_norm"], out["mla_kv_norm"] = qk[0], qk[1]
    r += n_qk
    out["fox_forget_bias"] = a[r, :depth * 4].reshape(depth, 4)
    out["loss"] = a[r, depth * 4]
    return out


def step(x, target, w, m, v):
    depth = w["ffn1_norm"].shape[0]
    seq, d = x.shape[1], x.shape[2]
    rank = w["mla_q_norm"].shape[1]
    x = x.reshape(seq, d)
    target = target.reshape(seq, d)

    consts = (*rope_tables(seq, MLA_ROPE), *[jnp.pad(a, ((0, 0), (0, HEAD - PARTIAL_ROPE)), constant_values=c)
                                             for a, c in zip(rope_tables(seq, PARTIAL_ROPE), (1.0, 0.0))])
    order = [(l, k) for l in range(depth) for k in GROUPS]
    small_of = lambda l: {n: w[n][l] for n in ("mix_norm", "fox_forget_bias", "mla_q_norm", "mla_kv_norm")}

    view = lambda n, a: a.transpose(0, 2, 1) if n in TRANSPOSED else a

    def shards(l, k):
        if k == "mix":
            main, tail = split_w_in(w["w_in"][l:l + 1])
            xs = [main, tail, *[w[n][l:l + 1] for n in GROUPS[k][2:]]]
        else:
            xs = [view(n, w[n])[l:l + 1] for n in GROUPS[k]]
        return [a.astype(COMM) for a in xs]

    handles, fresh = {}, []

    def launch(i, dep):
        l, k = order[i]
        xs = shards(l, k)
        if dep is not None:
            xs = lax.optimization_barrier((xs, dep))[0]
        handles[i], token = exchange_start(xs, False, f"gather_start_{k}{l}")
        fresh.append(token)

    def take_tokens():
        tok = functools.reduce(jnp.add, fresh) if fresh else None
        fresh.clear()
        return tok

    launched = min(2, len(order))
    for i in range(launched):
        launch(i, None)
    gathered, saved = {}, {}
    for i, (l, k) in enumerate(order):
        wts = dict(zip(GROUPS[k], exchange_wait(handles[i], fresh[-1] if i == 0 else x, f"gather_wait_{k}{l}")))
        gathered[l, k] = wts
        while launched < min(len(order), i + 1 + PREFETCH):
            launch(launched, (x, wts[GROUPS[k][0]]))
            launched += 1
        tok = take_tokens()
        if k == "mix":
            x, saved[l, k] = mixer_fwd(x, {**wts, **small_of(l)}, l, consts, dep=tok)
        else:
            x, saved[l, k] = ffn_fwd(x, w[f"{k}_norm"][l], wts[f"{k}_w_gate"], wts[f"{k}_w_up"], wts[f"{k}_w_down"], 0,
                                     f"{k}_{l}", dep=tok)
    loss, dx, d_final = loss_head(x, w["final_norm"], target, "loss_head")

    small = {n: [None] * depth for n in SMALL_D + ("mla_q_norm", "mla_kv_norm", "fox_forget_bias")}
    pending, tok = [], None

    def send(l, k):
        def start(*grads):
            handle, token = exchange_start(list(grads), True, f"scatter_start_{k}{l}")
            pending.append((l, k, handle))
            return token
        return start

    for l, k in reversed(order):
        wts = gathered[l, k]
        if k == "mix":
            dx, gm = mixer_bwd(dx, saved[l, k], {**wts, **small_of(l)}, l, consts, dep=tok)
            tok = send(l, k)(*[gm[n] for n in GROUPS[k]])
        else:
            dx, dg = ffn_bwd(dx, saved[l, k], w[f"{k}_norm"][l], wts[f"{k}_w_gate"], wts[f"{k}_w_up"], wts[f"{k}_w_down"],
                             0, f"{k}_{l}", send(l, k), dep=tok)
            gm, tok = {f"{k}_norm": dg}, None
        for n in small:
            if n in gm:
                small[n][l] = gm[n]
    big = {n: [None] * depth for k in GROUPS for n in GROUPS[k]}

    def land(l, k, handle, after):
        for n, a in zip(GROUPS[k], exchange_wait(handle, after, f"scatter_wait_{k}{l}")):
            big[n][l] = a

    for l, k, handle in pending[:-1]:
        land(l, k, handle, dx)

    out = {}

    def update(name, parts, shape, dep=None):
        shape_v = (shape[0], shape[2], shape[1]) if name in TRANSPOSED else shape
        flat = lambda a: view(name, a).reshape(-1, shape_v[-1])
        res = adamw(parts, flat(w[name]), flat(m[name]), flat(v[name]), f"adamw_{name}", dep=dep)
        for kind, r in zip(("grad", "delta", "new_m", "new_v"), res):
            out[f"{kind}_{name}"] = view(name, r.reshape(shape_v))

    last = GROUPS[pending[-1][1]]
    for n in BIG:
        if n != "w_in" and n not in last:
            update(n, [a.reshape(NDEV, -1, a.shape[-1]) for a in big[n]], w[n].shape, dep=tok)
    g_in = [merge_w_in(sum_parts(big["w_in_main"][l].reshape(NDEV, -1, MAIN_W), f"sum_w_in_main{l}"),
                       sum_parts(big["w_in_tail"][l].reshape(NDEV, -1, TAIL_W), f"sum_w_in_tail{l}"))[None]
            for l in range(depth)]
    update("w_in", g_in, w["w_in"].shape, dep=tok)
    names = list(out)
    out.update(zip(names, lax.optimization_barrier([out[n] for n in names])))
    land(*pending[-1], out[names[0]])
    for n in last:
        update(n, [a.reshape(NDEV, -1, a.shape[-1]) for a in big[n]], w[n].shape)
    out["grad_x"] = dx.reshape(1, seq, d)

    part = {n: jnp.stack(g).reshape(depth, -1) for n, g in small.items()}
    part["final_norm"], part["loss"] = d_final, loss[0, 0:1]
    parts = exchange(pack_small(part, depth, d), False, "gather_small")
    zero = jnp.zeros((1,), F32)
    packed = [pack_small({**{n: a[n] for n in part if n != "loss"}, "loss": zero}, depth, d) for a in (w, m, v)]
    res = [unpack_small(r, depth, d, rank) for r in adamw([parts], *packed, "adamw_small")]
    out["loss"] = res[0]["loss"]
    for n in small.keys() | {"final_norm"}:
        for kind, r in zip(("grad", "delta", "new_m", "new_v"), res):
            out[f"{kind}_{n}"] = r[n].reshape(w[n].shape)
    return out


def kernel(x, ffn1_norm, ffn1_w_gate, ffn1_w_up, ffn1_w_down, mix_norm, w_in, fox_forget_bias, mla_q_norm, mla_kv_norm, mla_w_uq, mla_w_ukv, w_out, ffn2_norm, ffn2_w_gate, ffn2_w_up, ffn2_w_down, final_norm, loss_target, m_ffn1_norm, m_ffn1_w_gate, m_ffn1_w_up, m_ffn1_w_down, m_mix_norm, m_w_in, m_fox_forget_bias, m_mla_q_norm, m_mla_kv_norm, m_mla_w_uq, m_mla_w_ukv, m_w_out, m_ffn2_norm, m_ffn2_w_gate, m_ffn2_w_up, m_ffn2_w_down, m_final_norm, v_ffn1_norm, v_ffn1_w_gate, v_ffn1_w_up, v_ffn1_w_down, v_mix_norm, v_w_in, v_fox_forget_bias, v_mla_q_norm, v_mla_kv_norm, v_mla_w_uq, v_mla_w_ukv, v_w_out, v_ffn2_norm, v_ffn2_w_gate, v_ffn2_w_up, v_ffn2_w_down, v_final_norm):
    args = locals()
    w = {n: args[n] for n in WEIGHTS}
    m = {n: args["m_" + n] for n in WEIGHTS}
    v = {n: args["v_" + n] for n in WEIGHTS}
    out = step(x, loss_target, w, m, v)
    return (out["loss"], out["grad_x"], *[out["grad_" + n] for n in WEIGHTS], *[out["delta_" + n] for n in WEIGHTS],
            *[out["new_m_" + n] for n in WEIGHTS], *[out["new_v_" + n] for n in WEIGHTS])
```

```python
import functools

import jax
import jax.numpy as jnp
from jax import lax
from jax.experimental import pallas as pl
from jax.experimental.pallas import tpu as pltpu

F32 = jnp.float32
MXU = jnp.bfloat16
ACT = jnp.bfloat16
COMM = jnp.bfloat16
HI = lax.Precision.HIGHEST
NN = (((1,), (0,)), ((), ()))
NT = (((1,), (1,)), ((), ()))
TN = (((0,), (0,)), ((), ()))

NDEV = 8
HEAD = 128
EPS = 1e-6
ROPE_THETA = 500000.0
PARTIAL_ROPE = HEAD // 4
MLA_ROPE = 64
MLA_QK = HEAD + MLA_ROPE
DIL_BRANCHES = (1, 4, 16)
NEG = -1e30
VMEM_LIMIT = 48 * 1024 * 1024

ADAMW_STEP_BYTES = 12 * 1024 * 1024

ADAM_LR, ADAM_B1, ADAM_B2, ADAM_EPS, ADAM_WD, ADAM_STEP = 0.001, 0.9, 0.999, 1e-08, 0.01, 10


def _cparams(n_axes):
    return pltpu.CompilerParams(dimension_semantics=("arbitrary",) * n_axes, vmem_limit_bytes=VMEM_LIMIT)


def _tile(n, t):
    t = min(n, t)
    assert n % t == 0, (n, t)
    return t


def _dep_spec(dep):
    nd = dep.ndim
    return pl.BlockSpec(dep.shape, lambda *_: (0,) * nd)


def mm(a, b, *, name, dims, grid, a_spec, b_spec, o_spec, out_shape, nk, extras=(), extra_specs=(), epilogue=None, dep=None):
    n_ex = len(extras)
    kaxis = len(grid) - 1
    if dep is not None:
        extras, extra_specs = [*extras, dep], [*extra_specs, _dep_spec(dep)]
    n_more = len(extras)
    n_out = len(out_shape) if isinstance(out_shape, (list, tuple)) else 1

    def body(a_ref, b_ref, *rest):
        ex, o_refs = rest[:n_ex], rest[n_more:n_more + n_out]
        part = lax.dot_general(a_ref[...].astype(MXU), b_ref[...].astype(MXU), dims, preferred_element_type=F32)

        def finish(acc):
            res = acc if epilogue is None else epilogue(acc, *[e[...] for e in ex])
            for o_ref, r in zip(o_refs, res if n_out > 1 else (res,)):
                o_ref[...] = r.astype(o_ref.dtype)

        if nk == 1:
            finish(part)
        else:
            acc_ref = rest[n_more + n_out]
            k = pl.program_id(kaxis)

            @pl.when(k == 0)
            def _():
                acc_ref[...] = part

            @pl.when(k > 0)
            def _():
                acc_ref[...] += part

            @pl.when(k == nk - 1)
            def _():
                finish(acc_ref[...])

    acc_shape = tuple(d for d in o_spec.block_shape if d is not None)
    return pl.pallas_call(
        body, name=name, grid=grid, in_specs=[a_spec, b_spec, *extra_specs],
        out_specs=[o_spec] * n_out if n_out > 1 else o_spec, out_shape=out_shape,
        scratch_shapes=[] if nk == 1 else [pltpu.VMEM(acc_shape, F32)], compiler_params=_cparams(len(grid)),
    )(a, b, *extras)


def rowwise(fn, ins, in_specs, outs, out_specs, grid, name, acc=None, dep=None):
    acc = acc or {}
    n_in = len(ins)
    if dep is not None:
        ins, in_specs = [*ins, dep], [*in_specs, _dep_spec(dep)]
    n_all = len(ins)

    def body(*refs):
        vals = fn(*[r[...] for r in refs[:n_in]])
        if not isinstance(vals, (tuple, list)):
            vals = (vals,)
        for i, (r, v) in enumerate(zip(refs[n_all:], vals)):
            if i in acc:
                first = functools.reduce(jnp.logical_and, [pl.program_id(ax) == 0 for ax in acc[i]])

                @pl.when(first)
                def _(r=r, v=v):
                    r[...] = v.astype(r.dtype)

                @pl.when(jnp.logical_not(first))
                def _(r=r, v=v):
                    r[...] += v.astype(r.dtype)
            else:
                r[...] = v.astype(r.dtype)

    return pl.pallas_call(
        body, name=name, grid=grid, in_specs=in_specs, out_specs=out_specs, out_shape=outs,
        compiler_params=_cparams(len(grid)),
    )(*ins)


def _rows(t, c, col=0):
    return pl.BlockSpec((t, c), lambda m, col=col: (m, col))


def _whole(shape):
    nd = len(shape)
    return pl.BlockSpec(shape, lambda *_: (0,) * nd)


def _rms(x, g):
    x = x.astype(F32)
    return x * lax.rsqrt(jnp.mean(x * x, axis=-1, keepdims=True) + EPS) * g


def _rms_bwd(x, g, dy):
    x = x.astype(F32)
    dy = dy.astype(F32)
    r = lax.rsqrt(jnp.mean(x * x, axis=-1, keepdims=True) + EPS)
    xh = x * r
    dg = jnp.sum(dy * xh, axis=0, keepdims=True)
    dxh = dy * g
    dx = r * (dxh - xh * jnp.mean(dxh * xh, axis=-1, keepdims=True))
    return dx, dg


def _swap_matrix(n):
    i = lax.broadcasted_iota(jnp.int32, (n, n), 0)
    j = lax.broadcasted_iota(jnp.int32, (n, n), 1)
    return jnp.where(((i + n // 2) % n) == j, 1.0, 0.0).astype(F32)


def _rope(x, cos, sin_signed, perm):
    return x * cos + jnp.dot(x, perm, precision=HI, preferred_element_type=F32) * sin_signed


def _rope_t(dy, cos, sin_signed, perm):
    return dy * cos + jnp.dot(dy * sin_signed, perm, precision=HI, preferred_element_type=F32)


def rope_tables(seq, dim):
    inv = 1.0 / (ROPE_THETA ** (jnp.arange(0, dim, 2, dtype=F32) / dim))
    ang = jnp.arange(seq, dtype=F32)[:, None] * inv[None, :]
    cos, sin = jnp.cos(ang), jnp.sin(ang)
    return jnp.concatenate([cos, cos], axis=1), jnp.concatenate([-sin, sin], axis=1)


HP = 2


def _triangle(nb, by_key):
    pairs = [(i, j) for j in range(nb) for i in range(j, nb)] if by_key else [(i, j) for i in range(nb) for j in range(i + 1)]
    return jnp.asarray([p[0] for p in pairs], jnp.int32), jnp.asarray([p[1] for p in pairs], jnp.int32)


_Q_BLOCK = lambda n, ii, jj: ii[n]
_K_BLOCK = lambda n, ii, jj: jj[n]


def _hspec(arr_kind, t, w, off, seq_of):
    if arr_kind == "cols":
        assert off % HP == 0
        return pl.BlockSpec((t, HP * w), lambda h, n, ii, jj: (seq_of(n, ii, jj), off // HP + h))
    return pl.BlockSpec((HP, t, w), lambda h, n, ii, jj: (h, seq_of(n, ii, jj), off))


def _head(ref, arr_kind, hh, w):
    return ref[:, hh * w:(hh + 1) * w] if arr_kind == "cols" else ref[hh]


def _colspec(t, seq_of):
    return pl.BlockSpec((HP, t, 1), lambda h, n, ii, jj: (h, seq_of(n, ii, jj), 0))


def _rowspec(t, seq_of):
    return pl.BlockSpec((HP, 1, t), lambda h, n, ii, jj: (h, 0, seq_of(n, ii, jj)))


def _causal(s, qi, kj, t, transposed=False):
    a = lax.broadcasted_iota(jnp.int32, (t, t), 0)
    b = lax.broadcasted_iota(jnp.int32, (t, t), 1)
    keep = (kj * t + a <= qi * t + b) if transposed else (kj * t + b <= qi * t + a)
    return jnp.where(keep, s, NEG)


def flash_fwd(q, k, v, cum, *, n_heads, seq, t, scale, name):
    nb = seq // t
    qs, ks = _Q_BLOCK, _K_BLOCK
    steps = _triangle(nb, by_key=False)
    ins = [q[0], k[0], v[0]]
    specs = [_hspec(q[1], t, q[2], q[3], qs), _hspec(k[1], t, k[2], k[3], ks), _hspec(v[1], t, v[2], v[3], ks)]
    if cum is not None:
        ins += [cum[0], cum[1]]
        specs += [_colspec(t, qs), _rowspec(t, ks)]

    def body(ii_ref, jj_ref, *refs):
        q_ref, k_ref, v_ref = refs[:3]
        o_ref, lse_ref, m_s, l_s, acc_s = refs[-5:]
        i, j = ii_ref[pl.program_id(1)], jj_ref[pl.program_id(1)]

        @pl.when(j == 0)
        def _():
            m_s[...] = jnp.full(m_s.shape, NEG, F32)
            l_s[...] = jnp.zeros(l_s.shape, F32)
            acc_s[...] = jnp.zeros(acc_s.shape, F32)

        def block(masked):
            new = []
            for hh in range(HP):
                qb = _head(q_ref, q[1], hh, q[2]).astype(MXU)
                kb = _head(k_ref, k[1], hh, k[2]).astype(MXU)
                s = lax.dot_general(qb, kb, NT, preferred_element_type=F32) * scale
                if cum is not None:
                    s = s + (refs[3][hh] - refs[4][hh])
                if masked:
                    s = _causal(s, i, j, t)
                m_old = m_s[hh]
                m_new = jnp.maximum(m_old, jnp.max(s, axis=1, keepdims=True))
                alpha = jnp.exp(m_old - m_new)
                p = jnp.exp(s - m_new)
                vb = _head(v_ref, v[1], hh, v[2]).astype(MXU)
                pv = jnp.dot(p.astype(MXU), vb, preferred_element_type=F32)
                new.append((m_new, alpha, alpha * l_s[hh] + jnp.sum(p, axis=1, keepdims=True), pv))
            for hh, (m_new, alpha, l_new, pv) in enumerate(new):
                acc_s[hh] = alpha * acc_s[hh] + pv
                l_s[hh] = l_new
                m_s[hh] = m_new

        @pl.when(j < i)
        def _():
            block(False)

        @pl.when(j == i)
        def _():
            block(True)

        @pl.when(j == i)
        def _():
            for hh in range(HP):
                o_ref[:, hh * HEAD:(hh + 1) * HEAD] = (acc_s[hh] / l_s[hh]).astype(o_ref.dtype)
                lse_ref[hh] = m_s[hh] + jnp.log(l_s[hh])

    return pl.pallas_call(
        body, name=name,
        grid_spec=pltpu.PrefetchScalarGridSpec(
            num_scalar_prefetch=2, grid=(n_heads // HP, steps[0].shape[0]), in_specs=specs,
            out_specs=[pl.BlockSpec((t, HP * HEAD), lambda h, n, ii, jj: (ii[n], h)), _colspec(t, qs)],
            scratch_shapes=[pltpu.VMEM((HP, t, 1), F32), pltpu.VMEM((HP, t, 1), F32), pltpu.VMEM((HP, t, HEAD), F32)]),
        out_shape=[jax.ShapeDtypeStruct((seq, n_heads * HEAD), F32), jax.ShapeDtypeStruct((n_heads, seq, 1), F32)],
        compiler_params=_cparams(2),
    )(*steps, *ins)


def flash_bwd_dq(q, k, v, do, lse, delta, cum, *, n_heads, seq, t, scale, name):
    nb = seq // t
    qs, ks = _Q_BLOCK, _K_BLOCK
    steps = _triangle(nb, by_key=False)
    ins = [q[0], k[0], v[0], do, lse, delta]
    specs = [_hspec(q[1], t, q[2], q[3], qs), _hspec(k[1], t, k[2], k[3], ks), _hspec(v[1], t, v[2], v[3], ks),
             _hspec("cols", t, HEAD, 0, qs), _colspec(t, qs), _colspec(t, qs)]
    if cum is not None:
        ins += [cum[0], cum[1]]
        specs += [_colspec(t, qs), _rowspec(t, ks)]
    wq = q[2]
    n_out = 1 if cum is None else 2

    def body(ii_ref, jj_ref, *refs):
        q_ref, k_ref, v_ref, do_ref, lse_ref, dl_ref = refs[:6]
        outs = refs[-2 * n_out:-n_out]
        accs = refs[-n_out:]
        i, j = ii_ref[pl.program_id(1)], jj_ref[pl.program_id(1)]

        @pl.when(j == 0)
        def _():
            for a in accs:
                a[...] = jnp.zeros(a.shape, F32)

        def block(masked):
            for hh in range(HP):
                kb = _head(k_ref, k[1], hh, k[2]).astype(MXU)
                s = lax.dot_general(_head(q_ref, q[1], hh, q[2]).astype(MXU), kb, NT, preferred_element_type=F32) * scale
                if cum is not None:
                    s = s + (refs[6][hh] - refs[7][hh])
                if masked:
                    s = _causal(s, i, j, t)
                p = jnp.exp(s - lse_ref[hh])
                dp = lax.dot_general(_head(do_ref, "cols", hh, HEAD).astype(MXU), _head(v_ref, v[1], hh, v[2]).astype(MXU),
                                     NT, preferred_element_type=F32)
                ds = p * (dp - dl_ref[hh])
                accs[0][hh] += jnp.dot(ds.astype(MXU), kb, preferred_element_type=F32)
                if cum is not None:
                    accs[1][hh] += jnp.sum(ds, axis=1, keepdims=True)

        @pl.when(j < i)
        def _():
            block(False)

        @pl.when(j == i)
        def _():
            block(True)

        @pl.when(j == i)
        def _():
            outs[0][...] = accs[0][...] * scale
            if cum is not None:
                outs[1][...] = accs[1][...]

    out_specs = [pl.BlockSpec((HP, t, wq), lambda h, n, ii, jj: (h, ii[n], 0))]
    out_shape = [jax.ShapeDtypeStruct((n_heads, seq, wq), F32)]
    scratch = [pltpu.VMEM((HP, t, wq), F32)]
    if cum is not None:
        out_specs.append(_colspec(t, qs))
        out_shape.append(jax.ShapeDtypeStruct((n_heads, seq, 1), F32))
        scratch.append(pltpu.VMEM((HP, t, 1), F32))
    res = pl.pallas_call(
        body, name=name,
        grid_spec=pltpu.PrefetchScalarGridSpec(num_scalar_prefetch=2, grid=(n_heads // HP, steps[0].shape[0]),
                                               in_specs=specs, out_specs=out_specs, scratch_shapes=scratch),
        out_shape=out_shape, compiler_params=_cparams(2),
    )(*steps, *ins)
    return res[0] if cum is None else res


def flash_bwd_dkv(q, k, v, do, lse_row, delta_row, cum, *, n_heads, seq, t, scale, name):
    nb = seq // t
    qs, ks = _Q_BLOCK, _K_BLOCK
    steps = _triangle(nb, by_key=True)
    ins = [q[0], k[0], v[0], do, lse_row, delta_row]
    specs = [_hspec(q[1], t, q[2], q[3], qs), _hspec(k[1], t, k[2], k[3], ks), _hspec(v[1], t, v[2], v[3], ks),
             _hspec("cols", t, HEAD, 0, qs), _rowspec(t, qs), _rowspec(t, qs)]
    if cum is not None:
        ins += [cum[0], cum[1]]
        specs += [_colspec(t, ks), _rowspec(t, qs)]
    wk = k[2]
    n_out = 2 if cum is None else 3

    def body(ii_ref, jj_ref, *refs):
        q_ref, k_ref, v_ref, do_ref, lse_ref, dl_ref = refs[:6]
        outs = refs[-2 * n_out:-n_out]
        accs = refs[-n_out:]
        i, j = ii_ref[pl.program_id(1)], jj_ref[pl.program_id(1)]

        @pl.when(i == j)
        def _():
            for a in accs:
                a[...] = jnp.zeros(a.shape, F32)

        def block(masked):
            for hh in range(HP):
                qb = _head(q_ref, q[1], hh, q[2]).astype(MXU)
                dob = _head(do_ref, "cols", hh, HEAD).astype(MXU)
                st = lax.dot_general(_head(k_ref, k[1], hh, k[2]).astype(MXU), qb, NT, preferred_element_type=F32) * scale
                if cum is not None:
                    st = st + (refs[7][hh] - refs[6][hh])
                if masked:
                    st = _causal(st, i, j, t, transposed=True)
                pt = jnp.exp(st - lse_ref[hh])
                dpt = lax.dot_general(_head(v_ref, v[1], hh, v[2]).astype(MXU), dob, NT, preferred_element_type=F32)
                dst = pt * (dpt - dl_ref[hh])
                accs[0][hh] += jnp.dot(dst.astype(MXU), qb, preferred_element_type=F32)
                accs[1][hh] += jnp.dot(pt.astype(MXU), dob, preferred_element_type=F32)
                if cum is not None:
                    accs[2][hh] -= jnp.sum(dst, axis=1, keepdims=True)

        @pl.when(i > j)
        def _():
            block(False)

        @pl.when(i == j)
        def _():
            block(True)

        @pl.when(i == nb - 1)
        def _():
            outs[0][...] = accs[0][...] * scale
            for o, a in zip(outs[1:], accs[1:]):
                o[...] = a[...]

    out_specs = [pl.BlockSpec((HP, t, wk), lambda h, n, ii, jj: (h, jj[n], 0)),
                 pl.BlockSpec((HP, t, HEAD), lambda h, n, ii, jj: (h, jj[n], 0))]
    out_shape = [jax.ShapeDtypeStruct((n_heads, seq, wk), F32), jax.ShapeDtypeStruct((n_heads, seq, HEAD), F32)]
    scratch = [pltpu.VMEM((HP, t, wk), F32), pltpu.VMEM((HP, t, HEAD), F32)]
    if cum is not None:
        out_specs.append(_colspec(t, ks))
        out_shape.append(jax.ShapeDtypeStruct((n_heads, seq, 1), F32))
        scratch.append(pltpu.VMEM((HP, t, 1), F32))
    return pl.pallas_call(
        body, name=name,
        grid_spec=pltpu.PrefetchScalarGridSpec(num_scalar_prefetch=2, grid=(n_heads // HP, steps[0].shape[0]),
                                               in_specs=specs, out_specs=out_specs, scratch_shapes=scratch),
        out_shape=out_shape, compiler_params=_cparams(2),
    )(*steps, *ins)


def attn_delta(do, o, *, n_heads, seq, name):
    t = _tile(seq, 512)
    spec = pl.BlockSpec((t, HEAD), lambda h, m: (m, h))
    return rowwise(
        lambda a, b: jnp.sum(a.astype(F32) * b.astype(F32), axis=1, keepdims=True), [do, o], [spec, spec],
        jax.ShapeDtypeStruct((n_heads, seq, 1), F32), pl.BlockSpec((None, t, 1), lambda h, m: (h, m, 0)),
        (n_heads, seq // t), name)


def _dil_scores(q, kc, kp, n, scale):
    i = lax.broadcasted_iota(jnp.int32, (HEAD, HEAD), 0)
    j = lax.broadcasted_iota(jnp.int32, (HEAD, HEAD), 1)
    sc = lax.dot_general(q, kc, NT, preferred_element_type=F32) * scale
    sp = lax.dot_general(q, kp, NT, preferred_element_type=F32) * scale
    sc = jnp.where(j <= i, sc, NEG)
    sp = jnp.where(jnp.logical_and(j >= i, n > 0), sp, NEG)
    return sc, sp


def _strip_spec(length, n_heads, col_blocks, off):
    return pl.BlockSpec((length, HEAD), lambda r, h: (0, r * col_blocks + off + h))


def dil_fwd(q, k, v, *, seq, dil, n_heads, name):
    length = seq // dil
    nb = length // HEAD
    scale = HEAD ** -0.5
    view = lambda a: a.reshape(length, dil * a.shape[1])
    spec = _strip_spec(length, n_heads, n_heads, 0)

    def body(q_ref, k_ref, v_ref, o_ref, lse_ref):
        def step(n, carry):
            cur = pl.ds(pl.multiple_of(n * HEAD, HEAD), HEAD)
            prev = pl.ds(pl.multiple_of(jnp.maximum(n - 1, 0) * HEAD, HEAD), HEAD)
            qb = q_ref[cur, :].astype(MXU)
            sc, sp = _dil_scores(qb, k_ref[cur, :].astype(MXU), k_ref[prev, :].astype(MXU), n, scale)
            m = jnp.maximum(jnp.max(sc, axis=1, keepdims=True), jnp.max(sp, axis=1, keepdims=True))
            ec, ep = jnp.exp(sc - m), jnp.exp(sp - m)
            l = jnp.sum(ec, axis=1, keepdims=True) + jnp.sum(ep, axis=1, keepdims=True)
            o = jnp.dot((ec / l).astype(MXU), v_ref[cur, :].astype(MXU), preferred_element_type=F32)
            o = o + jnp.dot((ep / l).astype(MXU), v_ref[prev, :].astype(MXU), preferred_element_type=F32)
            o_ref[cur, :] = o
            lse_ref[cur, :] = jnp.broadcast_to(m + jnp.log(l), (HEAD, HEAD))
            return carry

        lax.fori_loop(0, nb, step, 0)

    out = jax.ShapeDtypeStruct((length, dil * n_heads * HEAD), F32)
    o, lse = pl.pallas_call(
        body, name=name, grid=(dil, n_heads), in_specs=[spec, spec, spec], out_specs=[spec, spec], out_shape=[out, out],
        compiler_params=_cparams(2),
    )(view(q), view(k), view(v))
    return o.reshape(seq, -1), lse.reshape(seq, -1)


def dil_bwd(q, k, v, o, lse, do, dlse, *, seq, dil, n_heads, name):
    length = seq // dil
    nb = length // HEAD
    scale = HEAD ** -0.5
    view = lambda a: a.reshape(length, dil * a.shape[1])
    spec = _strip_spec(length, n_heads, n_heads, 0)

    def body(q_ref, k_ref, v_ref, o_ref, lse_ref, do_ref, dlse_ref, dq_ref, dk_ref, dv_ref):
        dk_ref[...] = jnp.zeros(dk_ref.shape, F32)
        dv_ref[...] = jnp.zeros(dv_ref.shape, F32)

        def step(n, carry):
            cur = pl.ds(pl.multiple_of(n * HEAD, HEAD), HEAD)
            prev = pl.ds(pl.multiple_of(jnp.maximum(n - 1, 0) * HEAD, HEAD), HEAD)
            qb = q_ref[cur, :].astype(MXU)
            kc, kp = k_ref[cur, :].astype(MXU), k_ref[prev, :].astype(MXU)
            vc, vp = v_ref[cur, :].astype(MXU), v_ref[prev, :].astype(MXU)
            sc, sp = _dil_scores(qb, kc, kp, n, scale)
            lse_b = jnp.max(lse_ref[cur, :], axis=1, keepdims=True)
            pc, pp = jnp.exp(sc - lse_b), jnp.exp(sp - lse_b)
            dob = do_ref[cur, :]
            shift = jnp.sum(dlse_ref[cur, :], axis=1, keepdims=True) - jnp.sum(dob * o_ref[cur, :], axis=1, keepdims=True)
            dob = dob.astype(MXU)
            dsc = pc * (lax.dot_general(dob, vc, NT, preferred_element_type=F32) + shift)
            dsp = pp * (lax.dot_general(dob, vp, NT, preferred_element_type=F32) + shift)
            dscb, dspb = dsc.astype(MXU), dsp.astype(MXU)
            dq = jnp.dot(dscb, kc, preferred_element_type=F32) + jnp.dot(dspb, kp, preferred_element_type=F32)
            dq_ref[cur, :] = dq * scale
            dk_ref[cur, :] += lax.dot_general(dscb, qb, TN, preferred_element_type=F32) * scale
            dv_ref[cur, :] += lax.dot_general(pc.astype(MXU), dob, TN, preferred_element_type=F32)
            dk_ref[prev, :] += lax.dot_general(dspb, qb, TN, preferred_element_type=F32) * scale
            dv_ref[prev, :] += lax.dot_general(pp.astype(MXU), dob, TN, preferred_element_type=F32)
            return carry

        lax.fori_loop(0, nb, step, 0)

    out = jax.ShapeDtypeStruct((length, dil * n_heads * HEAD), F32)
    res = pl.pallas_call(
        body, name=name, grid=(dil, n_heads), in_specs=[spec] * 7, out_specs=[spec] * 3, out_shape=[out] * 3,
        compiler_params=_cparams(2),
    )(*[view(a) for a in (q, k, v, o, lse, do, dlse)])
    return [r.reshape(seq, -1) for r in res]


def _tri(n, kind):
    i = lax.broadcasted_iota(jnp.int32, (n, n), 0)
    j = lax.broadcasted_iota(jnp.int32, (n, n), 1)
    return jnp.where({"le": i <= j, "ge": i >= j}[kind], 1.0, 0.0).astype(F32)


def _block_matrix(n_rows, per_head, kind):
    r = lax.broadcasted_iota(jnp.int32, (n_rows, n_rows), 0)
    c = lax.broadcasted_iota(jnp.int32, (n_rows, n_rows), 1)
    same = (r // per_head) == (c // per_head)
    rel = {"lt": c < r, "gt": c > r, "all": c == c}[kind]
    return jnp.where(jnp.logical_and(same, rel), 1.0, 0.0).astype(F32)


def _lane_pick(x, lane):
    j = lax.broadcasted_iota(jnp.int32, x.shape, 1)
    return jnp.sum(jnp.where(j == lane, x, 0.0), axis=1, keepdims=True)


def _log_sigmoid(z):
    return jnp.minimum(z, 0.0) - jnp.log1p(jnp.exp(-jnp.abs(z)))


def fox_gate_fwd(z, bias_rows, per_head, name):
    n_rows = z.shape[0]

    def body(z_ref, b_ref, c_ref):
        logf = _log_sigmoid(z_ref[...] + b_ref[...])
        within = jnp.dot(logf, _tri(HEAD, "le"), precision=HI, preferred_element_type=F32)
        tot = jnp.broadcast_to(_lane_pick(within, HEAD - 1), (n_rows, HEAD))
        c_ref[...] = within + jnp.dot(_block_matrix(n_rows, per_head, "lt"), tot, precision=HI, preferred_element_type=F32)

    return pl.pallas_call(body, name=name, out_shape=jax.ShapeDtypeStruct(z.shape, F32),
                          compiler_params=pltpu.CompilerParams(vmem_limit_bytes=VMEM_LIMIT))(z, bias_rows)


def fox_gate_bwd(z, bias_rows, dcum_q, dcum_k, per_head, name):
    n_rows = z.shape[0]

    def body(z_ref, b_ref, dcq_ref, dck_ref, dz_ref, db_ref):
        within = jnp.dot(dcq_ref[...] + dck_ref[...], _tri(HEAD, "ge"), precision=HI, preferred_element_type=F32)
        tot = jnp.broadcast_to(_lane_pick(within, 0), (n_rows, HEAD))
        dlogf = within + jnp.dot(_block_matrix(n_rows, per_head, "gt"), tot, precision=HI, preferred_element_type=F32)
        dz = dlogf * jax.nn.sigmoid(-(z_ref[...] + b_ref[...]))
        dz_ref[...] = dz
        rs = jnp.broadcast_to(jnp.sum(dz, axis=1, keepdims=True), (n_rows, HEAD))
        db_ref[...] = jnp.dot(_block_matrix(n_rows, per_head, "all"), rs, precision=HI, preferred_element_type=F32)

    shp = jax.ShapeDtypeStruct(z.shape, F32)
    return pl.pallas_call(body, name=name, out_shape=[shp, shp],
                          compiler_params=pltpu.CompilerParams(vmem_limit_bytes=VMEM_LIMIT))(z, bias_rows, dcum_q, dcum_k)


def exchange(x, scatter, name):
    blk = x.shape[1:] if scatter else x.shape

    def body(x_ref, o_ref, send_sems, recv_sems, local_sem):
        mx, my, mc = lax.axis_index("x"), lax.axis_index("y"), lax.axis_index("c")
        me = 4 * mx + 2 * my + mc
        flip = lambda v, f: 1 - v if f else v
        local = pltpu.make_async_copy(x_ref.at[me] if scatter else x_ref, o_ref.at[me], local_sem)
        local.start()
        sends, recvs = [], []
        for n in range(1, NDEV):
            px, py, pc = flip(mx, n & 4), flip(my, n & 2), flip(mc, n & 1)
            p = 4 * px + 2 * py + pc
            sends.append(pltpu.make_async_remote_copy(
                src_ref=x_ref.at[p] if scatter else x_ref, dst_ref=o_ref.at[me], send_sem=send_sems.at[n - 1],
                recv_sem=recv_sems.at[n - 1], device_id=(px, py, pc), device_id_type=pl.DeviceIdType.MESH))
            recvs.append(pltpu.make_async_remote_copy(
                src_ref=x_ref.at[me] if scatter else x_ref, dst_ref=o_ref.at[p], send_sem=send_sems.at[n - 1],
                recv_sem=recv_sems.at[n - 1], device_id=(px, py, pc), device_id_type=pl.DeviceIdType.MESH))
        for cp in sends:
            cp.start()
        for cp in recvs:
            cp.wait_recv()
        for cp in sends:
            cp.wait_send()
        local.wait()

    hbm = pl.BlockSpec(memory_space=pltpu.HBM)
    return pl.pallas_call(
        body, name=name, in_specs=[hbm], out_specs=hbm, out_shape=jax.ShapeDtypeStruct((NDEV, *blk), x.dtype),
        scratch_shapes=[pltpu.SemaphoreType.DMA((NDEV - 1,)), pltpu.SemaphoreType.DMA((NDEV - 1,)), pltpu.SemaphoreType.DMA],
    )(x)


def _exchange_copies(x_refs, land_refs, send_sems, recv_sems, local_sems, scatter, with_recvs):
    mx, my, mc = lax.axis_index("x"), lax.axis_index("y"), lax.axis_index("c")
    me = 4 * mx + 2 * my + mc
    flip = lambda v, f: 1 - v if f else v
    local, sends, recvs = [], [], []
    for a, (x_ref, o_ref) in enumerate(zip(x_refs, land_refs)):
        local.append(pltpu.make_async_copy(x_ref.at[me] if scatter else x_ref, o_ref.at[me], local_sems.at[a]))
        for n in range(1, NDEV):
            px, py, pc = flip(mx, n & 4), flip(my, n & 2), flip(mc, n & 1)
            p = 4 * px + 2 * py + pc
            sem = (NDEV - 1) * a + n - 1
            mk = lambda src, dst: pltpu.make_async_remote_copy(
                src_ref=src, dst_ref=dst, send_sem=send_sems.at[sem], recv_sem=recv_sems.at[sem],
                device_id=(px, py, pc), device_id_type=pl.DeviceIdType.MESH)
            sends.append(mk(x_ref.at[p] if scatter else x_ref, o_ref.at[me]))
            if with_recvs:
                recvs.append(mk(x_ref.at[me] if scatter else x_ref, o_ref.at[p]))
    return local, sends, recvs


_HBM = pl.BlockSpec(memory_space=pltpu.HBM)
_SEM = pl.BlockSpec(memory_space=pltpu.SEMAPHORE)
_EFFECT = pltpu.SideEffectType.DATAFLOW_SIDE_EFFECTING


def exchange_start(xs, scatter, name):
    n = len(xs)
    lands = [jax.ShapeDtypeStruct((NDEV, *(x.shape[1:] if scatter else x.shape)), x.dtype) for x in xs]

    def body(*refs):
        x_refs, land_refs = refs[:n], refs[n:2 * n]
        send_sems, recv_sems, local_sems = refs[2 * n:2 * n + 3]
        token = refs[-1]
        local, sends, _ = _exchange_copies(x_refs, land_refs, send_sems, recv_sems, local_sems, scatter, False)
        for cp in local + sends:
            cp.start()
        token[...] = jnp.zeros(token.shape, token.dtype)

    n_sem = (NDEV - 1) * n
    out = pl.pallas_call(
        body, name=name,
        out_shape=(pltpu.SemaphoreType.DMA((n_sem,)), pltpu.SemaphoreType.DMA((n_sem,)), pltpu.SemaphoreType.DMA((n,)),
                   *[pltpu.HBM(x.shape, x.dtype) for x in xs], *[pltpu.HBM(s.shape, s.dtype) for s in lands],
                   jax.ShapeDtypeStruct((8, HEAD), F32)),
        in_specs=[_HBM] * (2 * n), out_specs=(_SEM, _SEM, _SEM, *[_HBM] * (2 * n), pl.BlockSpec(memory_space=pltpu.VMEM)),
        input_output_aliases={i: 3 + i for i in range(2 * n)},
        compiler_params=pltpu.CompilerParams(has_side_effects=_EFFECT),
    )(*[pltpu.with_memory_space_constraint(x, pltpu.HBM) for x in xs],
      *[pltpu.with_memory_space_constraint(lax.empty(s.shape, s.dtype), pltpu.HBM) for s in lands])
    return (out[:3], out[3:3 + n], out[3 + n:3 + 2 * n], scatter), out[-1]


def exchange_wait(handle, after, name):
    sems, xs, lands, scatter = handle
    n = len(xs)

    def body(*refs):
        x_refs, land_refs = refs[:n], refs[n:2 * n]
        send_sems, recv_sems, local_sems = refs[2 * n:2 * n + 3]
        local, sends, recvs = _exchange_copies(x_refs, land_refs, send_sems, recv_sems, local_sems, scatter, True)
        for cp in sends:
            cp.wait_send()
        for cp in recvs:
            cp.wait_recv()
        for cp in local:
            cp.wait()

    out = pl.pallas_call(
        body, name=name, out_shape=tuple(pltpu.HBM(a.shape, a.dtype) for a in (*xs, *lands)),
        in_specs=[_HBM] * (2 * n) + [_SEM] * 3 + [pl.BlockSpec(memory_space=pl.ANY)], out_specs=tuple([_HBM] * (2 * n)),
        input_output_aliases={i: i for i in range(2 * n)}, compiler_params=pltpu.CompilerParams(has_side_effects=_EFFECT),
    )(*xs, *lands, *sems, after)
    return list(out[n:])


def adamw(parts, w, m, v, name, dep=None):
    depth = len(parts)
    deps = [] if dep is None else [dep]
    n_parts, rows, cols = parts[0].shape
    t = rows
    for cand in (256, 128, 64, 32, 16, 8):
        if rows % cand == 0 and (n_parts * parts[0].dtype.itemsize + 7 * 4) * cand * cols <= ADAMW_STEP_BYTES:
            t = cand
            break
    nr = rows // t

    def body(*refs):
        p_refs = refs[:depth]
        w_ref, m_ref, v_ref = refs[depth:depth + 3]
        g_out, d_out, m_out, v_out = refs[depth + 3 + len(deps):]
        layer = pl.program_id(0)
        for i in range(depth):
            @pl.when(layer == i)
            def _(p=p_refs[i]):
                g = p[0].astype(F32)
                for j in range(1, n_parts):
                    g = g + p[j].astype(F32)
                m_new = ADAM_B1 * m_ref[...] + (1.0 - ADAM_B1) * g
                v_new = ADAM_B2 * v_ref[...] + (1.0 - ADAM_B2) * jnp.square(g)
                m_hat = m_new / (1.0 - ADAM_B1 ** ADAM_STEP)
                v_hat = v_new / (1.0 - ADAM_B2 ** ADAM_STEP)
                g_out[...] = g
                d_out[...] = -ADAM_LR * (m_hat / (jnp.sqrt(v_hat) + ADAM_EPS) + ADAM_WD * w_ref[...])
                m_out[...] = m_new
                v_out[...] = v_new

    def part_spec(i):
        return pl.BlockSpec((n_parts, t, cols), lambda l, r: (0, jnp.where(l < i, 0, jnp.where(l == i, r, nr - 1)), 0))

    spec = pl.BlockSpec((t, cols), lambda l, r: (l * nr + r, 0))
    out = jax.ShapeDtypeStruct((depth * rows, cols), F32)
    return pl.pallas_call(
        body, name=name, grid=(depth, nr),
        in_specs=[*[part_spec(i) for i in range(depth)], spec, spec, spec, *[_dep_spec(a) for a in deps]],
        out_specs=[spec] * 4, out_shape=[out] * 4, compiler_params=_cparams(2),
    )(*parts, w, m, v, *deps)


def sum_parts(parts, name):
    n_parts, rows, cols = parts.shape
    t = _tile(rows, 128)

    def fn(p):
        g = p[0].astype(F32)
        for i in range(1, n_parts):
            g = g + p[i].astype(F32)
        return g

    return rowwise(fn, [parts], [pl.BlockSpec((n_parts, t, cols), lambda r: (0, r, 0))],
                   jax.ShapeDtypeStruct((rows, cols), F32), pl.BlockSpec((t, cols), lambda r: (r, 0)), (rows // t,), name)


def _sd(shape, dtype=F32):
    return jax.ShapeDtypeStruct(shape, dtype)


def rms_fwd(x, g, name, col=0, width=None, dep=None):
    seq = x.shape[0]
    width = width or x.shape[1]
    t = _tile(seq, 512)
    return rowwise(_rms, [x, g.reshape(1, width)], [_rows(t, width, col), _whole((1, width))], _sd((seq, width), ACT),
                   _rows(t, width), (seq // t,), name, dep=dep)


def rms_bwd(x, g, dy, name, col=0, width=None, add=None, dx_dtype=F32):
    seq = x.shape[0]
    width = width or x.shape[1]
    t = _tile(seq, 512)
    ins, specs = [x, g.reshape(1, width), dy], [_rows(t, width, col), _whole((1, width)), _rows(t, width)]
    if add is None:
        fn = _rms_bwd
    else:
        ins.append(add)
        specs.append(_rows(t, width))

        def fn(x_, g_, dy_, add_):
            dx, dg = _rms_bwd(x_, g_, dy_)
            return dx + add_, dg
    return rowwise(fn, ins, specs, [_sd((seq, width), dx_dtype), _sd((1, width))], [_rows(t, width), _whole((1, width))],
                   (seq // t,), name, acc={1: (0,)})


def ffn_fwd(x, g, wg, wu, wd, l, tag, dep=None):
    seq, d = x.shape
    f = wg.shape[2]
    tm = _tile(seq, 1024)
    h = rms_fwd(x, g, f"{tag}_rms", dep=dep)
    hid_spec = pl.BlockSpec((None, tm, f), lambda j, m, k: (j, m, 0))
    up = lambda w, nm, **kw: mm(
        h, w, name=nm, dims=NT, grid=(NDEV, seq // tm, 1), nk=1,
        a_spec=pl.BlockSpec((tm, d), lambda j, m, k: (m, 0)),
        b_spec=pl.BlockSpec((None, None, f, d), lambda j, m, k: (j, l, 0, 0)), o_spec=hid_spec, **kw)
    a = up(wg, f"{tag}_gate", out_shape=_sd((NDEV, seq, f), ACT))
    b, hid = up(wu, f"{tag}_up", out_shape=[_sd((NDEV, seq, f), ACT)] * 2, extras=[a], extra_specs=[hid_spec],
                epilogue=lambda acc, a_: (acc, jax.nn.silu(a_.astype(F32)) * acc))
    tn = _tile(d, 1024)
    out = mm(hid, wd, name=f"{tag}_down", dims=NN, grid=(seq // tm, d // tn, NDEV), nk=NDEV,
             a_spec=pl.BlockSpec((None, tm, f), lambda m, n, k: (k, m, 0)),
             b_spec=pl.BlockSpec((None, None, f, tn), lambda m, n, k: (k, l, 0, n)),
             o_spec=pl.BlockSpec((tm, tn), lambda m, n, k: (m, n)), out_shape=_sd((seq, d)),
             extras=[x], extra_specs=[pl.BlockSpec((tm, tn), lambda m, n, k: (m, n))],
             epilogue=lambda acc, x_: x_ + 0.5 * acc)
    return out, (x, h, a, b, hid)


def ffn_bwd(dout, saved, g, wg, wu, wd, l, tag, send, dep=None):
    x, h, a, b, hid = saved
    seq, d = x.shape
    f = wg.shape[2]
    tm = _tile(seq, 1024)
    tk = _tile(seq, 1024)
    def act_bwd(acc, a_, b_):
        dh_, a_, b_ = 0.5 * acc, a_.astype(F32), b_.astype(F32)
        sig = jax.nn.sigmoid(a_)
        return dh_ * b_ * sig * (1.0 + a_ * (1.0 - sig)), dh_ * a_ * sig

    hid_spec = pl.BlockSpec((None, tm, f), lambda j, m, k: (j, m, 0))
    da, db = mm(dout, wd, name=f"{tag}_dhid", dims=NT, grid=(NDEV, seq // tm, 1), nk=1,
                a_spec=pl.BlockSpec((tm, d), lambda j, m, k: (m, 0)),
                b_spec=pl.BlockSpec((None, None, f, d), lambda j, m, k: (j, l, 0, 0)),
                o_spec=hid_spec, out_shape=[_sd((NDEV, seq, f), ACT)] * 2, extras=[a, b], extra_specs=[hid_spec] * 2,
                epilogue=act_bwd, dep=dep)
    tn = _tile(d, 1024)
    dw = lambda act, rhs, nm, epi: mm(
        act, rhs, name=nm, dims=TN, grid=(NDEV, d // tn, seq // tk), nk=seq // tk,
        a_spec=pl.BlockSpec((None, tk, f), lambda j, n, k: (j, k, 0)),
        b_spec=pl.BlockSpec((tk, tn), lambda j, n, k: (k, n)),
        o_spec=pl.BlockSpec((None, f, tn), lambda j, n, k: (j, 0, n)), out_shape=_sd((NDEV, f, d), COMM), epilogue=epi)
    dwd = dw(hid, dout, f"{tag}_dwd", lambda acc: 0.5 * acc)
    dwg, dwu = dw(da, h, f"{tag}_dwg", None), dw(db, h, f"{tag}_dwu", None)
    token = send(dwg, dwu, dwd)

    def dh_body(da_ref, db_ref, wg_ref, wu_ref, dep_ref, o_ref, acc_ref):
        k = pl.program_id(2)
        part = jnp.dot(da_ref[...].astype(MXU), wg_ref[...].astype(MXU), preferred_element_type=F32)
        part = part + jnp.dot(db_ref[...].astype(MXU), wu_ref[...].astype(MXU), preferred_element_type=F32)

        @pl.when(k == 0)
        def _():
            acc_ref[...] = part

        @pl.when(k > 0)
        def _():
            acc_ref[...] += part

        @pl.when(k == NDEV - 1)
        def _():
            o_ref[...] = acc_ref[...]

    act_spec = pl.BlockSpec((None, tm, f), lambda m, n, k: (k, m, 0))
    w_spec = pl.BlockSpec((None, None, f, tn), lambda m, n, k: (k, l, 0, n))
    dh = pl.pallas_call(
        dh_body, name=f"{tag}_dh", grid=(seq // tm, d // tn, NDEV),
        in_specs=[act_spec, act_spec, w_spec, w_spec, _dep_spec(token)],
        out_specs=pl.BlockSpec((tm, tn), lambda m, n, k: (m, n)), out_shape=_sd((seq, d)),
        scratch_shapes=[pltpu.VMEM((tm, tn), F32)], compiler_params=_cparams(3),
    )(da, db, wg, wu, token)
    return rms_bwd(x, g, dh, f"{tag}_drms", add=dout)


def _dense(a, w, l, name, out_dtype=F32, extras=(), epilogue=None):
    seq, kdim = a.shape
    n = w.shape[3]
    w2 = w.reshape(kdim, n)
    tm, tn, tk = _tile(seq, 1024), _tile(n, 1024), _tile(kdim, 1024)
    nk = kdim // tk
    return mm(a, w2, name=name, dims=NN, grid=(seq // tm, n // tn, nk), nk=nk,
              a_spec=pl.BlockSpec((tm, tk), lambda m, c, k: (m, k)),
              b_spec=pl.BlockSpec((tk, tn), lambda m, c, k: (k, c)),
              o_spec=pl.BlockSpec((tm, tn), lambda m, c, k: (m, c)), out_shape=_sd((seq, n), out_dtype),
              extras=list(extras), extra_specs=[pl.BlockSpec((tm, tn), lambda m, c, k: (m, c))] * len(extras),
              epilogue=epilogue)


def _dense_dx(dy, w, l, name, extras=(), epilogue=None, dep=None):
    seq, n = dy.shape
    kb = w.shape[2]
    tm = _tile(seq, 1024)
    return mm(dy, w, name=name, dims=NT, grid=(seq // tm, NDEV, 1), nk=1,
              a_spec=pl.BlockSpec((tm, n), lambda m, j, k: (m, 0)),
              b_spec=pl.BlockSpec((None, None, kb, n), lambda m, j, k: (j, l, 0, 0)),
              o_spec=pl.BlockSpec((tm, kb), lambda m, j, k: (m, j)), out_shape=_sd((seq, NDEV * kb)),
              extras=list(extras), extra_specs=[pl.BlockSpec((tm, kb), lambda m, j, k: (m, j))] * len(extras),
              epilogue=epilogue, dep=dep)


def _dense_dw(a, dy, kb, name):
    seq, n = dy.shape
    rows = NDEV * kb
    tk, tn, tr = _tile(seq, 1024), _tile(n, 1024), _tile(rows, 1024)
    out = mm(a, dy, name=name, dims=TN, grid=(rows // tr, n // tn, seq // tk), nk=seq // tk,
             a_spec=pl.BlockSpec((tk, tr), lambda j, c, k: (k, j)),
             b_spec=pl.BlockSpec((tk, tn), lambda j, c, k: (k, c)),
             o_spec=pl.BlockSpec((tr, tn), lambda j, c, k: (j, c)), out_shape=_sd((rows, n), COMM))
    return out.reshape(NDEV, kb, n)


def _heads_up(a, w, l, name, out_dtype):
    seq, r = a.shape
    c = w.shape[3]
    tm = _tile(seq, 1024)
    return mm(a, w, name=name, dims=NN, grid=(NDEV, seq // tm, 1), nk=1,
              a_spec=pl.BlockSpec((tm, r), lambda j, m, k: (m, 0)),
              b_spec=pl.BlockSpec((None, None, r, c), lambda j, m, k: (j, l, 0, 0)),
              o_spec=pl.BlockSpec((None, tm, c), lambda j, m, k: (j, m, 0)), out_shape=_sd((NDEV, seq, c), out_dtype))


def _heads_dx(dy, w, l, name):
    _, seq, c = dy.shape
    r = w.shape[2]
    tm = _tile(seq, 1024)
    return mm(dy, w, name=name, dims=NT, grid=(seq // tm, 1, NDEV), nk=NDEV,
              a_spec=pl.BlockSpec((None, tm, c), lambda m, n, k: (k, m, 0)),
              b_spec=pl.BlockSpec((None, None, r, c), lambda m, n, k: (k, l, 0, 0)),
              o_spec=pl.BlockSpec((tm, r), lambda m, n, k: (m, 0)), out_shape=_sd((seq, r)))


def _heads_dw(a, dy, name):
    seq, r = a.shape
    c = dy.shape[2]
    tk = _tile(seq, 1024)
    return mm(a, dy, name=name, dims=TN, grid=(NDEV, 1, seq // tk), nk=seq // tk,
              a_spec=pl.BlockSpec((tk, r), lambda j, n, k: (k, 0)),
              b_spec=pl.BlockSpec((None, tk, c), lambda j, n, k: (j, k, 0)),
              o_spec=pl.BlockSpec((None, r, c), lambda j, n, k: (j, 0, 0)), out_shape=_sd((NDEV, r, c), COMM))


C_FQ, C_FK, C_FV, C_CQ, C_CKV, C_DQ, C_DK, C_DV = range(8)
MAIN_W = 8 * 512
TAIL_W = 128


def split_w_in(w):
    fq, fk, fv, fl, cq, ckv, kr, dq, dk, dv = jnp.split(w, [512, 1024, 1536, 1540, 2052, 2564, 2628, 3140, 3652], axis=-1)
    main = jnp.concatenate([fq, fk, fv, cq, ckv, dq, dk, dv], axis=-1)
    pad = jnp.zeros((*w.shape[:-1], TAIL_W - 68), w.dtype)
    return main, jnp.concatenate([kr, fl, pad], axis=-1)


def merge_w_in(main, tail):
    fq, fk, fv, cq, ckv, dq, dk, dv = jnp.split(main, 8, axis=-1)
    return jnp.concatenate([fq, fk, fv, tail[..., 64:68], cq, ckv, tail[..., 0:64], dq, dk, dv], axis=-1)


def mixer_fwd(x, p, l, consts, dep=None):
    seq, d = x.shape
    nfox, nmla, ndil = 4, 8, 4
    cos_m, sin_m, cos_p, sin_p = consts
    t = _tile(seq, 512)
    tf = _tile(seq, 512)
    h = rms_fwd(x, p["mix_norm"], f"mix{l}_rms", dep=dep)
    proj = _dense(h, p["w_in_main"], 0, f"mix{l}_proj")
    tail = _dense(h, p["w_in_tail"], 0, f"mix{l}_tail")

    nb = seq // HEAD
    z = tail[:, 64:68].T.reshape(nfox * nb, HEAD)
    bias_rows = jnp.repeat(p["fox_forget_bias"], nb).reshape(nfox * nb, 1)
    cum = fox_gate_fwd(z, bias_rows, nb, f"mix{l}_gate").reshape(nfox, seq)
    cum2 = (cum.reshape(nfox, seq, 1), cum.reshape(nfox, 1, seq))
    fox_qkv = ((proj, "cols", HEAD, C_FQ * 4), (proj, "cols", HEAD, C_FK * 4), (proj, "cols", HEAD, C_FV * 4))
    out_a, lse_a = flash_fwd(*fox_qkv, cum2, n_heads=nfox, seq=seq, t=tf, scale=HEAD ** -0.5, name=f"mix{l}_fox")

    cq = rms_fwd(proj, p["mla_q_norm"], f"mix{l}_cq", col=C_CQ, width=512)
    ckv = rms_fwd(proj, p["mla_kv_norm"], f"mix{l}_ckv", col=C_CKV, width=512)
    q_raw = _heads_up(cq, p["mla_w_uq"], 0, f"mix{l}_uq", F32)
    kv = _heads_up(ckv, p["mla_w_ukv"], 0, f"mix{l}_ukv", ACT)

    def mla_prep(q_, kv_, tail_, cos_, sin_, q_out, k_out):
        perm = _swap_matrix(MLA_ROPE)
        c, s = cos_[...], sin_[...]
        q_out[:, 0:HEAD] = q_[:, 0:HEAD].astype(q_out.dtype)
        q_out[:, HEAD:MLA_QK] = _rope(q_[:, HEAD:MLA_QK], c, s, perm).astype(q_out.dtype)
        k_out[:, 0:HEAD] = kv_[:, 0:HEAD].astype(k_out.dtype)
        k_out[:, HEAD:MLA_QK] = _rope(tail_[:, 0:MLA_ROPE], c, s, perm).astype(k_out.dtype)

    hs = lambda w: pl.BlockSpec((None, t, w), lambda hh, m: (hh, m, 0))
    rs = lambda w: pl.BlockSpec((t, w), lambda hh, m: (m, 0))
    q_b, k_b = pl.pallas_call(
        lambda q_, kv_, tl_, c_, s_, qo, ko: mla_prep(q_[...], kv_[...], tl_[...], c_, s_, qo, ko),
        name=f"mix{l}_mla_prep", grid=(nmla, seq // t),
        in_specs=[hs(MLA_QK), hs(2 * HEAD), rs(TAIL_W), rs(MLA_ROPE), rs(MLA_ROPE)], out_specs=[hs(MLA_QK), hs(MLA_QK)],
        out_shape=[_sd((nmla, seq, MLA_QK), ACT)] * 2, compiler_params=_cparams(2),
    )(q_raw, kv, tail, cos_m, sin_m)
    mla_qkv = ((q_b, "heads", MLA_QK, 0), (k_b, "heads", MLA_QK, 0), (kv, "heads", HEAD, 1))
    out_b, lse_b = flash_fwd(*mla_qkv, None, n_heads=nmla, seq=seq, t=tf, scale=MLA_QK ** -0.5, name=f"mix{l}_mla")

    wd_ = ndil * HEAD

    def dil_prep(q_, k_, c_, s_):
        perm = _pad_perm(PARTIAL_ROPE)
        rot = lambda a: jnp.concatenate(
            [_rope(a[:, i * HEAD:(i + 1) * HEAD], c_, s_, perm) for i in range(ndil)], axis=1)
        return rot(q_), rot(k_)

    dq_r, dk_r = rowwise(dil_prep, [proj, proj, cos_p, sin_p],
                         [_rows(t, wd_, C_DQ), _rows(t, wd_, C_DK), _rows(t, HEAD), _rows(t, HEAD)],
                         [_sd((seq, wd_), ACT)] * 2, [_rows(t, wd_)] * 2, (seq // t,), f"mix{l}_dil_prep")
    dv = proj[:, C_DV * 512:(C_DV + 1) * 512]
    branches = [dil_fwd(dq_r, dk_r, dv, seq=seq, dil=dl, n_heads=ndil, name=f"mix{l}_dil{dl}") for dl in DIL_BRANCHES]

    def mix(o1, o2, o3, l1, l2, l3):
        m = jnp.maximum(jnp.maximum(l1, l2), l3)
        e1, e2, e3 = jnp.exp(l1 - m), jnp.exp(l2 - m), jnp.exp(l3 - m)
        return (e1 * o1 + e2 * o2 + e3 * o3) / (e1 + e2 + e3)

    out_c = rowwise(mix, [b[0] for b in branches] + [b[1] for b in branches], [_rows(t, wd_)] * 6, _sd((seq, wd_)),
                    _rows(t, wd_), (seq // t,), f"mix{l}_dil_mix")

    mixed = jnp.concatenate([out_a, out_b, out_c], axis=1)
    out = _dense(mixed, p["w_out"], 0, f"mix{l}_out", extras=[x], epilogue=lambda acc, x_: x_ + acc)
    saved = dict(x=x, h=h, proj=proj, tail=tail, z=z, bias_rows=bias_rows, cum2=cum2, out_a=out_a, lse_a=lse_a, cq=cq,
                 ckv=ckv, q_raw=q_raw, kv=kv, q_b=q_b, k_b=k_b, out_b=out_b, lse_b=lse_b, dq_r=dq_r, dk_r=dk_r, dv=dv,
                 branches=branches, out_c=out_c, mixed=mixed)
    return out, saved


def _pad_perm(n):
    i = lax.broadcasted_iota(jnp.int32, (HEAD, HEAD), 0)
    j = lax.broadcasted_iota(jnp.int32, (HEAD, HEAD), 1)
    inside = jnp.logical_and(i < n, j < n)
    return jnp.where(jnp.logical_and(inside, ((i + n // 2) % n) == j), 1.0, 0.0).astype(F32)


def mixer_bwd(dout, sv, p, l, consts, dep=None):
    seq, d = dout.shape
    nfox, nmla, ndil = 4, 8, 4
    cos_m, sin_m, cos_p, sin_p = consts
    t = _tile(seq, 512)
    tf = _tile(seq, 512)
    nb = seq // HEAD
    proj, tail = sv["proj"], sv["tail"]
    dmixed = _dense_dx(dout, p["w_out"], 0, f"mix{l}_dmixed", dep=dep)
    dw_out = _dense_dw(sv["mixed"], dout, d // NDEV, f"mix{l}_dw_out")
    do_a, do_b, do_c = dmixed[:, 0:512], dmixed[:, 512:1536], dmixed[:, 1536:2048]

    fox_qkv = ((proj, "cols", HEAD, C_FQ * 4), (proj, "cols", HEAD, C_FK * 4), (proj, "cols", HEAD, C_FV * 4))
    delta_a = attn_delta(do_a, sv["out_a"], n_heads=nfox, seq=seq, name=f"mix{l}_fox_delta")
    row = lambda a: a.reshape(a.shape[0], 1, seq)
    dfq, dcum_q = flash_bwd_dq(*fox_qkv, do_a, sv["lse_a"], delta_a, sv["cum2"], n_heads=nfox, seq=seq, t=tf,
                               scale=HEAD ** -0.5, name=f"mix{l}_fox_dq")
    dfk, dfv, dcum_k = flash_bwd_dkv(*fox_qkv, do_a, row(sv["lse_a"]), row(delta_a), sv["cum2"], n_heads=nfox, seq=seq,
                                     t=tf, scale=HEAD ** -0.5, name=f"mix{l}_fox_dkv")
    dz, dbias = fox_gate_bwd(sv["z"], sv["bias_rows"], dcum_q.reshape(nfox * nb, HEAD), dcum_k.reshape(nfox * nb, HEAD),
                             nb, f"mix{l}_dgate")
    d_fox_bias = dbias.reshape(nfox, nb, HEAD)[:, 0, 0]
    dfl = dz.reshape(nfox, seq).T
    unheads = lambda a: a.transpose(1, 0, 2).reshape(seq, -1)

    mla_qkv = ((sv["q_b"], "heads", MLA_QK, 0), (sv["k_b"], "heads", MLA_QK, 0), (sv["kv"], "heads", HEAD, 1))
    delta_b = attn_delta(do_b, sv["out_b"], n_heads=nmla, seq=seq, name=f"mix{l}_mla_delta")
    dq_b = flash_bwd_dq(*mla_qkv, do_b, sv["lse_b"], delta_b, None, n_heads=nmla, seq=seq, t=tf, scale=MLA_QK ** -0.5,
                        name=f"mix{l}_mla_dq")
    dk_b, dv_b = flash_bwd_dkv(*mla_qkv, do_b, row(sv["lse_b"]), row(delta_b), None, n_heads=nmla, seq=seq, t=tf,
                               scale=MLA_QK ** -0.5, name=f"mix{l}_mla_dkv")

    def mla_unprep(dq_, dk_, dv_, cos_, sin_, dq_out, dkv_out, dkr_out):
        perm = _swap_matrix(MLA_ROPE)
        c, s = cos_[...], sin_[...]
        dq_out[:, 0:HEAD] = dq_[:, 0:HEAD].astype(dq_out.dtype)
        dq_out[:, HEAD:MLA_QK] = _rope_t(dq_[:, HEAD:MLA_QK], c, s, perm).astype(dq_out.dtype)
        dkv_out[:, 0:HEAD] = dk_[:, 0:HEAD].astype(dkv_out.dtype)
        dkv_out[:, HEAD:2 * HEAD] = dv_.astype(dkv_out.dtype)
        dkr = _rope_t(dk_[:, HEAD:MLA_QK], c, s, perm)
        first = pl.program_id(1) == 0

        @pl.when(first)
        def _():
            dkr_out[...] = dkr

        @pl.when(jnp.logical_not(first))
        def _():
            dkr_out[...] += dkr

    hs = lambda w: pl.BlockSpec((None, t, w), lambda m, hh: (hh, m, 0))
    rs = lambda w: pl.BlockSpec((t, w), lambda m, hh: (m, 0))
    dq_raw, dkv, dk_r = pl.pallas_call(
        lambda a, b, c, cs, sn, o1, o2, o3: mla_unprep(a[...], b[...], c[...], cs, sn, o1, o2, o3),
        name=f"mix{l}_mla_unprep", grid=(seq // t, nmla),
        in_specs=[hs(MLA_QK), hs(MLA_QK), hs(HEAD), rs(MLA_ROPE), rs(MLA_ROPE)],
        out_specs=[hs(MLA_QK), hs(2 * HEAD), rs(MLA_ROPE)],
        out_shape=[_sd((nmla, seq, MLA_QK), ACT), _sd((nmla, seq, 2 * HEAD), ACT), _sd((seq, MLA_ROPE))],
        compiler_params=_cparams(2),
    )(dq_b, dk_b, dv_b, cos_m, sin_m)
    dcq_n = _heads_dx(dq_raw, p["mla_w_uq"], 0, f"mix{l}_dcq")
    dckv_n = _heads_dx(dkv, p["mla_w_ukv"], 0, f"mix{l}_dckv")
    dw_uq = _heads_dw(sv["cq"], dq_raw, f"mix{l}_dw_uq")
    dw_ukv = _heads_dw(sv["ckv"], dkv, f"mix{l}_dw_ukv")
    dcq, dg_q = rms_bwd(proj, p["mla_q_norm"], dcq_n, f"mix{l}_dcq_rms", col=C_CQ, width=512)
    dckv, dg_kv = rms_bwd(proj, p["mla_kv_norm"], dckv_n, f"mix{l}_dckv_rms", col=C_CKV, width=512)

    wd_ = ndil * HEAD
    outs = [b[0] for b in sv["branches"]]
    lses = [b[1] for b in sv["branches"]]

    def mix_bwd(do_, o1, o2, o3, l1, l2, l3):
        m = jnp.maximum(jnp.maximum(l1, l2), l3)
        e1, e2, e3 = jnp.exp(l1 - m), jnp.exp(l2 - m), jnp.exp(l3 - m)
        z_ = e1 + e2 + e3
        w1, w2, w3 = e1 / z_, e2 / z_, e3 / z_
        out = w1 * o1 + w2 * o2 + w3 * o3
        return (w1 * do_, w2 * do_, w3 * do_, do_ * w1 * (o1 - out), do_ * w2 * (o2 - out), do_ * w3 * (o3 - out))

    mb = rowwise(mix_bwd, [do_c] + outs + lses, [_rows(t, wd_)] * 7, [_sd((seq, wd_))] * 6, [_rows(t, wd_)] * 6,
                 (seq // t,), f"mix{l}_dil_dmix")
    grads = [dil_bwd(sv["dq_r"], sv["dk_r"], sv["dv"], outs[i], lses[i], mb[i], mb[3 + i], seq=seq, dil=dl,
                     n_heads=ndil, name=f"mix{l}_dil{dl}_bwd") for i, dl in enumerate(DIL_BRANCHES)]

    def dil_unprep(q1, q2, q3, k1, k2, k3, v1, v2, v3, c_, s_):
        perm = _pad_perm(PARTIAL_ROPE)
        rot_t = lambda a: jnp.concatenate(
            [_rope_t(a[:, i * HEAD:(i + 1) * HEAD], c_, s_, perm) for i in range(ndil)], axis=1)
        return rot_t(q1 + q2 + q3), rot_t(k1 + k2 + k3), v1 + v2 + v3

    ddq, ddk, ddv = rowwise(dil_unprep, [g[0] for g in grads] + [g[1] for g in grads] + [g[2] for g in grads] + [cos_p, sin_p],
                            [_rows(t, wd_)] * 9 + [_rows(t, HEAD)] * 2, [_sd((seq, wd_))] * 3, [_rows(t, wd_)] * 3,
                            (seq // t,), f"mix{l}_dil_unprep")

    dproj = jnp.concatenate([unheads(dfq), unheads(dfk), unheads(dfv), dcq, dckv, ddq, ddk, ddv], axis=1).astype(ACT)
    dtail = jnp.concatenate([dk_r, dfl, jnp.zeros((seq, TAIL_W - 68), F32)], axis=1)
    dh = _dense_dx(dproj, p["w_in_main"], 0, f"mix{l}_dh_main")
    dh = _dense_dx(dtail, p["w_in_tail"], 0, f"mix{l}_dh_tail", extras=[dh], epilogue=lambda acc, prev: acc + prev)
    dw_main = _dense_dw(sv["h"], dproj, d // NDEV, f"mix{l}_dw_in_main")
    dw_tail = _dense_dw(sv["h"], dtail, d // NDEV, f"mix{l}_dw_in_tail")
    dx, dg_mix = rms_bwd(sv["x"], p["mix_norm"], dh, f"mix{l}_drms", add=dout)
    return dx, dict(mix_norm=dg_mix, w_in_main=dw_main, w_in_tail=dw_tail, fox_forget_bias=d_fox_bias, mla_q_norm=dg_q,
                    mla_kv_norm=dg_kv, mla_w_uq=dw_uq, mla_w_ukv=dw_ukv, w_out=dw_out)


def loss_head(x, g, target, name):
    seq, d = x.shape
    t = _tile(seq, 512)

    def fn(x_, g_, tgt):
        err = _rms(x_, g_) - tgt
        part = 0.5 * jnp.sum(jnp.mean(err * err, axis=-1, keepdims=True), axis=0, keepdims=True)
        dx, dg = _rms_bwd(x_, g_, err / d)
        return jnp.broadcast_to(part, (1, HEAD)), dx, dg

    return rowwise(fn, [x, g.reshape(1, d), target], [_rows(t, d), _whole((1, d)), _rows(t, d)],
                   [_sd((1, HEAD)), _sd((seq, d)), _sd((1, d))], [_whole((1, HEAD)), _rows(t, d), _whole((1, d))],
                   (seq // t,), name, acc={0: (0,), 2: (0,)})


BIG = ("ffn1_w_gate", "ffn1_w_up", "ffn1_w_down", "w_in", "mla_w_uq", "mla_w_ukv", "w_out", "ffn2_w_gate", "ffn2_w_up",
       "ffn2_w_down")
GROUPS = {
    "ffn1": ("ffn1_w_gate", "ffn1_w_up", "ffn1_w_down"),
    "mix": ("w_in_main", "w_in_tail", "mla_w_uq", "mla_w_ukv", "w_out"),
    "ffn2": ("ffn2_w_gate", "ffn2_w_up", "ffn2_w_down"),
}
TRANSPOSED = ("ffn1_w_gate", "ffn1_w_up", "ffn2_w_gate", "ffn2_w_up")
PREFETCH = 2
SMALL_D = ("ffn1_norm", "mix_norm", "ffn2_norm")
WEIGHTS = ("ffn1_norm", "ffn1_w_gate", "ffn1_w_up", "ffn1_w_down", "mix_norm", "w_in", "fox_forget_bias", "mla_q_norm",
           "mla_kv_norm", "mla_w_uq", "mla_w_ukv", "w_out", "ffn2_norm", "ffn2_w_gate", "ffn2_w_up", "ffn2_w_down",
           "final_norm")


def pack_small(vals, depth, d):
    rows = [vals[n].reshape(depth, d) for n in SMALL_D]
    rows.append(vals["final_norm"].reshape(1, d))
    qk = jnp.concatenate([vals["mla_q_norm"].reshape(-1), vals["mla_kv_norm"].reshape(-1)])
    rows.append(jnp.pad(qk, (0, -qk.shape[0] % d)).reshape(-1, d))
    last = jnp.concatenate([vals["fox_forget_bias"].reshape(-1), vals["loss"].reshape(-1)])
    rows.append(jnp.pad(last, (0, d - last.shape[0])).reshape(1, d))
    out = jnp.concatenate(rows, axis=0)
    return jnp.pad(out, ((0, -out.shape[0] % 8), (0, 0)))


def unpack_small(a, depth, d, rank):
    out, r = {}, 0
    for n in SMALL_D:
        out[n] = a[r:r + depth]
        r += depth
    out["final_norm"] = a[r]
    r += 1
    n_qk = -(-2 * depth * rank // d)
    qk = a[r:r + n_qk].reshape(-1)[:2 * depth * rank].reshape(2, depth, rank)
    out["mla_q_norm"], out["mla_kv_norm"] = qk[0], qk[1]
    r += n_qk
    out["fox_forget_bias"] = a[r, :depth * 4].reshape(depth, 4)
    out["loss"] = a[r, depth * 4]
    return out


def step(x, target, w, m, v):
    depth = w["ffn1_norm"].shape[0]
    seq, d = x.shape[1], x.shape[2]
    rank = w["mla_q_norm"].shape[1]
    x = x.reshape(seq, d)
    target = target.reshape(seq, d)

    consts = (*rope_tables(seq, MLA_ROPE), *[jnp.pad(a, ((0, 0), (0, HEAD - PARTIAL_ROPE)), constant_values=c)
                                             for a, c in zip(rope_tables(seq, PARTIAL_ROPE), (1.0, 0.0))])
    order = [(l, k) for l in range(depth) for k in GROUPS]
    small_of = lambda l: {n: w[n][l] for n in ("mix_norm", "fox_forget_bias", "mla_q_norm", "mla_kv_norm")}

    view = lambda n, a: a.transpose(0, 2, 1) if n in TRANSPOSED else a

    def shards(l, k):
        if k == "mix":
            main, tail = split_w_in(w["w_in"][l:l + 1])
            xs = [main, tail, *[w[n][l:l + 1] for n in GROUPS[k][2:]]]
        else:
            xs = [view(n, w[n])[l:l + 1] for n in GROUPS[k]]
        return [a.astype(COMM) for a in xs]

    handles, fresh = {}, []

    def launch(i, dep):
        l, k = order[i]
        xs = shards(l, k)
        if dep is not None:
            xs = lax.optimization_barrier((xs, dep))[0]
        handles[i], token = exchange_start(xs, False, f"gather_start_{k}{l}")
        fresh.append(token)

    def take_tokens():
        tok = functools.reduce(jnp.add, fresh) if fresh else None
        fresh.clear()
        return tok

    launched = min(2, len(order))
    for i in range(launched):
        launch(i, None)
    gathered, saved = {}, {}
    for i, (l, k) in enumerate(order):
        wts = dict(zip(GROUPS[k], exchange_wait(handles[i], fresh[-1] if i == 0 else x, f"gather_wait_{k}{l}")))
        gathered[l, k] = wts
        while launched < min(len(order), i + 1 + PREFETCH):
            launch(launched, (x, wts[GROUPS[k][0]]))
            launched += 1
        tok = take_tokens()
        if k == "mix":
            x, saved[l, k] = mixer_fwd(x, {**wts, **small_of(l)}, l, consts, dep=tok)
        else:
            x, saved[l, k] = ffn_fwd(x, w[f"{k}_norm"][l], wts[f"{k}_w_gate"], wts[f"{k}_w_up"], wts[f"{k}_w_down"], 0,
                                     f"{k}_{l}", dep=tok)
    loss, dx, d_final = loss_head(x, w["final_norm"], target, "loss_head")

    small = {n: [None] * depth for n in SMALL_D + ("mla_q_norm", "mla_kv_norm", "fox_forget_bias")}
    pending, tok = [], None

    def send(l, k):
        def start(*grads):
            handle, token = exchange_start(list(grads), True, f"scatter_start_{k}{l}")
            pending.append((l, k, handle))
            return token
        return start

    for l, k in reversed(order):
        wts = gathered[l, k]
        if k == "mix":
            dx, gm = mixer_bwd(dx, saved[l, k], {**wts, **small_of(l)}, l, consts, dep=tok)
            tok = send(l, k)(*[gm[n] for n in GROUPS[k]])
        else:
            dx, dg = ffn_bwd(dx, saved[l, k], w[f"{k}_norm"][l], wts[f"{k}_w_gate"], wts[f"{k}_w_up"], wts[f"{k}_w_down"],
                             0, f"{k}_{l}", send(l, k), dep=tok)
            gm, tok = {f"{k}_norm": dg}, None
        for n in small:
            if n in gm:
                small[n][l] = gm[n]
    big = {n: [None] * depth for k in GROUPS for n in GROUPS[k]}

    def land(l, k, handle, after):
        for n, a in zip(GROUPS[k], exchange_wait(handle, after, f"scatter_wait_{k}{l}")):
            big[n][l] = a

    for l, k, handle in pending[:-1]:
        land(l, k, handle, dx)

    out = {}

    def update(name, parts, shape, dep=None):
        shape_v = (shape[0], shape[2], shape[1]) if name in TRANSPOSED else shape
        flat = lambda a: view(name, a).reshape(-1, shape_v[-1])
        res = adamw(parts, flat(w[name]), flat(m[name]), flat(v[name]), f"adamw_{name}", dep=dep)
        for kind, r in zip(("grad", "delta", "new_m", "new_v"), res):
            out[f"{kind}_{name}"] = view(name, r.reshape(shape_v))

    last = GROUPS[pending[-1][1]]
    for n in BIG:
        if n != "w_in" and n not in last:
            update(n, [a.reshape(NDEV, -1, a.shape[-1]) for a in big[n]], w[n].shape, dep=tok)
    g_in = [merge_w_in(sum_parts(big["w_in_main"][l].reshape(NDEV, -1, MAIN_W), f"sum_w_in_main{l}"),
                       sum_parts(big["w_in_tail"][l].reshape(NDEV, -1, TAIL_W), f"sum_w_in_tail{l}"))[None]
            for l in range(depth)]
    update("w_in", g_in, w["w_in"].shape, dep=tok)
    names = list(out)
    out.update(zip(names, lax.optimization_barrier([out[n] for n in names])))
    land(*pending[-1], out[names[0]])
    for n in last:
        update(n, [a.reshape(NDEV, -1, a.shape[-1]) for a in big[n]], w[n].shape)
    out["grad_x"] = dx.reshape(1, seq, d)

    part = {n: jnp.stack(g).reshape(depth, -1) for n, g in small.items()}
    part["final_norm"], part["loss"] = d_final, loss[0, 0:1]
    parts = exchange(pack_small(part, depth, d), False, "gather_small")
    zero = jnp.zeros((1,), F32)
    packed = [pack_small({**{n: a[n] for n in part if n != "loss"}, "loss": zero}, depth, d) for a in (w, m, v)]
    res = [unpack_small(r, depth, d, rank) for r in adamw([parts], *packed, "adamw_small")]
    out["loss"] = res[0]["loss"]
    for n in small.keys() | {"final_norm"}:
        for kind, r in zip(("grad", "delta", "new_m", "new_v"), res):
            out[f"{kind}_{n}"] = r[n].reshape(w[n].shape)
    return out


def kernel(x, ffn1_norm, ffn1_w_gate, ffn1_w_up, ffn1_w_down, mix_norm, w_in, fox_forget_bias, mla_q_norm, mla_kv_norm, mla_w_uq, mla_w_ukv, w_out, ffn2_norm, ffn2_w_gate, ffn2_w_up, ffn2_w_down, final_norm, loss_target, m_ffn1_norm, m_ffn1_w_gate, m_ffn1_w_up, m_ffn1_w_down, m_mix_norm, m_w_in, m_fox_forget_bias, m_mla_q_norm, m_mla_kv_norm, m_mla_w_uq, m_mla_w_ukv, m_w_out, m_ffn2_norm, m_ffn2_w_gate, m_ffn2_w_up, m_ffn2_w_down, m_final_norm, v_ffn1_norm, v_ffn1_w_gate, v_ffn1_w_up, v_ffn1_w_down, v_mix_norm, v_w_in, v_fox_forget_bias, v_mla_q_norm, v_mla_kv_norm, v_mla_w_uq, v_mla_w_ukv, v_w_out, v_ffn2_norm, v_ffn2_w_gate, v_ffn2_w_up, v_ffn2_w_down, v_final_norm):
    args = locals()
    w = {n: args[n] for n in WEIGHTS}
    m = {n: args["m_" + n] for n in WEIGHTS}
    v = {n: args["v_" + n] for n in WEIGHTS}
    out = step(x, loss_target, w, m, v)
    return (out["loss"], out["grad_x"], *[out["grad_" + n] for n in WEIGHTS], *[out["delta_" + n] for n in WEIGHTS],
            *[out["new_m_" + n] for n in WEIGHTS], *[out["new_v_" + n] for n in WEIGHTS])
```

```python
import functools

import jax
import jax.numpy as jnp
from jax import lax
from jax.experimental import pallas as pl
from jax.experimental.pallas import tpu as pltpu

F32 = jnp.float32
MXU = jnp.bfloat16
ACT = jnp.bfloat16
COMM = jnp.bfloat16
HI = lax.Precision.HIGHEST
NN = (((1,), (0,)), ((), ()))
NT = (((1,), (1,)), ((), ()))
TN = (((0,), (0,)), ((), ()))

NDEV = 8
HEAD = 128
EPS = 1e-6
ROPE_THETA = 500000.0
PARTIAL_ROPE = HEAD // 4
MLA_ROPE = 64
MLA_QK = HEAD + MLA_ROPE
DIL_BRANCHES = (1, 4, 16)
NEG = -1e30
VMEM_LIMIT = 48 * 1024 * 1024

ADAMW_STEP_BYTES = 12 * 1024 * 1024

ADAM_LR, ADAM_B1, ADAM_B2, ADAM_EPS, ADAM_WD, ADAM_STEP = 0.001, 0.9, 0.999, 1e-08, 0.01, 10


def _cparams(n_axes):
    return pltpu.CompilerParams(dimension_semantics=("arbitrary",) * n_axes, vmem_limit_bytes=VMEM_LIMIT)


def _tile(n, t):
    t = min(n, t)
    assert n % t == 0, (n, t)
    return t


def _dep_spec(dep):
    nd = dep.ndim
    return pl.BlockSpec(dep.shape, lambda *_: (0,) * nd)


def mm(a, b, *, name, dims, grid, a_spec, b_spec, o_spec, out_shape, nk, extras=(), extra_specs=(), epilogue=None, dep=None):
    n_ex = len(extras)
    kaxis = len(grid) - 1
    if dep is not None:
        extras, extra_specs = [*extras, dep], [*extra_specs, _dep_spec(dep)]
    n_more = len(extras)
    n_out = len(out_shape) if isinstance(out_shape, (list, tuple)) else 1

    def body(a_ref, b_ref, *rest):
        ex, o_refs = rest[:n_ex], rest[n_more:n_more + n_out]

        def finish(acc):
            res = acc if epilogue is None else epilogue(acc, *[e[...] for e in ex])
            for o_ref, r in zip(o_refs, res if n_out > 1 else (res,)):
                o_ref[...] = r.astype(o_ref.dtype)

        part = lax.dot_general(a_ref[...].astype(MXU), b_ref[...].astype(MXU), dims, preferred_element_type=F32)
        if nk == 1:
            finish(part)
        else:
            acc_ref = rest[n_more + n_out]
            k = pl.program_id(kaxis)

            @pl.when(k == 0)
            def _():
                acc_ref[...] = part

            @pl.when(k > 0)
            def _():
                acc_ref[...] += part

            @pl.when(k == nk - 1)
            def _():
                finish(acc_ref[...])

    acc_shape = tuple(d for d in o_spec.block_shape if d is not None)
    return pl.pallas_call(
        body, name=name, grid=grid, in_specs=[a_spec, b_spec, *extra_specs],
        out_specs=[o_spec] * n_out if n_out > 1 else o_spec, out_shape=out_shape,
        scratch_shapes=[] if nk == 1 else [pltpu.VMEM(acc_shape, F32)], compiler_params=_cparams(len(grid)),
    )(a, b, *extras)


def rowwise(fn, ins, in_specs, outs, out_specs, grid, name, acc=None, dep=None):
    acc = acc or {}
    n_in = len(ins)
    if dep is not None:
        ins, in_specs = [*ins, dep], [*in_specs, _dep_spec(dep)]
    n_all = len(ins)

    def body(*refs):
        vals = fn(*[r[...] for r in refs[:n_in]])
        if not isinstance(vals, (tuple, list)):
            vals = (vals,)
        for i, (r, v) in enumerate(zip(refs[n_all:], vals)):
            if i in acc:
                first = functools.reduce(jnp.logical_and, [pl.program_id(ax) == 0 for ax in acc[i]])

                @pl.when(first)
                def _(r=r, v=v):
                    r[...] = v.astype(r.dtype)

                @pl.when(jnp.logical_not(first))
                def _(r=r, v=v):
                    r[...] += v.astype(r.dtype)
            else:
                r[...] = v.astype(r.dtype)

    return pl.pallas_call(
        body, name=name, grid=grid, in_specs=in_specs, out_specs=out_specs, out_shape=outs,
        compiler_params=_cparams(len(grid)),
    )(*ins)


def _rows(t, c, col=0):
    return pl.BlockSpec((t, c), lambda m, col=col: (m, col))


def _whole(shape):
    nd = len(shape)
    return pl.BlockSpec(shape, lambda *_: (0,) * nd)


def _rms(x, g):
    x = x.astype(F32)
    return x * lax.rsqrt(jnp.mean(x * x, axis=-1, keepdims=True) + EPS) * g


def _rms_bwd(x, g, dy):
    x = x.astype(F32)
    dy = dy.astype(F32)
    r = lax.rsqrt(jnp.mean(x * x, axis=-1, keepdims=True) + EPS)
    xh = x * r
    dg = jnp.sum(dy * xh, axis=0, keepdims=True)
    dxh = dy * g
    dx = r * (dxh - xh * jnp.mean(dxh * xh, axis=-1, keepdims=True))
    return dx, dg


def _swap_matrix(n):
    i = lax.broadcasted_iota(jnp.int32, (n, n), 0)
    j = lax.broadcasted_iota(jnp.int32, (n, n), 1)
    return jnp.where(((i + n // 2) % n) == j, 1.0, 0.0).astype(F32)


def _rope(x, cos, sin_signed, perm):
    return x * cos + jnp.dot(x, perm, precision=HI, preferred_element_type=F32) * sin_signed


def _rope_t(dy, cos, sin_signed, perm):
    return dy * cos + jnp.dot(dy * sin_signed, perm, precision=HI, preferred_element_type=F32)


def rope_tables(seq, dim):
    inv = 1.0 / (ROPE_THETA ** (jnp.arange(0, dim, 2, dtype=F32) / dim))
    ang = jnp.arange(seq, dtype=F32)[:, None] * inv[None, :]
    cos, sin = jnp.cos(ang), jnp.sin(ang)
    return jnp.concatenate([cos, cos], axis=1), jnp.concatenate([-sin, sin], axis=1)


HP = 2


def _triangle(nb, by_key):
    pairs = [(i, j) for j in range(nb) for i in range(j, nb)] if by_key else [(i, j) for i in range(nb) for j in range(i + 1)]
    return jnp.asarray([p[0] for p in pairs], jnp.int32), jnp.asarray([p[1] for p in pairs], jnp.int32)


_Q_BLOCK = lambda n, ii, jj: ii[n]
_K_BLOCK = lambda n, ii, jj: jj[n]


def _hspec(arr_kind, t, w, off, seq_of):
    if arr_kind == "cols":
        assert off % HP == 0
        return pl.BlockSpec((t, HP * w), lambda h, n, ii, jj: (seq_of(n, ii, jj), off // HP + h))
    return pl.BlockSpec((HP, t, w), lambda h, n, ii, jj: (h, seq_of(n, ii, jj), off))


def _head(ref, arr_kind, hh, w):
    return ref[:, hh * w:(hh + 1) * w] if arr_kind == "cols" else ref[hh]


def _colspec(t, seq_of):
    return pl.BlockSpec((HP, t, 1), lambda h, n, ii, jj: (h, seq_of(n, ii, jj), 0))


def _rowspec(t, seq_of):
    return pl.BlockSpec((HP, 1, t), lambda h, n, ii, jj: (h, 0, seq_of(n, ii, jj)))


def _causal(s, qi, kj, t, transposed=False):
    a = lax.broadcasted_iota(jnp.int32, (t, t), 0)
    b = lax.broadcasted_iota(jnp.int32, (t, t), 1)
    keep = (kj * t + a <= qi * t + b) if transposed else (kj * t + b <= qi * t + a)
    return jnp.where(keep, s, NEG)


def flash_fwd(q, k, v, cum, *, n_heads, seq, t, scale, name):
    nb = seq // t
    qs, ks = _Q_BLOCK, _K_BLOCK
    steps = _triangle(nb, by_key=False)
    ins = [q[0], k[0], v[0]]
    specs = [_hspec(q[1], t, q[2], q[3], qs), _hspec(k[1], t, k[2], k[3], ks), _hspec(v[1], t, v[2], v[3], ks)]
    if cum is not None:
        ins += [cum[0], cum[1]]
        specs += [_colspec(t, qs), _rowspec(t, ks)]

    def body(ii_ref, jj_ref, *refs):
        q_ref, k_ref, v_ref = refs[:3]
        o_ref, lse_ref, m_s, l_s, acc_s = refs[-5:]
        i, j = ii_ref[pl.program_id(1)], jj_ref[pl.program_id(1)]

        @pl.when(j == 0)
        def _():
            m_s[...] = jnp.full(m_s.shape, NEG, F32)
            l_s[...] = jnp.zeros(l_s.shape, F32)
            acc_s[...] = jnp.zeros(acc_s.shape, F32)

        def block(masked):
            new = []
            for hh in range(HP):
                qb = _head(q_ref, q[1], hh, q[2]).astype(MXU)
                kb = _head(k_ref, k[1], hh, k[2]).astype(MXU)
                s = lax.dot_general(qb, kb, NT, preferred_element_type=F32) * scale
                if cum is not None:
                    s = s + (refs[3][hh] - refs[4][hh])
                if masked:
                    s = _causal(s, i, j, t)
                m_old = m_s[hh]
                m_new = jnp.maximum(m_old, jnp.max(s, axis=1, keepdims=True))
                alpha = jnp.exp(m_old - m_new)
                p = jnp.exp(s - m_new)
                vb = _head(v_ref, v[1], hh, v[2]).astype(MXU)
                pv = jnp.dot(p.astype(MXU), vb, preferred_element_type=F32)
                new.append((m_new, alpha, alpha * l_s[hh] + jnp.sum(p, axis=1, keepdims=True), pv))
            for hh, (m_new, alpha, l_new, pv) in enumerate(new):
                acc_s[hh] = alpha * acc_s[hh] + pv
                l_s[hh] = l_new
                m_s[hh] = m_new

        @pl.when(j < i)
        def _():
            block(False)

        @pl.when(j == i)
        def _():
            block(True)

        @pl.when(j == i)
        def _():
            for hh in range(HP):
                o_ref[:, hh * HEAD:(hh + 1) * HEAD] = (acc_s[hh] / l_s[hh]).astype(o_ref.dtype)
                lse_ref[hh] = m_s[hh] + jnp.log(l_s[hh])

    return pl.pallas_call(
        body, name=name,
        grid_spec=pltpu.PrefetchScalarGridSpec(
            num_scalar_prefetch=2, grid=(n_heads // HP, steps[0].shape[0]), in_specs=specs,
            out_specs=[pl.BlockSpec((t, HP * HEAD), lambda h, n, ii, jj: (ii[n], h)), _colspec(t, qs)],
            scratch_shapes=[pltpu.VMEM((HP, t, 1), F32), pltpu.VMEM((HP, t, 1), F32), pltpu.VMEM((HP, t, HEAD), F32)]),
        out_shape=[jax.ShapeDtypeStruct((seq, n_heads * HEAD), F32), jax.ShapeDtypeStruct((n_heads, seq, 1), F32)],
        compiler_params=_cparams(2),
    )(*steps, *ins)


def flash_bwd_dq(q, k, v, do, lse, delta, cum, *, n_heads, seq, t, scale, name):
    nb = seq // t
    qs, ks = _Q_BLOCK, _K_BLOCK
    steps = _triangle(nb, by_key=False)
    ins = [q[0], k[0], v[0], do, lse, delta]
    specs = [_hspec(q[1], t, q[2], q[3], qs), _hspec(k[1], t, k[2], k[3], ks), _hspec(v[1], t, v[2], v[3], ks),
             _hspec("cols", t, HEAD, 0, qs), _colspec(t, qs), _colspec(t, qs)]
    if cum is not None:
        ins += [cum[0], cum[1]]
        specs += [_colspec(t, qs), _rowspec(t, ks)]
    wq = q[2]
    n_out = 1 if cum is None else 2

    def body(ii_ref, jj_ref, *refs):
        q_ref, k_ref, v_ref, do_ref, lse_ref, dl_ref = refs[:6]
        outs = refs[-2 * n_out:-n_out]
        accs = refs[-n_out:]
        i, j = ii_ref[pl.program_id(1)], jj_ref[pl.program_id(1)]

        @pl.when(j == 0)
        def _():
            for a in accs:
                a[...] = jnp.zeros(a.shape, F32)

        def block(masked):
            for hh in range(HP):
                kb = _head(k_ref, k[1], hh, k[2]).astype(MXU)
                s = lax.dot_general(_head(q_ref, q[1], hh, q[2]).astype(MXU), kb, NT, preferred_element_type=F32) * scale
                if cum is not None:
                    s = s + (refs[6][hh] - refs[7][hh])
                if masked:
                    s = _causal(s, i, j, t)
                p = jnp.exp(s - lse_ref[hh])
                dp = lax.dot_general(_head(do_ref, "cols", hh, HEAD).astype(MXU), _head(v_ref, v[1], hh, v[2]).astype(MXU),
                                     NT, preferred_element_type=F32)
                ds = p * (dp - dl_ref[hh])
                accs[0][hh] += jnp.dot(ds.astype(MXU), kb, preferred_element_type=F32)
                if cum is not None:
                    accs[1][hh] += jnp.sum(ds, axis=1, keepdims=True)

        @pl.when(j < i)
        def _():
            block(False)

        @pl.when(j == i)
        def _():
            block(True)

        @pl.when(j == i)
        def _():
            outs[0][...] = accs[0][...] * scale
            if cum is not None:
                outs[1][...] = accs[1][...]

    out_specs = [pl.BlockSpec((HP, t, wq), lambda h, n, ii, jj: (h, ii[n], 0))]
    out_shape = [jax.ShapeDtypeStruct((n_heads, seq, wq), F32)]
    scratch = [pltpu.VMEM((HP, t, wq), F32)]
    if cum is not None:
        out_specs.append(_colspec(t, qs))
        out_shape.append(jax.ShapeDtypeStruct((n_heads, seq, 1), F32))
        scratch.append(pltpu.VMEM((HP, t, 1), F32))
    res = pl.pallas_call(
        body, name=name,
        grid_spec=pltpu.PrefetchScalarGridSpec(num_scalar_prefetch=2, grid=(n_heads // HP, steps[0].shape[0]),
                                               in_specs=specs, out_specs=out_specs, scratch_shapes=scratch),
        out_shape=out_shape, compiler_params=_cparams(2),
    )(*steps, *ins)
    return res[0] if cum is None else res


def flash_bwd_dkv(q, k, v, do, lse_row, delta_row, cum, *, n_heads, seq, t, scale, name):
    nb = seq // t
    qs, ks = _Q_BLOCK, _K_BLOCK
    steps = _triangle(nb, by_key=True)
    ins = [q[0], k[0], v[0], do, lse_row, delta_row]
    specs = [_hspec(q[1], t, q[2], q[3], qs), _hspec(k[1], t, k[2], k[3], ks), _hspec(v[1], t, v[2], v[3], ks),
             _hspec("cols", t, HEAD, 0, qs), _rowspec(t, qs), _rowspec(t, qs)]
    if cum is not None:
        ins += [cum[0], cum[1]]
        specs += [_colspec(t, ks), _rowspec(t, qs)]
    wk = k[2]
    n_out = 2 if cum is None else 3

    def body(ii_ref, jj_ref, *refs):
        q_ref, k_ref, v_ref, do_ref, lse_ref, dl_ref = refs[:6]
        outs = refs[-2 * n_out:-n_out]
        accs = refs[-n_out:]
        i, j = ii_ref[pl.program_id(1)], jj_ref[pl.program_id(1)]

        @pl.when(i == j)
        def _():
            for a in accs:
                a[...] = jnp.zeros(a.shape, F32)

        def block(masked):
            for hh in range(HP):
                qb = _head(q_ref, q[1], hh, q[2]).astype(MXU)
                dob = _head(do_ref, "cols", hh, HEAD).astype(MXU)
                st = lax.dot_general(_head(k_ref, k[1], hh, k[2]).astype(MXU), qb, NT, preferred_element_type=F32) * scale
                if cum is not None:
                    st = st + (refs[7][hh] - refs[6][hh])
                if masked:
                    st = _causal(st, i, j, t, transposed=True)
                pt = jnp.exp(st - lse_ref[hh])
                dpt = lax.dot_general(_head(v_ref, v[1], hh, v[2]).astype(MXU), dob, NT, preferred_element_type=F32)
                dst = pt * (dpt - dl_ref[hh])
                accs[0][hh] += jnp.dot(dst.astype(MXU), qb, preferred_element_type=F32)
                accs[1][hh] += jnp.dot(pt.astype(MXU), dob, preferred_element_type=F32)
                if cum is not None:
                    accs[2][hh] -= jnp.sum(dst, axis=1, keepdims=True)

        @pl.when(i > j)
        def _():
            block(False)

        @pl.when(i == j)
        def _():
            block(True)

        @pl.when(i == nb - 1)
        def _():
            outs[0][...] = accs[0][...] * scale
            for o, a in zip(outs[1:], accs[1:]):
                o[...] = a[...]

    out_specs = [pl.BlockSpec((HP, t, wk), lambda h, n, ii, jj: (h, jj[n], 0)),
                 pl.BlockSpec((HP, t, HEAD), lambda h, n, ii, jj: (h, jj[n], 0))]
    out_shape = [jax.ShapeDtypeStruct((n_heads, seq, wk), F32), jax.ShapeDtypeStruct((n_heads, seq, HEAD), F32)]
    scratch = [pltpu.VMEM((HP, t, wk), F32), pltpu.VMEM((HP, t, HEAD), F32)]
    if cum is not None:
        out_specs.append(_colspec(t, ks))
        out_shape.append(jax.ShapeDtypeStruct((n_heads, seq, 1), F32))
        scratch.append(pltpu.VMEM((HP, t, 1), F32))
    return pl.pallas_call(
        body, name=name,
        grid_spec=pltpu.PrefetchScalarGridSpec(num_scalar_prefetch=2, grid=(n_heads // HP, steps[0].shape[0]),
                                               in_specs=specs, out_specs=out_specs, scratch_shapes=scratch),
        out_shape=out_shape, compiler_params=_cparams(2),
    )(*steps, *ins)


def attn_delta(do, o, *, n_heads, seq, name):
    t = _tile(seq, 512)
    spec = pl.BlockSpec((t, HEAD), lambda h, m: (m, h))
    return rowwise(
        lambda a, b: jnp.sum(a.astype(F32) * b.astype(F32), axis=1, keepdims=True), [do, o], [spec, spec],
        jax.ShapeDtypeStruct((n_heads, seq, 1), F32), pl.BlockSpec((None, t, 1), lambda h, m: (h, m, 0)),
        (n_heads, seq // t), name)


def _dil_scores(q, kc, kp, n, scale):
    i = lax.broadcasted_iota(jnp.int32, (HEAD, HEAD), 0)
    j = lax.broadcasted_iota(jnp.int32, (HEAD, HEAD), 1)
    sc = lax.dot_general(q, kc, NT, preferred_element_type=F32) * scale
    sp = lax.dot_general(q, kp, NT, preferred_element_type=F32) * scale
    sc = jnp.where(j <= i, sc, NEG)
    sp = jnp.where(jnp.logical_and(j >= i, n > 0), sp, NEG)
    return sc, sp


def _strip_spec(length, n_heads, col_blocks, off):
    return pl.BlockSpec((length, HEAD), lambda r, h: (0, r * col_blocks + off + h))


def dil_fwd(q, k, v, *, seq, dil, n_heads, name):
    length = seq // dil
    nb = length // HEAD
    scale = HEAD ** -0.5
    view = lambda a: a.reshape(length, dil * a.shape[1])
    spec = _strip_spec(length, n_heads, n_heads, 0)

    def body(q_ref, k_ref, v_ref, o_ref, lse_ref):
        def step(n, carry):
            cur = pl.ds(pl.multiple_of(n * HEAD, HEAD), HEAD)
            prev = pl.ds(pl.multiple_of(jnp.maximum(n - 1, 0) * HEAD, HEAD), HEAD)
            qb = q_ref[cur, :].astype(MXU)
            sc, sp = _dil_scores(qb, k_ref[cur, :].astype(MXU), k_ref[prev, :].astype(MXU), n, scale)
            m = jnp.maximum(jnp.max(sc, axis=1, keepdims=True), jnp.max(sp, axis=1, keepdims=True))
            ec, ep = jnp.exp(sc - m), jnp.exp(sp - m)
            l = jnp.sum(ec, axis=1, keepdims=True) + jnp.sum(ep, axis=1, keepdims=True)
            o = jnp.dot((ec / l).astype(MXU), v_ref[cur, :].astype(MXU), preferred_element_type=F32)
            o = o + jnp.dot((ep / l).astype(MXU), v_ref[prev, :].astype(MXU), preferred_element_type=F32)
            o_ref[cur, :] = o
            lse_ref[cur, :] = jnp.broadcast_to(m + jnp.log(l), (HEAD, HEAD))
            return carry

        lax.fori_loop(0, nb, step, 0, unroll=min(8, nb))

    out = jax.ShapeDtypeStruct((length, dil * n_heads * HEAD), F32)
    o, lse = pl.pallas_call(
        body, name=name, grid=(dil, n_heads), in_specs=[spec, spec, spec], out_specs=[spec, spec], out_shape=[out, out],
        compiler_params=_cparams(2),
    )(view(q), view(k), view(v))
    return o.reshape(seq, -1), lse.reshape(seq, -1)


def dil_bwd(q, k, v, o, lse, do, dlse, *, seq, dil, n_heads, name):
    length = seq // dil
    nb = length // HEAD
    scale = HEAD ** -0.5
    view = lambda a: a.reshape(length, dil * a.shape[1])
    spec = _strip_spec(length, n_heads, n_heads, 0)

    def body(q_ref, k_ref, v_ref, o_ref, lse_ref, do_ref, dlse_ref, dq_ref, dk_ref, dv_ref):
        dk_ref[...] = jnp.zeros(dk_ref.shape, F32)
        dv_ref[...] = jnp.zeros(dv_ref.shape, F32)

        def step(n, carry):
            cur = pl.ds(pl.multiple_of(n * HEAD, HEAD), HEAD)
            prev = pl.ds(pl.multiple_of(jnp.maximum(n - 1, 0) * HEAD, HEAD), HEAD)
            qb = q_ref[cur, :].astype(MXU)
            kc, kp = k_ref[cur, :].astype(MXU), k_ref[prev, :].astype(MXU)
            vc, vp = v_ref[cur, :].astype(MXU), v_ref[prev, :].astype(MXU)
            sc, sp = _dil_scores(qb, kc, kp, n, scale)
            lse_b = jnp.max(lse_ref[cur, :], axis=1, keepdims=True)
            pc, pp = jnp.exp(sc - lse_b), jnp.exp(sp - lse_b)
            dob = do_ref[cur, :]
            shift = jnp.sum(dlse_ref[cur, :], axis=1, keepdims=True) - jnp.sum(dob * o_ref[cur, :], axis=1, keepdims=True)
            dob = dob.astype(MXU)
            dsc = pc * (lax.dot_general(dob, vc, NT, preferred_element_type=F32) + shift)
            dsp = pp * (lax.dot_general(dob, vp, NT, preferred_element_type=F32) + shift)
            dscb, dspb = dsc.astype(MXU), dsp.astype(MXU)
            dq = jnp.dot(dscb, kc, preferred_element_type=F32) + jnp.dot(dspb, kp, preferred_element_type=F32)
            dq_ref[cur, :] = dq * scale
            dk_ref[cur, :] += lax.dot_general(dscb, qb, TN, preferred_element_type=F32) * scale
            dv_ref[cur, :] += lax.dot_general(pc.astype(MXU), dob, TN, preferred_element_type=F32)
            dk_ref[prev, :] += lax.dot_general(dspb, qb, TN, preferred_element_type=F32) * scale
            dv_ref[prev, :] += lax.dot_general(pp.astype(MXU), dob, TN, preferred_element_type=F32)
            return carry

        lax.fori_loop(0, nb, step, 0, unroll=min(8, nb))

    out = jax.ShapeDtypeStruct((length, dil * n_heads * HEAD), F32)
    res = pl.pallas_call(
        body, name=name, grid=(dil, n_heads), in_specs=[spec] * 7, out_specs=[spec] * 3, out_shape=[out] * 3,
        compiler_params=_cparams(2),
    )(*[view(a) for a in (q, k, v, o, lse, do, dlse)])
    return [r.reshape(seq, -1) for r in res]


def _tri(n, kind):
    i = lax.broadcasted_iota(jnp.int32, (n, n), 0)
    j = lax.broadcasted_iota(jnp.int32, (n, n), 1)
    return jnp.where({"le": i <= j, "ge": i >= j}[kind], 1.0, 0.0).astype(F32)


def _block_matrix(n_rows, per_head, kind):
    r = lax.broadcasted_iota(jnp.int32, (n_rows, n_rows), 0)
    c = lax.broadcasted_iota(jnp.int32, (n_rows, n_rows), 1)
    same = (r // per_head) == (c // per_head)
    rel = {"lt": c < r, "gt": c > r, "all": c == c}[kind]
    return jnp.where(jnp.logical_and(same, rel), 1.0, 0.0).astype(F32)


def _lane_pick(x, lane):
    j = lax.broadcasted_iota(jnp.int32, x.shape, 1)
    return jnp.sum(jnp.where(j == lane, x, 0.0), axis=1, keepdims=True)


def _log_sigmoid(z):
    return jnp.minimum(z, 0.0) - jnp.log1p(jnp.exp(-jnp.abs(z)))


def fox_gate_fwd(z, bias_rows, per_head, name):
    n_rows = z.shape[0]

    def body(z_ref, b_ref, c_ref):
        logf = _log_sigmoid(z_ref[...] + b_ref[...])
        within = jnp.dot(logf, _tri(HEAD, "le"), precision=HI, preferred_element_type=F32)
        tot = jnp.broadcast_to(_lane_pick(within, HEAD - 1), (n_rows, HEAD))
        c_ref[...] = within + jnp.dot(_block_matrix(n_rows, per_head, "lt"), tot, precision=HI, preferred_element_type=F32)

    return pl.pallas_call(body, name=name, out_shape=jax.ShapeDtypeStruct(z.shape, F32),
                          compiler_params=pltpu.CompilerParams(vmem_limit_bytes=VMEM_LIMIT))(z, bias_rows)


def fox_gate_bwd(z, bias_rows, dcum_q, dcum_k, per_head, name):
    n_rows = z.shape[0]

    def body(z_ref, b_ref, dcq_ref, dck_ref, dz_ref, db_ref):
        within = jnp.dot(dcq_ref[...] + dck_ref[...], _tri(HEAD, "ge"), precision=HI, preferred_element_type=F32)
        tot = jnp.broadcast_to(_lane_pick(within, 0), (n_rows, HEAD))
        dlogf = within + jnp.dot(_block_matrix(n_rows, per_head, "gt"), tot, precision=HI, preferred_element_type=F32)
        dz = dlogf * jax.nn.sigmoid(-(z_ref[...] + b_ref[...]))
        dz_ref[...] = dz
        rs = jnp.broadcast_to(jnp.sum(dz, axis=1, keepdims=True), (n_rows, HEAD))
        db_ref[...] = jnp.dot(_block_matrix(n_rows, per_head, "all"), rs, precision=HI, preferred_element_type=F32)

    shp = jax.ShapeDtypeStruct(z.shape, F32)
    return pl.pallas_call(body, name=name, out_shape=[shp, shp],
                          compiler_params=pltpu.CompilerParams(vmem_limit_bytes=VMEM_LIMIT))(z, bias_rows, dcum_q, dcum_k)


def exchange(x, scatter, name):
    blk = x.shape[1:] if scatter else x.shape

    def body(x_ref, o_ref, send_sems, recv_sems, local_sem):
        mx, my, mc = lax.axis_index("x"), lax.axis_index("y"), lax.axis_index("c")
        me = 4 * mx + 2 * my + mc
        flip = lambda v, f: 1 - v if f else v
        local = pltpu.make_async_copy(x_ref.at[me] if scatter else x_ref, o_ref.at[me], local_sem)
        local.start()
        sends, recvs = [], []
        for n in range(1, NDEV):
            px, py, pc = flip(mx, n & 4), flip(my, n & 2), flip(mc, n & 1)
            p = 4 * px + 2 * py + pc
            sends.append(pltpu.make_async_remote_copy(
                src_ref=x_ref.at[p] if scatter else x_ref, dst_ref=o_ref.at[me], send_sem=send_sems.at[n - 1],
                recv_sem=recv_sems.at[n - 1], device_id=(px, py, pc), device_id_type=pl.DeviceIdType.MESH))
            recvs.append(pltpu.make_async_remote_copy(
                src_ref=x_ref.at[me] if scatter else x_ref, dst_ref=o_ref.at[p], send_sem=send_sems.at[n - 1],
                recv_sem=recv_sems.at[n - 1], device_id=(px, py, pc), device_id_type=pl.DeviceIdType.MESH))
        for cp in sends:
            cp.start()
        for cp in recvs:
            cp.wait_recv()
        for cp in sends:
            cp.wait_send()
        local.wait()

    hbm = pl.BlockSpec(memory_space=pltpu.HBM)
    return pl.pallas_call(
        body, name=name, in_specs=[hbm], out_specs=hbm, out_shape=jax.ShapeDtypeStruct((NDEV, *blk), x.dtype),
        scratch_shapes=[pltpu.SemaphoreType.DMA((NDEV - 1,)), pltpu.SemaphoreType.DMA((NDEV - 1,)), pltpu.SemaphoreType.DMA],
    )(x)


def _exchange_copies(x_refs, land_refs, send_sems, recv_sems, local_sems, scatter, with_recvs):
    mx, my, mc = lax.axis_index("x"), lax.axis_index("y"), lax.axis_index("c")
    me = 4 * mx + 2 * my + mc
    flip = lambda v, f: 1 - v if f else v
    local, sends, recvs = [], [], []
    for a, (x_ref, o_ref) in enumerate(zip(x_refs, land_refs)):
        local.append(pltpu.make_async_copy(x_ref.at[me] if scatter else x_ref, o_ref.at[me], local_sems.at[a]))
        for n in range(1, NDEV):
            px, py, pc = flip(mx, n & 4), flip(my, n & 2), flip(mc, n & 1)
            p = 4 * px + 2 * py + pc
            sem = (NDEV - 1) * a + n - 1
            mk = lambda src, dst: pltpu.make_async_remote_copy(
                src_ref=src, dst_ref=dst, send_sem=send_sems.at[sem], recv_sem=recv_sems.at[sem],
                device_id=(px, py, pc), device_id_type=pl.DeviceIdType.MESH)
            sends.append(mk(x_ref.at[p] if scatter else x_ref, o_ref.at[me]))
            if with_recvs:
                recvs.append(mk(x_ref.at[me] if scatter else x_ref, o_ref.at[p]))
    return local, sends, recvs


_HBM = pl.BlockSpec(memory_space=pltpu.HBM)
_SEM = pl.BlockSpec(memory_space=pltpu.SEMAPHORE)
_EFFECT = pltpu.SideEffectType.DATAFLOW_SIDE_EFFECTING


def exchange_start(xs, scatter, name):
    n = len(xs)
    lands = [jax.ShapeDtypeStruct((NDEV, *(x.shape[1:] if scatter else x.shape)), x.dtype) for x in xs]

    def body(*refs):
        x_refs, land_refs = refs[:n], refs[n:2 * n]
        send_sems, recv_sems, local_sems = refs[2 * n:2 * n + 3]
        token = refs[-1]
        local, sends, _ = _exchange_copies(x_refs, land_refs, send_sems, recv_sems, local_sems, scatter, False)
        for cp in local + sends:
            cp.start()
        token[...] = jnp.zeros(token.shape, token.dtype)

    n_sem = (NDEV - 1) * n
    out = pl.pallas_call(
        body, name=name,
        out_shape=(pltpu.SemaphoreType.DMA((n_sem,)), pltpu.SemaphoreType.DMA((n_sem,)), pltpu.SemaphoreType.DMA((n,)),
                   *[pltpu.HBM(x.shape, x.dtype) for x in xs], *[pltpu.HBM(s.shape, s.dtype) for s in lands],
                   jax.ShapeDtypeStruct((8, HEAD), F32)),
        in_specs=[_HBM] * (2 * n), out_specs=(_SEM, _SEM, _SEM, *[_HBM] * (2 * n), pl.BlockSpec(memory_space=pltpu.VMEM)),
        input_output_aliases={i: 3 + i for i in range(2 * n)},
        compiler_params=pltpu.CompilerParams(has_side_effects=_EFFECT),
    )(*[pltpu.with_memory_space_constraint(x, pltpu.HBM) for x in xs],
      *[pltpu.with_memory_space_constraint(lax.empty(s.shape, s.dtype), pltpu.HBM) for s in lands])
    return (out[:3], out[3:3 + n], out[3 + n:3 + 2 * n], scatter), out[-1]


def exchange_wait(handle, after, name):
    sems, xs, lands, scatter = handle
    n = len(xs)

    def body(*refs):
        x_refs, land_refs = refs[:n], refs[n:2 * n]
        send_sems, recv_sems, local_sems = refs[2 * n:2 * n + 3]
        local, sends, recvs = _exchange_copies(x_refs, land_refs, send_sems, recv_sems, local_sems, scatter, True)
        for cp in sends:
            cp.wait_send()
        for cp in recvs:
            cp.wait_recv()
        for cp in local:
            cp.wait()

    out = pl.pallas_call(
        body, name=name, out_shape=tuple(pltpu.HBM(a.shape, a.dtype) for a in (*xs, *lands)),
        in_specs=[_HBM] * (2 * n) + [_SEM] * 3 + [pl.BlockSpec(memory_space=pl.ANY)], out_specs=tuple([_HBM] * (2 * n)),
        input_output_aliases={i: i for i in range(2 * n)}, compiler_params=pltpu.CompilerParams(has_side_effects=_EFFECT),
    )(*xs, *lands, *sems, after)
    return list(out[n:])


def adamw(parts, w, m, v, name, dep=None):
    depth = len(parts)
    deps = [] if dep is None else [dep]
    n_parts, rows, cols = parts[0].shape
    t = rows
    for cand in (256, 128, 64, 32, 16, 8):
        if rows % cand == 0 and (n_parts * parts[0].dtype.itemsize + 7 * 4) * cand * cols <= ADAMW_STEP_BYTES:
            t = cand
            break
    nr = rows // t

    def body(*refs):
        p_refs = refs[:depth]
        w_ref, m_ref, v_ref = refs[depth:depth + 3]
        g_out, d_out, m_out, v_out = refs[depth + 3 + len(deps):]
        layer = pl.program_id(0)
        for i in range(depth):
            @pl.when(layer == i)
            def _(p=p_refs[i]):
                g = p[0].astype(F32)
                for j in range(1, n_parts):
                    g = g + p[j].astype(F32)
                m_new = ADAM_B1 * m_ref[...] + (1.0 - ADAM_B1) * g
                v_new = ADAM_B2 * v_ref[...] + (1.0 - ADAM_B2) * jnp.square(g)
                m_hat = m_new / (1.0 - ADAM_B1 ** ADAM_STEP)
                v_hat = v_new / (1.0 - ADAM_B2 ** ADAM_STEP)
                g_out[...] = g
                d_out[...] = -ADAM_LR * (m_hat / (jnp.sqrt(v_hat) + ADAM_EPS) + ADAM_WD * w_ref[...])
                m_out[...] = m_new
                v_out[...] = v_new

    def part_spec(i):
        return pl.BlockSpec((n_parts, t, cols), lambda l, r: (0, jnp.where(l < i, 0, jnp.where(l == i, r, nr - 1)), 0))

    spec = pl.BlockSpec((t, cols), lambda l, r: (l * nr + r, 0))
    out = jax.ShapeDtypeStruct((depth * rows, cols), F32)
    return pl.pallas_call(
        body, name=name, grid=(depth, nr),
        in_specs=[*[part_spec(i) for i in range(depth)], spec, spec, spec, *[_dep_spec(a) for a in deps]],
        out_specs=[spec] * 4, out_shape=[out] * 4, compiler_params=_cparams(2),
    )(*parts, w, m, v, *deps)


def sum_parts(parts, name):
    n_parts, rows, cols = parts.shape
    t = _tile(rows, 128)

    def fn(p):
        g = p[0].astype(F32)
        for i in range(1, n_parts):
            g = g + p[i].astype(F32)
        return g

    return rowwise(fn, [parts], [pl.BlockSpec((n_parts, t, cols), lambda r: (0, r, 0))],
                   jax.ShapeDtypeStruct((rows, cols), F32), pl.BlockSpec((t, cols), lambda r: (r, 0)), (rows // t,), name)


def _sd(shape, dtype=F32):
    return jax.ShapeDtypeStruct(shape, dtype)


def rms_fwd(x, g, name, col=0, width=None, dep=None):
    seq = x.shape[0]
    width = width or x.shape[1]
    t = _tile(seq, 512)
    return rowwise(_rms, [x, g.reshape(1, width)], [_rows(t, width, col), _whole((1, width))], _sd((seq, width), ACT),
                   _rows(t, width), (seq // t,), name, dep=dep)


def rms_bwd(x, g, dy, name, col=0, width=None, add=None, dx_dtype=F32):
    seq = x.shape[0]
    width = width or x.shape[1]
    t = _tile(seq, 512)
    ins, specs = [x, g.reshape(1, width), dy], [_rows(t, width, col), _whole((1, width)), _rows(t, width)]
    if add is None:
        fn = _rms_bwd
    else:
        ins.append(add)
        specs.append(_rows(t, width))

        def fn(x_, g_, dy_, add_):
            dx, dg = _rms_bwd(x_, g_, dy_)
            return dx + add_, dg
    return rowwise(fn, ins, specs, [_sd((seq, width), dx_dtype), _sd((1, width))], [_rows(t, width), _whole((1, width))],
                   (seq // t,), name, acc={1: (0,)})


def ffn_fwd(x, g, wg, wu, wd, l, tag, dep=None):
    seq, d = x.shape
    f = wg.shape[2]
    tm = _tile(seq, 1024)
    h = rms_fwd(x, g, f"{tag}_rms", dep=dep)
    hid_spec = pl.BlockSpec((None, tm, f), lambda j, m, k: (j, m, 0))
    up = lambda w, nm, **kw: mm(
        h, w, name=nm, dims=NT, grid=(NDEV, seq // tm, 1), nk=1,
        a_spec=pl.BlockSpec((tm, d), lambda j, m, k: (m, 0)),
        b_spec=pl.BlockSpec((None, None, f, d), lambda j, m, k: (j, l, 0, 0)), o_spec=hid_spec, **kw)
    a = up(wg, f"{tag}_gate", out_shape=_sd((NDEV, seq, f), ACT))
    b, hid = up(wu, f"{tag}_up", out_shape=[_sd((NDEV, seq, f), ACT)] * 2, extras=[a], extra_specs=[hid_spec],
                epilogue=lambda acc, a_: (acc, jax.nn.silu(a_.astype(F32)) * acc))
    tn = _tile(d, 1024)
    out = mm(hid, wd, name=f"{tag}_down", dims=NN, grid=(seq // tm, d // tn, NDEV), nk=NDEV,
             a_spec=pl.BlockSpec((None, tm, f), lambda m, n, k: (k, m, 0)),
             b_spec=pl.BlockSpec((None, None, f, tn), lambda m, n, k: (k, l, 0, n)),
             o_spec=pl.BlockSpec((tm, tn), lambda m, n, k: (m, n)), out_shape=_sd((seq, d)),
             extras=[x], extra_specs=[pl.BlockSpec((tm, tn), lambda m, n, k: (m, n))],
             epilogue=lambda acc, x_: x_ + 0.5 * acc)
    return out, (x, h, a, b, hid)


def ffn_bwd(dout, saved, g, wg, wu, wd, l, tag, send, dep=None):
    x, h, a, b, hid = saved
    seq, d = x.shape
    f = wg.shape[2]
    tm = _tile(seq, 1024)
    tk = _tile(seq, 1024)
    def act_bwd(acc, a_, b_):
        dh_, a_, b_ = 0.5 * acc, a_.astype(F32), b_.astype(F32)
        sig = jax.nn.sigmoid(a_)
        return dh_ * b_ * sig * (1.0 + a_ * (1.0 - sig)), dh_ * a_ * sig

    hid_spec = pl.BlockSpec((None, tm, f), lambda j, m, k: (j, m, 0))
    da, db = mm(dout, wd, name=f"{tag}_dhid", dims=NT, grid=(NDEV, seq // tm, 1), nk=1,
                a_spec=pl.BlockSpec((tm, d), lambda j, m, k: (m, 0)),
                b_spec=pl.BlockSpec((None, None, f, d), lambda j, m, k: (j, l, 0, 0)),
                o_spec=hid_spec, out_shape=[_sd((NDEV, seq, f), ACT)] * 2, extras=[a, b], extra_specs=[hid_spec] * 2,
                epilogue=act_bwd, dep=dep)
    tn = _tile(d, 1024)
    dw = lambda act, rhs, nm, epi: mm(
        act, rhs, name=nm, dims=TN, grid=(NDEV, d // tn, seq // tk), nk=seq // tk,
        a_spec=pl.BlockSpec((None, tk, f), lambda j, n, k: (j, k, 0)),
        b_spec=pl.BlockSpec((tk, tn), lambda j, n, k: (k, n)),
        o_spec=pl.BlockSpec((None, f, tn), lambda j, n, k: (j, 0, n)), out_shape=_sd((NDEV, f, d), COMM), epilogue=epi)
    dwd = dw(hid, dout, f"{tag}_dwd", lambda acc: 0.5 * acc)
    dwg, dwu = dw(da, h, f"{tag}_dwg", None), dw(db, h, f"{tag}_dwu", None)
    token = send(dwg, dwu, dwd)

    def dh_body(da_ref, db_ref, wg_ref, wu_ref, dep_ref, o_ref, acc_ref):
        k = pl.program_id(2)
        part = jnp.dot(da_ref[...].astype(MXU), wg_ref[...].astype(MXU), preferred_element_type=F32)
        part = part + jnp.dot(db_ref[...].astype(MXU), wu_ref[...].astype(MXU), preferred_element_type=F32)

        @pl.when(k == 0)
        def _():
            acc_ref[...] = part

        @pl.when(k > 0)
        def _():
            acc_ref[...] += part

        @pl.when(k == NDEV - 1)
        def _():
            o_ref[...] = acc_ref[...]

    act_spec = pl.BlockSpec((None, tm, f), lambda m, n, k: (k, m, 0))
    w_spec = pl.BlockSpec((None, None, f, tn), lambda m, n, k: (k, l, 0, n))
    dh = pl.pallas_call(
        dh_body, name=f"{tag}_dh", grid=(seq // tm, d // tn, NDEV),
        in_specs=[act_spec, act_spec, w_spec, w_spec, _dep_spec(token)],
        out_specs=pl.BlockSpec((tm, tn), lambda m, n, k: (m, n)), out_shape=_sd((seq, d)),
        scratch_shapes=[pltpu.VMEM((tm, tn), F32)], compiler_params=_cparams(3),
    )(da, db, wg, wu, token)
    return rms_bwd(x, g, dh, f"{tag}_drms", add=dout)


def _dense(a, w, l, name, out_dtype=F32, extras=(), epilogue=None):
    seq, kdim = a.shape
    n = w.shape[3]
    w2 = w.reshape(kdim, n)
    tm, tn, tk = _tile(seq, 1024), _tile(n, 1024), _tile(kdim, 1024)
    nk = kdim // tk
    return mm(a, w2, name=name, dims=NN, grid=(seq // tm, n // tn, nk), nk=nk,
              a_spec=pl.BlockSpec((tm, tk), lambda m, c, k: (m, k)),
              b_spec=pl.BlockSpec((tk, tn), lambda m, c, k: (k, c)),
              o_spec=pl.BlockSpec((tm, tn), lambda m, c, k: (m, c)), out_shape=_sd((seq, n), out_dtype),
              extras=list(extras), extra_specs=[pl.BlockSpec((tm, tn), lambda m, c, k: (m, c))] * len(extras),
              epilogue=epilogue)


def _dense_dx(dy, w, l, name, extras=(), epilogue=None, dep=None):
    seq, n = dy.shape
    kb = w.shape[2]
    tm = _tile(seq, 1024)
    return mm(dy, w, name=name, dims=NT, grid=(seq // tm, NDEV, 1), nk=1,
              a_spec=pl.BlockSpec((tm, n), lambda m, j, k: (m, 0)),
              b_spec=pl.BlockSpec((None, None, kb, n), lambda m, j, k: (j, l, 0, 0)),
              o_spec=pl.BlockSpec((tm, kb), lambda m, j, k: (m, j)), out_shape=_sd((seq, NDEV * kb)),
              extras=list(extras), extra_specs=[pl.BlockSpec((tm, kb), lambda m, j, k: (m, j))] * len(extras),
              epilogue=epilogue, dep=dep)


def _dense_dw(a, dy, kb, name):
    seq, n = dy.shape
    rows = NDEV * kb
    tk, tn, tr = _tile(seq, 1024), _tile(n, 1024), _tile(rows, 1024)
    out = mm(a, dy, name=name, dims=TN, grid=(rows // tr, n // tn, seq // tk), nk=seq // tk,
             a_spec=pl.BlockSpec((tk, tr), lambda j, c, k: (k, j)),
             b_spec=pl.BlockSpec((tk, tn), lambda j, c, k: (k, c)),
             o_spec=pl.BlockSpec((tr, tn), lambda j, c, k: (j, c)), out_shape=_sd((rows, n), COMM))
    return out.reshape(NDEV, kb, n)


def _heads_up(a, w, l, name, out_dtype):
    seq, r = a.shape
    c = w.shape[3]
    tm = _tile(seq, 1024)
    return mm(a, w, name=name, dims=NN, grid=(NDEV, seq // tm, 1), nk=1,
              a_spec=pl.BlockSpec((tm, r), lambda j, m, k: (m, 0)),
              b_spec=pl.BlockSpec((None, None, r, c), lambda j, m, k: (j, l, 0, 0)),
              o_spec=pl.BlockSpec((None, tm, c), lambda j, m, k: (j, m, 0)), out_shape=_sd((NDEV, seq, c), out_dtype))


def _heads_dx(dy, w, l, name):
    _, seq, c = dy.shape
    r = w.shape[2]
    tm = _tile(seq, 1024)
    return mm(dy, w, name=name, dims=NT, grid=(seq // tm, 1, NDEV), nk=NDEV,
              a_spec=pl.BlockSpec((None, tm, c), lambda m, n, k: (k, m, 0)),
              b_spec=pl.BlockSpec((None, None, r, c), lambda m, n, k: (k, l, 0, 0)),
              o_spec=pl.BlockSpec((tm, r), lambda m, n, k: (m, 0)), out_shape=_sd((seq, r)))


def _heads_dw(a, dy, name):
    seq, r = a.shape
    c = dy.shape[2]
    tk = _tile(seq, 1024)
    return mm(a, dy, name=name, dims=TN, grid=(NDEV, 1, seq // tk), nk=seq // tk,
              a_spec=pl.BlockSpec((tk, r), lambda j, n, k: (k, 0)),
              b_spec=pl.BlockSpec((None, tk, c), lambda j, n, k: (j, k, 0)),
              o_spec=pl.BlockSpec((None, r, c), lambda j, n, k: (j, 0, 0)), out_shape=_sd((NDEV, r, c), COMM))


C_FQ, C_FK, C_FV, C_CQ, C_CKV, C_DQ, C_DK, C_DV = range(8)
MAIN_W = 8 * 512
TAIL_W = 128


def split_w_in(w):
    fq, fk, fv, fl, cq, ckv, kr, dq, dk, dv = jnp.split(w, [512, 1024, 1536, 1540, 2052, 2564, 2628, 3140, 3652], axis=-1)
    main = jnp.concatenate([fq, fk, fv, cq, ckv, dq, dk, dv], axis=-1)
    pad = jnp.zeros((*w.shape[:-1], TAIL_W - 68), w.dtype)
    return main, jnp.concatenate([kr, fl, pad], axis=-1)


def merge_w_in(main, tail):
    fq, fk, fv, cq, ckv, dq, dk, dv = jnp.split(main, 8, axis=-1)
    return jnp.concatenate([fq, fk, fv, tail[..., 64:68], cq, ckv, tail[..., 0:64], dq, dk, dv], axis=-1)


def mixer_fwd(x, p, l, consts, dep=None):
    seq, d = x.shape
    nfox, nmla, ndil = 4, 8, 4
    cos_m, sin_m, cos_p, sin_p = consts
    t = _tile(seq, 512)
    tf = _tile(seq, 512)
    h = rms_fwd(x, p["mix_norm"], f"mix{l}_rms", dep=dep)
    proj = _dense(h, p["w_in_main"], 0, f"mix{l}_proj")
    tail = _dense(h, p["w_in_tail"], 0, f"mix{l}_tail")

    nb = seq // HEAD
    z = tail[:, 64:68].T.reshape(nfox * nb, HEAD)
    bias_rows = jnp.repeat(p["fox_forget_bias"], nb).reshape(nfox * nb, 1)
    cum = fox_gate_fwd(z, bias_rows, nb, f"mix{l}_gate").reshape(nfox, seq)
    cum2 = (cum.reshape(nfox, seq, 1), cum.reshape(nfox, 1, seq))
    fox_qkv = ((proj, "cols", HEAD, C_FQ * 4), (proj, "cols", HEAD, C_FK * 4), (proj, "cols", HEAD, C_FV * 4))
    out_a, lse_a = flash_fwd(*fox_qkv, cum2, n_heads=nfox, seq=seq, t=tf, scale=HEAD ** -0.5, name=f"mix{l}_fox")

    cq = rms_fwd(proj, p["mla_q_norm"], f"mix{l}_cq", col=C_CQ, width=512)
    ckv = rms_fwd(proj, p["mla_kv_norm"], f"mix{l}_ckv", col=C_CKV, width=512)
    q_raw = _heads_up(cq, p["mla_w_uq"], 0, f"mix{l}_uq", F32)
    kv = _heads_up(ckv, p["mla_w_ukv"], 0, f"mix{l}_ukv", ACT)

    def mla_prep(q_, kv_, tail_, cos_, sin_, q_out, k_out):
        perm = _swap_matrix(MLA_ROPE)
        c, s = cos_[...], sin_[...]
        q_out[:, 0:HEAD] = q_[:, 0:HEAD].astype(q_out.dtype)
        q_out[:, HEAD:MLA_QK] = _rope(q_[:, HEAD:MLA_QK], c, s, perm).astype(q_out.dtype)
        k_out[:, 0:HEAD] = kv_[:, 0:HEAD].astype(k_out.dtype)
        k_out[:, HEAD:MLA_QK] = _rope(tail_[:, 0:MLA_ROPE], c, s, perm).astype(k_out.dtype)

    hs = lambda w: pl.BlockSpec((None, t, w), lambda hh, m: (hh, m, 0))
    rs = lambda w: pl.BlockSpec((t, w), lambda hh, m: (m, 0))
    q_b, k_b = pl.pallas_call(
        lambda q_, kv_, tl_, c_, s_, qo, ko: mla_prep(q_[...], kv_[...], tl_[...], c_, s_, qo, ko),
        name=f"mix{l}_mla_prep", grid=(nmla, seq // t),
        in_specs=[hs(MLA_QK), hs(2 * HEAD), rs(TAIL_W), rs(MLA_ROPE), rs(MLA_ROPE)], out_specs=[hs(MLA_QK), hs(MLA_QK)],
        out_shape=[_sd((nmla, seq, MLA_QK), ACT)] * 2, compiler_params=_cparams(2),
    )(q_raw, kv, tail, cos_m, sin_m)
    mla_qkv = ((q_b, "heads", MLA_QK, 0), (k_b, "heads", MLA_QK, 0), (kv, "heads", HEAD, 1))
    out_b, lse_b = flash_fwd(*mla_qkv, None, n_heads=nmla, seq=seq, t=tf, scale=MLA_QK ** -0.5, name=f"mix{l}_mla")

    wd_ = ndil * HEAD

    def dil_prep(q_, k_, c_, s_):
        perm = _pad_perm(PARTIAL_ROPE)
        rot = lambda a: jnp.concatenate(
            [_rope(a[:, i * HEAD:(i + 1) * HEAD], c_, s_, perm) for i in range(ndil)], axis=1)
        return rot(q_), rot(k_)

    dq_r, dk_r = rowwise(dil_prep, [proj, proj, cos_p, sin_p],
                         [_rows(t, wd_, C_DQ), _rows(t, wd_, C_DK), _rows(t, HEAD), _rows(t, HEAD)],
                         [_sd((seq, wd_), ACT)] * 2, [_rows(t, wd_)] * 2, (seq // t,), f"mix{l}_dil_prep")
    dv = proj[:, C_DV * 512:(C_DV + 1) * 512]
    branches = [dil_fwd(dq_r, dk_r, dv, seq=seq, dil=dl, n_heads=ndil, name=f"mix{l}_dil{dl}") for dl in DIL_BRANCHES]

    def mix(o1, o2, o3, l1, l2, l3):
        m = jnp.maximum(jnp.maximum(l1, l2), l3)
        e1, e2, e3 = jnp.exp(l1 - m), jnp.exp(l2 - m), jnp.exp(l3 - m)
        return (e1 * o1 + e2 * o2 + e3 * o3) / (e1 + e2 + e3)

    out_c = rowwise(mix, [b[0] for b in branches] + [b[1] for b in branches], [_rows(t, wd_)] * 6, _sd((seq, wd_)),
                    _rows(t, wd_), (seq // t,), f"mix{l}_dil_mix")

    mixed = jnp.concatenate([out_a, out_b, out_c], axis=1)
    out = _dense(mixed, p["w_out"], 0, f"mix{l}_out", extras=[x], epilogue=lambda acc, x_: x_ + acc)
    saved = dict(x=x, h=h, proj=proj, tail=tail, z=z, bias_rows=bias_rows, cum2=cum2, out_a=out_a, lse_a=lse_a, cq=cq,
                 ckv=ckv, q_raw=q_raw, kv=kv, q_b=q_b, k_b=k_b, out_b=out_b, lse_b=lse_b, dq_r=dq_r, dk_r=dk_r, dv=dv,
                 branches=branches, out_c=out_c, mixed=mixed)
    return out, saved


def _pad_perm(n):
    i = lax.broadcasted_iota(jnp.int32, (HEAD, HEAD), 0)
    j = lax.broadcasted_iota(jnp.int32, (HEAD, HEAD), 1)
    inside = jnp.logical_and(i < n, j < n)
    return jnp.where(jnp.logical_and(inside, ((i + n // 2) % n) == j), 1.0, 0.0).astype(F32)


def mixer_bwd(dout, sv, p, l, consts, dep=None):
    seq, d = dout.shape
    nfox, nmla, ndil = 4, 8, 4
    cos_m, sin_m, cos_p, sin_p = consts
    t = _tile(seq, 512)
    tf = _tile(seq, 512)
    nb = seq // HEAD
    proj, tail = sv["proj"], sv["tail"]
    dmixed = _dense_dx(dout, p["w_out"], 0, f"mix{l}_dmixed", dep=dep)
    dw_out = _dense_dw(sv["mixed"], dout, d // NDEV, f"mix{l}_dw_out")
    do_a, do_b, do_c = dmixed[:, 0:512], dmixed[:, 512:1536], dmixed[:, 1536:2048]

    fox_qkv = ((proj, "cols", HEAD, C_FQ * 4), (proj, "cols", HEAD, C_FK * 4), (proj, "cols", HEAD, C_FV * 4))
    delta_a = attn_delta(do_a, sv["out_a"], n_heads=nfox, seq=seq, name=f"mix{l}_fox_delta")
    row = lambda a: a.reshape(a.shape[0], 1, seq)
    dfq, dcum_q = flash_bwd_dq(*fox_qkv, do_a, sv["lse_a"], delta_a, sv["cum2"], n_heads=nfox, seq=seq, t=tf,
                               scale=HEAD ** -0.5, name=f"mix{l}_fox_dq")
    dfk, dfv, dcum_k = flash_bwd_dkv(*fox_qkv, do_a, row(sv["lse_a"]), row(delta_a), sv["cum2"], n_heads=nfox, seq=seq,
                                     t=tf, scale=HEAD ** -0.5, name=f"mix{l}_fox_dkv")
    dz, dbias = fox_gate_bwd(sv["z"], sv["bias_rows"], dcum_q.reshape(nfox * nb, HEAD), dcum_k.reshape(nfox * nb, HEAD),
                             nb, f"mix{l}_dgate")
    d_fox_bias = dbias.reshape(nfox, nb, HEAD)[:, 0, 0]
    dfl = dz.reshape(nfox, seq).T
    unheads = lambda a: a.transpose(1, 0, 2).reshape(seq, -1)

    mla_qkv = ((sv["q_b"], "heads", MLA_QK, 0), (sv["k_b"], "heads", MLA_QK, 0), (sv["kv"], "heads", HEAD, 1))
    delta_b = attn_delta(do_b, sv["out_b"], n_heads=nmla, seq=seq, name=f"mix{l}_mla_delta")
    dq_b = flash_bwd_dq(*mla_qkv, do_b, sv["lse_b"], delta_b, None, n_heads=nmla, seq=seq, t=tf, scale=MLA_QK ** -0.5,
                        name=f"mix{l}_mla_dq")
    dk_b, dv_b = flash_bwd_dkv(*mla_qkv, do_b, row(sv["lse_b"]), row(delta_b), None, n_heads=nmla, seq=seq, t=tf,
                               scale=MLA_QK ** -0.5, name=f"mix{l}_mla_dkv")

    def mla_unprep(dq_, dk_, dv_, cos_, sin_, dq_out, dkv_out, dkr_out):
        perm = _swap_matrix(MLA_ROPE)
        c, s = cos_[...], sin_[...]
        dq_out[:, 0:HEAD] = dq_[:, 0:HEAD].astype(dq_out.dtype)
        dq_out[:, HEAD:MLA_QK] = _rope_t(dq_[:, HEAD:MLA_QK], c, s, perm).astype(dq_out.dtype)
        dkv_out[:, 0:HEAD] = dk_[:, 0:HEAD].astype(dkv_out.dtype)
        dkv_out[:, HEAD:2 * HEAD] = dv_.astype(dkv_out.dtype)
        dkr = _rope_t(dk_[:, HEAD:MLA_QK], c, s, perm)
        first = pl.program_id(1) == 0

        @pl.when(first)
        def _():
            dkr_out[...] = dkr

        @pl.when(jnp.logical_not(first))
        def _():
            dkr_out[...] += dkr

    hs = lambda w: pl.BlockSpec((None, t, w), lambda m, hh: (hh, m, 0))
    rs = lambda w: pl.BlockSpec((t, w), lambda m, hh: (m, 0))
    dq_raw, dkv, dk_r = pl.pallas_call(
        lambda a, b, c, cs, sn, o1, o2, o3: mla_unprep(a[...], b[...], c[...], cs, sn, o1, o2, o3),
        name=f"mix{l}_mla_unprep", grid=(seq // t, nmla),
        in_specs=[hs(MLA_QK), hs(MLA_QK), hs(HEAD), rs(MLA_ROPE), rs(MLA_ROPE)],
        out_specs=[hs(MLA_QK), hs(2 * HEAD), rs(MLA_ROPE)],
        out_shape=[_sd((nmla, seq, MLA_QK), ACT), _sd((nmla, seq, 2 * HEAD), ACT), _sd((seq, MLA_ROPE))],
        compiler_params=_cparams(2),
    )(dq_b, dk_b, dv_b, cos_m, sin_m)
    dcq_n = _heads_dx(dq_raw, p["mla_w_uq"], 0, f"mix{l}_dcq")
    dckv_n = _heads_dx(dkv, p["mla_w_ukv"], 0, f"mix{l}_dckv")
    dw_uq = _heads_dw(sv["cq"], dq_raw, f"mix{l}_dw_uq")
    dw_ukv = _heads_dw(sv["ckv"], dkv, f"mix{l}_dw_ukv")
    dcq, dg_q = rms_bwd(proj, p["mla_q_norm"], dcq_n, f"mix{l}_dcq_rms", col=C_CQ, width=512)
    dckv, dg_kv = rms_bwd(proj, p["mla_kv_norm"], dckv_n, f"mix{l}_dckv_rms", col=C_CKV, width=512)

    wd_ = ndil * HEAD
    outs = [b[0] for b in sv["branches"]]
    lses = [b[1] for b in sv["branches"]]

    def mix_bwd(do_, o1, o2, o3, l1, l2, l3):
        m = jnp.maximum(jnp.maximum(l1, l2), l3)
        e1, e2, e3 = jnp.exp(l1 - m), jnp.exp(l2 - m), jnp.exp(l3 - m)
        z_ = e1 + e2 + e3
        w1, w2, w3 = e1 / z_, e2 / z_, e3 / z_
        out = w1 * o1 + w2 * o2 + w3 * o3
        return (w1 * do_, w2 * do_, w3 * do_, do_ * w1 * (o1 - out), do_ * w2 * (o2 - out), do_ * w3 * (o3 - out))

    mb = rowwise(mix_bwd, [do_c] + outs + lses, [_rows(t, wd_)] * 7, [_sd((seq, wd_))] * 6, [_rows(t, wd_)] * 6,
                 (seq // t,), f"mix{l}_dil_dmix")
    grads = [dil_bwd(sv["dq_r"], sv["dk_r"], sv["dv"], outs[i], lses[i], mb[i], mb[3 + i], seq=seq, dil=dl,
                     n_heads=ndil, name=f"mix{l}_dil{dl}_bwd") for i, dl in enumerate(DIL_BRANCHES)]

    def dil_unprep(q1, q2, q3, k1, k2, k3, v1, v2, v3, c_, s_):
        perm = _pad_perm(PARTIAL_ROPE)
        rot_t = lambda a: jnp.concatenate(
            [_rope_t(a[:, i * HEAD:(i + 1) * HEAD], c_, s_, perm) for i in range(ndil)], axis=1)
        return rot_t(q1 + q2 + q3), rot_t(k1 + k2 + k3), v1 + v2 + v3

    ddq, ddk, ddv = rowwise(dil_unprep, [g[0] for g in grads] + [g[1] for g in grads] + [g[2] for g in grads] + [cos_p, sin_p],
                            [_rows(t, wd_)] * 9 + [_rows(t, HEAD)] * 2, [_sd((seq, wd_))] * 3, [_rows(t, wd_)] * 3,
                            (seq // t,), f"mix{l}_dil_unprep")

    dproj = jnp.concatenate([unheads(dfq), unheads(dfk), unheads(dfv), dcq, dckv, ddq, ddk, ddv], axis=1).astype(ACT)
    dtail = jnp.concatenate([dk_r, dfl, jnp.zeros((seq, TAIL_W - 68), F32)], axis=1)
    dh = _dense_dx(dproj, p["w_in_main"], 0, f"mix{l}_dh_main")
    dh = _dense_dx(dtail, p["w_in_tail"], 0, f"mix{l}_dh_tail", extras=[dh], epilogue=lambda acc, prev: acc + prev)
    dw_main = _dense_dw(sv["h"], dproj, d // NDEV, f"mix{l}_dw_in_main")
    dw_tail = _dense_dw(sv["h"], dtail, d // NDEV, f"mix{l}_dw_in_tail")
    dx, dg_mix = rms_bwd(sv["x"], p["mix_norm"], dh, f"mix{l}_drms", add=dout)
    return dx, dict(mix_norm=dg_mix, w_in_main=dw_main, w_in_tail=dw_tail, fox_forget_bias=d_fox_bias, mla_q_norm=dg_q,
                    mla_kv_norm=dg_kv, mla_w_uq=dw_uq, mla_w_ukv=dw_ukv, w_out=dw_out)


def loss_head(x, g, target, name):
    seq, d = x.shape
    t = _tile(seq, 512)

    def fn(x_, g_, tgt):
        err = _rms(x_, g_) - tgt
        part = 0.5 * jnp.sum(jnp.mean(err * err, axis=-1, keepdims=True), axis=0, keepdims=True)
        dx, dg = _rms_bwd(x_, g_, err / d)
        return jnp.broadcast_to(part, (1, HEAD)), dx, dg

    return rowwise(fn, [x, g.reshape(1, d), target], [_rows(t, d), _whole((1, d)), _rows(t, d)],
                   [_sd((1, HEAD)), _sd((seq, d)), _sd((1, d))], [_whole((1, HEAD)), _rows(t, d), _whole((1, d))],
                   (seq // t,), name, acc={0: (0,), 2: (0,)})


BIG = ("ffn1_w_gate", "ffn1_w_up", "ffn1_w_down", "w_in", "mla_w_uq", "mla_w_ukv", "w_out", "ffn2_w_gate", "ffn2_w_up",
       "ffn2_w_down")
GROUPS = {
    "ffn1": ("ffn1_w_gate", "ffn1_w_up", "ffn1_w_down"),
    "mix": ("w_in_main", "w_in_tail", "mla_w_uq", "mla_w_ukv", "w_out"),
    "ffn2": ("ffn2_w_gate", "ffn2_w_up", "ffn2_w_down"),
}
TRANSPOSED = ("ffn1_w_gate", "ffn1_w_up", "ffn2_w_gate", "ffn2_w_up")
PREFETCH = 2
SMALL_D = ("ffn1_norm", "mix_norm", "ffn2_norm")
WEIGHTS = ("ffn1_norm", "ffn1_w_gate", "ffn1_w_up", "ffn1_w_down", "mix_norm", "w_in", "fox_forget_bias", "mla_q_norm",
           "mla_kv_norm", "mla_w_uq", "mla_w_ukv", "w_out", "ffn2_norm", "ffn2_w_gate", "ffn2_w_up", "ffn2_w_down",
           "final_norm")


def pack_small(vals, depth, d):
    rows = [vals[n].reshape(depth, d) for n in SMALL_D]
    rows.append(vals["final_norm"].reshape(1, d))
    qk = jnp.concatenate([vals["mla_q_norm"].reshape(-1), vals["mla_kv_norm"].reshape(-1)])
    rows.append(jnp.pad(qk, (0, -qk.shape[0] % d)).reshape(-1, d))
    last = jnp.concatenate([vals["fox_forget_bias"].reshape(-1), vals["loss"].reshape(-1)])
    rows.append(jnp.pad(last, (0, d - last.shape[0])).reshape(1, d))
    out = jnp.concatenate(rows, axis=0)
    return jnp.pad(out, ((0, -out.shape[0] % 8), (0, 0)))


def unpack_small(a, depth, d, rank):
    out, r = {}, 0
    for n in SMALL_D:
        out[n] = a[r:r + depth]
        r += depth
    out["final_norm"] = a[r]
    r += 1
    n_qk = -(-2 * depth * rank // d)
    qk = a[r:r + n_qk].reshape(-1)[:2 * depth * rank].reshape(2, depth, rank)
    out["mla_q_norm"], out["mla_kv_norm"] = qk[0], qk[1]
    r += n_qk
    out["fox_forget_bias"] = a[r, :depth * 4].reshape(depth, 4)
    out["loss"] = a[r, depth * 4]
    return out


def step(x, target, w, m, v):
    depth = w["ffn1_norm"].shape[0]
    seq, d = x.shape[1], x.shape[2]
    rank = w["mla_q_norm"].shape[1]
    x = x.reshape(seq, d)
    target = target.reshape(seq, d)

    consts = (*rope_tables(seq, MLA_ROPE), *[jnp.pad(a, ((0, 0), (0, HEAD - PARTIAL_ROPE)), constant_values=c)
                                             for a, c in zip(rope_tables(seq, PARTIAL_ROPE), (1.0, 0.0))])
    order = [(l, k) for l in range(depth) for k in GROUPS]
    small_of = lambda l: {n: w[n][l] for n in ("mix_norm", "fox_forget_bias", "mla_q_norm", "mla_kv_norm")}

    view = lambda n, a: a.transpose(0, 2, 1) if n in TRANSPOSED else a

    def shards(l, k):
        if k == "mix":
            main, tail = split_w_in(w["w_in"][l:l + 1])
            xs = [main, tail, *[w[n][l:l + 1] for n in GROUPS[k][2:]]]
        else:
            xs = [view(n, w[n])[l:l + 1] for n in GROUPS[k]]
        return [a.astype(COMM) for a in xs]

    handles, fresh = {}, []

    def launch(i, dep):
        l, k = order[i]
        xs = shards(l, k)
        if dep is not None:
            xs = lax.optimization_barrier((xs, dep))[0]
        handles[i], token = exchange_start(xs, False, f"gather_start_{k}{l}")
        fresh.append(token)

    def take_tokens():
        tok = functools.reduce(jnp.add, fresh) if fresh else None
        fresh.clear()
        return tok

    launched = min(2, len(order))
    for i in range(launched):
        launch(i, None)
    gathered, saved = {}, {}
    for i, (l, k) in enumerate(order):
        wts = dict(zip(GROUPS[k], exchange_wait(handles[i], fresh[-1] if i == 0 else x, f"gather_wait_{k}{l}")))
        gathered[l, k] = wts
        while launched < min(len(order), i + 1 + PREFETCH):
            launch(launched, (x, wts[GROUPS[k][0]]))
            launched += 1
        tok = take_tokens()
        if k == "mix":
            x, saved[l, k] = mixer_fwd(x, {**wts, **small_of(l)}, l, consts, dep=tok)
        else:
            x, saved[l, k] = ffn_fwd(x, w[f"{k}_norm"][l], wts[f"{k}_w_gate"], wts[f"{k}_w_up"], wts[f"{k}_w_down"], 0,
                                     f"{k}_{l}", dep=tok)
    loss, dx, d_final = loss_head(x, w["final_norm"], target, "loss_head")

    small = {n: [None] * depth for n in SMALL_D + ("mla_q_norm", "mla_kv_norm", "fox_forget_bias")}
    pending, tok = [], None

    def send(l, k):
        def start(*grads):
            handle, token = exchange_start(list(grads), True, f"scatter_start_{k}{l}")
            pending.append((l, k, handle))
            return token
        return start

    for l, k in reversed(order):
        wts = gathered[l, k]
        if k == "mix":
            dx, gm = mixer_bwd(dx, saved[l, k], {**wts, **small_of(l)}, l, consts, dep=tok)
            tok = send(l, k)(*[gm[n] for n in GROUPS[k]])
        else:
            dx, dg = ffn_bwd(dx, saved[l, k], w[f"{k}_norm"][l], wts[f"{k}_w_gate"], wts[f"{k}_w_up"], wts[f"{k}_w_down"],
                             0, f"{k}_{l}", send(l, k), dep=tok)
            gm, tok = {f"{k}_norm": dg}, None
        for n in small:
            if n in gm:
                small[n][l] = gm[n]
    big = {n: [None] * depth for k in GROUPS for n in GROUPS[k]}

    def land(l, k, handle, after):
        for n, a in zip(GROUPS[k], exchange_wait(handle, after, f"scatter_wait_{k}{l}")):
            big[n][l] = a

    for l, k, handle in pending[:-1]:
        land(l, k, handle, dx)

    out = {}

    def update(name, parts, shape, dep=None):
        shape_v = (shape[0], shape[2], shape[1]) if name in TRANSPOSED else shape
        flat = lambda a: view(name, a).reshape(-1, shape_v[-1])
        res = adamw(parts, flat(w[name]), flat(m[name]), flat(v[name]), f"adamw_{name}", dep=dep)
        for kind, r in zip(("grad", "delta", "new_m", "new_v"), res):
            out[f"{kind}_{name}"] = view(name, r.reshape(shape_v))

    last = GROUPS[pending[-1][1]]
    for n in BIG:
        if n != "w_in" and n not in last:
            update(n, [a.reshape(NDEV, -1, a.shape[-1]) for a in big[n]], w[n].shape, dep=tok)
    g_in = [merge_w_in(sum_parts(big["w_in_main"][l].reshape(NDEV, -1, MAIN_W), f"sum_w_in_main{l}"),
                       sum_parts(big["w_in_tail"][l].reshape(NDEV, -1, TAIL_W), f"sum_w_in_tail{l}"))[None]
            for l in range(depth)]
    update("w_in", g_in, w["w_in"].shape, dep=tok)
    names = list(out)
    out.update(zip(names, lax.optimization_barrier([out[n] for n in names])))
    land(*pending[-1], out[names[0]])
    for n in last:
        update(n, [a.reshape(NDEV, -1, a.shape[-1]) for a in big[n]], w[n].shape)
    out["grad_x"] = dx.reshape(1, seq, d)

    part = {n: jnp.stack(g).reshape(depth, -1) for n, g in small.items()}
    part["final_norm"], part["loss"] = d_final, loss[0, 0:1]
    parts = exchange(pack_small(part, depth, d), False, "gather_small")
    zero = jnp.zeros((1,), F32)
    packed = [pack_small({**{n: a[n] for n in part if n != "loss"}, "loss": zero}, depth, d) for a in (w, m, v)]
    res = [unpack_small(r, depth, d, rank) for r in adamw([parts], *packed, "adamw_small")]
    out["loss"] = res[0]["loss"]
    for n in small.keys() | {"final_norm"}:
        for kind, r in zip(("grad", "delta", "new_m", "new_v"), res):
            out[f"{kind}_{n}"] = r[n].reshape(w[n].shape)
    return out


def kernel(x, ffn1_norm, ffn1_w_gate, ffn1_w_up, ffn1_w_down, mix_norm, w_in, fox_forget_bias, mla_q_norm, mla_kv_norm, mla_w_uq, mla_w_ukv, w_out, ffn2_norm, ffn2_w_gate, ffn2_w_up, ffn2_w_down, final_norm, loss_target, m_ffn1_norm, m_ffn1_w_gate, m_ffn1_w_up, m_ffn1_w_down, m_mix_norm, m_w_in, m_fox_forget_bias, m_mla_q_norm, m_mla_kv_norm, m_mla_w_uq, m_mla_w_ukv, m_w_out, m_ffn2_norm, m_ffn2_w_gate, m_ffn2_w_up, m_ffn2_w_down, m_final_norm, v_ffn1_norm, v_ffn1_w_gate, v_ffn1_w_up, v_ffn1_w_down, v_mix_norm, v_w_in, v_fox_forget_bias, v_mla_q_norm, v_mla_kv_norm, v_mla_w_uq, v_mla_w_ukv, v_w_out, v_ffn2_norm, v_ffn2_w_gate, v_ffn2_w_up, v_ffn2_w_down, v_final_norm):
    args = locals()
    w = {n: args[n] for n in WEIGHTS}
    m = {n: args["m_" + n] for n in WEIGHTS}
    v = {n: args["v_" + n] for n in WEIGHTS}
    out = step(x, loss_target, w, m, v)
    return (out["loss"], out["grad_x"], *[out["grad_" + n] for n in WEIGHTS], *[out["delta_" + n] for n in WEIGHTS],
            *[out["new_m_" + n] for n in WEIGHTS], *[out["new_v_" + n] for n in WEIGHTS])
```

```python
import functools

import jax
import jax.numpy as jnp
from jax import lax
from jax.experimental import pallas as pl
from jax.experimental.pallas import tpu as pltpu

F32 = jnp.float32
MXU = jnp.bfloat16
ACT = jnp.bfloat16
COMM = jnp.bfloat16
HI = lax.Precision.HIGHEST
NN = (((1,), (0,)), ((), ()))
NT = (((1,), (1,)), ((), ()))
TN = (((0,), (0,)), ((), ()))

NDEV = 8
HEAD = 128
EPS = 1e-6
ROPE_THETA = 500000.0
PARTIAL_ROPE = HEAD // 4
MLA_ROPE = 64
MLA_QK = HEAD + MLA_ROPE
DIL_BRANCHES = (1, 4, 16)
NEG = -1e30
VMEM_LIMIT = 48 * 1024 * 1024

ADAMW_STEP_BYTES = 12 * 1024 * 1024

ADAM_LR, ADAM_B1, ADAM_B2, ADAM_EPS, ADAM_WD, ADAM_STEP = 0.001, 0.9, 0.999, 1e-08, 0.01, 10


def _cparams(n_axes):
    return pltpu.CompilerParams(dimension_semantics=("arbitrary",) * n_axes, vmem_limit_bytes=VMEM_LIMIT)


def _tile(n, t):
    t = min(n, t)
    assert n % t == 0, (n, t)
    return t


def _dep_spec(dep):
    nd = dep.ndim
    return pl.BlockSpec(dep.shape, lambda *_: (0,) * nd)


def mm(a, b, *, name, dims, grid, a_spec, b_spec, o_spec, out_shape, nk, extras=(), extra_specs=(), epilogue=None, dep=None):
    n_ex = len(extras)
    kaxis = len(grid) - 1
    if dep is not None:
        extras, extra_specs = [*extras, dep], [*extra_specs, _dep_spec(dep)]
    n_more = len(extras)
    n_out = len(out_shape) if isinstance(out_shape, (list, tuple)) else 1

    def body(a_ref, b_ref, *rest):
        ex, o_refs = rest[:n_ex], rest[n_more:n_more + n_out]

        def finish(acc):
            res = acc if epilogue is None else epilogue(acc, *[e[...] for e in ex])
            for o_ref, r in zip(o_refs, res if n_out > 1 else (res,)):
                o_ref[...] = r.astype(o_ref.dtype)

        part = lax.dot_general(a_ref[...].astype(MXU), b_ref[...].astype(MXU), dims, preferred_element_type=F32)
        if nk == 1:
            finish(part)
        else:
            acc_ref = rest[n_more + n_out]
            k = pl.program_id(kaxis)

            @pl.when(k == 0)
            def _():
                acc_ref[...] = part

            @pl.when(k > 0)
            def _():
                acc_ref[...] += part

            @pl.when(k == nk - 1)
            def _():
                finish(acc_ref[...])

    acc_shape = tuple(d for d in o_spec.block_shape if d is not None)
    return pl.pallas_call(
        body, name=name, grid=grid, in_specs=[a_spec, b_spec, *extra_specs],
        out_specs=[o_spec] * n_out if n_out > 1 else o_spec, out_shape=out_shape,
        scratch_shapes=[] if nk == 1 else [pltpu.VMEM(acc_shape, F32)], compiler_params=_cparams(len(grid)),
    )(a, b, *extras)


def rowwise(fn, ins, in_specs, outs, out_specs, grid, name, acc=None, dep=None):
    acc = acc or {}
    n_in = len(ins)
    if dep is not None:
        ins, in_specs = [*ins, dep], [*in_specs, _dep_spec(dep)]
    n_all = len(ins)

    def body(*refs):
        vals = fn(*[r[...] for r in refs[:n_in]])
        if not isinstance(vals, (tuple, list)):
            vals = (vals,)
        for i, (r, v) in enumerate(zip(refs[n_all:], vals)):
            if i in acc:
                first = functools.reduce(jnp.logical_and, [pl.program_id(ax) == 0 for ax in acc[i]])

                @pl.when(first)
                def _(r=r, v=v):
                    r[...] = v.astype(r.dtype)

                @pl.when(jnp.logical_not(first))
                def _(r=r, v=v):
                    r[...] += v.astype(r.dtype)
            else:
                r[...] = v.astype(r.dtype)

    return pl.pallas_call(
        body, name=name, grid=grid, in_specs=in_specs, out_specs=out_specs, out_shape=outs,
        compiler_params=_cparams(len(grid)),
    )(*ins)


def _rows(t, c, col=0):
    return pl.BlockSpec((t, c), lambda m, col=col: (m, col))


def _whole(shape):
    nd = len(shape)
    return pl.BlockSpec(shape, lambda *_: (0,) * nd)


def _rms(x, g):
    x = x.astype(F32)
    return x * lax.rsqrt(jnp.mean(x * x, axis=-1, keepdims=True) + EPS) * g


def _rms_bwd(x, g, dy):
    x = x.astype(F32)
    dy = dy.astype(F32)
    r = lax.rsqrt(jnp.mean(x * x, axis=-1, keepdims=True) + EPS)
    xh = x * r
    dg = jnp.sum(dy * xh, axis=0, keepdims=True)
    dxh = dy * g
    dx = r * (dxh - xh * jnp.mean(dxh * xh, axis=-1, keepdims=True))
    return dx, dg


def _swap_matrix(n):
    i = lax.broadcasted_iota(jnp.int32, (n, n), 0)
    j = lax.broadcasted_iota(jnp.int32, (n, n), 1)
    return jnp.where(((i + n // 2) % n) == j, 1.0, 0.0).astype(F32)


def _rope(x, cos, sin_signed, perm):
    return x * cos + jnp.dot(x, perm, precision=HI, preferred_element_type=F32) * sin_signed


def _rope_t(dy, cos, sin_signed, perm):
    return dy * cos + jnp.dot(dy * sin_signed, perm, precision=HI, preferred_element_type=F32)


def rope_tables(seq, dim):
    inv = 1.0 / (ROPE_THETA ** (jnp.arange(0, dim, 2, dtype=F32) / dim))
    ang = jnp.arange(seq, dtype=F32)[:, None] * inv[None, :]
    cos, sin = jnp.cos(ang), jnp.sin(ang)
    return jnp.concatenate([cos, cos], axis=1), jnp.concatenate([-sin, sin], axis=1)


HP = 2


def _triangle(nb, by_key):
    pairs = [(i, j) for j in range(nb) for i in range(j, nb)] if by_key else [(i, j) for i in range(nb) for j in range(i + 1)]
    return jnp.asarray([p[0] for p in pairs], jnp.int32), jnp.asarray([p[1] for p in pairs], jnp.int32)


_Q_BLOCK = lambda n, ii, jj: ii[n]
_K_BLOCK = lambda n, ii, jj: jj[n]


def _hspec(arr_kind, t, w, off, seq_of):
    if arr_kind == "cols":
        assert off % HP == 0
        return pl.BlockSpec((t, HP * w), lambda h, n, ii, jj: (seq_of(n, ii, jj), off // HP + h))
    return pl.BlockSpec((HP, t, w), lambda h, n, ii, jj: (h, seq_of(n, ii, jj), off))


def _head(ref, arr_kind, hh, w):
    return ref[:, hh * w:(hh + 1) * w] if arr_kind == "cols" else ref[hh]


def _colspec(t, seq_of):
    return pl.BlockSpec((HP, t, 1), lambda h, n, ii, jj: (h, seq_of(n, ii, jj), 0))


def _rowspec(t, seq_of):
    return pl.BlockSpec((HP, 1, t), lambda h, n, ii, jj: (h, 0, seq_of(n, ii, jj)))


def _causal(s, qi, kj, t, transposed=False):
    a = lax.broadcasted_iota(jnp.int32, (t, t), 0)
    b = lax.broadcasted_iota(jnp.int32, (t, t), 1)
    keep = (kj * t + a <= qi * t + b) if transposed else (kj * t + b <= qi * t + a)
    return jnp.where(keep, s, NEG)


def flash_fwd(q, k, v, cum, *, n_heads, seq, t, scale, name):
    nb = seq // t
    qs, ks = _Q_BLOCK, _K_BLOCK
    steps = _triangle(nb, by_key=False)
    ins = [q[0], k[0], v[0]]
    specs = [_hspec(q[1], t, q[2], q[3], qs), _hspec(k[1], t, k[2], k[3], ks), _hspec(v[1], t, v[2], v[3], ks)]
    if cum is not None:
        ins += [cum[0], cum[1]]
        specs += [_colspec(t, qs), _rowspec(t, ks)]

    def body(ii_ref, jj_ref, *refs):
        q_ref, k_ref, v_ref = refs[:3]
        o_ref, lse_ref, m_s, l_s, acc_s = refs[-5:]
        i, j = ii_ref[pl.program_id(1)], jj_ref[pl.program_id(1)]

        @pl.when(j == 0)
        def _():
            m_s[...] = jnp.full(m_s.shape, NEG, F32)
            l_s[...] = jnp.zeros(l_s.shape, F32)
            acc_s[...] = jnp.zeros(acc_s.shape, F32)

        def block(masked):
            new = []
            for hh in range(HP):
                qb = _head(q_ref, q[1], hh, q[2]).astype(MXU)
                kb = _head(k_ref, k[1], hh, k[2]).astype(MXU)
                s = lax.dot_general(qb, kb, NT, preferred_element_type=F32) * scale
                if cum is not None:
                    s = s + (refs[3][hh] - refs[4][hh])
                if masked:
                    s = _causal(s, i, j, t)
                m_old = m_s[hh]
                m_new = jnp.maximum(m_old, jnp.max(s, axis=1, keepdims=True))
                alpha = jnp.exp(m_old - m_new)
                p = jnp.exp(s - m_new)
                vb = _head(v_ref, v[1], hh, v[2]).astype(MXU)
                pv = jnp.dot(p.astype(MXU), vb, preferred_element_type=F32)
                new.append((m_new, alpha, alpha * l_s[hh] + jnp.sum(p, axis=1, keepdims=True), pv))
            for hh, (m_new, alpha, l_new, pv) in enumerate(new):
                acc_s[hh] = alpha * acc_s[hh] + pv
                l_s[hh] = l_new
                m_s[hh] = m_new

        @pl.when(j < i)
        def _():
            block(False)

        @pl.when(j == i)
        def _():
            block(True)

        @pl.when(j == i)
        def _():
            for hh in range(HP):
                o_ref[:, hh * HEAD:(hh + 1) * HEAD] = (acc_s[hh] / l_s[hh]).astype(o_ref.dtype)
                lse_ref[hh] = m_s[hh] + jnp.log(l_s[hh])

    return pl.pallas_call(
        body, name=name,
        grid_spec=pltpu.PrefetchScalarGridSpec(
            num_scalar_prefetch=2, grid=(n_heads // HP, steps[0].shape[0]), in_specs=specs,
            out_specs=[pl.BlockSpec((t, HP * HEAD), lambda h, n, ii, jj: (ii[n], h)), _colspec(t, qs)],
            scratch_shapes=[pltpu.VMEM((HP, t, 1), F32), pltpu.VMEM((HP, t, 1), F32), pltpu.VMEM((HP, t, HEAD), F32)]),
        out_shape=[jax.ShapeDtypeStruct((seq, n_heads * HEAD), F32), jax.ShapeDtypeStruct((n_heads, seq, 1), F32)],
        compiler_params=_cparams(2),
    )(*steps, *ins)


def flash_bwd_dq(q, k, v, do, lse, delta, cum, *, n_heads, seq, t, scale, name):
    nb = seq // t
    qs, ks = _Q_BLOCK, _K_BLOCK
    steps = _triangle(nb, by_key=False)
    ins = [q[0], k[0], v[0], do, lse, delta]
    specs = [_hspec(q[1], t, q[2], q[3], qs), _hspec(k[1], t, k[2], k[3], ks), _hspec(v[1], t, v[2], v[3], ks),
             _hspec("cols", t, HEAD, 0, qs), _colspec(t, qs), _colspec(t, qs)]
    if cum is not None:
        ins += [cum[0], cum[1]]
        specs += [_colspec(t, qs), _rowspec(t, ks)]
    wq = q[2]
    n_out = 1 if cum is None else 2

    def body(ii_ref, jj_ref, *refs):
        q_ref, k_ref, v_ref, do_ref, lse_ref, dl_ref = refs[:6]
        outs = refs[-2 * n_out:-n_out]
        accs = refs[-n_out:]
        i, j = ii_ref[pl.program_id(1)], jj_ref[pl.program_id(1)]

        @pl.when(j == 0)
        def _():
            for a in accs:
                a[...] = jnp.zeros(a.shape, F32)

        def block(masked):
            for hh in range(HP):
                kb = _head(k_ref, k[1], hh, k[2]).astype(MXU)
                s = lax.dot_general(_head(q_ref, q[1], hh, q[2]).astype(MXU), kb, NT, preferred_element_type=F32) * scale
                if cum is not None:
                    s = s + (refs[6][hh] - refs[7][hh])
                if masked:
                    s = _causal(s, i, j, t)
                p = jnp.exp(s - lse_ref[hh])
                dp = lax.dot_general(_head(do_ref, "cols", hh, HEAD).astype(MXU), _head(v_ref, v[1], hh, v[2]).astype(MXU),
                                     NT, preferred_element_type=F32)
                ds = p * (dp - dl_ref[hh])
                accs[0][hh] += jnp.dot(ds.astype(MXU), kb, preferred_element_type=F32)
                if cum is not None:
                    accs[1][hh] += jnp.sum(ds, axis=1, keepdims=True)

        @pl.when(j < i)
        def _():
            block(False)

        @pl.when(j == i)
        def _():
            block(True)

        @pl.when(j == i)
        def _():
            outs[0][...] = accs[0][...] * scale
            if cum is not None:
                outs[1][...] = accs[1][...]

    out_specs = [pl.BlockSpec((HP, t, wq), lambda h, n, ii, jj: (h, ii[n], 0))]
    out_shape = [jax.ShapeDtypeStruct((n_heads, seq, wq), F32)]
    scratch = [pltpu.VMEM((HP, t, wq), F32)]
    if cum is not None:
        out_specs.append(_colspec(t, qs))
        out_shape.append(jax.ShapeDtypeStruct((n_heads, seq, 1), F32))
        scratch.append(pltpu.VMEM((HP, t, 1), F32))
    res = pl.pallas_call(
        body, name=name,
        grid_spec=pltpu.PrefetchScalarGridSpec(num_scalar_prefetch=2, grid=(n_heads // HP, steps[0].shape[0]),
                                               in_specs=specs, out_specs=out_specs, scratch_shapes=scratch),
        out_shape=out_shape, compiler_params=_cparams(2),
    )(*steps, *ins)
    return res[0] if cum is None else res


def flash_bwd_dkv(q, k, v, do, lse_row, delta_row, cum, *, n_heads, seq, t, scale, name):
    nb = seq // t
    qs, ks = _Q_BLOCK, _K_BLOCK
    steps = _triangle(nb, by_key=True)
    ins = [q[0], k[0], v[0], do, lse_row, delta_row]
    specs = [_hspec(q[1], t, q[2], q[3], qs), _hspec(k[1], t, k[2], k[3], ks), _hspec(v[1], t, v[2], v[3], ks),
             _hspec("cols", t, HEAD, 0, qs), _rowspec(t, qs), _rowspec(t, qs)]
    if cum is not None:
        ins += [cum[0], cum[1]]
        specs += [_colspec(t, ks), _rowspec(t, qs)]
    wk = k[2]
    n_out = 2 if cum is None else 3

    def body(ii_ref, jj_ref, *refs):
        q_ref, k_ref, v_ref, do_ref, lse_ref, dl_ref = refs[:6]
        outs = refs[-2 * n_out:-n_out]
        accs = refs[-n_out:]
        i, j = ii_ref[pl.program_id(1)], jj_ref[pl.program_id(1)]

        @pl.when(i == j)
        def _():
            for a in accs:
                a[...] = jnp.zeros(a.shape, F32)

        def block(masked):
            for hh in range(HP):
                qb = _head(q_ref, q[1], hh, q[2]).astype(MXU)
                dob = _head(do_ref, "cols", hh, HEAD).astype(MXU)
                st = lax.dot_general(_head(k_ref, k[1], hh, k[2]).astype(MXU), qb, NT, preferred_element_type=F32) * scale
                if cum is not None:
                    st = st + (refs[7][hh] - refs[6][hh])
                if masked:
                    st = _causal(st, i, j, t, transposed=True)
                pt = jnp.exp(st - lse_ref[hh])
                dpt = lax.dot_general(_head(v_ref, v[1], hh, v[2]).astype(MXU), dob, NT, preferred_element_type=F32)
                dst = pt * (dpt - dl_ref[hh])
                accs[0][hh] += jnp.dot(dst.astype(MXU), qb, preferred_element_type=F32)
                accs[1][hh] += jnp.dot(pt.astype(MXU), dob, preferred_element_type=F32)
                if cum is not None:
                    accs[2][hh] -= jnp.sum(dst, axis=1, keepdims=True)

        @pl.when(i > j)
        def _():
            block(False)

        @pl.when(i == j)
        def _():
            block(True)

        @pl.when(i == nb - 1)
        def _():
            outs[0][...] = accs[0][...] * scale
            for o, a in zip(outs[1:], accs[1:]):
                o[...] = a[...]

    out_specs = [pl.BlockSpec((HP, t, wk), lambda h, n, ii, jj: (h, jj[n], 0)),
                 pl.BlockSpec((HP, t, HEAD), lambda h, n, ii, jj: (h, jj[n], 0))]
    out_shape = [jax.ShapeDtypeStruct((n_heads, seq, wk), F32), jax.ShapeDtypeStruct((n_heads, seq, HEAD), F32)]
    scratch = [pltpu.VMEM((HP, t, wk), F32), pltpu.VMEM((HP, t, HEAD), F32)]
    if cum is not None:
        out_specs.append(_colspec(t, ks))
        out_shape.append(jax.ShapeDtypeStruct((n_heads, seq, 1), F32))
        scratch.append(pltpu.VMEM((HP, t, 1), F32))
    return pl.pallas_call(
        body, name=name,
        grid_spec=pltpu.PrefetchScalarGridSpec(num_scalar_prefetch=2, grid=(n_heads // HP, steps[0].shape[0]),
                                               in_specs=specs, out_specs=out_specs, scratch_shapes=scratch),
        out_shape=out_shape, compiler_params=_cparams(2),
    )(*steps, *ins)


def attn_delta(do, o, *, n_heads, seq, name):
    t = _tile(seq, 512)
    spec = pl.BlockSpec((t, HEAD), lambda h, m: (m, h))
    return rowwise(
        lambda a, b: jnp.sum(a.astype(F32) * b.astype(F32), axis=1, keepdims=True), [do, o], [spec, spec],
        jax.ShapeDtypeStruct((n_heads, seq, 1), F32), pl.BlockSpec((None, t, 1), lambda h, m: (h, m, 0)),
        (n_heads, seq // t), name)


def _dil_scores(q, kc, kp, n, scale):
    i = lax.broadcasted_iota(jnp.int32, (HEAD, HEAD), 0)
    j = lax.broadcasted_iota(jnp.int32, (HEAD, HEAD), 1)
    sc = lax.dot_general(q, kc, NT, preferred_element_type=F32) * scale
    sp = lax.dot_general(q, kp, NT, preferred_element_type=F32) * scale
    sc = jnp.where(j <= i, sc, NEG)
    sp = jnp.where(jnp.logical_and(j >= i, n > 0), sp, NEG)
    return sc, sp


def _strip_spec(length, n_heads, col_blocks, off):
    return pl.BlockSpec((length, HEAD), lambda r, h: (0, r * col_blocks + off + h))


def dil_fwd(q, k, v, *, seq, dil, n_heads, name):
    length = seq // dil
    nb = length // HEAD
    scale = HEAD ** -0.5
    view = lambda a: a.reshape(length, dil * a.shape[1])
    spec = _strip_spec(length, n_heads, n_heads, 0)

    def body(q_ref, k_ref, v_ref, o_ref, lse_ref):
        def step(n, carry):
            cur = pl.ds(pl.multiple_of(n * HEAD, HEAD), HEAD)
            prev = pl.ds(pl.multiple_of(jnp.maximum(n - 1, 0) * HEAD, HEAD), HEAD)
            qb = q_ref[cur, :].astype(MXU)
            sc, sp = _dil_scores(qb, k_ref[cur, :].astype(MXU), k_ref[prev, :].astype(MXU), n, scale)
            m = jnp.maximum(jnp.max(sc, axis=1, keepdims=True), jnp.max(sp, axis=1, keepdims=True))
            ec, ep = jnp.exp(sc - m), jnp.exp(sp - m)
            l = jnp.sum(ec, axis=1, keepdims=True) + jnp.sum(ep, axis=1, keepdims=True)
            o = jnp.dot((ec / l).astype(MXU), v_ref[cur, :].astype(MXU), preferred_element_type=F32)
            o = o + jnp.dot((ep / l).astype(MXU), v_ref[prev, :].astype(MXU), preferred_element_type=F32)
            o_ref[cur, :] = o
            lse_ref[cur, :] = jnp.broadcast_to(m + jnp.log(l), (HEAD, HEAD))
            return carry

        lax.fori_loop(0, nb, step, 0, unroll=min(8, nb))

    out = jax.ShapeDtypeStruct((length, dil * n_heads * HEAD), F32)
    o, lse = pl.pallas_call(
        body, name=name, grid=(dil, n_heads), in_specs=[spec, spec, spec], out_specs=[spec, spec], out_shape=[out, out],
        compiler_params=_cparams(2),
    )(view(q), view(k), view(v))
    return o.reshape(seq, -1), lse.reshape(seq, -1)


def dil_bwd(q, k, v, o, lse, do, dlse, *, seq, dil, n_heads, name):
    length = seq // dil
    nb = length // HEAD
    scale = HEAD ** -0.5
    view = lambda a: a.reshape(length, dil * a.shape[1])
    spec = _strip_spec(length, n_heads, n_heads, 0)

    def body(q_ref, k_ref, v_ref, o_ref, lse_ref, do_ref, dlse_ref, dq_ref, dk_ref, dv_ref):
        dk_ref[...] = jnp.zeros(dk_ref.shape, F32)
        dv_ref[...] = jnp.zeros(dv_ref.shape, F32)

        def step(n, carry):
            cur = pl.ds(pl.multiple_of(n * HEAD, HEAD), HEAD)
            prev = pl.ds(pl.multiple_of(jnp.maximum(n - 1, 0) * HEAD, HEAD), HEAD)
            qb = q_ref[cur, :].astype(MXU)
            kc, kp = k_ref[cur, :].astype(MXU), k_ref[prev, :].astype(MXU)
            vc, vp = v_ref[cur, :].astype(MXU), v_ref[prev, :].astype(MXU)
            sc, sp = _dil_scores(qb, kc, kp, n, scale)
            lse_b = jnp.max(lse_ref[cur, :], axis=1, keepdims=True)
            pc, pp = jnp.exp(sc - lse_b), jnp.exp(sp - lse_b)
            dob = do_ref[cur, :]
            shift = jnp.sum(dlse_ref[cur, :], axis=1, keepdims=True) - jnp.sum(dob * o_ref[cur, :], axis=1, keepdims=True)
            dob = dob.astype(MXU)
            dsc = pc * (lax.dot_general(dob, vc, NT, preferred_element_type=F32) + shift)
            dsp = pp * (lax.dot_general(dob, vp, NT, preferred_element_type=F32) + shift)
            dscb, dspb = dsc.astype(MXU), dsp.astype(MXU)
            dq = jnp.dot(dscb, kc, preferred_element_type=F32) + jnp.dot(dspb, kp, preferred_element_type=F32)
            dq_ref[cur, :] = dq * scale
            dk_ref[cur, :] += lax.dot_general(dscb, qb, TN, preferred_element_type=F32) * scale
            dv_ref[cur, :] += lax.dot_general(pc.astype(MXU), dob, TN, preferred_element_type=F32)
            dk_ref[prev, :] += lax.dot_general(dspb, qb, TN, preferred_element_type=F32) * scale
            dv_ref[prev, :] += lax.dot_general(pp.astype(MXU), dob, TN, preferred_element_type=F32)
            return carry

        lax.fori_loop(0, nb, step, 0, unroll=min(8, nb))

    out = jax.ShapeDtypeStruct((length, dil * n_heads * HEAD), F32)
    res = pl.pallas_call(
        body, name=name, grid=(dil, n_heads), in_specs=[spec] * 7, out_specs=[spec] * 3, out_shape=[out] * 3,
        compiler_params=_cparams(2),
    )(*[view(a) for a in (q, k, v, o, lse, do, dlse)])
    return [r.reshape(seq, -1) for r in res]


def _tri(n, kind):
    i = lax.broadcasted_iota(jnp.int32, (n, n), 0)
    j = lax.broadcasted_iota(jnp.int32, (n, n), 1)
    return jnp.where({"le": i <= j, "ge": i >= j}[kind], 1.0, 0.0).astype(F32)


def _block_matrix(n_rows, per_head, kind):
    r = lax.broadcasted_iota(jnp.int32, (n_rows, n_rows), 0)
    c = lax.broadcasted_iota(jnp.int32, (n_rows, n_rows), 1)
    same = (r // per_head) == (c // per_head)
    rel = {"lt": c < r, "gt": c > r, "all": c == c}[kind]
    return jnp.where(jnp.logical_and(same, rel), 1.0, 0.0).astype(F32)


def _lane_pick(x, lane):
    j = lax.broadcasted_iota(jnp.int32, x.shape, 1)
    return jnp.sum(jnp.where(j == lane, x, 0.0), axis=1, keepdims=True)


def _log_sigmoid(z):
    return jnp.minimum(z, 0.0) - jnp.log1p(jnp.exp(-jnp.abs(z)))


def fox_gate_fwd(z, bias_rows, per_head, name):
    n_rows = z.shape[0]

    def body(z_ref, b_ref, c_ref):
        logf = _log_sigmoid(z_ref[...] + b_ref[...])
        within = jnp.dot(logf, _tri(HEAD, "le"), precision=HI, preferred_element_type=F32)
        tot = jnp.broadcast_to(_lane_pick(within, HEAD - 1), (n_rows, HEAD))
        c_ref[...] = within + jnp.dot(_block_matrix(n_rows, per_head, "lt"), tot, precision=HI, preferred_element_type=F32)

    return pl.pallas_call(body, name=name, out_shape=jax.ShapeDtypeStruct(z.shape, F32),
                          compiler_params=pltpu.CompilerParams(vmem_limit_bytes=VMEM_LIMIT))(z, bias_rows)


def fox_gate_bwd(z, bias_rows, dcum_q, dcum_k, per_head, name):
    n_rows = z.shape[0]

    def body(z_ref, b_ref, dcq_ref, dck_ref, dz_ref, db_ref):
        within = jnp.dot(dcq_ref[...] + dck_ref[...], _tri(HEAD, "ge"), precision=HI, preferred_element_type=F32)
        tot = jnp.broadcast_to(_lane_pick(within, 0), (n_rows, HEAD))
        dlogf = within + jnp.dot(_block_matrix(n_rows, per_head, "gt"), tot, precision=HI, preferred_element_type=F32)
        dz = dlogf * jax.nn.sigmoid(-(z_ref[...] + b_ref[...]))
        dz_ref[...] = dz
        rs = jnp.broadcast_to(jnp.sum(dz, axis=1, keepdims=True), (n_rows, HEAD))
        db_ref[...] = jnp.dot(_block_matrix(n_rows, per_head, "all"), rs, precision=HI, preferred_element_type=F32)

    shp = jax.ShapeDtypeStruct(z.shape, F32)
    return pl.pallas_call(body, name=name, out_shape=[shp, shp],
                          compiler_params=pltpu.CompilerParams(vmem_limit_bytes=VMEM_LIMIT))(z, bias_rows, dcum_q, dcum_k)


def exchange(x, scatter, name):
    blk = x.shape[1:] if scatter else x.shape

    def body(x_ref, o_ref, send_sems, recv_sems, local_sem):
        mx, my, mc = lax.axis_index("x"), lax.axis_index("y"), lax.axis_index("c")
        me = 4 * mx + 2 * my + mc
        flip = lambda v, f: 1 - v if f else v
        local = pltpu.make_async_copy(x_ref.at[me] if scatter else x_ref, o_ref.at[me], local_sem)
        local.start()
        sends, recvs = [], []
        for n in range(1, NDEV):
            px, py, pc = flip(mx, n & 4), flip(my, n & 2), flip(mc, n & 1)
            p = 4 * px + 2 * py + pc
            sends.append(pltpu.make_async_remote_copy(
                src_ref=x_ref.at[p] if scatter else x_ref, dst_ref=o_ref.at[me], send_sem=send_sems.at[n - 1],
                recv_sem=recv_sems.at[n - 1], device_id=(px, py, pc), device_id_type=pl.DeviceIdType.MESH))
            recvs.append(pltpu.make_async_remote_copy(
                src_ref=x_ref.at[me] if scatter else x_ref, dst_ref=o_ref.at[p], send_sem=send_sems.at[n - 1],
                recv_sem=recv_sems.at[n - 1], device_id=(px, py, pc), device_id_type=pl.DeviceIdType.MESH))
        for cp in sends:
            cp.start()
        for cp in recvs:
            cp.wait_recv()
        for cp in sends:
            cp.wait_send()
        local.wait()

    hbm = pl.BlockSpec(memory_space=pltpu.HBM)
    return pl.pallas_call(
        body, name=name, in_specs=[hbm], out_specs=hbm, out_shape=jax.ShapeDtypeStruct((NDEV, *blk), x.dtype),
        scratch_shapes=[pltpu.SemaphoreType.DMA((NDEV - 1,)), pltpu.SemaphoreType.DMA((NDEV - 1,)), pltpu.SemaphoreType.DMA],
    )(x)


def _exchange_copies(x_refs, land_refs, send_sems, recv_sems, local_sems, scatter, with_recvs):
    mx, my, mc = lax.axis_index("x"), lax.axis_index("y"), lax.axis_index("c")
    me = 4 * mx + 2 * my + mc
    flip = lambda v, f: 1 - v if f else v
    local, sends, recvs = [], [], []
    for a, (x_ref, o_ref) in enumerate(zip(x_refs, land_refs)):
        local.append(pltpu.make_async_copy(x_ref.at[me] if scatter else x_ref, o_ref.at[me], local_sems.at[a]))
        for n in range(1, NDEV):
            px, py, pc = flip(mx, n & 4), flip(my, n & 2), flip(mc, n & 1)
            p = 4 * px + 2 * py + pc
            sem = (NDEV - 1) * a + n - 1
            mk = lambda src, dst: pltpu.make_async_remote_copy(
                src_ref=src, dst_ref=dst, send_sem=send_sems.at[sem], recv_sem=recv_sems.at[sem],
                device_id=(px, py, pc), device_id_type=pl.DeviceIdType.MESH)
            sends.append(mk(x_ref.at[p] if scatter else x_ref, o_ref.at[me]))
            if with_recvs:
                recvs.append(mk(x_ref.at[me] if scatter else x_ref, o_ref.at[p]))
    return local, sends, recvs


_HBM = pl.BlockSpec(memory_space=pltpu.HBM)
_SEM = pl.BlockSpec(memory_space=pltpu.SEMAPHORE)
_EFFECT = pltpu.SideEffectType.DATAFLOW_SIDE_EFFECTING


def exchange_start(xs, scatter, name):
    n = len(xs)
    lands = [jax.ShapeDtypeStruct((NDEV, *(x.shape[1:] if scatter else x.shape)), x.dtype) for x in xs]

    def body(*refs):
        x_refs, land_refs = refs[:n], refs[n:2 * n]
        send_sems, recv_sems, local_sems = refs[2 * n:2 * n + 3]
        token = refs[-1]
        local, sends, _ = _exchange_copies(x_refs, land_refs, send_sems, recv_sems, local_sems, scatter, False)
        for cp in local + sends:
            cp.start()
        token[...] = jnp.zeros(token.shape, token.dtype)

    n_sem = (NDEV - 1) * n
    out = pl.pallas_call(
        body, name=name,
        out_shape=(pltpu.SemaphoreType.DMA((n_sem,)), pltpu.SemaphoreType.DMA((n_sem,)), pltpu.SemaphoreType.DMA((n,)),
                   *[pltpu.HBM(x.shape, x.dtype) for x in xs], *[pltpu.HBM(s.shape, s.dtype) for s in lands],
                   jax.ShapeDtypeStruct((8, HEAD), F32)),
        in_specs=[_HBM] * (2 * n), out_specs=(_SEM, _SEM, _SEM, *[_HBM] * (2 * n), pl.BlockSpec(memory_space=pltpu.VMEM)),
        input_output_aliases={i: 3 + i for i in range(2 * n)},
        compiler_params=pltpu.CompilerParams(has_side_effects=_EFFECT),
    )(*[pltpu.with_memory_space_constraint(x, pltpu.HBM) for x in xs],
      *[pltpu.with_memory_space_constraint(lax.empty(s.shape, s.dtype), pltpu.HBM) for s in lands])
    return (out[:3], out[3:3 + n], out[3 + n:3 + 2 * n], scatter), out[-1]


def exchange_wait(handle, after, name):
    sems, xs, lands, scatter = handle
    n = len(xs)

    def body(*refs):
        x_refs, land_refs = refs[:n], refs[n:2 * n]
        send_sems, recv_sems, local_sems = refs[2 * n:2 * n + 3]
        local, sends, recvs = _exchange_copies(x_refs, land_refs, send_sems, recv_sems, local_sems, scatter, True)
        for cp in sends:
            cp.wait_send()
        for cp in recvs:
            cp.wait_recv()
        for cp in local:
            cp.wait()

    out = pl.pallas_call(
        body, name=name, out_shape=tuple(pltpu.HBM(a.shape, a.dtype) for a in (*xs, *lands)),
        in_specs=[_HBM] * (2 * n) + [_SEM] * 3 + [pl.BlockSpec(memory_space=pl.ANY)], out_specs=tuple([_HBM] * (2 * n)),
        input_output_aliases={i: i for i in range(2 * n)}, compiler_params=pltpu.CompilerParams(has_side_effects=_EFFECT),
    )(*xs, *lands, *sems, after)
    return list(out[n:])


def adamw(parts, w, m, v, name, dep=None):
    depth = len(parts)
    deps = [] if dep is None else [dep]
    n_parts, rows, cols = parts[0].shape
    t = rows
    for cand in (256, 128, 64, 32, 16, 8):
        if rows % cand == 0 and (n_parts * parts[0].dtype.itemsize + 7 * 4) * cand * cols <= ADAMW_STEP_BYTES:
            t = cand
            break
    nr = rows // t

    def body(*refs):
        p_refs = refs[:depth]
        w_ref, m_ref, v_ref = refs[depth:depth + 3]
        g_out, d_out, m_out, v_out = refs[depth + 3 + len(deps):]
        layer = pl.program_id(0)
        for i in range(depth):
            @pl.when(layer == i)
            def _(p=p_refs[i]):
                g = p[0].astype(F32)
                for j in range(1, n_parts):
                    g = g + p[j].astype(F32)
                m_new = ADAM_B1 * m_ref[...] + (1.0 - ADAM_B1) * g
                v_new = ADAM_B2 * v_ref[...] + (1.0 - ADAM_B2) * jnp.square(g)
                m_hat = m_new / (1.0 - ADAM_B1 ** ADAM_STEP)
                v_hat = v_new / (1.0 - ADAM_B2 ** ADAM_STEP)
                g_out[...] = g
                d_out[...] = -ADAM_LR * (m_hat / (jnp.sqrt(v_hat) + ADAM_EPS) + ADAM_WD * w_ref[...])
                m_out[...] = m_new
                v_out[...] = v_new

    def part_spec(i):
        return pl.BlockSpec((n_parts, t, cols), lambda l, r: (0, jnp.where(l < i, 0, jnp.where(l == i, r, nr - 1)), 0))

    spec = pl.BlockSpec((t, cols), lambda l, r: (l * nr + r, 0))
    out = jax.ShapeDtypeStruct((depth * rows, cols), F32)
    return pl.pallas_call(
        body, name=name, grid=(depth, nr),
        in_specs=[*[part_spec(i) for i in range(depth)], spec, spec, spec, *[_dep_spec(a) for a in deps]],
        out_specs=[spec] * 4, out_shape=[out] * 4, compiler_params=_cparams(2),
    )(*parts, w, m, v, *deps)


def sum_parts(parts, name):
    n_parts, rows, cols = parts.shape
    t = _tile(rows, 128)

    def fn(p):
        g = p[0].astype(F32)
        for i in range(1, n_parts):
            g = g + p[i].astype(F32)
        return g

    return rowwise(fn, [parts], [pl.BlockSpec((n_parts, t, cols), lambda r: (0, r, 0))],
                   jax.ShapeDtypeStruct((rows, cols), F32), pl.BlockSpec((t, cols), lambda r: (r, 0)), (rows // t,), name)


def _sd(shape, dtype=F32):
    return jax.ShapeDtypeStruct(shape, dtype)


def rms_fwd(x, g, name, col=0, width=None, dep=None):
    seq = x.shape[0]
    width = width or x.shape[1]
    t = _tile(seq, 512)
    return rowwise(_rms, [x, g.reshape(1, width)], [_rows(t, width, col), _whole((1, width))], _sd((seq, width), ACT),
                   _rows(t, width), (seq // t,), name, dep=dep)


def rms_bwd(x, g, dy, name, col=0, width=None, add=None, dx_dtype=F32):
    seq = x.shape[0]
    width = width or x.shape[1]
    t = _tile(seq, 512)
    ins, specs = [x, g.reshape(1, width), dy], [_rows(t, width, col), _whole((1, width)), _rows(t, width)]
    if add is None:
        fn = _rms_bwd
    else:
        ins.append(add)
        specs.append(_rows(t, width))

        def fn(x_, g_, dy_, add_):
            dx, dg = _rms_bwd(x_, g_, dy_)
            return dx + add_, dg
    return rowwise(fn, ins, specs, [_sd((seq, width), dx_dtype), _sd((1, width))], [_rows(t, width), _whole((1, width))],
                   (seq // t,), name, acc={1: (0,)})


def ffn_fwd(x, g, wg, wu, wd, l, tag, dep=None):
    seq, d = x.shape
    f = wg.shape[2]
    tm = _tile(seq, 1024)
    h = rms_fwd(x, g, f"{tag}_rms", dep=dep)
    hid_spec = pl.BlockSpec((None, tm, f), lambda j, m, k: (j, m, 0))
    up = lambda w, nm, **kw: mm(
        h, w, name=nm, dims=NT, grid=(NDEV, seq // tm, 1), nk=1,
        a_spec=pl.BlockSpec((tm, d), lambda j, m, k: (m, 0)),
        b_spec=pl.BlockSpec((None, None, f, d), lambda j, m, k: (j, l, 0, 0)), o_spec=hid_spec, **kw)
    a = up(wg, f"{tag}_gate", out_shape=_sd((NDEV, seq, f), ACT))
    b, hid = up(wu, f"{tag}_up", out_shape=[_sd((NDEV, seq, f), ACT)] * 2, extras=[a], extra_specs=[hid_spec],
                epilogue=lambda acc, a_: (acc, jax.nn.silu(a_.astype(F32)) * acc))
    tn = _tile(d, 1024)
    out = mm(hid, wd, name=f"{tag}_down", dims=NN, grid=(seq // tm, d // tn, NDEV), nk=NDEV,
             a_spec=pl.BlockSpec((None, tm, f), lambda m, n, k: (k, m, 0)),
             b_spec=pl.BlockSpec((None, None, f, tn), lambda m, n, k: (k, l, 0, n)),
             o_spec=pl.BlockSpec((tm, tn), lambda m, n, k: (m, n)), out_shape=_sd((seq, d)),
             extras=[x], extra_specs=[pl.BlockSpec((tm, tn), lambda m, n, k: (m, n))],
             epilogue=lambda acc, x_: x_ + 0.5 * acc)
    return out, (x, h, a, b, hid)


def ffn_bwd(dout, saved, g, wg, wu, wd, l, tag, send, dep=None):
    x, h, a, b, hid = saved
    seq, d = x.shape
    f = wg.shape[2]
    tm = _tile(seq, 1024)
    tk = _tile(seq, 1024)
    def act_bwd(acc, a_, b_):
        dh_, a_, b_ = 0.5 * acc, a_.astype(F32), b_.astype(F32)
        sig = jax.nn.sigmoid(a_)
        return dh_ * b_ * sig * (1.0 + a_ * (1.0 - sig)), dh_ * a_ * sig

    hid_spec = pl.BlockSpec((None, tm, f), lambda j, m, k: (j, m, 0))
    da, db = mm(dout, wd, name=f"{tag}_dhid", dims=NT, grid=(NDEV, seq // tm, 1), nk=1,
                a_spec=pl.BlockSpec((tm, d), lambda j, m, k: (m, 0)),
                b_spec=pl.BlockSpec((None, None, f, d), lambda j, m, k: (j, l, 0, 0)),
                o_spec=hid_spec, out_shape=[_sd((NDEV, seq, f), ACT)] * 2, extras=[a, b], extra_specs=[hid_spec] * 2,
                epilogue=act_bwd, dep=dep)
    tn = _tile(d, 1024)
    dw = lambda act, rhs, nm, epi: mm(
        act, rhs, name=nm, dims=TN, grid=(NDEV, d // tn, seq // tk), nk=seq // tk,
        a_spec=pl.BlockSpec((None, tk, f), lambda j, n, k: (j, k, 0)),
        b_spec=pl.BlockSpec((tk, tn), lambda j, n, k: (k, n)),
        o_spec=pl.BlockSpec((None, f, tn), lambda j, n, k: (j, 0, n)), out_shape=_sd((NDEV, f, d), COMM), epilogue=epi)
    dwd = dw(hid, dout, f"{tag}_dwd", lambda acc: 0.5 * acc)
    dwg, dwu = dw(da, h, f"{tag}_dwg", None), dw(db, h, f"{tag}_dwu", None)
    token = send(dwg, dwu, dwd)

    def dh_body(da_ref, db_ref, wg_ref, wu_ref, dep_ref, o_ref, acc_ref):
        k = pl.program_id(2)
        part = jnp.dot(da_ref[...].astype(MXU), wg_ref[...].astype(MXU), preferred_element_type=F32)
        part = part + jnp.dot(db_ref[...].astype(MXU), wu_ref[...].astype(MXU), preferred_element_type=F32)

        @pl.when(k == 0)
        def _():
            acc_ref[...] = part

        @pl.when(k > 0)
        def _():
            acc_ref[...] += part

        @pl.when(k == NDEV - 1)
        def _():
            o_ref[...] = acc_ref[...]

    act_spec = pl.BlockSpec((None, tm, f), lambda m, n, k: (k, m, 0))
    w_spec = pl.BlockSpec((None, None, f, tn), lambda m, n, k: (k, l, 0, n))
    dh = pl.pallas_call(
        dh_body, name=f"{tag}_dh", grid=(seq // tm, d // tn, NDEV),
        in_specs=[act_spec, act_spec, w_spec, w_spec, _dep_spec(token)],
        out_specs=pl.BlockSpec((tm, tn), lambda m, n, k: (m, n)), out_shape=_sd((seq, d)),
        scratch_shapes=[pltpu.VMEM((tm, tn), F32)], compiler_params=_cparams(3),
    )(da, db, wg, wu, token)
    return rms_bwd(x, g, dh, f"{tag}_drms", add=dout)


def _dense(a, w, l, name, out_dtype=F32, extras=(), epilogue=None):
    seq, kdim = a.shape
    n = w.shape[3]
    w2 = w.reshape(kdim, n)
    tm, tn, tk = _tile(seq, 1024), _tile(n, 1024), _tile(kdim, 1024)
    nk = kdim // tk
    return mm(a, w2, name=name, dims=NN, grid=(seq // tm, n // tn, nk), nk=nk,
              a_spec=pl.BlockSpec((tm, tk), lambda m, c, k: (m, k)),
              b_spec=pl.BlockSpec((tk, tn), lambda m, c, k: (k, c)),
              o_spec=pl.BlockSpec((tm, tn), lambda m, c, k: (m, c)), out_shape=_sd((seq, n), out_dtype),
              extras=list(extras), extra_specs=[pl.BlockSpec((tm, tn), lambda m, c, k: (m, c))] * len(extras),
              epilogue=epilogue)


def _dense_dx(dy, w, l, name, extras=(), epilogue=None, dep=None):
    seq, n = dy.shape
    kb = w.shape[2]
    tm = _tile(seq, 1024)
    return mm(dy, w, name=name, dims=NT, grid=(seq // tm, NDEV, 1), nk=1,
              a_spec=pl.BlockSpec((tm, n), lambda m, j, k: (m, 0)),
              b_spec=pl.BlockSpec((None, None, kb, n), lambda m, j, k: (j, l, 0, 0)),
              o_spec=pl.BlockSpec((tm, kb), lambda m, j, k: (m, j)), out_shape=_sd((seq, NDEV * kb)),
              extras=list(extras), extra_specs=[pl.BlockSpec((tm, kb), lambda m, j, k: (m, j))] * len(extras),
              epilogue=epilogue, dep=dep)


def _dense_dw(a, dy, kb, name):
    seq, n = dy.shape
    rows = NDEV * kb
    tk, tn, tr = _tile(seq, 1024), _tile(n, 1024), _tile(rows, 1024)
    out = mm(a, dy, name=name, dims=TN, grid=(rows // tr, n // tn, seq // tk), nk=seq // tk,
             a_spec=pl.BlockSpec((tk, tr), lambda j, c, k: (k, j)),
             b_spec=pl.BlockSpec((tk, tn), lambda j, c, k: (k, c)),
             o_spec=pl.BlockSpec((tr, tn), lambda j, c, k: (j, c)), out_shape=_sd((rows, n), COMM))
    return out.reshape(NDEV, kb, n)


def _heads_up(a, w, l, name, out_dtype):
    seq, r = a.shape
    c = w.shape[3]
    tm = _tile(seq, 1024)
    return mm(a, w, name=name, dims=NN, grid=(NDEV, seq // tm, 1), nk=1,
              a_spec=pl.BlockSpec((tm, r), lambda j, m, k: (m, 0)),
              b_spec=pl.BlockSpec((None, None, r, c), lambda j, m, k: (j, l, 0, 0)),
              o_spec=pl.BlockSpec((None, tm, c), lambda j, m, k: (j, m, 0)), out_shape=_sd((NDEV, seq, c), out_dtype))


def _heads_dx(dy, w, l, name):
    _, seq, c = dy.shape
    r = w.shape[2]
    tm = _tile(seq, 1024)
    return mm(dy, w, name=name, dims=NT, grid=(seq // tm, 1, NDEV), nk=NDEV,
              a_spec=pl.BlockSpec((None, tm, c), lambda m, n, k: (k, m, 0)),
              b_spec=pl.BlockSpec((None, None, r, c), lambda m, n, k: (k, l, 0, 0)),
              o_spec=pl.BlockSpec((tm, r), lambda m, n, k: (m, 0)), out_shape=_sd((seq, r)))


def _heads_dw(a, dy, name):
    seq, r = a.shape
    c = dy.shape[2]
    tk = _tile(seq, 1024)
    return mm(a, dy, name=name, dims=TN, grid=(NDEV, 1, seq // tk), nk=seq // tk,
              a_spec=pl.BlockSpec((tk, r), lambda j, n, k: (k, 0)),
              b_spec=pl.BlockSpec((None, tk, c), lambda j, n, k: (j, k, 0)),
              o_spec=pl.BlockSpec((None, r, c), lambda j, n, k: (j, 0, 0)), out_shape=_sd((NDEV, r, c), COMM))


C_FQ, C_FK, C_FV, C_CQ, C_CKV, C_DQ, C_DK, C_DV = range(8)
MAIN_W = 8 * 512
TAIL_W = 128


def split_w_in(w):
    fq, fk, fv, fl, cq, ckv, kr, dq, dk, dv = jnp.split(w, [512, 1024, 1536, 1540, 2052, 2564, 2628, 3140, 3652], axis=-1)
    main = jnp.concatenate([fq, fk, fv, cq, ckv, dq, dk, dv], axis=-1)
    pad = jnp.zeros((*w.shape[:-1], TAIL_W - 68), w.dtype)
    return main, jnp.concatenate([kr, fl, pad], axis=-1)


def merge_w_in(main, tail):
    fq, fk, fv, cq, ckv, dq, dk, dv = jnp.split(main, 8, axis=-1)
    return jnp.concatenate([fq, fk, fv, tail[..., 64:68], cq, ckv, tail[..., 0:64], dq, dk, dv], axis=-1)


def mixer_fwd(x, p, l, consts, dep=None):
    seq, d = x.shape
    nfox, nmla, ndil = 4, 8, 4
    cos_m, sin_m, cos_p, sin_p = consts
    t = _tile(seq, 512)
    tf = _tile(seq, 1024)
    h = rms_fwd(x, p["mix_norm"], f"mix{l}_rms", dep=dep)
    proj = _dense(h, p["w_in_main"], 0, f"mix{l}_proj")
    tail = _dense(h, p["w_in_tail"], 0, f"mix{l}_tail")

    nb = seq // HEAD
    z = tail[:, 64:68].T.reshape(nfox * nb, HEAD)
    bias_rows = jnp.repeat(p["fox_forget_bias"], nb).reshape(nfox * nb, 1)
    cum = fox_gate_fwd(z, bias_rows, nb, f"mix{l}_gate").reshape(nfox, seq)
    cum2 = (cum.reshape(nfox, seq, 1), cum.reshape(nfox, 1, seq))
    fox_qkv = ((proj, "cols", HEAD, C_FQ * 4), (proj, "cols", HEAD, C_FK * 4), (proj, "cols", HEAD, C_FV * 4))
    out_a, lse_a = flash_fwd(*fox_qkv, cum2, n_heads=nfox, seq=seq, t=tf, scale=HEAD ** -0.5, name=f"mix{l}_fox")

    cq = rms_fwd(proj, p["mla_q_norm"], f"mix{l}_cq", col=C_CQ, width=512)
    ckv = rms_fwd(proj, p["mla_kv_norm"], f"mix{l}_ckv", col=C_CKV, width=512)
    q_raw = _heads_up(cq, p["mla_w_uq"], 0, f"mix{l}_uq", F32)
    kv = _heads_up(ckv, p["mla_w_ukv"], 0, f"mix{l}_ukv", ACT)

    def mla_prep(q_, kv_, tail_, cos_, sin_, q_out, k_out):
        perm = _swap_matrix(MLA_ROPE)
        c, s = cos_[...], sin_[...]
        q_out[:, 0:HEAD] = q_[:, 0:HEAD].astype(q_out.dtype)
        q_out[:, HEAD:MLA_QK] = _rope(q_[:, HEAD:MLA_QK], c, s, perm).astype(q_out.dtype)
        k_out[:, 0:HEAD] = kv_[:, 0:HEAD].astype(k_out.dtype)
        k_out[:, HEAD:MLA_QK] = _rope(tail_[:, 0:MLA_ROPE], c, s, perm).astype(k_out.dtype)

    hs = lambda w: pl.BlockSpec((None, t, w), lambda hh, m: (hh, m, 0))
    rs = lambda w: pl.BlockSpec((t, w), lambda hh, m: (m, 0))
    q_b, k_b = pl.pallas_call(
        lambda q_, kv_, tl_, c_, s_, qo, ko: mla_prep(q_[...], kv_[...], tl_[...], c_, s_, qo, ko),
        name=f"mix{l}_mla_prep", grid=(nmla, seq // t),
        in_specs=[hs(MLA_QK), hs(2 * HEAD), rs(TAIL_W), rs(MLA_ROPE), rs(MLA_ROPE)], out_specs=[hs(MLA_QK), hs(MLA_QK)],
        out_shape=[_sd((nmla, seq, MLA_QK), ACT)] * 2, compiler_params=_cparams(2),
    )(q_raw, kv, tail, cos_m, sin_m)
    mla_qkv = ((q_b, "heads", MLA_QK, 0), (k_b, "heads", MLA_QK, 0), (kv, "heads", HEAD, 1))
    out_b, lse_b = flash_fwd(*mla_qkv, None, n_heads=nmla, seq=seq, t=tf, scale=MLA_QK ** -0.5, name=f"mix{l}_mla")

    wd_ = ndil * HEAD

    def dil_prep(q_, k_, c_, s_):
        perm = _pad_perm(PARTIAL_ROPE)
        rot = lambda a: jnp.concatenate(
            [_rope(a[:, i * HEAD:(i + 1) * HEAD], c_, s_, perm) for i in range(ndil)], axis=1)
        return rot(q_), rot(k_)

    dq_r, dk_r = rowwise(dil_prep, [proj, proj, cos_p, sin_p],
                         [_rows(t, wd_, C_DQ), _rows(t, wd_, C_DK), _rows(t, HEAD), _rows(t, HEAD)],
                         [_sd((seq, wd_), ACT)] * 2, [_rows(t, wd_)] * 2, (seq // t,), f"mix{l}_dil_prep")
    dv = proj[:, C_DV * 512:(C_DV + 1) * 512]
    branches = [dil_fwd(dq_r, dk_r, dv, seq=seq, dil=dl, n_heads=ndil, name=f"mix{l}_dil{dl}") for dl in DIL_BRANCHES]

    def mix(o1, o2, o3, l1, l2, l3):
        m = jnp.maximum(jnp.maximum(l1, l2), l3)
        e1, e2, e3 = jnp.exp(l1 - m), jnp.exp(l2 - m), jnp.exp(l3 - m)
        return (e1 * o1 + e2 * o2 + e3 * o3) / (e1 + e2 + e3)

    out_c = rowwise(mix, [b[0] for b in branches] + [b[1] for b in branches], [_rows(t, wd_)] * 6, _sd((seq, wd_)),
                    _rows(t, wd_), (seq // t,), f"mix{l}_dil_mix")

    mixed = jnp.concatenate([out_a, out_b, out_c], axis=1)
    out = _dense(mixed, p["w_out"], 0, f"mix{l}_out", extras=[x], epilogue=lambda acc, x_: x_ + acc)
    saved = dict(x=x, h=h, proj=proj, tail=tail, z=z, bias_rows=bias_rows, cum2=cum2, out_a=out_a, lse_a=lse_a, cq=cq,
                 ckv=ckv, q_raw=q_raw, kv=kv, q_b=q_b, k_b=k_b, out_b=out_b, lse_b=lse_b, dq_r=dq_r, dk_r=dk_r, dv=dv,
                 branches=branches, out_c=out_c, mixed=mixed)
    return out, saved


def _pad_perm(n):
    i = lax.broadcasted_iota(jnp.int32, (HEAD, HEAD), 0)
    j = lax.broadcasted_iota(jnp.int32, (HEAD, HEAD), 1)
    inside = jnp.logical_and(i < n, j < n)
    return jnp.where(jnp.logical_and(inside, ((i + n // 2) % n) == j), 1.0, 0.0).astype(F32)


def mixer_bwd(dout, sv, p, l, consts, dep=None):
    seq, d = dout.shape
    nfox, nmla, ndil = 4, 8, 4
    cos_m, sin_m, cos_p, sin_p = consts
    t = _tile(seq, 512)
    tf = _tile(seq, 1024)
    nb = seq // HEAD
    proj, tail = sv["proj"], sv["tail"]
    dmixed = _dense_dx(dout, p["w_out"], 0, f"mix{l}_dmixed", dep=dep)
    dw_out = _dense_dw(sv["mixed"], dout, d // NDEV, f"mix{l}_dw_out")
    do_a, do_b, do_c = dmixed[:, 0:512], dmixed[:, 512:1536], dmixed[:, 1536:2048]

    fox_qkv = ((proj, "cols", HEAD, C_FQ * 4), (proj, "cols", HEAD, C_FK * 4), (proj, "cols", HEAD, C_FV * 4))
    delta_a = attn_delta(do_a, sv["out_a"], n_heads=nfox, seq=seq, name=f"mix{l}_fox_delta")
    row = lambda a: a.reshape(a.shape[0], 1, seq)
    dfq, dcum_q = flash_bwd_dq(*fox_qkv, do_a, sv["lse_a"], delta_a, sv["cum2"], n_heads=nfox, seq=seq, t=tf,
                               scale=HEAD ** -0.5, name=f"mix{l}_fox_dq")
    dfk, dfv, dcum_k = flash_bwd_dkv(*fox_qkv, do_a, row(sv["lse_a"]), row(delta_a), sv["cum2"], n_heads=nfox, seq=seq,
                                     t=tf, scale=HEAD ** -0.5, name=f"mix{l}_fox_dkv")
    dz, dbias = fox_gate_bwd(sv["z"], sv["bias_rows"], dcum_q.reshape(nfox * nb, HEAD), dcum_k.reshape(nfox * nb, HEAD),
                             nb, f"mix{l}_dgate")
    d_fox_bias = dbias.reshape(nfox, nb, HEAD)[:, 0, 0]
    dfl = dz.reshape(nfox, seq).T
    unheads = lambda a: a.transpose(1, 0, 2).reshape(seq, -1)

    mla_qkv = ((sv["q_b"], "heads", MLA_QK, 0), (sv["k_b"], "heads", MLA_QK, 0), (sv["kv"], "heads", HEAD, 1))
    delta_b = attn_delta(do_b, sv["out_b"], n_heads=nmla, seq=seq, name=f"mix{l}_mla_delta")
    dq_b = flash_bwd_dq(*mla_qkv, do_b, sv["lse_b"], delta_b, None, n_heads=nmla, seq=seq, t=tf, scale=MLA_QK ** -0.5,
                        name=f"mix{l}_mla_dq")
    dk_b, dv_b = flash_bwd_dkv(*mla_qkv, do_b, row(sv["lse_b"]), row(delta_b), None, n_heads=nmla, seq=seq, t=tf,
                               scale=MLA_QK ** -0.5, name=f"mix{l}_mla_dkv")

    def mla_unprep(dq_, dk_, dv_, cos_, sin_, dq_out, dkv_out, dkr_out):
        perm = _swap_matrix(MLA_ROPE)
        c, s = cos_[...], sin_[...]
        dq_out[:, 0:HEAD] = dq_[:, 0:HEAD].astype(dq_out.dtype)
        dq_out[:, HEAD:MLA_QK] = _rope_t(dq_[:, HEAD:MLA_QK], c, s, perm).astype(dq_out.dtype)
        dkv_out[:, 0:HEAD] = dk_[:, 0:HEAD].astype(dkv_out.dtype)
        dkv_out[:, HEAD:2 * HEAD] = dv_.astype(dkv_out.dtype)
        dkr = _rope_t(dk_[:, HEAD:MLA_QK], c, s, perm)
        first = pl.program_id(1) == 0

        @pl.when(first)
        def _():
            dkr_out[...] = dkr

        @pl.when(jnp.logical_not(first))
        def _():
            dkr_out[...] += dkr

    hs = lambda w: pl.BlockSpec((None, t, w), lambda m, hh: (hh, m, 0))
    rs = lambda w: pl.BlockSpec((t, w), lambda m, hh: (m, 0))
    dq_raw, dkv, dk_r = pl.pallas_call(
        lambda a, b, c, cs, sn, o1, o2, o3: mla_unprep(a[...], b[...], c[...], cs, sn, o1, o2, o3),
        name=f"mix{l}_mla_unprep", grid=(seq // t, nmla),
        in_specs=[hs(MLA_QK), hs(MLA_QK), hs(HEAD), rs(MLA_ROPE), rs(MLA_ROPE)],
        out_specs=[hs(MLA_QK), hs(2 * HEAD), rs(MLA_ROPE)],
        out_shape=[_sd((nmla, seq, MLA_QK), ACT), _sd((nmla, seq, 2 * HEAD), ACT), _sd((seq, MLA_ROPE))],
        compiler_params=_cparams(2),
    )(dq_b, dk_b, dv_b, cos_m, sin_m)
    dcq_n = _heads_dx(dq_raw, p["mla_w_uq"], 0, f"mix{l}_dcq")
    dckv_n = _heads_dx(dkv, p["mla_w_ukv"], 0, f"mix{l}_dckv")
    dw_uq = _heads_dw(sv["cq"], dq_raw, f"mix{l}_dw_uq")
    dw_ukv = _heads_dw(sv["ckv"], dkv, f"mix{l}_dw_ukv")
    dcq, dg_q = rms_bwd(proj, p["mla_q_norm"], dcq_n, f"mix{l}_dcq_rms", col=C_CQ, width=512)
    dckv, dg_kv = rms_bwd(proj, p["mla_kv_norm"], dckv_n, f"mix{l}_dckv_rms", col=C_CKV, width=512)

    wd_ = ndil * HEAD
    outs = [b[0] for b in sv["branches"]]
    lses = [b[1] for b in sv["branches"]]

    def mix_bwd(do_, o1, o2, o3, l1, l2, l3):
        m = jnp.maximum(jnp.maximum(l1, l2), l3)
        e1, e2, e3 = jnp.exp(l1 - m), jnp.exp(l2 - m), jnp.exp(l3 - m)
        z_ = e1 + e2 + e3
        w1, w2, w3 = e1 / z_, e2 / z_, e3 / z_
        out = w1 * o1 + w2 * o2 + w3 * o3
        return (w1 * do_, w2 * do_, w3 * do_, do_ * w1 * (o1 - out), do_ * w2 * (o2 - out), do_ * w3 * (o3 - out))

    mb = rowwise(mix_bwd, [do_c] + outs + lses, [_rows(t, wd_)] * 7, [_sd((seq, wd_))] * 6, [_rows(t, wd_)] * 6,
                 (seq // t,), f"mix{l}_dil_dmix")
    grads = [dil_bwd(sv["dq_r"], sv["dk_r"], sv["dv"], outs[i], lses[i], mb[i], mb[3 + i], seq=seq, dil=dl,
                     n_heads=ndil, name=f"mix{l}_dil{dl}_bwd") for i, dl in enumerate(DIL_BRANCHES)]

    def dil_unprep(q1, q2, q3, k1, k2, k3, v1, v2, v3, c_, s_):
        perm = _pad_perm(PARTIAL_ROPE)
        rot_t = lambda a: jnp.concatenate(
            [_rope_t(a[:, i * HEAD:(i + 1) * HEAD], c_, s_, perm) for i in range(ndil)], axis=1)
        return rot_t(q1 + q2 + q3), rot_t(k1 + k2 + k3), v1 + v2 + v3

    ddq, ddk, ddv = rowwise(dil_unprep, [g[0] for g in grads] + [g[1] for g in grads] + [g[2] for g in grads] + [cos_p, sin_p],
                            [_rows(t, wd_)] * 9 + [_rows(t, HEAD)] * 2, [_sd((seq, wd_))] * 3, [_rows(t, wd_)] * 3,
                            (seq // t,), f"mix{l}_dil_unprep")

    dproj = jnp.concatenate([unheads(dfq), unheads(dfk), unheads(dfv), dcq, dckv, ddq, ddk, ddv], axis=1).astype(ACT)
    dtail = jnp.concatenate([dk_r, dfl, jnp.zeros((seq, TAIL_W - 68), F32)], axis=1)
    dh = _dense_dx(dproj, p["w_in_main"], 0, f"mix{l}_dh_main")
    dh = _dense_dx(dtail, p["w_in_tail"], 0, f"mix{l}_dh_tail", extras=[dh], epilogue=lambda acc, prev: acc + prev)
    dw_main = _dense_dw(sv["h"], dproj, d // NDEV, f"mix{l}_dw_in_main")
    dw_tail = _dense_dw(sv["h"], dtail, d // NDEV, f"mix{l}_dw_in_tail")
    dx, dg_mix = rms_bwd(sv["x"], p["mix_norm"], dh, f"mix{l}_drms", add=dout)
    return dx, dict(mix_norm=dg_mix, w_in_main=dw_main, w_in_tail=dw_tail, fox_forget_bias=d_fox_bias, mla_q_norm=dg_q,
                    mla_kv_norm=dg_kv, mla_w_uq=dw_uq, mla_w_ukv=dw_ukv, w_out=dw_out)


def loss_head(x, g, target, name):
    seq, d = x.shape
    t = _tile(seq, 512)

    def fn(x_, g_, tgt):
        err = _rms(x_, g_) - tgt
        part = 0.5 * jnp.sum(jnp.mean(err * err, axis=-1, keepdims=True), axis=0, keepdims=True)
        dx, dg = _rms_bwd(x_, g_, err / d)
        return jnp.broadcast_to(part, (1, HEAD)), dx, dg

    return rowwise(fn, [x, g.reshape(1, d), target], [_rows(t, d), _whole((1, d)), _rows(t, d)],
                   [_sd((1, HEAD)), _sd((seq, d)), _sd((1, d))], [_whole((1, HEAD)), _rows(t, d), _whole((1, d))],
                   (seq // t,), name, acc={0: (0,), 2: (0,)})


BIG = ("ffn1_w_gate", "ffn1_w_up", "ffn1_w_down", "w_in", "mla_w_uq", "mla_w_ukv", "w_out", "ffn2_w_gate", "ffn2_w_up",
       "ffn2_w_down")
GROUPS = {
    "ffn1": ("ffn1_w_gate", "ffn1_w_up", "ffn1_w_down"),
    "mix": ("w_in_main", "w_in_tail", "mla_w_uq", "mla_w_ukv", "w_out"),
    "ffn2": ("ffn2_w_gate", "ffn2_w_up", "ffn2_w_down"),
}
TRANSPOSED = ("ffn1_w_gate", "ffn1_w_up", "ffn2_w_gate", "ffn2_w_up")
PREFETCH = 2
SMALL_D = ("ffn1_norm", "mix_norm", "ffn2_norm")
WEIGHTS = ("ffn1_norm", "ffn1_w_gate", "ffn1_w_up", "ffn1_w_down", "mix_norm", "w_in", "fox_forget_bias", "mla_q_norm",
           "mla_kv_norm", "mla_w_uq", "mla_w_ukv", "w_out", "ffn2_norm", "ffn2_w_gate", "ffn2_w_up", "ffn2_w_down",
           "final_norm")


def pack_small(vals, depth, d):
    rows = [vals[n].reshape(depth, d) for n in SMALL_D]
    rows.append(vals["final_norm"].reshape(1, d))
    qk = jnp.concatenate([vals["mla_q_norm"].reshape(-1), vals["mla_kv_norm"].reshape(-1)])
    rows.append(jnp.pad(qk, (0, -qk.shape[0] % d)).reshape(-1, d))
    last = jnp.concatenate([vals["fox_forget_bias"].reshape(-1), vals["loss"].reshape(-1)])
    rows.append(jnp.pad(last, (0, d - last.shape[0])).reshape(1, d))
    out = jnp.concatenate(rows, axis=0)
    return jnp.pad(out, ((0, -out.shape[0] % 8), (0, 0)))


def unpack_small(a, depth, d, rank):
    out, r = {}, 0
    for n in SMALL_D:
        out[n] = a[r:r + depth]
        r += depth
    out["final_norm"] = a[r]
    r += 1
    n_qk = -(-2 * depth * rank // d)
    qk = a[r:r + n_qk].reshape(-1)[:2 * depth * rank].reshape(2, depth, rank)
    out["mla_q_norm"], out["mla_kv_norm"] = qk[0], qk[1]
    r += n_qk
    out["fox_forget_bias"] = a[r, :depth * 4].reshape(depth, 4)
    out["loss"] = a[r, depth * 4]
    return out


def step(x, target, w, m, v):
    depth = w["ffn1_norm"].shape[0]
    seq, d = x.shape[1], x.shape[2]
    rank = w["mla_q_norm"].shape[1]
    x = x.reshape(seq, d)
    target = target.reshape(seq, d)

    consts = (*rope_tables(seq, MLA_ROPE), *[jnp.pad(a, ((0, 0), (0, HEAD - PARTIAL_ROPE)), constant_values=c)
                                             for a, c in zip(rope_tables(seq, PARTIAL_ROPE), (1.0, 0.0))])
    order = [(l, k) for l in range(depth) for k in GROUPS]
    small_of = lambda l: {n: w[n][l] for n in ("mix_norm", "fox_forget_bias", "mla_q_norm", "mla_kv_norm")}

    view = lambda n, a: a.transpose(0, 2, 1) if n in TRANSPOSED else a

    def shards(l, k):
        if k == "mix":
            main, tail = split_w_in(w["w_in"][l:l + 1])
            xs = [main, tail, *[w[n][l:l + 1] for n in GROUPS[k][2:]]]
        else:
            xs = [view(n, w[n])[l:l + 1] for n in GROUPS[k]]
        return [a.astype(COMM) for a in xs]

    handles, fresh = {}, []

    def launch(i, dep):
        l, k = order[i]
        xs = shards(l, k)
        if dep is not None:
            xs = lax.optimization_barrier((xs, dep))[0]
        handles[i], token = exchange_start(xs, False, f"gather_start_{k}{l}")
        fresh.append(token)

    def take_tokens():
        tok = functools.reduce(jnp.add, fresh) if fresh else None
        fresh.clear()
        return tok

    launched = min(2, len(order))
    for i in range(launched):
        launch(i, None)
    gathered, saved = {}, {}
    for i, (l, k) in enumerate(order):
        wts = dict(zip(GROUPS[k], exchange_wait(handles[i], fresh[-1] if i == 0 else x, f"gather_wait_{k}{l}")))
        gathered[l, k] = wts
        while launched < min(len(order), i + 1 + PREFETCH):
            launch(launched, (x, wts[GROUPS[k][0]]))
            launched += 1
        tok = take_tokens()
        if k == "mix":
            x, saved[l, k] = mixer_fwd(x, {**wts, **small_of(l)}, l, consts, dep=tok)
        else:
            x, saved[l, k] = ffn_fwd(x, w[f"{k}_norm"][l], wts[f"{k}_w_gate"], wts[f"{k}_w_up"], wts[f"{k}_w_down"], 0,
                                     f"{k}_{l}", dep=tok)
    loss, dx, d_final = loss_head(x, w["final_norm"], target, "loss_head")

    small = {n: [None] * depth for n in SMALL_D + ("mla_q_norm", "mla_kv_norm", "fox_forget_bias")}
    pending, tok = [], None

    def send(l, k):
        def start(*grads):
            handle, token = exchange_start(list(grads), True, f"scatter_start_{k}{l}")
            pending.append((l, k, handle))
            return token
        return start

    for l, k in reversed(order):
        wts = gathered[l, k]
        if k == "mix":
            dx, gm = mixer_bwd(dx, saved[l, k], {**wts, **small_of(l)}, l, consts, dep=tok)
            tok = send(l, k)(*[gm[n] for n in GROUPS[k]])
        else:
            dx, dg = ffn_bwd(dx, saved[l, k], w[f"{k}_norm"][l], wts[f"{k}_w_gate"], wts[f"{k}_w_up"], wts[f"{k}_w_down"],
                             0, f"{k}_{l}", send(l, k), dep=tok)
            gm, tok = {f"{k}_norm": dg}, None
        for n in small:
            if n in gm:
                small[n][l] = gm[n]
    big = {n: [None] * depth for k in GROUPS for n in GROUPS[k]}

    def land(l, k, handle, after):
        for n, a in zip(GROUPS[k], exchange_wait(handle, after, f"scatter_wait_{k}{l}")):
            big[n][l] = a

    for l, k, handle in pending[:-1]:
        land(l, k, handle, dx)

    out = {}

    def update(name, parts, shape, dep=None):
        shape_v = (shape[0], shape[2], shape[1]) if name in TRANSPOSED else shape
        flat = lambda a: view(name, a).reshape(-1, shape_v[-1])
        res = adamw(parts, flat(w[name]), flat(m[name]), flat(v[name]), f"adamw_{name}", dep=dep)
        for kind, r in zip(("grad", "delta", "new_m", "new_v"), res):
            out[f"{kind}_{name}"] = view(name, r.reshape(shape_v))

    last = GROUPS[pending[-1][1]]
    for n in BIG:
        if n != "w_in" and n not in last:
            update(n, [a.reshape(NDEV, -1, a.shape[-1]) for a in big[n]], w[n].shape, dep=tok)
    g_in = [merge_w_in(sum_parts(big["w_in_main"][l].reshape(NDEV, -1, MAIN_W), f"sum_w_in_main{l}"),
                       sum_parts(big["w_in_tail"][l].reshape(NDEV, -1, TAIL_W), f"sum_w_in_tail{l}"))[None]
            for l in range(depth)]
    update("w_in", g_in, w["w_in"].shape, dep=tok)
    names = list(out)
    out.update(zip(names, lax.optimization_barrier([out[n] for n in names])))
    land(*pending[-1], out[names[0]])
    for n in last:
        update(n, [a.reshape(NDEV, -1, a.shape[-1]) for a in big[n]], w[n].shape)
    out["grad_x"] = dx.reshape(1, seq, d)

    part = {n: jnp.stack(g).reshape(depth, -1) for n, g in small.items()}
    part["final_norm"], part["loss"] = d_final, loss[0, 0:1]
    parts = exchange(pack_small(part, depth, d), False, "gather_small")
    zero = jnp.zeros((1,), F32)
    packed = [pack_small({**{n: a[n] for n in part if n != "loss"}, "loss": zero}, depth, d) for a in (w, m, v)]
    res = [unpack_small(r, depth, d, rank) for r in adamw([parts], *packed, "adamw_small")]
    out["loss"] = res[0]["loss"]
    for n in small.keys() | {"final_norm"}:
        for kind, r in zip(("grad", "delta", "new_m", "new_v"), res):
            out[f"{kind}_{n}"] = r[n].reshape(w[n].shape)
    return out


def kernel(x, ffn1_norm, ffn1_w_gate, ffn1_w_up, ffn1_w_down, mix_norm, w_in, fox_forget_bias, mla_q_norm, mla_kv_norm, mla_w_uq, mla_w_ukv, w_out, ffn2_norm, ffn2_w_gate, ffn2_w_up, ffn2_w_down, final_norm, loss_target, m_ffn1_norm, m_ffn1_w_gate, m_ffn1_w_up, m_ffn1_w_down, m_mix_norm, m_w_in, m_fox_forget_bias, m_mla_q_norm, m_mla_kv_norm, m_mla_w_uq, m_mla_w_ukv, m_w_out, m_ffn2_norm, m_ffn2_w_gate, m_ffn2_w_up, m_ffn2_w_down, m_final_norm, v_ffn1_norm, v_ffn1_w_gate, v_ffn1_w_up, v_ffn1_w_down, v_mix_norm, v_w_in, v_fox_forget_bias, v_mla_q_norm, v_mla_kv_norm, v_mla_w_uq, v_mla_w_ukv, v_w_out, v_ffn2_norm, v_ffn2_w_gate, v_ffn2_w_up, v_ffn2_w_down, v_final_norm):
    args = locals()
    w = {n: args[n] for n in WEIGHTS}
    m = {n: args["m_" + n] for n in WEIGHTS}
    v = {n: args["v_" + n] for n in WEIGHTS}
    out = step(x, loss_target, w, m, v)
    return (out["loss"], out["grad_x"], *[out["grad_" + n] for n in WEIGHTS], *[out["delta_" + n] for n in WEIGHTS],
            *[out["new_m_" + n] for n in WEIGHTS], *[out["new_v_" + n] for n in WEIGHTS])
```
